```python
import math
import jax, jax.numpy as jnp
from jax import lax
import numpy as np

D_MODEL = 4096
BATCH = 4
SEQ = 4096
DEPTH = 1

ATTN_HEADS = 16
ATTN_KV_HEADS = 4
ATTN_HEAD_DIM = 128
WINDOW = 128
ATTN_BLOCK = 128
N_BUCKETS = 32
MAX_DISTANCE = 128
RET_HEADS = 8
RET_QK_DIM = 128
RET_V_DIM = 256
RET_CHUNK = 128
ROPE_BASE = 10000.0
N_EXPERTS = 32
TOP_K = 4
D_FF = 1024
SWIGLU_LIMIT = 7.0
SWIGLU_ALPHA = 1.702

EPS = 1e-6
NEG_INF = -1e30

ATTN_Q_W = ATTN_HEADS * ATTN_HEAD_DIM
ATTN_KV_W = ATTN_KV_HEADS * ATTN_HEAD_DIM
RET_QK_W = RET_HEADS * RET_QK_DIM
RET_V_W = RET_HEADS * RET_V_DIM
PROJ_WIDTHS = (ATTN_Q_W, ATTN_KV_W, ATTN_KV_W, RET_QK_W, RET_QK_W, RET_V_W, RET_V_W, D_MODEL, D_MODEL)
IN_PROJ_W = sum(PROJ_WIDTHS)
N_ADA = 6

kernel_name = "hybrid_gated_swa_retention_moe"


def rms_norm(x, w):
    xf = x.astype(jnp.float32)
    return xf * lax.rsqrt(jnp.mean(xf * xf, axis=-1, keepdims=True) + EPS) * w.astype(jnp.float32)


def t5_bucket(rel):
    nb = N_BUCKETS // 2
    max_exact = nb // 2
    base = jnp.where(rel > 0, nb, 0)
    n = jnp.abs(rel)
    nf = jnp.maximum(n, 1).astype(jnp.float32)
    large = max_exact + (jnp.log(nf / max_exact) / math.log(MAX_DISTANCE / max_exact) * (nb - max_exact)).astype(jnp.int32)
    large = jnp.minimum(large, nb - 1)
    return base + jnp.where(n < max_exact, n, large)


def windowed_gqa_attention(q, k, v, bias_table, sink, q_norm_w, k_norm_w):
    B, S = q.shape[0], q.shape[1]
    blk = ATTN_BLOCK
    nb = S // blk
    G = ATTN_HEADS // ATTN_KV_HEADS
    side = -(-WINDOW // blk)
    nkb = 2 * side + 1
    q = rms_norm(q, q_norm_w) * (ATTN_HEAD_DIM ** -0.5)
    k = rms_norm(k, k_norm_w)
    v = v.astype(jnp.float32)
    qb = q.reshape(B, nb, blk, ATTN_KV_HEADS, G, ATTN_HEAD_DIM)
    pad = ((0, 0), (side * blk, side * blk), (0, 0), (0, 0))
    kp = jnp.pad(k, pad).reshape(B, nb + 2 * side, blk, ATTN_KV_HEADS, ATTN_HEAD_DIM)
    vp = jnp.pad(v, pad).reshape(B, nb + 2 * side, blk, ATTN_KV_HEADS, ATTN_HEAD_DIM)
    kb = jnp.concatenate([kp[:, o:o + nb] for o in range(nkb)], axis=2)
    vb = jnp.concatenate([vp[:, o:o + nb] for o in range(nkb)], axis=2)
    logits = jnp.einsum('bnqhgd,bnkhd->bnhgqk', qb, kb)
    qi = jnp.arange(blk, dtype=jnp.int32)[:, None]
    kj = jnp.arange(nkb * blk, dtype=jnp.int32)[None, :]
    rel = kj - side * blk - qi
    bias = bias_table[t5_bucket(rel)].astype(jnp.float32)
    bias = jnp.transpose(bias, (2, 0, 1)).reshape(ATTN_KV_HEADS, G, blk, nkb * blk)
    key_pos = jnp.arange(nb, dtype=jnp.int32)[:, None] * blk + kj - side * blk
    in_range = (key_pos >= 0) & (key_pos < S)
    valid = (jnp.abs(rel) <= WINDOW)[None] & in_range[:, None, :]
    logits = jnp.where(valid[None, :, None, None], logits + bias, NEG_INF)
    sink_l = sink.astype(jnp.float32).reshape(1, 1, ATTN_KV_HEADS, G, 1, 1)
    m = jnp.maximum(jnp.max(logits, axis=-1, keepdims=True), sink_l)
    p = jnp.exp(logits - m)
    denom = jnp.sum(p, axis=-1, keepdims=True) + jnp.exp(sink_l - m)
    out = jnp.einsum('bnhgqk,bnkhd->bnqhgd', p / denom, vb)
    return out.reshape(B, S, ATTN_Q_W)


def rope(x, pos):
    d = x.shape[-1]
    half = d // 2
    inv = ROPE_BASE ** (-jnp.arange(0, d, 2, dtype=jnp.float32) / d)
    ang = pos[:, None] * inv[None, :]
    cos = jnp.cos(ang)[:, None, :]
    sin = jnp.sin(ang)[:, None, :]
    x1, x2 = x[..., :half], x[..., half:]
    return jnp.concatenate([x1 * cos - x2 * sin, x1 * sin + x2 * cos], axis=-1)


def retention_direction(q, k, v, log_gamma, strict):
    B, S, H, dk = q.shape
    dv = v.shape[-1]
    C = RET_CHUNK
    nc = S // C
    qc = q.reshape(B, nc, C, H, dk)
    kc = k.reshape(B, nc, C, H, dk)
    vc = v.reshape(B, nc, C, H, dv)
    i = jnp.arange(C, dtype=jnp.int32)
    diff = (i[:, None] - i[None, :]).astype(jnp.float32)
    mask = (i[:, None] > i[None, :]) if strict else (i[:, None] >= i[None, :])
    decay = jnp.where(mask[None], jnp.exp(jnp.where(mask, diff, 0.0)[None] * log_gamma[:, None, None]), 0.0)
    scores = jnp.einsum('bnihd,bnjhd->bnhij', qc, kc) * decay
    intra = jnp.einsum('bnhij,bnjhe->bnihe', scores, vc)
    fi = i.astype(jnp.float32)
    k_decay = jnp.exp((C - 1 - fi)[:, None] * log_gamma[None, :])
    q_decay = jnp.exp((fi + 1.0)[:, None] * log_gamma[None, :])
    kv = jnp.einsum('bnjhd,bnjhe->nbhde', kc * k_decay[:, :, None], vc)
    chunk_decay = jnp.exp(C * log_gamma)[:, None, None]

    def step(state, kv_n):
        return state * chunk_decay + kv_n, state

    _, prev = lax.scan(step, jnp.zeros_like(kv[0]), kv)
    cross = jnp.einsum('bnihd,nbhde->bnihe', qc * q_decay[:, :, None], prev)
    return (intra + cross).reshape(B, S, H, dv)


def retention_branch(q, k, v, g, decay_fwd, decay_bwd, gn_w, gn_b):
    B, S = q.shape[0], q.shape[1]
    pos = jnp.arange(S, dtype=jnp.float32)
    q = rope(q.astype(jnp.float32), pos)
    k = rope(k.astype(jnp.float32), pos) * (RET_QK_DIM ** -0.5)
    v = v.astype(jnp.float32)
    lg_f = -jnp.exp(decay_fwd.astype(jnp.float32))
    lg_b = -jnp.exp(decay_bwd.astype(jnp.float32))
    y_f = retention_direction(q, k, v, lg_f, strict=False)
    y_b = retention_direction(q[:, ::-1], k[:, ::-1], v[:, ::-1], lg_b, strict=True)[:, ::-1]
    y = y_f + y_b
    mu = jnp.mean(y, axis=-1, keepdims=True)
    var = jnp.mean(jnp.square(y - mu), axis=-1, keepdims=True)
    y = ((y - mu) * lax.rsqrt(var + EPS)).reshape(B, S, RET_V_W)
    y = y * gn_w.astype(jnp.float32) + gn_b.astype(jnp.float32)
    return jax.nn.silu(g.astype(jnp.float32)) * y


def moe_ffn(h, router_w, router_b, w1, b1, w2, b2):
    B, S, D = h.shape
    t = h.reshape(B * S, D)
    logits = (t @ router_w + router_b).astype(jnp.float32)
    top_v, top_i = lax.top_k(logits, TOP_K)
    top_w = jax.nn.softmax(top_v, axis=-1)
    combine = jnp.einsum('tk,tke->te', top_w, jax.nn.one_hot(top_i, N_EXPERTS, dtype=jnp.float32))
    y = jnp.zeros((B * S, D), jnp.float32)
    for e in range(N_EXPERTS):
        hh = t @ w1[e] + b1[e]
        gate = jnp.minimum(hh[:, :D_FF], SWIGLU_LIMIT)
        up = jnp.clip(hh[:, D_FF:], -SWIGLU_LIMIT, SWIGLU_LIMIT)
        act = gate * jax.nn.sigmoid(SWIGLU_ALPHA * gate) * (up + 1.0)
        y = y + combine[:, e:e + 1] * (act @ w2[e] + b2[e])
    return y.reshape(B, S, D)


def setup_inputs(seed: int = 0) -> dict:
    key = jax.random.key(seed)
    ks = jax.random.split(key, 24)
    f32 = jnp.float32
    D, L, E, F = D_MODEL, DEPTH, N_EXPERTS, D_FF
    nrm = lambda k, shape, s: jax.random.normal(k, shape, f32) * s
    h_idx = jnp.arange(RET_HEADS, dtype=f32)
    base_decay = jnp.log(-jnp.log1p(-(2.0 ** (-5.0 - h_idx))))
    return {
        "x": nrm(ks[0], (BATCH, SEQ, D), 1.0),
        "c": nrm(ks[1], (BATCH, D), 1.0),
        "rel_bias": nrm(ks[2], (N_BUCKETS, ATTN_HEADS), 0.5),
        "ada_w": nrm(ks[3], (L, D, N_ADA * D), 0.5 * D ** -0.5),
        "ada_b": nrm(ks[4], (L, N_ADA * D), 0.02),
        "norm_mix_w": 1.0 + nrm(ks[5], (L, D), 0.02),
        "w_in": nrm(ks[6], (L, D, IN_PROJ_W), D ** -0.5),
        "q_norm_w": 1.0 + nrm(ks[7], (L, ATTN_HEAD_DIM), 0.02),
        "k_norm_w": 1.0 + nrm(ks[8], (L, ATTN_HEAD_DIM), 0.02),
        "attn_sink": nrm(ks[9], (L, ATTN_HEADS), 0.5),
        "ret_decay_fwd": base_decay[None, :] + nrm(ks[10], (L, RET_HEADS), 0.05),
        "ret_decay_bwd": base_decay[None, :] + nrm(ks[11], (L, RET_HEADS), 0.05),
        "ret_gn_w": 1.0 + nrm(ks[12], (L, RET_V_W), 0.02),
        "ret_gn_b": nrm(ks[13], (L, RET_V_W), 0.02),
        "w_up_attn": nrm(ks[14], (L, ATTN_Q_W, D), ATTN_Q_W ** -0.5),
        "w_up_ret": nrm(ks[15], (L, RET_V_W, D), RET_V_W ** -0.5),
        "w_out": nrm(ks[16], (L, D, D), D ** -0.5),
        "norm_ffn_w": 1.0 + nrm(ks[17], (L, D), 0.02),
        "router_w": nrm(ks[18], (L, D, E), D ** -0.5),
        "router_b": nrm(ks[19], (L, E), 0.01),
        "expert_w1": nrm(ks[20], (L, E, D, 2 * F), D ** -0.5),
        "expert_b1": nrm(ks[21], (L, E, 2 * F), 0.01),
        "expert_w2": nrm(ks[22], (L, E, F, D), F ** -0.5),
        "expert_b2": nrm(ks[23], (L, E, D), 0.01),
    }


def reference(x, c, rel_bias, ada_w, ada_b, norm_mix_w, w_in, q_norm_w, k_norm_w, attn_sink,
              ret_decay_fwd, ret_decay_bwd, ret_gn_w, ret_gn_b, w_up_attn, w_up_ret, w_out,
              norm_ffn_w, router_w, router_b, expert_w1, expert_b1, expert_w2, expert_b2):
    B, S, D = x.shape
    split_idx = [int(v) for v in np.cumsum(PROJ_WIDTHS)[:-1]]
    cs = jax.nn.silu(c)
    for l in range(DEPTH):
        mod = (cs @ ada_w[l] + ada_b[l])[:, None, :]
        sh_m, sc_m, g_m, sh_f, sc_f, g_f = jnp.split(mod, N_ADA, axis=-1)
        h = rms_norm(x, norm_mix_w[l]) * (1.0 + sc_m) + sh_m
        proj = h @ w_in[l]
        aq, ak, av, rq, rk, rv, rg, ga, gr = jnp.split(proj, split_idx, axis=-1)
        attn = windowed_gqa_attention(
            aq.reshape(B, S, ATTN_HEADS, ATTN_HEAD_DIM),
            ak.reshape(B, S, ATTN_KV_HEADS, ATTN_HEAD_DIM),
            av.reshape(B, S, ATTN_KV_HEADS, ATTN_HEAD_DIM),
            rel_bias, attn_sink[l], q_norm_w[l], k_norm_w[l])
        ret = retention_branch(
            rq.reshape(B, S, RET_HEADS, RET_QK_DIM),
            rk.reshape(B, S, RET_HEADS, RET_QK_DIM),
            rv.reshape(B, S, RET_HEADS, RET_V_DIM),
            rg, ret_decay_fwd[l], ret_decay_bwd[l], ret_gn_w[l], ret_gn_b[l])
        merged = jax.nn.sigmoid(ga) * (attn @ w_up_attn[l]) + jax.nn.sigmoid(gr) * (ret @ w_up_ret[l])
        x = x + g_m * (merged @ w_out[l])
        h = rms_norm(x, norm_ffn_w[l]) * (1.0 + sc_f) + sh_f
        x = x + g_f * moe_ffn(h, router_w[l], router_b[l], expert_w1[l], expert_b1[l], expert_w2[l], expert_b2[l])
    return x
```

```python
import functools
import math

import jax
import jax.numpy as jnp
from jax import lax
from jax.experimental import pallas as pl
from jax.experimental.pallas import tpu as pltpu

F32 = jnp.float32
BF16 = jnp.bfloat16
U32 = jnp.uint32
I32 = jnp.int32

ATTN_HEADS = 16
ATTN_KV_HEADS = 4
ATTN_HEAD_DIM = 128
WINDOW = 128
ATTN_BLOCK = 128
N_BUCKETS = 32
MAX_DISTANCE = 128
RET_HEADS = 8
RET_QK_DIM = 128
RET_V_DIM = 256
RET_CHUNK = 128
ROPE_BASE = 10000.0
TOP_K = 4
SWIGLU_LIMIT = 7.0
SWIGLU_ALPHA = 1.702
N_ADA = 6
EPS = 1e-6
NEG_INF = -1e30
NEG_BIG = -3.0e38

ATTN_Q_W = ATTN_HEADS * ATTN_HEAD_DIM
ATTN_KV_W = ATTN_KV_HEADS * ATTN_HEAD_DIM
RET_QK_W = RET_HEADS * RET_QK_DIM
RET_V_W = RET_HEADS * RET_V_DIM

VMEM_LIMIT_BYTES = 56 * 1024 * 1024
MOE_TILE = 256


def _cparams(sem):
    return pltpu.CompilerParams(dimension_semantics=sem, vmem_limit_bytes=VMEM_LIMIT_BYTES)


def _dot(a, b):
    return jnp.dot(a, b, preferred_element_type=F32)


def _dot_nt(a, b):
    return lax.dot_general(a, b, (((1,), (1,)), ((), ())), preferred_element_type=F32)


def _dot_tn(a, b):
    return lax.dot_general(a, b, (((0,), (0,)), ((), ())), preferred_element_type=F32)


def _sigmoid(x):
    return 1.0 / (1.0 + jnp.exp(-x))


def _ada_kernel(c_ref, w_ref, b_ref, o_ref):
    c = c_ref[...]
    cs = (c * _sigmoid(c)).astype(BF16)
    o_ref[...] = _dot(cs, w_ref[...].astype(BF16)) + b_ref[...]


def _ada_mod(c, ada_w, ada_b):
    B, D = c.shape
    N = ada_w.shape[1]
    rows = 8
    cp = jnp.zeros((rows, D), F32).at[:B].set(c)
    tn = min(512, N)
    out = pl.pallas_call(
        _ada_kernel,
        out_shape=jax.ShapeDtypeStruct((rows, N), F32),
        grid=(N // tn,),
        in_specs=[
            pl.BlockSpec((rows, D), lambda j: (0, 0)),
            pl.BlockSpec((D, tn), lambda j: (0, j)),
            pl.BlockSpec((1, tn), lambda j: (0, j)),
        ],
        out_specs=pl.BlockSpec((rows, tn), lambda j: (0, j)),
        compiler_params=_cparams(("arbitrary",)),
        name="ada_mod",
    )(cp, ada_w, ada_b.reshape(1, N))
    return out[:B].reshape(B, N_ADA, D)


def _norm_mod_value(x, nw, shift, scale):
    ms = jnp.mean(x * x, axis=-1, keepdims=True)
    h = x * lax.rsqrt(ms + EPS) * nw
    return h * (1.0 + scale) + shift


def _norm_mod_kernel(x_ref, nw_ref, mod_ref, o_ref, *, shift_idx, scale_idx):
    h = _norm_mod_value(x_ref[...], nw_ref[...], mod_ref[shift_idx:shift_idx + 1, :],
                        mod_ref[scale_idx:scale_idx + 1, :])
    o_ref[...] = h.astype(BF16)


def _norm_mod(x2, nw, mod, S, shift_idx, scale_idx):
    T, D = x2.shape
    tm = min(256, S)
    return pl.pallas_call(
        functools.partial(_norm_mod_kernel, shift_idx=shift_idx, scale_idx=scale_idx),
        out_shape=jax.ShapeDtypeStruct((T, D), BF16),
        grid=(T // tm,),
        in_specs=[
            pl.BlockSpec((tm, D), lambda i: (i, 0)),
            pl.BlockSpec((1, D), lambda i: (0, 0)),
            pl.BlockSpec((None, N_ADA, D), lambda i: ((i * tm) // S, 0, 0)),
        ],
        out_specs=pl.BlockSpec((tm, D), lambda i: (i, 0)),
        compiler_params=_cparams(("arbitrary",)),
        name="norm_mod",
    )(x2, nw.reshape(1, D), mod)


def _matmul_kernel(a_ref, w_ref, o_ref):
    o_ref[...] = _dot(a_ref[...], w_ref[...]).astype(o_ref.dtype)


def _in_proj(h, w):
    T, K = h.shape
    N = w.shape[1]
    tm = min(1024, T)
    tn = 1024 if N % 1024 == 0 else 512
    return pl.pallas_call(
        _matmul_kernel,
        out_shape=jax.ShapeDtypeStruct((T, N), BF16),
        grid=(T // tm, N // tn),
        in_specs=[
            pl.BlockSpec((tm, K), lambda i, j: (i, 0)),
            pl.BlockSpec((K, tn), lambda i, j: (0, j)),
        ],
        out_specs=pl.BlockSpec((tm, tn), lambda i, j: (i, j)),
        compiler_params=_cparams(("arbitrary", "arbitrary")),
        name="in_proj",
    )(h, w)


def _t5_bucket(rel):
    nb = N_BUCKETS // 2
    max_exact = nb // 2
    base = jnp.where(rel > 0, nb, 0)
    n = jnp.abs(rel)
    nf = jnp.maximum(n, 1).astype(F32)
    large = max_exact + (jnp.log(nf / max_exact) / math.log(MAX_DISTANCE / max_exact) * (nb - max_exact)).astype(I32)
    large = jnp.minimum(large, nb - 1)
    return base + jnp.where(n < max_exact, n, large)


def _attn_bias_table(rel_bias):
    blk = ATTN_BLOCK
    qi = jnp.arange(blk, dtype=I32)[:, None]
    kj = jnp.arange(3 * blk, dtype=I32)[None, :]
    rel = kj - blk - qi
    bias = rel_bias[_t5_bucket(rel)].astype(F32)
    bias = jnp.where((jnp.abs(rel) <= WINDOW)[:, :, None], bias, NEG_INF)
    return jnp.transpose(bias, (2, 0, 1))


def _attn_kernel(sink_ref, q_ref, kp_ref, kc_ref, kn_ref, vp_ref, vc_ref, vn_ref, bias_ref, qw_ref, kw_ref, o_ref):
    n = pl.program_id(1)
    nb = pl.num_programs(1)
    blk, hd = ATTN_BLOCK, ATTN_HEAD_DIM
    G = ATTN_HEADS // ATTN_KV_HEADS
    col = lax.broadcasted_iota(I32, (1, 3 * blk), 1)
    oob = jnp.logical_or(jnp.logical_and(col < blk, n == 0), jnp.logical_and(col >= 2 * blk, n == nb - 1))
    qw = qw_ref[...]
    kw = kw_ref[...]
    scale = hd ** -0.5
    for hk in range(ATTN_KV_HEADS):
        sl = slice(hk * hd, (hk + 1) * hd)
        k3 = jnp.concatenate([kp_ref[:, sl], kc_ref[:, sl], kn_ref[:, sl]], axis=0).astype(F32)
        k3 = k3 * lax.rsqrt(jnp.mean(k3 * k3, axis=-1, keepdims=True) + EPS) * kw
        k3 = k3.astype(BF16)
        v3 = jnp.concatenate([vp_ref[:, sl], vc_ref[:, sl], vn_ref[:, sl]], axis=0)
        qs, sinks = [], []
        for g in range(G):
            h = hk * G + g
            q = q_ref[:, h * hd:(h + 1) * hd].astype(F32)
            q = q * lax.rsqrt(jnp.mean(q * q, axis=-1, keepdims=True) + EPS) * qw * scale
            qs.append(q.astype(BF16))
            sinks.append(jnp.full((blk, 1), sink_ref[h], F32))
        q4 = jnp.concatenate(qs, axis=0)
        sink4 = jnp.concatenate(sinks, axis=0)
        bias4 = bias_ref[hk * G:(hk + 1) * G].reshape(G * blk, 3 * blk)
        logits = _dot_nt(q4, k3) + bias4
        logits = jnp.where(oob, NEG_INF, logits)
        m = jnp.maximum(jnp.max(logits, axis=-1, keepdims=True), sink4)
        p = jnp.exp(logits - m)
        denom = jnp.sum(p, axis=-1, keepdims=True) + jnp.exp(sink4 - m)
        out = _dot(p.astype(BF16), v3) / denom
        for g in range(G):
            h = hk * G + g
            o_ref[:, h * hd:(h + 1) * hd] = out[g * blk:(g + 1) * blk].astype(o_ref.dtype)


def _window_attention(proj, bias_tab, sink, q_norm_w, k_norm_w, B, S):
    T = proj.shape[0]
    blk = ATTN_BLOCK
    nb = S // blk
    kcol = ATTN_Q_W // ATTN_KV_W
    vcol = kcol + 1

    def kv_spec(col, off):
        return pl.BlockSpec((blk, ATTN_KV_W), lambda b, n, s: (b * nb + jnp.clip(n + off, 0, nb - 1), col))

    grid_spec = pltpu.PrefetchScalarGridSpec(
        num_scalar_prefetch=1,
        grid=(B, nb),
        in_specs=[
            pl.BlockSpec((blk, ATTN_Q_W), lambda b, n, s: (b * nb + n, 0)),
            kv_spec(kcol, -1), kv_spec(kcol, 0), kv_spec(kcol, 1),
            kv_spec(vcol, -1), kv_spec(vcol, 0), kv_spec(vcol, 1),
            pl.BlockSpec((ATTN_HEADS, blk, 3 * blk), lambda b, n, s: (0, 0, 0)),
            pl.BlockSpec((1, ATTN_HEAD_DIM), lambda b, n, s: (0, 0)),
            pl.BlockSpec((1, ATTN_HEAD_DIM), lambda b, n, s: (0, 0)),
        ],
        out_specs=pl.BlockSpec((blk, ATTN_Q_W), lambda b, n, s: (b * nb + n, 0)),
    )
    return pl.pallas_call(
        _attn_kernel,
        out_shape=jax.ShapeDtypeStruct((T, ATTN_Q_W), BF16),
        grid_spec=grid_spec,
        compiler_params=_cparams(("arbitrary", "arbitrary")),
        name="window_attn",
    )(sink.astype(F32), proj, proj, proj, proj, proj, proj, proj, bias_tab,
      q_norm_w.reshape(1, -1), k_norm_w.reshape(1, -1))


def _rope_tables(S):
    d = RET_QK_DIM
    inv = ROPE_BASE ** (-jnp.arange(0, d, 2, dtype=F32) / d)
    ang = jnp.arange(S, dtype=F32)[:, None] * inv[None, :]
    cos, sin = jnp.cos(ang), jnp.sin(ang)
    return jnp.concatenate([cos, cos], axis=-1), jnp.concatenate([-sin, sin], axis=-1)


def _ret_kernel(df_ref, db_ref, q_ref, k_ref, v_ref, g_ref, cos_ref, sin_ref, gw_ref, gb_ref, o_ref,
                qr, kr, yf, st, *, S):
    h = pl.program_id(1)
    C = RET_CHUNK
    nc = S // C
    dk = RET_QK_DIM
    half = dk // 2

    rc = min(512, S)

    def rope_body(c, carry):
        rows = pl.ds(pl.multiple_of(c * rc, rc), rc)
        co = cos_ref[rows, :]
        si = sin_ref[rows, :]
        q = q_ref[rows, :].astype(F32)
        k = k_ref[rows, :].astype(F32)
        qr[rows, :] = q * co + pltpu.roll(q, half, 1) * si
        kr[rows, :] = (k * co + pltpu.roll(k, half, 1) * si) * (dk ** -0.5)
        return carry

    lax.fori_loop(0, S // rc, rope_body, 0)

    ri = lax.broadcasted_iota(I32, (C, C), 0)
    ci = lax.broadcasted_iota(I32, (C, C), 1)
    rowf = lax.broadcasted_iota(I32, (C, 1), 0).astype(F32)
    lg_f = -jnp.exp(jnp.full((1, 1), df_ref[h], F32))
    lg_b = -jnp.exp(jnp.full((1, 1), db_ref[h], F32))

    mask_f = ri >= ci
    dec_f = jnp.where(mask_f, jnp.exp(jnp.where(mask_f, (ri - ci).astype(F32), 0.0) * lg_f), 0.0)
    qdec_f = jnp.exp((rowf + 1.0) * lg_f)
    kdec_f = jnp.exp((C - 1.0 - rowf) * lg_f)
    cd_f = jnp.exp(C * lg_f)
    mask_b = ci > ri
    dec_b = jnp.where(mask_b, jnp.exp(jnp.where(mask_b, (ci - ri).astype(F32), 0.0) * lg_b), 0.0)
    qdec_b = jnp.exp((C - rowf) * lg_b)
    kdec_b = jnp.exp(rowf * lg_b)
    cd_b = jnp.exp(C * lg_b)

    def chunk(n, dec, qdec, kdec, cd):
        rows = pl.ds(pl.multiple_of(n * C, C), C)
        q = qr[rows, :]
        k = kr[rows, :]
        v = v_ref[rows, :]
        scores = _dot_nt(q.astype(BF16), k.astype(BF16)) * dec
        intra = _dot(scores.astype(BF16), v)
        state = st[...]
        cross = _dot((q * qdec).astype(BF16), state.astype(BF16))
        st[...] = state * cd + _dot_tn((k * kdec).astype(BF16), v)
        return rows, intra + cross

    st[...] = jnp.zeros_like(st)

    def fwd_body(n, carry):
        rows, y = chunk(n, dec_f, qdec_f, kdec_f, cd_f)
        yf[rows, :] = y
        return carry

    lax.fori_loop(0, nc, fwd_body, 0)

    st[...] = jnp.zeros_like(st)
    gw = gw_ref[...]
    gb = gb_ref[...]

    def bwd_body(t, carry):
        rows, yb = chunk(nc - 1 - t, dec_b, qdec_b, kdec_b, cd_b)
        y = yf[rows, :] + yb
        mu = jnp.mean(y, axis=-1, keepdims=True)
        yc = y - mu
        var = jnp.mean(yc * yc, axis=-1, keepdims=True)
        yn = yc * lax.rsqrt(var + EPS) * gw + gb
        g = g_ref[rows, :].astype(F32)
        o_ref[rows, :] = (g * _sigmoid(g) * yn).astype(o_ref.dtype)
        return carry

    lax.fori_loop(0, nc, bwd_body, 0)


def _retention(proj, decay_fwd, decay_bwd, gn_w, gn_b, B, S):
    T = proj.shape[0]
    dk, dv = RET_QK_DIM, RET_V_DIM
    q_off = (ATTN_Q_W + 2 * ATTN_KV_W) // dk
    k_off = q_off + RET_QK_W // dk
    v_off = (ATTN_Q_W + 2 * ATTN_KV_W + 2 * RET_QK_W) // dv
    g_off = v_off + RET_V_W // dv
    cos, sin = _rope_tables(S)
    grid_spec = pltpu.PrefetchScalarGridSpec(
        num_scalar_prefetch=2,
        grid=(B, RET_HEADS),
        in_specs=[
            pl.BlockSpec((S, dk), lambda b, h, *_: (b, q_off + h)),
            pl.BlockSpec((S, dk), lambda b, h, *_: (b, k_off + h)),
            pl.BlockSpec((S, dv), lambda b, h, *_: (b, v_off + h)),
            pl.BlockSpec((S, dv), lambda b, h, *_: (b, g_off + h)),
            pl.BlockSpec((S, dk), lambda b, h, *_: (0, 0)),
            pl.BlockSpec((S, dk), lambda b, h, *_: (0, 0)),
            pl.BlockSpec((1, dv), lambda b, h, *_: (0, h)),
            pl.BlockSpec((1, dv), lambda b, h, *_: (0, h)),
        ],
        out_specs=pl.BlockSpec((S, dv), lambda b, h, *_: (b, h)),
        scratch_shapes=[
            pltpu.VMEM((S, dk), F32),
            pltpu.VMEM((S, dk), F32),
            pltpu.VMEM((S, dv), F32),
            pltpu.VMEM((dk, dv), F32),
        ],
    )
    return pl.pallas_call(
        functools.partial(_ret_kernel, S=S),
        out_shape=jax.ShapeDtypeStruct((T, RET_V_W), BF16),
        grid_spec=grid_spec,
        compiler_params=_cparams(("arbitrary", "arbitrary")),
        name="retention",
    )(decay_fwd.astype(F32), decay_bwd.astype(F32), proj, proj, proj, proj, cos, sin,
      gn_w.reshape(1, -1), gn_b.reshape(1, -1))


def _merge_kernel(a_ref, r_ref, wa_ref, wr_ref, ga_ref, gr_ref, o_ref):
    ua = _dot(a_ref[...], wa_ref[...])
    ur = _dot(r_ref[...], wr_ref[...])
    out = _sigmoid(ga_ref[...].astype(F32)) * ua + _sigmoid(gr_ref[...].astype(F32)) * ur
    o_ref[...] = out.astype(o_ref.dtype)


def _merge_up(attn, ret, wa, wr, proj, D):
    T = attn.shape[0]
    tm = min(512, T)
    tn = min(1024, D)
    ga_off = (ATTN_Q_W + 2 * ATTN_KV_W + 2 * RET_QK_W + 2 * RET_V_W) // tn
    gr_off = ga_off + D // tn
    return pl.pallas_call(
        _merge_kernel,
        out_shape=jax.ShapeDtypeStruct((T, D), BF16),
        grid=(T // tm, D // tn),
        in_specs=[
            pl.BlockSpec((tm, ATTN_Q_W), lambda i, j: (i, 0)),
            pl.BlockSpec((tm, RET_V_W), lambda i, j: (i, 0)),
            pl.BlockSpec((ATTN_Q_W, tn), lambda i, j: (0, j)),
            pl.BlockSpec((RET_V_W, tn), lambda i, j: (0, j)),
            pl.BlockSpec((tm, tn), lambda i, j: (i, ga_off + j)),
            pl.BlockSpec((tm, tn), lambda i, j: (i, gr_off + j)),
        ],
        out_specs=pl.BlockSpec((tm, tn), lambda i, j: (i, j)),
        compiler_params=_cparams(("arbitrary", "arbitrary")),
        name="merge_up",
    )(attn, ret, wa, wr, proj, proj)


def _out_proj_kernel(m_ref, w_ref, x_ref, mod_ref, o_ref, *, gate_idx):
    y = _dot(m_ref[...], w_ref[...])
    o_ref[...] = x_ref[...] + mod_ref[gate_idx:gate_idx + 1, :] * y


def _out_proj(merged, w, x2, mod, S, gate_idx):
    T, D = x2.shape
    tm = min(512, S)
    tn = min(1024, D)
    return pl.pallas_call(
        functools.partial(_out_proj_kernel, gate_idx=gate_idx),
        out_shape=jax.ShapeDtypeStruct((T, D), F32),
        grid=(T // tm, D // tn),
        in_specs=[
            pl.BlockSpec((tm, D), lambda i, j: (i, 0)),
            pl.BlockSpec((D, tn), lambda i, j: (0, j)),
            pl.BlockSpec((tm, tn), lambda i, j: (i, j)),
            pl.BlockSpec((None, N_ADA, tn), lambda i, j: ((i * tm) // S, 0, j)),
        ],
        out_specs=pl.BlockSpec((tm, tn), lambda i, j: (i, j)),
        compiler_params=_cparams(("arbitrary", "arbitrary")),
        name="out_proj",
    )(merged, w, x2, mod)


def _pack_bf16_pairs(v):
    n = v.shape[1] // 2
    bits = lax.bitcast_convert_type(v.astype(BF16).astype(F32), U32)
    return jnp.bitwise_or(jnp.bitwise_and(bits[:, n:], jnp.uint32(0xFFFF0000)),
                          jnp.right_shift(bits[:, :n], jnp.uint32(16)))


def _unpack_bf16_pairs(w):
    lo = lax.bitcast_convert_type(jnp.left_shift(w, jnp.uint32(16)), F32)
    hi = lax.bitcast_convert_type(jnp.bitwise_and(w, jnp.uint32(0xFFFF0000)), F32)
    return jnp.concatenate([lo, hi], axis=1)


def _norm_router_kernel(x_ref, nw_ref, mod_ref, rwt_ref, rb_ref, hp_ref, e_ref, w_ref, r_ref, cnt_ref, carry,
                        *, shift_idx, scale_idx):
    i = pl.program_id(0)

    @pl.when(i == 0)
    def _():
        carry[...] = jnp.zeros_like(carry)

    h = _norm_mod_value(x_ref[...], nw_ref[...], mod_ref[shift_idx:shift_idx + 1, :],
                        mod_ref[scale_idx:scale_idx + 1, :])
    hp_ref[...] = _pack_bf16_pairs(h)
    hb = h.astype(BF16)
    tm = hb.shape[0]
    E = rwt_ref.shape[0]
    logits = _dot_nt(rwt_ref[...], hb) + rb_ref[...]
    iota_e = lax.broadcasted_iota(I32, (E, tm), 0)
    vals, idxs = [], []
    work = logits
    sel = jnp.zeros((E, tm), F32)
    for _k in range(TOP_K):
        m = jnp.max(work, axis=0, keepdims=True)
        idx = jnp.min(jnp.where(work == m, iota_e, E), axis=0, keepdims=True)
        hit = iota_e == idx
        vals.append(m)
        idxs.append(idx)
        work = jnp.where(hit, NEG_BIG, work)
        sel = sel + hit.astype(F32)
    ex = [jnp.exp(v - vals[0]) for v in vals]
    tot = ex[0]
    for v in ex[1:]:
        tot = tot + v
    ri = lax.broadcasted_iota(I32, (tm, tm), 0)
    ci = lax.broadcasted_iota(I32, (tm, tm), 1)
    upper = (ri < ci).astype(BF16)
    before = _dot(sel.astype(BF16), upper) + carry[:, 0:1]
    for k in range(TOP_K):
        e_ref[k:k + 1, :] = idxs[k]
        w_ref[k:k + 1, :] = ex[k] / tot
        r_ref[k:k + 1, :] = jnp.sum(jnp.where(iota_e == idxs[k], before, 0.0), axis=0, keepdims=True).astype(I32)
    carry[...] = carry[...] + jnp.sum(sel, axis=1, keepdims=True)
    cnt_ref[...] = carry[...]


def _norm_router(x1, nw, mod, router_w, router_b, S, shift_idx, scale_idx):
    T, D = x1.shape
    E = router_w.shape[1]
    tm = min(256, S)
    outs = pl.pallas_call(
        functools.partial(_norm_router_kernel, shift_idx=shift_idx, scale_idx=scale_idx),
        out_shape=(
            jax.ShapeDtypeStruct((T, D // 2), U32),
            jax.ShapeDtypeStruct((TOP_K, T), I32),
            jax.ShapeDtypeStruct((TOP_K, T), F32),
            jax.ShapeDtypeStruct((TOP_K, T), I32),
            jax.ShapeDtypeStruct((E, 128), F32),
        ),
        grid=(T // tm,),
        in_specs=[
            pl.BlockSpec((tm, D), lambda i: (i, 0)),
            pl.BlockSpec((1, D), lambda i: (0, 0)),
            pl.BlockSpec((None, N_ADA, D), lambda i: ((i * tm) // S, 0, 0)),
            pl.BlockSpec((E, D), lambda i: (0, 0)),
            pl.BlockSpec((E, 1), lambda i: (0, 0)),
        ],
        out_specs=(
            pl.BlockSpec((tm, D // 2), lambda i: (i, 0)),
            pl.BlockSpec((TOP_K, tm), lambda i: (0, i)),
            pl.BlockSpec((TOP_K, tm), lambda i: (0, i)),
            pl.BlockSpec((TOP_K, tm), lambda i: (0, i)),
            pl.BlockSpec((E, 128), lambda i: (0, 0)),
        ),
        scratch_shapes=[pltpu.VMEM((E, 128), F32)],
        compiler_params=_cparams(("arbitrary",)),
        name="norm_router",
    )(x1, nw.reshape(1, D), mod, router_w.T.astype(BF16), router_b.reshape(E, 1).astype(F32))
    return outs


def _dispatch_kernel(pos_ref, hp_ref, xs_in_ref, xs_ref, sem, *, tm, T):
    del xs_in_ref
    base = pl.program_id(0) * tm

    def row_copy(t, k):
        dst = pos_ref[k * T + base + t]
        return pltpu.make_async_copy(hp_ref.at[pl.ds(t, 1)], xs_ref.at[pl.ds(dst, 1)], sem)

    def start_body(t, carry):
        for k in range(TOP_K):
            row_copy(t, k).start()
        return carry

    lax.fori_loop(0, tm, start_body, 0)

    def wait_body(t, carry):
        for k in range(TOP_K):
            row_copy(t, k).wait()
        return carry

    lax.fori_loop(0, tm, wait_body, 0)


def _dispatch(pos_flat, hp, rows):
    T, W = hp.shape
    tm = min(256, T)
    xs0 = jnp.zeros((rows, W), U32)
    grid_spec = pltpu.PrefetchScalarGridSpec(
        num_scalar_prefetch=1,
        grid=(T // tm,),
        in_specs=[
            pl.BlockSpec((tm, W), lambda i, p: (i, 0)),
            pl.BlockSpec(memory_space=pl.ANY),
        ],
        out_specs=pl.BlockSpec(memory_space=pl.ANY),
        scratch_shapes=[pltpu.SemaphoreType.DMA],
    )
    return pl.pallas_call(
        functools.partial(_dispatch_kernel, tm=tm, T=T),
        out_shape=jax.ShapeDtypeStruct((rows, W), U32),
        grid_spec=grid_spec,
        input_output_aliases={2: 0},
        compiler_params=_cparams(("arbitrary",)),
        name="dispatch",
    )(pos_flat, hp, xs0)


def _expert_up_kernel(te_ref, nu_ref, xs_ref, w1_ref, b1_ref, o_ref):
    @pl.when(pl.program_id(0) < nu_ref[0])
    def _():
        x = _unpack_bf16_pairs(xs_ref[...]).astype(BF16)
        hh = _dot(x, w1_ref[...]) + b1_ref[...]
        F = hh.shape[1] // 2
        gate = jnp.minimum(hh[:, :F], SWIGLU_LIMIT)
        up = jnp.clip(hh[:, F:], -SWIGLU_LIMIT, SWIGLU_LIMIT)
        act = gate * _sigmoid(SWIGLU_ALPHA * gate) * (up + 1.0)
        o_ref[...] = act.astype(o_ref.dtype)

    @pl.when(pl.program_id(0) >= nu_ref[0])
    def _():
        o_ref[...] = jnp.zeros_like(o_ref)


def _expert_down_kernel(te_ref, nu_ref, a_ref, w2_ref, b2_ref, o_ref):
    @pl.when(pl.program_id(0) < nu_ref[0])
    def _():
        y = _dot(a_ref[...], w2_ref[...]) + b2_ref[...]
        o_ref[...] = _pack_bf16_pairs(y)

    @pl.when(pl.program_id(0) >= nu_ref[0])
    def _():
        o_ref[...] = jnp.zeros_like(o_ref)


def _expert_ffn(xs, tile_e, n_used, w1, b1, w2, b2):
    R, W = xs.shape
    E, D, F2 = w1.shape
    F = F2 // 2
    tm = MOE_TILE
    nt = R // tm

    def row_map(i, te, nu):
        return (jnp.minimum(i, nu[0] - 1), 0)

    def out_map(i, te, nu):
        return (i, 0)

    def exp_map(i, te, nu):
        return (te[i], 0, 0)

    act = pl.pallas_call(
        _expert_up_kernel,
        out_shape=jax.ShapeDtypeStruct((R, F), BF16),
        grid_spec=pltpu.PrefetchScalarGridSpec(
            num_scalar_prefetch=2,
            grid=(nt,),
            in_specs=[
                pl.BlockSpec((tm, W), row_map),
                pl.BlockSpec((None, D, F2), exp_map),
                pl.BlockSpec((None, 1, F2), exp_map),
            ],
            out_specs=pl.BlockSpec((tm, F), out_map),
        ),
        compiler_params=_cparams(("arbitrary",)),
        name="expert_up",
    )(tile_e, n_used, xs, w1, b1.reshape(E, 1, F2))
    y = pl.pallas_call(
        _expert_down_kernel,
        out_shape=jax.ShapeDtypeStruct((R, D // 2), U32),
        grid_spec=pltpu.PrefetchScalarGridSpec(
            num_scalar_prefetch=2,
            grid=(nt,),
            in_specs=[
                pl.BlockSpec((tm, F), row_map),
                pl.BlockSpec((None, F, D), exp_map),
                pl.BlockSpec((None, 1, D), exp_map),
            ],
            out_specs=pl.BlockSpec((tm, D // 2), out_map),
        ),
        compiler_params=_cparams(("arbitrary",)),
        name="expert_down",
    )(tile_e, n_used, act, w2, b2.reshape(E, 1, D))
    return y


def _combine_kernel(pos_ref, x_ref, w_ref, mod_ref, y_ref, o_ref, buf, sem, *, tm, T, gate_idx):
    base = pl.program_id(0) * tm

    def row_copy(t, k):
        src = pos_ref[k * T + base + t]
        return pltpu.make_async_copy(y_ref.at[pl.ds(src, 1)], buf.at[k, pl.ds(t, 1)], sem)

    def start_body(t, carry):
        for k in range(TOP_K):
            row_copy(t, k).start()
        return carry

    lax.fori_loop(0, tm, start_body, 0)

    def wait_body(t, carry):
        for k in range(TOP_K):
            row_copy(t, k).wait()
        return carry

    lax.fori_loop(0, tm, wait_body, 0)

    w = w_ref[...]
    acc = w[:, 0:1] * _unpack_bf16_pairs(buf[0])
    for k in range(1, TOP_K):
        acc = acc + w[:, k:k + 1] * _unpack_bf16_pairs(buf[k])
    o_ref[...] = x_ref[...] + mod_ref[gate_idx:gate_idx + 1, :] * acc


def _combine(pos_flat, x1, w_tk, mod, y, S, gate_idx):
    T, D = x1.shape
    tm = min(128, S)
    grid_spec = pltpu.PrefetchScalarGridSpec(
        num_scalar_prefetch=1,
        grid=(T // tm,),
        in_specs=[
            pl.BlockSpec((tm, D), lambda i, p: (i, 0)),
            pl.BlockSpec((tm, TOP_K), lambda i, p: (i, 0)),
            pl.BlockSpec((None, N_ADA, D), lambda i, p: ((i * tm) // S, 0, 0)),
            pl.BlockSpec(memory_space=pl.ANY),
        ],
        out_specs=pl.BlockSpec((tm, D), lambda i, p: (i, 0)),
        scratch_shapes=[pltpu.VMEM((TOP_K, tm, D // 2), U32), pltpu.SemaphoreType.DMA],
    )
    return pl.pallas_call(
        functools.partial(_combine_kernel, tm=tm, T=T, gate_idx=gate_idx),
        out_shape=jax.ShapeDtypeStruct((T, D), F32),
        grid_spec=grid_spec,
        compiler_params=_cparams(("arbitrary",)),
        name="combine",
    )(pos_flat, x1, w_tk, mod, y)


def _layer(x2, mod, B, S, rel_bias, norm_mix_w, w_in, q_norm_w, k_norm_w, attn_sink, ret_decay_fwd, ret_decay_bwd,
           ret_gn_w, ret_gn_b, w_up_attn, w_up_ret, w_out, norm_ffn_w, router_w, router_b,
           expert_w1, expert_b1, expert_w2, expert_b2):
    T, D = x2.shape
    E = router_w.shape[1]
    h = _norm_mod(x2, norm_mix_w, mod, S, 0, 1)
    proj = _in_proj(h, w_in.astype(BF16))
    attn = _window_attention(proj, _attn_bias_table(rel_bias), attn_sink, q_norm_w, k_norm_w, B, S)
    ret = _retention(proj, ret_decay_fwd, ret_decay_bwd, ret_gn_w, ret_gn_b, B, S)
    merged = _merge_up(attn, ret, w_up_attn.astype(BF16), w_up_ret.astype(BF16), proj, D)
    x1 = _out_proj(merged, w_out.astype(BF16), x2, mod, S, 2)

    hp, top_e, top_w, rank, cnt = _norm_router(x1, norm_ffn_w, mod, router_w, router_b, S, 3, 4)
    tm = MOE_TILE
    counts = cnt[:, 0].astype(I32)
    tiles_per_e = (counts + tm - 1) // tm
    tile_end = jnp.cumsum(tiles_per_e)
    group_start = (tile_end - tiles_per_e) * tm
    n_tiles = (T * TOP_K) // tm + E
    n_used = tile_end[-1]
    tile_ids = jnp.minimum(jnp.arange(n_tiles, dtype=I32), n_used - 1)
    tile_e = jnp.minimum(jnp.searchsorted(tile_end, tile_ids, side="right"), E - 1).astype(I32)
    pos = (group_start[top_e] + rank).astype(I32)
    pos_flat = pos.reshape(-1)

    xs = _dispatch(pos_flat, hp, n_tiles * tm)
    y = _expert_ffn(xs, tile_e, n_used.reshape(1).astype(I32), expert_w1.astype(BF16), expert_b1,
                    expert_w2.astype(BF16), expert_b2)
    return _combine(pos_flat, x1, top_w.T, mod, y, S, 5)


def kernel(x, c, rel_bias, ada_w, ada_b, norm_mix_w, w_in, q_norm_w, k_norm_w, attn_sink, ret_decay_fwd,
           ret_decay_bwd, ret_gn_w, ret_gn_b, w_up_attn, w_up_ret, w_out, norm_ffn_w, router_w, router_b,
           expert_w1, expert_b1, expert_w2, expert_b2):
    B, S, D = x.shape
    x2 = x.reshape(B * S, D)
    for l in range(ada_w.shape[0]):
        mod = _ada_mod(c, ada_w[l], ada_b[l])
        x2 = _layer(x2, mod, B, S, rel_bias, norm_mix_w[l], w_in[l], q_norm_w[l], k_norm_w[l], attn_sink[l],
                    ret_decay_fwd[l], ret_decay_bwd[l], ret_gn_w[l], ret_gn_b[l], w_up_attn[l], w_up_ret[l],
                    w_out[l], norm_ffn_w[l], router_w[l], router_b[l], expert_w1[l], expert_b1[l],
                    expert_w2[l], expert_b2[l])
    return x2.reshape(B, S, D)
```

```python
import functools
import math

import jax
import jax.numpy as jnp
from jax import lax
from jax.experimental import pallas as pl
from jax.experimental.pallas import tpu as pltpu

F32 = jnp.float32
BF16 = jnp.bfloat16
U32 = jnp.uint32
I32 = jnp.int32

ATTN_HEADS = 16
ATTN_KV_HEADS = 4
ATTN_HEAD_DIM = 128
WINDOW = 128
ATTN_BLOCK = 128
N_BUCKETS = 32
MAX_DISTANCE = 128
RET_HEADS = 8
RET_QK_DIM = 128
RET_V_DIM = 256
RET_CHUNK = 128
ROPE_BASE = 10000.0
TOP_K = 4
SWIGLU_LIMIT = 7.0
SWIGLU_ALPHA = 1.702
N_ADA = 6
EPS = 1e-6
NEG_INF = -1e30
NEG_BIG = -3.0e38

ATTN_Q_W = ATTN_HEADS * ATTN_HEAD_DIM
ATTN_KV_W = ATTN_KV_HEADS * ATTN_HEAD_DIM
RET_QK_W = RET_HEADS * RET_QK_DIM
RET_V_W = RET_HEADS * RET_V_DIM

VMEM_LIMIT_BYTES = 56 * 1024 * 1024
MOE_TILE = 256
RET_TILE = 256


def _cparams(sem):
    return pltpu.CompilerParams(dimension_semantics=sem, vmem_limit_bytes=VMEM_LIMIT_BYTES)


def _dot(a, b):
    return jnp.dot(a, b, preferred_element_type=F32)


def _dot_nt(a, b):
    return lax.dot_general(a, b, (((1,), (1,)), ((), ())), preferred_element_type=F32)


def _dot_tn(a, b):
    return lax.dot_general(a, b, (((0,), (0,)), ((), ())), preferred_element_type=F32)


def _sigmoid(x):
    return 1.0 / (1.0 + jnp.exp(-x))


def _ada_kernel(c_ref, w_ref, b_ref, o_ref):
    c = c_ref[...]
    cs = (c * _sigmoid(c)).astype(BF16)
    o_ref[...] = _dot(cs, w_ref[...].astype(BF16)) + b_ref[...]


def _ada_mod(c, ada_w, ada_b):
    B, D = c.shape
    N = ada_w.shape[1]
    rows = 8
    cp = jnp.zeros((rows, D), F32).at[:B].set(c)
    tn = min(512, N)
    out = pl.pallas_call(
        _ada_kernel,
        out_shape=jax.ShapeDtypeStruct((rows, N), F32),
        grid=(N // tn,),
        in_specs=[
            pl.BlockSpec((rows, D), lambda j: (0, 0)),
            pl.BlockSpec((D, tn), lambda j: (0, j)),
            pl.BlockSpec((1, tn), lambda j: (0, j)),
        ],
        out_specs=pl.BlockSpec((rows, tn), lambda j: (0, j)),
        compiler_params=_cparams(("arbitrary",)),
        name="ada_mod",
    )(cp, ada_w, ada_b.reshape(1, N))
    return out[:B].reshape(B, N_ADA, D)


def _norm_mod_value(x, nw, shift, scale):
    ms = jnp.mean(x * x, axis=-1, keepdims=True)
    h = x * lax.rsqrt(ms + EPS) * nw
    return h * (1.0 + scale) + shift


def _norm_mod_kernel(x_ref, nw_ref, mod_ref, o_ref, *, shift_idx, scale_idx):
    h = _norm_mod_value(x_ref[...], nw_ref[...], mod_ref[shift_idx:shift_idx + 1, :],
                        mod_ref[scale_idx:scale_idx + 1, :])
    o_ref[...] = h.astype(BF16)


def _norm_mod(x2, nw, mod, S, shift_idx, scale_idx):
    T, D = x2.shape
    tm = min(256, S)
    return pl.pallas_call(
        functools.partial(_norm_mod_kernel, shift_idx=shift_idx, scale_idx=scale_idx),
        out_shape=jax.ShapeDtypeStruct((T, D), BF16),
        grid=(T // tm,),
        in_specs=[
            pl.BlockSpec((tm, D), lambda i: (i, 0)),
            pl.BlockSpec((1, D), lambda i: (0, 0)),
            pl.BlockSpec((None, N_ADA, D), lambda i: ((i * tm) // S, 0, 0)),
        ],
        out_specs=pl.BlockSpec((tm, D), lambda i: (i, 0)),
        compiler_params=_cparams(("arbitrary",)),
        name="norm_mod",
    )(x2, nw.reshape(1, D), mod)


def _cast_weight_once(w_ref, wb_ref):
    @pl.when(pl.program_id(1) == 0)
    def _():
        wb_ref[...] = w_ref[...].astype(BF16)


def _in_proj_kernel(a_ref, w_ref, o_ref, wb):
    _cast_weight_once(w_ref, wb)
    o_ref[...] = _dot(a_ref[...], wb[...]).astype(o_ref.dtype)


def _in_proj(h, w):
    T, K = h.shape
    N = w.shape[1]
    tm = min(1024, T)
    tn = 512
    return pl.pallas_call(
        _in_proj_kernel,
        out_shape=jax.ShapeDtypeStruct((T, N), BF16),
        grid=(N // tn, T // tm),
        in_specs=[
            pl.BlockSpec((tm, K), lambda j, i: (i, 0)),
            pl.BlockSpec((K, tn), lambda j, i: (0, j)),
        ],
        out_specs=pl.BlockSpec((tm, tn), lambda j, i: (i, j)),
        scratch_shapes=[pltpu.VMEM((K, tn), BF16)],
        compiler_params=_cparams(("arbitrary", "arbitrary")),
        name="in_proj",
    )(h, w)


def _t5_bucket(rel):
    nb = N_BUCKETS // 2
    max_exact = nb // 2
    base = jnp.where(rel > 0, nb, 0)
    n = jnp.abs(rel)
    nf = jnp.maximum(n, 1).astype(F32)
    large = max_exact + (jnp.log(nf / max_exact) / math.log(MAX_DISTANCE / max_exact) * (nb - max_exact)).astype(I32)
    large = jnp.minimum(large, nb - 1)
    return base + jnp.where(n < max_exact, n, large)


def _attn_bias_table(rel_bias):
    blk = ATTN_BLOCK
    qi = jnp.arange(blk, dtype=I32)[:, None]
    kj = jnp.arange(3 * blk, dtype=I32)[None, :]
    rel = kj - blk - qi
    bucket = _t5_bucket(rel)
    table = rel_bias.astype(F32).T
    bias = jnp.zeros((table.shape[0],) + bucket.shape, F32)
    for b in range(N_BUCKETS):
        bias = jnp.where(bucket[None] == b, table[:, b][:, None, None], bias)
    return jnp.where((jnp.abs(rel) <= WINDOW)[None], bias, NEG_INF)


def _attn_kernel(sink_ref, q_ref, kp_ref, kc_ref, kn_ref, vp_ref, vc_ref, vn_ref, bias_ref, qw_ref, kw_ref, o_ref):
    n = pl.program_id(1)
    nb = pl.num_programs(1)
    blk, hd = ATTN_BLOCK, ATTN_HEAD_DIM
    G = ATTN_HEADS // ATTN_KV_HEADS
    col = lax.broadcasted_iota(I32, (1, 3 * blk), 1)
    oob = jnp.logical_or(jnp.logical_and(col < blk, n == 0), jnp.logical_and(col >= 2 * blk, n == nb - 1))
    qw = qw_ref[...]
    kw = kw_ref[...]
    scale = hd ** -0.5
    for hk in range(ATTN_KV_HEADS):
        sl = slice(hk * hd, (hk + 1) * hd)
        k3 = jnp.concatenate([kp_ref[:, sl], kc_ref[:, sl], kn_ref[:, sl]], axis=0).astype(F32)
        k3 = k3 * lax.rsqrt(jnp.mean(k3 * k3, axis=-1, keepdims=True) + EPS) * kw
        k3 = k3.astype(BF16)
        v3 = jnp.concatenate([vp_ref[:, sl], vc_ref[:, sl], vn_ref[:, sl]], axis=0)
        qs, sinks = [], []
        for g in range(G):
            h = hk * G + g
            q = q_ref[:, h * hd:(h + 1) * hd].astype(F32)
            q = q * lax.rsqrt(jnp.mean(q * q, axis=-1, keepdims=True) + EPS) * qw * scale
            qs.append(q.astype(BF16))
            sinks.append(jnp.full((blk, 1), sink_ref[h], F32))
        q4 = jnp.concatenate(qs, axis=0)
        sink4 = jnp.concatenate(sinks, axis=0)
        bias4 = bias_ref[hk * G:(hk + 1) * G].reshape(G * blk, 3 * blk)
        logits = _dot_nt(q4, k3) + bias4
        logits = jnp.where(oob, NEG_INF, logits)
        m = jnp.maximum(jnp.max(logits, axis=-1, keepdims=True), sink4)
        p = jnp.exp(logits - m)
        denom = jnp.sum(p, axis=-1, keepdims=True) + jnp.exp(sink4 - m)
        out = _dot(p.astype(BF16), v3) / denom
        for g in range(G):
            h = hk * G + g
            o_ref[:, h * hd:(h + 1) * hd] = out[g * blk:(g + 1) * blk].astype(o_ref.dtype)


def _window_attention(proj, bias_tab, sink, q_norm_w, k_norm_w, B, S):
    T = proj.shape[0]
    blk = ATTN_BLOCK
    nb = S // blk
    kcol = ATTN_Q_W // ATTN_KV_W
    vcol = kcol + 1

    def kv_spec(col, off):
        return pl.BlockSpec((blk, ATTN_KV_W), lambda b, n, s: (b * nb + jnp.clip(n + off, 0, nb - 1), col))

    grid_spec = pltpu.PrefetchScalarGridSpec(
        num_scalar_prefetch=1,
        grid=(B, nb),
        in_specs=[
            pl.BlockSpec((blk, ATTN_Q_W), lambda b, n, s: (b * nb + n, 0)),
            kv_spec(kcol, -1), kv_spec(kcol, 0), kv_spec(kcol, 1),
            kv_spec(vcol, -1), kv_spec(vcol, 0), kv_spec(vcol, 1),
            pl.BlockSpec((ATTN_HEADS, blk, 3 * blk), lambda b, n, s: (0, 0, 0)),
            pl.BlockSpec((1, ATTN_HEAD_DIM), lambda b, n, s: (0, 0)),
            pl.BlockSpec((1, ATTN_HEAD_DIM), lambda b, n, s: (0, 0)),
        ],
        out_specs=pl.BlockSpec((blk, ATTN_Q_W), lambda b, n, s: (b * nb + n, 0)),
    )
    return pl.pallas_call(
        _attn_kernel,
        out_shape=jax.ShapeDtypeStruct((T, ATTN_Q_W), BF16),
        grid_spec=grid_spec,
        compiler_params=_cparams(("arbitrary", "arbitrary")),
        name="window_attn",
    )(sink.astype(F32), proj, proj, proj, proj, proj, proj, proj, bias_tab,
      q_norm_w.reshape(1, -1), k_norm_w.reshape(1, -1))


def _rope_tables(S):
    d = RET_QK_DIM
    inv = ROPE_BASE ** (-jnp.arange(0, d, 2, dtype=F32) / d)
    ang = jnp.arange(S, dtype=F32)[:, None] * inv[None, :]
    cos, sin = jnp.cos(ang), jnp.sin(ang)
    return jnp.concatenate([cos, cos], axis=-1), jnp.concatenate([-sin, sin], axis=-1)


def _ret_kernel(df_ref, db_ref, q_ref, k_ref, v_ref, g_ref, cos_ref, sin_ref, gw_ref, gb_ref, o_ref,
                qb, kb, q2, kv, sprev, *, S, C):
    h = pl.program_id(1)
    nc = S // C
    dk = RET_QK_DIM
    half = dk // 2

    rowf = lax.broadcasted_iota(I32, (C, 1), 0).astype(F32)
    lg_f = -jnp.exp(jnp.full((1, 1), df_ref[h], F32))
    lg_b = -jnp.exp(jnp.full((1, 1), db_ref[h], F32))
    qdec_f = jnp.exp((rowf + 1.0) * lg_f)
    kdec_f = jnp.exp((C - 1.0 - rowf) * lg_f)
    qdec_b = jnp.exp((C - rowf) * lg_b)
    kdec_b = jnp.exp(rowf * lg_b)
    cd_f = jnp.exp(C * lg_f)
    cd_b = jnp.exp(C * lg_b)

    def a_body(n, carry):
        rows = pl.ds(pl.multiple_of(n * C, C), C)
        co = cos_ref[rows, :]
        si = sin_ref[rows, :]
        q = q_ref[rows, :].astype(F32)
        k = k_ref[rows, :].astype(F32)
        qr = q * co + pltpu.roll(q, half, 1) * si
        kr = (k * co + pltpu.roll(k, half, 1) * si) * (dk ** -0.5)
        qb[rows, :] = qr.astype(BF16)
        kb[rows, :] = kr.astype(BF16)
        q2[rows, :] = jnp.concatenate([qr * qdec_f, qr * qdec_b], axis=1).astype(BF16)
        k2 = jnp.concatenate([kr * kdec_f, kr * kdec_b], axis=1).astype(BF16)
        kv[n] = _dot_tn(k2, v_ref[rows, :])
        return carry

    lax.fori_loop(0, nc, a_body, 0, unroll=4)

    def scan_f(n, state):
        sprev[n, 0:dk, :] = state.astype(BF16)
        return state * cd_f + kv[n, 0:dk, :]

    lax.fori_loop(0, nc, scan_f, jnp.zeros((dk, RET_V_DIM), F32))

    def scan_b(t, state):
        n = nc - 1 - t
        sprev[n, dk:2 * dk, :] = state.astype(BF16)
        return state * cd_b + kv[n, dk:2 * dk, :]

    lax.fori_loop(0, nc, scan_b, jnp.zeros((dk, RET_V_DIM), F32))

    ri = lax.broadcasted_iota(I32, (C, C), 0)
    ci = lax.broadcasted_iota(I32, (C, C), 1)
    d = (ri - ci).astype(F32)
    dec = jnp.where(ri >= ci, jnp.exp(jnp.maximum(d, 0.0) * lg_f), jnp.exp(jnp.maximum(-d, 0.0) * lg_b))
    gw = gw_ref[...]
    gb = gb_ref[...]

    def c_body(n, carry):
        rows = pl.ds(pl.multiple_of(n * C, C), C)
        scores = _dot_nt(qb[rows, :], kb[rows, :]) * dec
        y = _dot(scores.astype(BF16), v_ref[rows, :]) + _dot(q2[rows, :], sprev[n])
        mu = jnp.mean(y, axis=-1, keepdims=True)
        yc = y - mu
        var = jnp.mean(yc * yc, axis=-1, keepdims=True)
        yn = yc * lax.rsqrt(var + EPS) * gw + gb
        g = g_ref[rows, :].astype(F32)
        o_ref[rows, :] = (g * _sigmoid(g) * yn).astype(o_ref.dtype)
        return carry

    lax.fori_loop(0, nc, c_body, 0, unroll=4)


def _retention(proj, decay_fwd, decay_bwd, gn_w, gn_b, B, S):
    T = proj.shape[0]
    dk, dv = RET_QK_DIM, RET_V_DIM
    q_off = (ATTN_Q_W + 2 * ATTN_KV_W) // dk
    k_off = q_off + RET_QK_W // dk
    v_off = (ATTN_Q_W + 2 * ATTN_KV_W + 2 * RET_QK_W) // dv
    g_off = v_off + RET_V_W // dv
    cos, sin = _rope_tables(S)
    C = min(RET_TILE, S)
    grid_spec = pltpu.PrefetchScalarGridSpec(
        num_scalar_prefetch=2,
        grid=(B, RET_HEADS),
        in_specs=[
            pl.BlockSpec((S, dk), lambda b, h, *_: (b, q_off + h)),
            pl.BlockSpec((S, dk), lambda b, h, *_: (b, k_off + h)),
            pl.BlockSpec((S, dv), lambda b, h, *_: (b, v_off + h)),
            pl.BlockSpec((S, dv), lambda b, h, *_: (b, g_off + h)),
            pl.BlockSpec((S, dk), lambda b, h, *_: (0, 0)),
            pl.BlockSpec((S, dk), lambda b, h, *_: (0, 0)),
            pl.BlockSpec((1, dv), lambda b, h, *_: (0, h)),
            pl.BlockSpec((1, dv), lambda b, h, *_: (0, h)),
        ],
        out_specs=pl.BlockSpec((S, dv), lambda b, h, *_: (b, h)),
        scratch_shapes=[
            pltpu.VMEM((S, dk), BF16),
            pltpu.VMEM((S, dk), BF16),
            pltpu.VMEM((S, 2 * dk), BF16),
            pltpu.VMEM((S // C, 2 * dk, dv), F32),
            pltpu.VMEM((S // C, 2 * dk, dv), BF16),
        ],
    )
    return pl.pallas_call(
        functools.partial(_ret_kernel, S=S, C=C),
        out_shape=jax.ShapeDtypeStruct((T, RET_V_W), BF16),
        grid_spec=grid_spec,
        compiler_params=_cparams(("arbitrary", "arbitrary")),
        name="retention",
    )(decay_fwd.astype(F32), decay_bwd.astype(F32), proj, proj, proj, proj, cos, sin,
      gn_w.reshape(1, -1), gn_b.reshape(1, -1))


def _merge_kernel(a_ref, r_ref, wa_ref, wr_ref, ga_ref, gr_ref, o_ref, wab, wrb):
    _cast_weight_once(wa_ref, wab)
    _cast_weight_once(wr_ref, wrb)
    ua = _dot(a_ref[...], wab[...])
    ur = _dot(r_ref[...], wrb[...])
    out = _sigmoid(ga_ref[...].astype(F32)) * ua + _sigmoid(gr_ref[...].astype(F32)) * ur
    o_ref[...] = out.astype(o_ref.dtype)


def _merge_up(attn, ret, wa, wr, proj, D):
    T = attn.shape[0]
    tm = min(1024, T)
    tn = min(512, D)
    ga_off = (ATTN_Q_W + 2 * ATTN_KV_W + 2 * RET_QK_W + 2 * RET_V_W) // tn
    gr_off = ga_off + D // tn
    return pl.pallas_call(
        _merge_kernel,
        out_shape=jax.ShapeDtypeStruct((T, D), BF16),
        grid=(D // tn, T // tm),
        in_specs=[
            pl.BlockSpec((tm, ATTN_Q_W), lambda j, i: (i, 0)),
            pl.BlockSpec((tm, RET_V_W), lambda j, i: (i, 0)),
            pl.BlockSpec((ATTN_Q_W, tn), lambda j, i: (0, j)),
            pl.BlockSpec((RET_V_W, tn), lambda j, i: (0, j)),
            pl.BlockSpec((tm, tn), lambda j, i: (i, ga_off + j)),
            pl.BlockSpec((tm, tn), lambda j, i: (i, gr_off + j)),
        ],
        out_specs=pl.BlockSpec((tm, tn), lambda j, i: (i, j)),
        scratch_shapes=[pltpu.VMEM((ATTN_Q_W, tn), BF16), pltpu.VMEM((RET_V_W, tn), BF16)],
        compiler_params=_cparams(("arbitrary", "arbitrary")),
        name="merge_up",
    )(attn, ret, wa, wr, proj, proj)


def _out_proj_kernel(m_ref, w_ref, x_ref, mod_ref, o_ref, wb, *, gate_idx):
    _cast_weight_once(w_ref, wb)
    y = _dot(m_ref[...], wb[...])
    o_ref[...] = x_ref[...] + mod_ref[gate_idx:gate_idx + 1, :] * y


def _out_proj(merged, w, x2, mod, S, gate_idx):
    T, D = x2.shape
    tm = min(1024, S)
    tn = min(512, D)
    return pl.pallas_call(
        functools.partial(_out_proj_kernel, gate_idx=gate_idx),
        out_shape=jax.ShapeDtypeStruct((T, D), F32),
        grid=(D // tn, T // tm),
        in_specs=[
            pl.BlockSpec((tm, D), lambda j, i: (i, 0)),
            pl.BlockSpec((D, tn), lambda j, i: (0, j)),
            pl.BlockSpec((tm, tn), lambda j, i: (i, j)),
            pl.BlockSpec((None, N_ADA, tn), lambda j, i: ((i * tm) // S, 0, j)),
        ],
        out_specs=pl.BlockSpec((tm, tn), lambda j, i: (i, j)),
        scratch_shapes=[pltpu.VMEM((D, tn), BF16)],
        compiler_params=_cparams(("arbitrary", "arbitrary")),
        name="out_proj",
    )(merged, w, x2, mod)


def _pack_bf16_pairs(v):
    n = v.shape[1] // 2
    bits = lax.bitcast_convert_type(v.astype(BF16).astype(F32), U32)
    return jnp.bitwise_or(jnp.bitwise_and(bits[:, n:], jnp.uint32(0xFFFF0000)),
                          jnp.right_shift(bits[:, :n], jnp.uint32(16)))


def _unpack_bf16_pairs(w):
    lo = lax.bitcast_convert_type(jnp.left_shift(w, jnp.uint32(16)), F32)
    hi = lax.bitcast_convert_type(jnp.bitwise_and(w, jnp.uint32(0xFFFF0000)), F32)
    return jnp.concatenate([lo, hi], axis=1)


def _norm_router_kernel(x_ref, nw_ref, mod_ref, rwt_ref, rb_ref, hp_ref, e_ref, w_ref, r_ref, cnt_ref, carry,
                        *, shift_idx, scale_idx):
    i = pl.program_id(0)

    @pl.when(i == 0)
    def _():
        carry[...] = jnp.zeros_like(carry)

    h = _norm_mod_value(x_ref[...], nw_ref[...], mod_ref[shift_idx:shift_idx + 1, :],
                        mod_ref[scale_idx:scale_idx + 1, :])
    hp_ref[...] = _pack_bf16_pairs(h)
    hb = h.astype(BF16)
    tm = hb.shape[0]
    E = rwt_ref.shape[0]
    logits = _dot_nt(rwt_ref[...], hb) + rb_ref[...]
    iota_e = lax.broadcasted_iota(I32, (E, tm), 0)
    vals, idxs = [], []
    work = logits
    sel = jnp.zeros((E, tm), F32)
    for _k in range(TOP_K):
        m = jnp.max(work, axis=0, keepdims=True)
        idx = jnp.min(jnp.where(work == m, iota_e, E), axis=0, keepdims=True)
        hit = iota_e == idx
        vals.append(m)
        idxs.append(idx)
        work = jnp.where(hit, NEG_BIG, work)
        sel = sel + hit.astype(F32)
    ex = [jnp.exp(v - vals[0]) for v in vals]
    tot = ex[0]
    for v in ex[1:]:
        tot = tot + v
    ri = lax.broadcasted_iota(I32, (tm, tm), 0)
    ci = lax.broadcasted_iota(I32, (tm, tm), 1)
    upper = (ri < ci).astype(BF16)
    before = _dot(sel.astype(BF16), upper) + carry[:, 0:1]
    for k in range(TOP_K):
        e_ref[k:k + 1, :] = idxs[k]
        w_ref[k:k + 1, :] = ex[k] / tot
        r_ref[k:k + 1, :] = jnp.sum(jnp.where(iota_e == idxs[k], before, 0.0), axis=0, keepdims=True).astype(I32)
    carry[...] = carry[...] + jnp.sum(sel, axis=1, keepdims=True)
    cnt_ref[...] = carry[...]


def _norm_router(x1, nw, mod, router_w, router_b, S, shift_idx, scale_idx):
    T, D = x1.shape
    E = router_w.shape[1]
    tm = min(256, S)
    outs = pl.pallas_call(
        functools.partial(_norm_router_kernel, shift_idx=shift_idx, scale_idx=scale_idx),
        out_shape=(
            jax.ShapeDtypeStruct((T, D // 2), U32),
            jax.ShapeDtypeStruct((TOP_K, T), I32),
            jax.ShapeDtypeStruct((TOP_K, T), F32),
            jax.ShapeDtypeStruct((TOP_K, T), I32),
            jax.ShapeDtypeStruct((E, 128), F32),
        ),
        grid=(T // tm,),
        in_specs=[
            pl.BlockSpec((tm, D), lambda i: (i, 0)),
            pl.BlockSpec((1, D), lambda i: (0, 0)),
            pl.BlockSpec((None, N_ADA, D), lambda i: ((i * tm) // S, 0, 0)),
            pl.BlockSpec((E, D), lambda i: (0, 0)),
            pl.BlockSpec((E, 1), lambda i: (0, 0)),
        ],
        out_specs=(
            pl.BlockSpec((tm, D // 2), lambda i: (i, 0)),
            pl.BlockSpec((TOP_K, tm), lambda i: (0, i)),
            pl.BlockSpec((TOP_K, tm), lambda i: (0, i)),
            pl.BlockSpec((TOP_K, tm), lambda i: (0, i)),
            pl.BlockSpec((E, 128), lambda i: (0, 0)),
        ),
        scratch_shapes=[pltpu.VMEM((E, 128), F32)],
        compiler_params=_cparams(("arbitrary",)),
        name="norm_router",
    )(x1, nw.reshape(1, D), mod, router_w.T.astype(BF16), router_b.reshape(E, 1).astype(F32))
    return outs


def _dispatch_kernel(pos_ref, pad_ref, hp_ref, xs_ref, zeros, sem, zsem, *, tm, T, E, n_tiles):
    base = pl.program_id(0) * tm

    @pl.when(pl.program_id(0) == 0)
    def _():
        zeros[...] = jnp.zeros_like(zeros)

        def pad_copy(row):
            return pltpu.make_async_copy(zeros.at[pl.ds(0, 1)], xs_ref.at[pl.ds(row, 1)], zsem)

        def expert_body(e, carry):
            start, count = pad_ref[e], pad_ref[E + e]

            def s_body(r, c):
                pad_copy(start + r).start()
                return c

            def w_body(r, c):
                pad_copy(start + r).wait()
                return c

            lax.fori_loop(0, count, s_body, 0)
            lax.fori_loop(0, count, w_body, 0)
            return carry

        lax.fori_loop(0, E, expert_body, 0)

        def tail_body(t, carry):
            cp = pltpu.make_async_copy(zeros, xs_ref.at[pl.ds(pl.multiple_of(t * MOE_TILE, MOE_TILE), MOE_TILE)], zsem)
            cp.start()
            cp.wait()
            return carry

        lax.fori_loop(pad_ref[2 * E], n_tiles, tail_body, 0)

    def row_copy(t, k):
        dst = pos_ref[k * T + base + t]
        return pltpu.make_async_copy(hp_ref.at[pl.ds(t, 1)], xs_ref.at[pl.ds(dst, 1)], sem)

    def start_body(t, carry):
        for k in range(TOP_K):
            row_copy(t, k).start()
        return carry

    lax.fori_loop(0, tm, start_body, 0)

    def wait_body(t, carry):
        for k in range(TOP_K):
            row_copy(t, k).wait()
        return carry

    lax.fori_loop(0, tm, wait_body, 0)


def _dispatch(pos_flat, pad_info, hp, n_tiles, E):
    T, W = hp.shape
    tm = min(256, T)
    grid_spec = pltpu.PrefetchScalarGridSpec(
        num_scalar_prefetch=2,
        grid=(T // tm,),
        in_specs=[pl.BlockSpec((tm, W), lambda i, p, q: (i, 0))],
        out_specs=pl.BlockSpec(memory_space=pl.ANY),
        scratch_shapes=[pltpu.VMEM((MOE_TILE, W), U32), pltpu.SemaphoreType.DMA, pltpu.SemaphoreType.DMA],
    )
    return pl.pallas_call(
        functools.partial(_dispatch_kernel, tm=tm, T=T, E=E, n_tiles=n_tiles),
        out_shape=jax.ShapeDtypeStruct((n_tiles * MOE_TILE, W), U32),
        grid_spec=grid_spec,
        compiler_params=_cparams(("arbitrary",)),
        name="dispatch",
    )(pos_flat, pad_info, hp)


def _expert_changed(te_ref, i):
    return jnp.logical_or(i == 0, te_ref[i] != te_ref[jnp.maximum(i - 1, 0)])


def _expert_up_kernel(te_ref, nu_ref, xs_ref, wg_ref, wu_ref, bg_ref, bu_ref, o_ref, wgb, wub):
    i = pl.program_id(1)

    @pl.when(_expert_changed(te_ref, i))
    def _():
        wgb[...] = wg_ref[...].astype(BF16)
        wub[...] = wu_ref[...].astype(BF16)

    @pl.when(i < nu_ref[0])
    def _():
        x = _unpack_bf16_pairs(xs_ref[...]).astype(BF16)
        gate = jnp.minimum(_dot(x, wgb[...]) + bg_ref[...], SWIGLU_LIMIT)
        up = jnp.clip(_dot(x, wub[...]) + bu_ref[...], -SWIGLU_LIMIT, SWIGLU_LIMIT)
        act = gate * _sigmoid(SWIGLU_ALPHA * gate) * (up + 1.0)
        o_ref[...] = act.astype(o_ref.dtype)

    @pl.when(i >= nu_ref[0])
    def _():
        o_ref[...] = jnp.zeros_like(o_ref)


def _expert_down_kernel(te_ref, nu_ref, a_ref, w2_ref, b2_ref, o_ref, w2b):
    i = pl.program_id(0)

    @pl.when(_expert_changed(te_ref, i))
    def _():
        w2b[...] = w2_ref[...].astype(BF16)

    @pl.when(i < nu_ref[0])
    def _():
        y = _dot(a_ref[...], w2b[...]) + b2_ref[...]
        o_ref[...] = _pack_bf16_pairs(y)

    @pl.when(i >= nu_ref[0])
    def _():
        o_ref[...] = jnp.zeros_like(o_ref)


def _expert_ffn(xs, tile_e, n_used, w1, b1, w2, b2):
    R, W = xs.shape
    E, D, F2 = w1.shape
    F = F2 // 2
    tm = MOE_TILE
    nt = R // tm

    tn = min(512, F)
    nf = F // tn
    b1r = b1.reshape(E, 1, F2)

    act = pl.pallas_call(
        _expert_up_kernel,
        out_shape=jax.ShapeDtypeStruct((R, F), BF16),
        grid_spec=pltpu.PrefetchScalarGridSpec(
            num_scalar_prefetch=2,
            grid=(nf, nt),
            in_specs=[
                pl.BlockSpec((tm, W), lambda n, i, te, nu: (jnp.minimum(i, nu[0] - 1), 0)),
                pl.BlockSpec((None, D, tn), lambda n, i, te, nu: (te[i], 0, n)),
                pl.BlockSpec((None, D, tn), lambda n, i, te, nu: (te[i], 0, nf + n)),
                pl.BlockSpec((None, 1, tn), lambda n, i, te, nu: (te[i], 0, n)),
                pl.BlockSpec((None, 1, tn), lambda n, i, te, nu: (te[i], 0, nf + n)),
            ],
            out_specs=pl.BlockSpec((tm, tn), lambda n, i, te, nu: (i, n)),
            scratch_shapes=[pltpu.VMEM((D, tn), BF16), pltpu.VMEM((D, tn), BF16)],
        ),
        compiler_params=_cparams(("arbitrary", "arbitrary")),
        name="expert_up",
    )(tile_e, n_used, xs, w1, w1, b1r, b1r)
    y = pl.pallas_call(
        _expert_down_kernel,
        out_shape=jax.ShapeDtypeStruct((R, D // 2), U32),
        grid_spec=pltpu.PrefetchScalarGridSpec(
            num_scalar_prefetch=2,
            grid=(nt,),
            in_specs=[
                pl.BlockSpec((tm, F), lambda i, te, nu: (jnp.minimum(i, nu[0] - 1), 0)),
                pl.BlockSpec((None, F, D), lambda i, te, nu: (te[i], 0, 0)),
                pl.BlockSpec((None, 1, D), lambda i, te, nu: (te[i], 0, 0)),
            ],
            out_specs=pl.BlockSpec((tm, D // 2), lambda i, te, nu: (i, 0)),
            scratch_shapes=[pltpu.VMEM((F, D), BF16)],
        ),
        compiler_params=_cparams(("arbitrary",)),
        name="expert_down",
    )(tile_e, n_used, act, w2, b2.reshape(E, 1, D))
    return y


def _combine_kernel(pos_ref, x_ref, w_ref, mod_ref, y_ref, o_ref, buf, sem, *, tm, T, gate_idx):
    base = pl.program_id(0) * tm

    def row_copy(t, k):
        src = pos_ref[k * T + base + t]
        return pltpu.make_async_copy(y_ref.at[pl.ds(src, 1)], buf.at[k, pl.ds(t, 1)], sem)

    def start_body(t, carry):
        for k in range(TOP_K):
            row_copy(t, k).start()
        return carry

    lax.fori_loop(0, tm, start_body, 0)

    def wait_body(t, carry):
        for k in range(TOP_K):
            row_copy(t, k).wait()
        return carry

    lax.fori_loop(0, tm, wait_body, 0)

    w = w_ref[...]
    acc = w[:, 0:1] * _unpack_bf16_pairs(buf[0])
    for k in range(1, TOP_K):
        acc = acc + w[:, k:k + 1] * _unpack_bf16_pairs(buf[k])
    o_ref[...] = x_ref[...] + mod_ref[gate_idx:gate_idx + 1, :] * acc


def _combine(pos_flat, x1, w_tk, mod, y, S, gate_idx):
    T, D = x1.shape
    tm = min(128, S)
    grid_spec = pltpu.PrefetchScalarGridSpec(
        num_scalar_prefetch=1,
        grid=(T // tm,),
        in_specs=[
            pl.BlockSpec((tm, D), lambda i, p: (i, 0)),
            pl.BlockSpec((tm, TOP_K), lambda i, p: (i, 0)),
            pl.BlockSpec((None, N_ADA, D), lambda i, p: ((i * tm) // S, 0, 0)),
            pl.BlockSpec(memory_space=pl.ANY),
        ],
        out_specs=pl.BlockSpec((tm, D), lambda i, p: (i, 0)),
        scratch_shapes=[pltpu.VMEM((TOP_K, tm, D // 2), U32), pltpu.SemaphoreType.DMA],
    )
    return pl.pallas_call(
        functools.partial(_combine_kernel, tm=tm, T=T, gate_idx=gate_idx),
        out_shape=jax.ShapeDtypeStruct((T, D), F32),
        grid_spec=grid_spec,
        compiler_params=_cparams(("arbitrary",)),
        name="combine",
    )(pos_flat, x1, w_tk, mod, y)


def _layer(x2, mod, B, S, rel_bias, norm_mix_w, w_in, q_norm_w, k_norm_w, attn_sink, ret_decay_fwd, ret_decay_bwd,
           ret_gn_w, ret_gn_b, w_up_attn, w_up_ret, w_out, norm_ffn_w, router_w, router_b,
           expert_w1, expert_b1, expert_w2, expert_b2):
    T, D = x2.shape
    E = router_w.shape[1]
    h = _norm_mod(x2, norm_mix_w, mod, S, 0, 1)
    proj = _in_proj(h, w_in)
    attn = _window_attention(proj, _attn_bias_table(rel_bias), attn_sink, q_norm_w, k_norm_w, B, S)
    ret = _retention(proj, ret_decay_fwd, ret_decay_bwd, ret_gn_w, ret_gn_b, B, S)
    merged = _merge_up(attn, ret, w_up_attn, w_up_ret, proj, D)
    x1 = _out_proj(merged, w_out, x2, mod, S, 2)

    hp, top_e, top_w, rank, cnt = _norm_router(x1, norm_ffn_w, mod, router_w, router_b, S, 3, 4)
    tm = MOE_TILE
    counts = cnt[:, 0].astype(I32)
    tiles_per_e = (counts + tm - 1) // tm
    tile_end = jnp.cumsum(tiles_per_e)
    group_start = (tile_end - tiles_per_e) * tm
    n_tiles = (T * TOP_K) // tm + E
    n_used = tile_end[-1]
    tile_ids = jnp.minimum(jnp.arange(n_tiles, dtype=I32), n_used - 1)
    tile_e = jnp.minimum(jnp.sum((tile_ids[:, None] >= tile_end[None, :]).astype(I32), axis=1), E - 1)
    e_ids = jnp.arange(E, dtype=I32)
    start_of = jnp.sum(jnp.where(top_e[:, :, None] == e_ids, group_start, 0), axis=-1)
    pos_flat = (start_of + rank).astype(I32).reshape(-1)
    pad_info = jnp.concatenate([group_start + counts, tiles_per_e * tm - counts, n_used.reshape(1)]).astype(I32)

    xs = _dispatch(pos_flat, pad_info, hp, n_tiles, E)
    y = _expert_ffn(xs, tile_e, n_used.reshape(1).astype(I32), expert_w1, expert_b1, expert_w2, expert_b2)
    return _combine(pos_flat, x1, top_w.T, mod, y, S, 5)


def kernel(x, c, rel_bias, ada_w, ada_b, norm_mix_w, w_in, q_norm_w, k_norm_w, attn_sink, ret_decay_fwd,
           ret_decay_bwd, ret_gn_w, ret_gn_b, w_up_attn, w_up_ret, w_out, norm_ffn_w, router_w, router_b,
           expert_w1, expert_b1, expert_w2, expert_b2):
    B, S, D = x.shape
    x2 = x.reshape(B * S, D)
    for l in range(ada_w.shape[0]):
        mod = _ada_mod(c, ada_w[l], ada_b[l])
        x2 = _layer(x2, mod, B, S, rel_bias, norm_mix_w[l], w_in[l], q_norm_w[l], k_norm_w[l], attn_sink[l],
                    ret_decay_fwd[l], ret_decay_bwd[l], ret_gn_w[l], ret_gn_b[l], w_up_attn[l], w_up_ret[l],
                    w_out[l], norm_ffn_w[l], router_w[l], router_b[l], expert_w1[l], expert_b1[l],
                    expert_w2[l], expert_b2[l])
    return x2.reshape(B, S, D)
```

```python
import functools
import math

import jax
import jax.numpy as jnp
from jax import lax
from jax.experimental import pallas as pl
from jax.experimental.pallas import tpu as pltpu

F32 = jnp.float32
BF16 = jnp.bfloat16
U32 = jnp.uint32
I32 = jnp.int32

ATTN_HEADS = 16
ATTN_KV_HEADS = 4
ATTN_HEAD_DIM = 128
WINDOW = 128
ATTN_BLOCK = 128
N_BUCKETS = 32
MAX_DISTANCE = 128
RET_HEADS = 8
RET_QK_DIM = 128
RET_V_DIM = 256
RET_CHUNK = 128
ROPE_BASE = 10000.0
TOP_K = 4
SWIGLU_LIMIT = 7.0
SWIGLU_ALPHA = 1.702
N_ADA = 6
EPS = 1e-6
NEG_INF = -1e30
NEG_BIG = -3.0e38

ATTN_Q_W = ATTN_HEADS * ATTN_HEAD_DIM
ATTN_KV_W = ATTN_KV_HEADS * ATTN_HEAD_DIM
RET_QK_W = RET_HEADS * RET_QK_DIM
RET_V_W = RET_HEADS * RET_V_DIM

VMEM_LIMIT_BYTES = 56 * 1024 * 1024
MOE_TILE = 256
RET_TILE = 256


def _cparams(sem):
    return pltpu.CompilerParams(dimension_semantics=sem, vmem_limit_bytes=VMEM_LIMIT_BYTES)


def _dot(a, b):
    return jnp.dot(a, b, preferred_element_type=F32)


def _dot_nt(a, b):
    return lax.dot_general(a, b, (((1,), (1,)), ((), ())), preferred_element_type=F32)


def _dot_tn(a, b):
    return lax.dot_general(a, b, (((0,), (0,)), ((), ())), preferred_element_type=F32)


def _sigmoid(x):
    return 1.0 / (1.0 + jnp.exp(-x))


def _ada_kernel(c_ref, w_ref, b_ref, o_ref):
    c = c_ref[...]
    cs = (c * _sigmoid(c)).astype(BF16)
    o_ref[...] = _dot(cs, w_ref[...].astype(BF16)) + b_ref[...]


def _ada_mod(c, ada_w, ada_b):
    B, D = c.shape
    N = ada_w.shape[1]
    rows = 8
    cp = jnp.zeros((rows, D), F32).at[:B].set(c)
    tn = min(512, N)
    out = pl.pallas_call(
        _ada_kernel,
        out_shape=jax.ShapeDtypeStruct((rows, N), F32),
        grid=(N // tn,),
        in_specs=[
            pl.BlockSpec((rows, D), lambda j: (0, 0)),
            pl.BlockSpec((D, tn), lambda j: (0, j)),
            pl.BlockSpec((1, tn), lambda j: (0, j)),
        ],
        out_specs=pl.BlockSpec((rows, tn), lambda j: (0, j)),
        compiler_params=_cparams(("arbitrary",)),
        name="ada_mod",
    )(cp, ada_w, ada_b.reshape(1, N))
    return out[:B].reshape(B, N_ADA, D)


def _norm_mod_value(x, nw, shift, scale):
    ms = jnp.mean(x * x, axis=-1, keepdims=True)
    h = x * lax.rsqrt(ms + EPS) * nw
    return h * (1.0 + scale) + shift


def _norm_mod_kernel(x_ref, nw_ref, mod_ref, o_ref, *, shift_idx, scale_idx):
    h = _norm_mod_value(x_ref[...], nw_ref[...], mod_ref[shift_idx:shift_idx + 1, :],
                        mod_ref[scale_idx:scale_idx + 1, :])
    o_ref[...] = h.astype(BF16)


def _norm_mod(x2, nw, mod, S, shift_idx, scale_idx):
    T, D = x2.shape
    tm = min(256, S)
    return pl.pallas_call(
        functools.partial(_norm_mod_kernel, shift_idx=shift_idx, scale_idx=scale_idx),
        out_shape=jax.ShapeDtypeStruct((T, D), BF16),
        grid=(T // tm,),
        in_specs=[
            pl.BlockSpec((tm, D), lambda i: (i, 0)),
            pl.BlockSpec((1, D), lambda i: (0, 0)),
            pl.BlockSpec((None, N_ADA, D), lambda i: ((i * tm) // S, 0, 0)),
        ],
        out_specs=pl.BlockSpec((tm, D), lambda i: (i, 0)),
        compiler_params=_cparams(("arbitrary",)),
        name="norm_mod",
    )(x2, nw.reshape(1, D), mod)


def _cast_weight_once(w_ref, wb_ref):
    @pl.when(pl.program_id(1) == 0)
    def _():
        wb_ref[...] = w_ref[...].astype(BF16)


def _in_proj_kernel(a_ref, w_ref, cw_ref, bd_ref, o_ref, wb, *, n_norm_tiles):
    j = pl.program_id(0)
    _cast_weight_once(w_ref, wb)

    @pl.when(j < n_norm_tiles)
    def _():
        acc = _dot(a_ref[...], wb[...])
        ms = _dot((acc * acc).astype(BF16), bd_ref[...]) * (1.0 / ATTN_HEAD_DIM)
        o_ref[...] = (acc * lax.rsqrt(ms + EPS) * cw_ref[...]).astype(o_ref.dtype)

    @pl.when(j >= n_norm_tiles)
    def _():
        o_ref[...] = _dot(a_ref[...], wb[...]).astype(o_ref.dtype)


def _in_proj(h, w, q_norm_w, k_norm_w):
    T, K = h.shape
    N = w.shape[1]
    tm = min(1024, T)
    tn = 512
    hd = ATTN_HEAD_DIM
    n_norm = ATTN_Q_W + ATTN_KV_W
    assert n_norm % tn == 0 and tn % hd == 0
    col_w = jnp.concatenate([jnp.tile(q_norm_w.astype(F32) * (hd ** -0.5), ATTN_HEADS),
                             jnp.tile(k_norm_w.astype(F32), ATTN_KV_HEADS)]).reshape(1, n_norm)
    lane_head = jnp.arange(tn, dtype=I32) // hd
    block_ones = (lane_head[:, None] == lane_head[None, :]).astype(BF16)
    last = n_norm // tn - 1
    return pl.pallas_call(
        functools.partial(_in_proj_kernel, n_norm_tiles=n_norm // tn),
        out_shape=jax.ShapeDtypeStruct((T, N), BF16),
        grid=(N // tn, T // tm),
        in_specs=[
            pl.BlockSpec((tm, K), lambda j, i: (i, 0)),
            pl.BlockSpec((K, tn), lambda j, i: (0, j)),
            pl.BlockSpec((1, tn), lambda j, i: (0, jnp.minimum(j, last))),
            pl.BlockSpec((tn, tn), lambda j, i: (0, 0)),
        ],
        out_specs=pl.BlockSpec((tm, tn), lambda j, i: (i, j)),
        scratch_shapes=[pltpu.VMEM((K, tn), BF16)],
        compiler_params=_cparams(("arbitrary", "arbitrary")),
        name="in_proj",
    )(h, w, col_w, block_ones)


def _t5_bucket(rel):
    nb = N_BUCKETS // 2
    max_exact = nb // 2
    base = jnp.where(rel > 0, nb, 0)
    n = jnp.abs(rel)
    nf = jnp.maximum(n, 1).astype(F32)
    large = max_exact + (jnp.log(nf / max_exact) / math.log(MAX_DISTANCE / max_exact) * (nb - max_exact)).astype(I32)
    large = jnp.minimum(large, nb - 1)
    return base + jnp.where(n < max_exact, n, large)


def _attn_bias_table(rel_bias):
    blk = ATTN_BLOCK
    qi = jnp.arange(blk, dtype=I32)[:, None]
    kj = jnp.arange(3 * blk, dtype=I32)[None, :]
    rel = kj - blk - qi
    bucket = _t5_bucket(rel)
    table = rel_bias.astype(F32).T
    bias = jnp.zeros((table.shape[0],) + bucket.shape, F32)
    for b in range(N_BUCKETS):
        bias = jnp.where(bucket[None] == b, table[:, b][:, None, None], bias)
    bias = jnp.where((jnp.abs(rel) <= WINDOW)[None], bias, NEG_INF)
    G = ATTN_HEADS // ATTN_KV_HEADS
    bias = bias.reshape(ATTN_KV_HEADS, G, blk, 3 * blk)
    return jnp.transpose(bias, (0, 3, 1, 2)).reshape(ATTN_KV_HEADS, 3 * blk, G * blk)


def _attn_kernel(sink_ref, q_ref, kp_ref, kc_ref, kn_ref, vp_ref, vc_ref, vn_ref, bias_ref, o_ref):
    n = pl.program_id(1)
    nb = pl.num_programs(1)
    blk, hd = ATTN_BLOCK, ATTN_HEAD_DIM
    G = ATTN_HEADS // ATTN_KV_HEADS
    row = lax.broadcasted_iota(I32, (3 * blk, 1), 0)
    oob = jnp.logical_or(jnp.logical_and(row < blk, n == 0), jnp.logical_and(row >= 2 * blk, n == nb - 1))
    for hk in range(ATTN_KV_HEADS):
        sl = slice(hk * hd, (hk + 1) * hd)
        k3 = jnp.concatenate([kp_ref[:, sl], kc_ref[:, sl], kn_ref[:, sl]], axis=0)
        v3 = jnp.concatenate([vp_ref[:, sl], vc_ref[:, sl], vn_ref[:, sl]], axis=0)
        q4 = jnp.concatenate([q_ref[:, (hk * G + g) * hd:(hk * G + g + 1) * hd] for g in range(G)],
                             axis=0)
        sink = jnp.concatenate([jnp.full((1, blk), sink_ref[hk * G + g], F32) for g in range(G)], axis=1)
        logits = _dot_nt(k3, q4) + bias_ref[hk]
        logits = jnp.where(oob, NEG_INF, logits)
        m = jnp.maximum(jnp.max(logits, axis=0, keepdims=True), sink)
        p = jnp.exp(logits - m)
        denom = jnp.sum(p, axis=0, keepdims=True) + jnp.exp(sink - m)
        out_t = _dot_tn(v3, p.astype(BF16)) * (1.0 / denom)
        out = out_t.T
        for g in range(G):
            h = hk * G + g
            o_ref[:, h * hd:(h + 1) * hd] = out[g * blk:(g + 1) * blk].astype(o_ref.dtype)


def _window_attention(proj, bias_tab, sink, B, S):
    T = proj.shape[0]
    blk = ATTN_BLOCK
    nb = S // blk
    kcol = ATTN_Q_W // ATTN_KV_W
    vcol = kcol + 1

    def kv_spec(col, off):
        return pl.BlockSpec((blk, ATTN_KV_W), lambda b, n, s: (b * nb + jnp.clip(n + off, 0, nb - 1), col))

    grid_spec = pltpu.PrefetchScalarGridSpec(
        num_scalar_prefetch=1,
        grid=(B, nb),
        in_specs=[
            pl.BlockSpec((blk, ATTN_Q_W), lambda b, n, s: (b * nb + n, 0)),
            kv_spec(kcol, -1), kv_spec(kcol, 0), kv_spec(kcol, 1),
            kv_spec(vcol, -1), kv_spec(vcol, 0), kv_spec(vcol, 1),
            pl.BlockSpec((ATTN_KV_HEADS, 3 * blk, (ATTN_HEADS // ATTN_KV_HEADS) * blk), lambda b, n, s: (0, 0, 0)),
        ],
        out_specs=pl.BlockSpec((blk, ATTN_Q_W), lambda b, n, s: (b * nb + n, 0)),
    )
    return pl.pallas_call(
        _attn_kernel,
        out_shape=jax.ShapeDtypeStruct((T, ATTN_Q_W), BF16),
        grid_spec=grid_spec,
        compiler_params=_cparams(("arbitrary", "arbitrary")),
        name="window_attn",
    )(sink.astype(F32), proj, proj, proj, proj, proj, proj, proj, bias_tab)


def _rope_tables(S):
    d = RET_QK_DIM
    inv = ROPE_BASE ** (-jnp.arange(0, d, 2, dtype=F32) / d)
    ang = jnp.arange(S, dtype=F32)[:, None] * inv[None, :]
    cos, sin = jnp.cos(ang), jnp.sin(ang)
    return jnp.concatenate([cos, cos], axis=-1), jnp.concatenate([-sin, sin], axis=-1)


def _ret_kernel(df_ref, db_ref, q_ref, k_ref, v_ref, g_ref, cos_ref, sin_ref, gw_ref, gb_ref, o_ref,
                qb, kb, q2, kv, sprev, *, S, C):
    h = pl.program_id(1)
    nc = S // C
    dk = RET_QK_DIM
    half = dk // 2

    rowf = lax.broadcasted_iota(I32, (C, 1), 0).astype(F32)
    lg_f = -jnp.exp(jnp.full((1, 1), df_ref[h], F32))
    lg_b = -jnp.exp(jnp.full((1, 1), db_ref[h], F32))
    qdec_f = jnp.exp((rowf + 1.0) * lg_f)
    kdec_f = jnp.exp((C - 1.0 - rowf) * lg_f)
    qdec_b = jnp.exp((C - rowf) * lg_b)
    kdec_b = jnp.exp(rowf * lg_b)
    cd_f = jnp.exp(C * lg_f)
    cd_b = jnp.exp(C * lg_b)

    def a_body(n, carry):
        rows = pl.ds(pl.multiple_of(n * C, C), C)
        co = cos_ref[rows, :]
        si = sin_ref[rows, :]
        q = q_ref[rows, :].astype(F32)
        k = k_ref[rows, :].astype(F32)
        qr = q * co + pltpu.roll(q, half, 1) * si
        kr = (k * co + pltpu.roll(k, half, 1) * si) * (dk ** -0.5)
        qb[rows, :] = qr.astype(BF16)
        kb[rows, :] = kr.astype(BF16)
        q2[rows, :] = jnp.concatenate([qr * qdec_f, qr * qdec_b], axis=1).astype(BF16)
        k2 = jnp.concatenate([kr * kdec_f, kr * kdec_b], axis=1).astype(BF16)
        kv[n] = _dot_tn(k2, v_ref[rows, :])
        return carry

    lax.fori_loop(0, nc, a_body, 0, unroll=4)

    def scan_f(n, state):
        sprev[n, 0:dk, :] = state.astype(BF16)
        return state * cd_f + kv[n, 0:dk, :]

    lax.fori_loop(0, nc, scan_f, jnp.zeros((dk, RET_V_DIM), F32))

    def scan_b(t, state):
        n = nc - 1 - t
        sprev[n, dk:2 * dk, :] = state.astype(BF16)
        return state * cd_b + kv[n, dk:2 * dk, :]

    lax.fori_loop(0, nc, scan_b, jnp.zeros((dk, RET_V_DIM), F32))

    ri = lax.broadcasted_iota(I32, (C, C), 0)
    ci = lax.broadcasted_iota(I32, (C, C), 1)
    d = (ri - ci).astype(F32)
    dec = jnp.where(ri >= ci, jnp.exp(jnp.maximum(d, 0.0) * lg_f), jnp.exp(jnp.maximum(-d, 0.0) * lg_b))
    gw = gw_ref[...]
    gb = gb_ref[...]

    def c_body(n, carry):
        rows = pl.ds(pl.multiple_of(n * C, C), C)
        scores = _dot_nt(qb[rows, :], kb[rows, :]) * dec
        y = _dot(scores.astype(BF16), v_ref[rows, :]) + _dot(q2[rows, :], sprev[n])
        mu = jnp.mean(y, axis=-1, keepdims=True)
        yc = y - mu
        var = jnp.mean(yc * yc, axis=-1, keepdims=True)
        yn = yc * lax.rsqrt(var + EPS) * gw + gb
        g = g_ref[rows, :].astype(F32)
        o_ref[rows, :] = (g * _sigmoid(g) * yn).astype(o_ref.dtype)
        return carry

    lax.fori_loop(0, nc, c_body, 0, unroll=4)


def _retention(proj, decay_fwd, decay_bwd, gn_w, gn_b, B, S):
    T = proj.shape[0]
    dk, dv = RET_QK_DIM, RET_V_DIM
    q_off = (ATTN_Q_W + 2 * ATTN_KV_W) // dk
    k_off = q_off + RET_QK_W // dk
    v_off = (ATTN_Q_W + 2 * ATTN_KV_W + 2 * RET_QK_W) // dv
    g_off = v_off + RET_V_W // dv
    cos, sin = _rope_tables(S)
    C = min(RET_TILE, S)
    grid_spec = pltpu.PrefetchScalarGridSpec(
        num_scalar_prefetch=2,
        grid=(B, RET_HEADS),
        in_specs=[
            pl.BlockSpec((S, dk), lambda b, h, *_: (b, q_off + h)),
            pl.BlockSpec((S, dk), lambda b, h, *_: (b, k_off + h)),
            pl.BlockSpec((S, dv), lambda b, h, *_: (b, v_off + h)),
            pl.BlockSpec((S, dv), lambda b, h, *_: (b, g_off + h)),
            pl.BlockSpec((S, dk), lambda b, h, *_: (0, 0)),
            pl.BlockSpec((S, dk), lambda b, h, *_: (0, 0)),
            pl.BlockSpec((1, dv), lambda b, h, *_: (0, h)),
            pl.BlockSpec((1, dv), lambda b, h, *_: (0, h)),
        ],
        out_specs=pl.BlockSpec((S, dv), lambda b, h, *_: (b, h)),
        scratch_shapes=[
            pltpu.VMEM((S, dk), BF16),
            pltpu.VMEM((S, dk), BF16),
            pltpu.VMEM((S, 2 * dk), BF16),
            pltpu.VMEM((S // C, 2 * dk, dv), F32),
            pltpu.VMEM((S // C, 2 * dk, dv), BF16),
        ],
    )
    return pl.pallas_call(
        functools.partial(_ret_kernel, S=S, C=C),
        out_shape=jax.ShapeDtypeStruct((T, RET_V_W), BF16),
        grid_spec=grid_spec,
        compiler_params=_cparams(("arbitrary", "arbitrary")),
        name="retention",
    )(decay_fwd.astype(F32), decay_bwd.astype(F32), proj, proj, proj, proj, cos, sin,
      gn_w.reshape(1, -1), gn_b.reshape(1, -1))


def _merge_kernel(a_ref, r_ref, wa_ref, wr_ref, ga_ref, gr_ref, o_ref, wab, wrb):
    _cast_weight_once(wa_ref, wab)
    _cast_weight_once(wr_ref, wrb)
    ua = _dot(a_ref[...], wab[...])
    ur = _dot(r_ref[...], wrb[...])
    out = _sigmoid(ga_ref[...].astype(F32)) * ua + _sigmoid(gr_ref[...].astype(F32)) * ur
    o_ref[...] = out.astype(o_ref.dtype)


def _merge_up(attn, ret, wa, wr, proj, D):
    T = attn.shape[0]
    tm = min(1024, T)
    tn = min(512, D)
    ga_off = (ATTN_Q_W + 2 * ATTN_KV_W + 2 * RET_QK_W + 2 * RET_V_W) // tn
    gr_off = ga_off + D // tn
    return pl.pallas_call(
        _merge_kernel,
        out_shape=jax.ShapeDtypeStruct((T, D), BF16),
        grid=(D // tn, T // tm),
        in_specs=[
            pl.BlockSpec((tm, ATTN_Q_W), lambda j, i: (i, 0)),
            pl.BlockSpec((tm, RET_V_W), lambda j, i: (i, 0)),
            pl.BlockSpec((ATTN_Q_W, tn), lambda j, i: (0, j)),
            pl.BlockSpec((RET_V_W, tn), lambda j, i: (0, j)),
            pl.BlockSpec((tm, tn), lambda j, i: (i, ga_off + j)),
            pl.BlockSpec((tm, tn), lambda j, i: (i, gr_off + j)),
        ],
        out_specs=pl.BlockSpec((tm, tn), lambda j, i: (i, j)),
        scratch_shapes=[pltpu.VMEM((ATTN_Q_W, tn), BF16), pltpu.VMEM((RET_V_W, tn), BF16)],
        compiler_params=_cparams(("arbitrary", "arbitrary")),
        name="merge_up",
    )(attn, ret, wa, wr, proj, proj)


def _out_proj_kernel(m_ref, w_ref, x_ref, mod_ref, o_ref, wb, *, gate_idx):
    _cast_weight_once(w_ref, wb)
    y = _dot(m_ref[...], wb[...])
    o_ref[...] = x_ref[...] + mod_ref[gate_idx:gate_idx + 1, :] * y


def _out_proj(merged, w, x2, mod, S, gate_idx):
    T, D = x2.shape
    tm = min(1024, S)
    tn = min(512, D)
    return pl.pallas_call(
        functools.partial(_out_proj_kernel, gate_idx=gate_idx),
        out_shape=jax.ShapeDtypeStruct((T, D), F32),
        grid=(D // tn, T // tm),
        in_specs=[
            pl.BlockSpec((tm, D), lambda j, i: (i, 0)),
            pl.BlockSpec((D, tn), lambda j, i: (0, j)),
            pl.BlockSpec((tm, tn), lambda j, i: (i, j)),
            pl.BlockSpec((None, N_ADA, tn), lambda j, i: ((i * tm) // S, 0, j)),
        ],
        out_specs=pl.BlockSpec((tm, tn), lambda j, i: (i, j)),
        scratch_shapes=[pltpu.VMEM((D, tn), BF16)],
        compiler_params=_cparams(("arbitrary", "arbitrary")),
        name="out_proj",
    )(merged, w, x2, mod)


def _pack_bf16_pairs(v):
    n = v.shape[1] // 2
    bits = lax.bitcast_convert_type(v.astype(BF16).astype(F32), U32)
    return jnp.bitwise_or(jnp.bitwise_and(bits[:, n:], jnp.uint32(0xFFFF0000)),
                          jnp.right_shift(bits[:, :n], jnp.uint32(16)))


def _unpack_bf16_pairs(w):
    lo = lax.bitcast_convert_type(jnp.left_shift(w, jnp.uint32(16)), F32)
    hi = lax.bitcast_convert_type(jnp.bitwise_and(w, jnp.uint32(0xFFFF0000)), F32)
    return jnp.concatenate([lo, hi], axis=1)


def _norm_router_kernel(x_ref, nw_ref, mod_ref, rwt_ref, rb_ref, hp_ref, e_ref, w_ref, r_ref, cnt_ref, carry,
                        *, shift_idx, scale_idx):
    i = pl.program_id(0)

    @pl.when(i == 0)
    def _():
        carry[...] = jnp.zeros_like(carry)

    h = _norm_mod_value(x_ref[...], nw_ref[...], mod_ref[shift_idx:shift_idx + 1, :],
                        mod_ref[scale_idx:scale_idx + 1, :])
    hp_ref[...] = _pack_bf16_pairs(h)
    hb = h.astype(BF16)
    tm = hb.shape[0]
    E = rwt_ref.shape[0]
    logits = _dot_nt(rwt_ref[...], hb) + rb_ref[...]
    iota_e = lax.broadcasted_iota(I32, (E, tm), 0)
    vals, idxs = [], []
    work = logits
    sel = jnp.zeros((E, tm), F32)
    for _k in range(TOP_K):
        m = jnp.max(work, axis=0, keepdims=True)
        idx = jnp.min(jnp.where(work == m, iota_e, E), axis=0, keepdims=True)
        hit = iota_e == idx
        vals.append(m)
        idxs.append(idx)
        work = jnp.where(hit, NEG_BIG, work)
        sel = sel + hit.astype(F32)
    ex = [jnp.exp(v - vals[0]) for v in vals]
    tot = ex[0]
    for v in ex[1:]:
        tot = tot + v
    ri = lax.broadcasted_iota(I32, (tm, tm), 0)
    ci = lax.broadcasted_iota(I32, (tm, tm), 1)
    upper = (ri < ci).astype(BF16)
    before = _dot(sel.astype(BF16), upper) + carry[:, 0:1]
    for k in range(TOP_K):
        e_ref[k:k + 1, :] = idxs[k]
        w_ref[k:k + 1, :] = ex[k] / tot
        r_ref[k:k + 1, :] = jnp.sum(jnp.where(iota_e == idxs[k], before, 0.0), axis=0, keepdims=True).astype(I32)
    carry[...] = carry[...] + jnp.sum(sel, axis=1, keepdims=True)
    cnt_ref[...] = carry[...]


def _norm_router(x1, nw, mod, router_w, router_b, S, shift_idx, scale_idx):
    T, D = x1.shape
    E = router_w.shape[1]
    tm = min(256, S)
    outs = pl.pallas_call(
        functools.partial(_norm_router_kernel, shift_idx=shift_idx, scale_idx=scale_idx),
        out_shape=(
            jax.ShapeDtypeStruct((T, D // 2), U32),
            jax.ShapeDtypeStruct((TOP_K, T), I32),
            jax.ShapeDtypeStruct((TOP_K, T), F32),
            jax.ShapeDtypeStruct((TOP_K, T), I32),
            jax.ShapeDtypeStruct((E, 128), F32),
        ),
        grid=(T // tm,),
        in_specs=[
            pl.BlockSpec((tm, D), lambda i: (i, 0)),
            pl.BlockSpec((1, D), lambda i: (0, 0)),
            pl.BlockSpec((None, N_ADA, D), lambda i: ((i * tm) // S, 0, 0)),
            pl.BlockSpec((E, D), lambda i: (0, 0)),
            pl.BlockSpec((E, 1), lambda i: (0, 0)),
        ],
        out_specs=(
            pl.BlockSpec((tm, D // 2), lambda i: (i, 0)),
            pl.BlockSpec((TOP_K, tm), lambda i: (0, i)),
            pl.BlockSpec((TOP_K, tm), lambda i: (0, i)),
            pl.BlockSpec((TOP_K, tm), lambda i: (0, i)),
            pl.BlockSpec((E, 128), lambda i: (0, 0)),
        ),
        scratch_shapes=[pltpu.VMEM((E, 128), F32)],
        compiler_params=_cparams(("arbitrary",)),
        name="norm_router",
    )(x1, nw.reshape(1, D), mod, router_w.T.astype(BF16), router_b.reshape(E, 1).astype(F32))
    return outs


def _dispatch_kernel(pos_ref, pad_ref, hp_ref, xs_ref, zeros, sem, zsem, *, tm, T, E, n_tiles):
    base = pl.program_id(0) * tm

    @pl.when(pl.program_id(0) == 0)
    def _():
        zeros[...] = jnp.zeros_like(zeros)
        def pad_copy(row):
            return pltpu.make_async_copy(zeros.at[pl.ds(0, 1)], xs_ref.at[pl.ds(row, 1)], zsem)

        def tail_copy(t):
            return pltpu.make_async_copy(
                zeros, xs_ref.at[pl.ds(pl.multiple_of(t * MOE_TILE, MOE_TILE), MOE_TILE)], zsem)

        def pad_start(e, carry):
            start = pad_ref[e]
            lax.fori_loop(0, pad_ref[E + e], lambda r, c: (pad_copy(start + r).start(), c)[1], 0)
            return carry

        def pad_wait(e, carry):
            start = pad_ref[e]
            lax.fori_loop(0, pad_ref[E + e], lambda r, c: (pad_copy(start + r).wait(), c)[1], 0)
            return carry

        def tail_start(t, carry):
            tail_copy(t).start()
            return carry

        def tail_wait(t, carry):
            tail_copy(t).wait()
            return carry

        lax.fori_loop(0, E, pad_start, 0)
        lax.fori_loop(pad_ref[2 * E], n_tiles, tail_start, 0)
        lax.fori_loop(0, E, pad_wait, 0)
        lax.fori_loop(pad_ref[2 * E], n_tiles, tail_wait, 0)

    def start_body(t, carry):
        for k in range(TOP_K):
            dst = pos_ref[k * T + base + t]
            pltpu.make_async_copy(hp_ref.at[pl.ds(t, 1)], xs_ref.at[pl.ds(dst, 1)], sem).start(priority=k % 2)
        return carry

    lax.fori_loop(0, tm, start_body, 0, unroll=8)
    for k in range(TOP_K):
        pltpu.make_async_copy(hp_ref, xs_ref.at[pl.ds(0, tm)], sem).wait()


def _dispatch(pos_flat, pad_info, hp, n_tiles, E):
    T, W = hp.shape
    tm = min(512, T)
    grid_spec = pltpu.PrefetchScalarGridSpec(
        num_scalar_prefetch=2,
        grid=(T // tm,),
        in_specs=[pl.BlockSpec((tm, W), lambda i, p, q: (i, 0))],
        out_specs=pl.BlockSpec(memory_space=pl.ANY),
        scratch_shapes=[pltpu.VMEM((MOE_TILE, W), U32), pltpu.SemaphoreType.DMA, pltpu.SemaphoreType.DMA],
    )
    return pl.pallas_call(
        functools.partial(_dispatch_kernel, tm=tm, T=T, E=E, n_tiles=n_tiles),
        out_shape=jax.ShapeDtypeStruct((n_tiles * MOE_TILE, W), U32),
        grid_spec=grid_spec,
        compiler_params=_cparams(("arbitrary",)),
        name="dispatch",
    )(pos_flat, pad_info, hp)


def _expert_changed(te_ref, i):
    return jnp.logical_or(i == 0, te_ref[i] != te_ref[jnp.maximum(i - 1, 0)])


def _expert_up_kernel(te_ref, nu_ref, xs_ref, wg_ref, wu_ref, bg_ref, bu_ref, o_ref, wgb, wub):
    i = pl.program_id(1)

    @pl.when(_expert_changed(te_ref, i))
    def _():
        wgb[...] = wg_ref[...].astype(BF16)
        wub[...] = wu_ref[...].astype(BF16)

    @pl.when(i < nu_ref[0])
    def _():
        x = _unpack_bf16_pairs(xs_ref[...]).astype(BF16)
        gate = jnp.minimum(_dot(x, wgb[...]) + bg_ref[...], SWIGLU_LIMIT)
        up = jnp.clip(_dot(x, wub[...]) + bu_ref[...], -SWIGLU_LIMIT, SWIGLU_LIMIT)
        act = gate * _sigmoid(SWIGLU_ALPHA * gate) * (up + 1.0)
        o_ref[...] = act.astype(o_ref.dtype)

    @pl.when(i >= nu_ref[0])
    def _():
        o_ref[...] = jnp.zeros_like(o_ref)


def _expert_down_kernel(te_ref, nu_ref, a_ref, w2_ref, b2_ref, o_ref, w2b):
    i = pl.program_id(0)

    @pl.when(_expert_changed(te_ref, i))
    def _():
        w2b[...] = w2_ref[...].astype(BF16)

    @pl.when(i < nu_ref[0])
    def _():
        y = _dot(a_ref[...], w2b[...]) + b2_ref[...]
        o_ref[...] = _pack_bf16_pairs(y)

    @pl.when(i >= nu_ref[0])
    def _():
        o_ref[...] = jnp.zeros_like(o_ref)


def _expert_ffn(xs, tile_e, n_used, w1, b1, w2, b2):
    R, W = xs.shape
    E, D, F2 = w1.shape
    F = F2 // 2
    tm = MOE_TILE
    nt = R // tm

    tn = min(512, F)
    nf = F // tn
    b1r = b1.reshape(E, 1, F2)

    act = pl.pallas_call(
        _expert_up_kernel,
        out_shape=jax.ShapeDtypeStruct((R, F), BF16),
        grid_spec=pltpu.PrefetchScalarGridSpec(
            num_scalar_prefetch=2,
            grid=(nf, nt),
            in_specs=[
                pl.BlockSpec((tm, W), lambda n, i, te, nu: (jnp.minimum(i, nu[0] - 1), 0)),
                pl.BlockSpec((None, D, tn), lambda n, i, te, nu: (te[i], 0, n)),
                pl.BlockSpec((None, D, tn), lambda n, i, te, nu: (te[i], 0, nf + n)),
                pl.BlockSpec((None, 1, tn), lambda n, i, te, nu: (te[i], 0, n)),
                pl.BlockSpec((None, 1, tn), lambda n, i, te, nu: (te[i], 0, nf + n)),
            ],
            out_specs=pl.BlockSpec((tm, tn), lambda n, i, te, nu: (i, n)),
            scratch_shapes=[pltpu.VMEM((D, tn), BF16), pltpu.VMEM((D, tn), BF16)],
        ),
        compiler_params=_cparams(("arbitrary", "arbitrary")),
        name="expert_up",
    )(tile_e, n_used, xs, w1, w1, b1r, b1r)
    y = pl.pallas_call(
        _expert_down_kernel,
        out_shape=jax.ShapeDtypeStruct((R, D // 2), U32),
        grid_spec=pltpu.PrefetchScalarGridSpec(
            num_scalar_prefetch=2,
            grid=(nt,),
            in_specs=[
                pl.BlockSpec((tm, F), lambda i, te, nu: (jnp.minimum(i, nu[0] - 1), 0)),
                pl.BlockSpec((None, F, D), lambda i, te, nu: (te[i], 0, 0)),
                pl.BlockSpec((None, 1, D), lambda i, te, nu: (te[i], 0, 0)),
            ],
            out_specs=pl.BlockSpec((tm, D // 2), lambda i, te, nu: (i, 0)),
            scratch_shapes=[pltpu.VMEM((F, D), BF16)],
        ),
        compiler_params=_cparams(("arbitrary",)),
        name="expert_down",
    )(tile_e, n_used, act, w2, b2.reshape(E, 1, D))
    return y


def _combine_kernel(pos_ref, x_ref, w_ref, mod_ref, y_ref, o_ref, buf, sem, *, tm, T, gate_idx):
    i = pl.program_id(0)
    n = pl.num_programs(0)
    W = buf.shape[-1]

    def issue(step, slot):
        base = step * tm

        def body(t, carry):
            for k in range(TOP_K):
                src = pos_ref[k * T + base + t]
                pltpu.make_async_copy(y_ref.at[pl.ds(src, 1)], buf.at[slot, k, pl.ds(t, 1)],
                                      sem.at[slot]).start(priority=k % 2)
            return carry

        lax.fori_loop(0, tm, body, 0, unroll=8)

    @pl.when(i == 0)
    def _():
        issue(0, 0)

    @pl.when(i + 1 < n)
    def _():
        issue(i + 1, (i + 1) % 2)

    slot = i % 2
    for k in range(TOP_K):
        pltpu.make_async_copy(y_ref.at[pl.ds(0, tm)], buf.at[slot, k], sem.at[slot]).wait()

    rc = 8
    cw = min(512, W)

    def rows_body(r, carry):
        rows = pl.ds(pl.multiple_of(r * rc, rc), rc)
        wv = w_ref[rows, :]
        wk = [jnp.broadcast_to(wv[:, k:k + 1], (rc, cw)) for k in range(TOP_K)]
        for c in range(W // cw):
            lo = hi = None
            for k in range(TOP_K):
                u = buf[slot, k, rows, c * cw:(c + 1) * cw]
                l = wk[k] * lax.bitcast_convert_type(jnp.left_shift(u, jnp.uint32(16)), F32)
                h = wk[k] * lax.bitcast_convert_type(jnp.bitwise_and(u, jnp.uint32(0xFFFF0000)), F32)
                lo = l if lo is None else lo + l
                hi = h if hi is None else hi + h
            for half, acc in ((0, lo), (1, hi)):
                cols = slice(half * W + c * cw, half * W + (c + 1) * cw)
                o_ref[rows, cols] = x_ref[rows, cols] + mod_ref[gate_idx:gate_idx + 1, cols] * acc
        return carry

    lax.fori_loop(0, tm // rc, rows_body, 0, unroll=2)


def _combine(pos_flat, x1, w_tk, mod, y, S, gate_idx):
    T, D = x1.shape
    tm = min(128, S)
    grid_spec = pltpu.PrefetchScalarGridSpec(
        num_scalar_prefetch=1,
        grid=(T // tm,),
        in_specs=[
            pl.BlockSpec((tm, D), lambda i, p: (i, 0)),
            pl.BlockSpec((tm, TOP_K), lambda i, p: (i, 0)),
            pl.BlockSpec((None, N_ADA, D), lambda i, p: ((i * tm) // S, 0, 0)),
            pl.BlockSpec(memory_space=pl.ANY),
        ],
        out_specs=pl.BlockSpec((tm, D), lambda i, p: (i, 0)),
        scratch_shapes=[pltpu.VMEM((2, TOP_K, tm, D // 2), U32), pltpu.SemaphoreType.DMA((2,))],
    )
    return pl.pallas_call(
        functools.partial(_combine_kernel, tm=tm, T=T, gate_idx=gate_idx),
        out_shape=jax.ShapeDtypeStruct((T, D), F32),
        grid_spec=grid_spec,
        compiler_params=_cparams(("arbitrary",)),
        name="combine",
    )(pos_flat, x1, w_tk, mod, y)


def _layer(x2, mod, B, S, rel_bias, norm_mix_w, w_in, q_norm_w, k_norm_w, attn_sink, ret_decay_fwd, ret_decay_bwd,
           ret_gn_w, ret_gn_b, w_up_attn, w_up_ret, w_out, norm_ffn_w, router_w, router_b,
           expert_w1, expert_b1, expert_w2, expert_b2):
    T, D = x2.shape
    E = router_w.shape[1]
    h = _norm_mod(x2, norm_mix_w, mod, S, 0, 1)
    proj = _in_proj(h, w_in, q_norm_w, k_norm_w)
    attn = _window_attention(proj, _attn_bias_table(rel_bias), attn_sink, B, S)
    ret = _retention(proj, ret_decay_fwd, ret_decay_bwd, ret_gn_w, ret_gn_b, B, S)
    merged = _merge_up(attn, ret, w_up_attn, w_up_ret, proj, D)
    x1 = _out_proj(merged, w_out, x2, mod, S, 2)

    hp, top_e, top_w, rank, cnt = _norm_router(x1, norm_ffn_w, mod, router_w, router_b, S, 3, 4)
    tm = MOE_TILE
    counts = cnt[:, 0].astype(I32)
    tiles_per_e = (counts + tm - 1) // tm
    tile_end = jnp.cumsum(tiles_per_e)
    group_start = (tile_end - tiles_per_e) * tm
    n_tiles = (T * TOP_K) // tm + E
    n_used = tile_end[-1]
    tile_ids = jnp.minimum(jnp.arange(n_tiles, dtype=I32), n_used - 1)
    tile_e = jnp.minimum(jnp.sum((tile_ids[:, None] >= tile_end[None, :]).astype(I32), axis=1), E - 1)
    e_ids = jnp.arange(E, dtype=I32)
    start_of = jnp.sum(jnp.where(top_e[:, :, None] == e_ids, group_start, 0), axis=-1)
    pos_flat = (start_of + rank).astype(I32).reshape(-1)
    pad_info = jnp.concatenate([group_start + counts, tiles_per_e * tm - counts, n_used.reshape(1)]).astype(I32)

    xs = _dispatch(pos_flat, pad_info, hp, n_tiles, E)
    y = _expert_ffn(xs, tile_e, n_used.reshape(1).astype(I32), expert_w1, expert_b1, expert_w2, expert_b2)
    return _combine(pos_flat, x1, top_w.T, mod, y, S, 5)


def kernel(x, c, rel_bias, ada_w, ada_b, norm_mix_w, w_in, q_norm_w, k_norm_w, attn_sink, ret_decay_fwd,
           ret_decay_bwd, ret_gn_w, ret_gn_b, w_up_attn, w_up_ret, w_out, norm_ffn_w, router_w, router_b,
           expert_w1, expert_b1, expert_w2, expert_b2):
    B, S, D = x.shape
    x2 = x.reshape(B * S, D)
    for l in range(ada_w.shape[0]):
        mod = _ada_mod(c, ada_w[l], ada_b[l])
        x2 = _layer(x2, mod, B, S, rel_bias, norm_mix_w[l], w_in[l], q_norm_w[l], k_norm_w[l], attn_sink[l],
                    ret_decay_fwd[l], ret_decay_bwd[l], ret_gn_w[l], ret_gn_b[l], w_up_attn[l], w_up_ret[l],
                    w_out[l], norm_ffn_w[l], router_w[l], router_b[l], expert_w1[l], expert_b1[l],
                    expert_w2[l], expert_b2[l])
    return x2.reshape(B, S, D)
```

```python
import functools
import math

import jax
import jax.numpy as jnp
from jax import lax
from jax.experimental import pallas as pl
from jax.experimental.pallas import tpu as pltpu

F32 = jnp.float32
BF16 = jnp.bfloat16
U32 = jnp.uint32
I32 = jnp.int32

ATTN_HEADS = 16
ATTN_KV_HEADS = 4
ATTN_HEAD_DIM = 128
WINDOW = 128
ATTN_BLOCK = 128
N_BUCKETS = 32
MAX_DISTANCE = 128
RET_HEADS = 8
RET_QK_DIM = 128
RET_V_DIM = 256
RET_CHUNK = 128
ROPE_BASE = 10000.0
TOP_K = 4
SWIGLU_LIMIT = 7.0
SWIGLU_ALPHA = 1.702
N_ADA = 6
EPS = 1e-6
NEG_INF = -1e30
NEG_BIG = -3.0e38

ATTN_Q_W = ATTN_HEADS * ATTN_HEAD_DIM
ATTN_KV_W = ATTN_KV_HEADS * ATTN_HEAD_DIM
RET_QK_W = RET_HEADS * RET_QK_DIM
RET_V_W = RET_HEADS * RET_V_DIM

VMEM_LIMIT_BYTES = 56 * 1024 * 1024
MOE_TILE = 256
RET_TILE = 256


def _cparams(sem):
    return pltpu.CompilerParams(dimension_semantics=sem, vmem_limit_bytes=VMEM_LIMIT_BYTES)


def _dot(a, b):
    return jnp.dot(a, b, preferred_element_type=F32)


def _dot_nt(a, b):
    return lax.dot_general(a, b, (((1,), (1,)), ((), ())), preferred_element_type=F32)


def _dot_tn(a, b):
    return lax.dot_general(a, b, (((0,), (0,)), ((), ())), preferred_element_type=F32)


def _sigmoid(x):
    return 1.0 / (1.0 + jnp.exp(-x))


def _ada_kernel(c_ref, w_ref, b_ref, o_ref):
    c = c_ref[...]
    cs = (c * _sigmoid(c)).astype(BF16)
    o_ref[...] = _dot(cs, w_ref[...].astype(BF16)) + b_ref[...]


def _ada_mod(c, ada_w, ada_b):
    B, D = c.shape
    N = ada_w.shape[1]
    rows = 8
    cp = jnp.zeros((rows, D), F32).at[:B].set(c)
    tn = min(512, N)
    out = pl.pallas_call(
        _ada_kernel,
        out_shape=jax.ShapeDtypeStruct((rows, N), F32),
        grid=(N // tn,),
        in_specs=[
            pl.BlockSpec((rows, D), lambda j: (0, 0)),
            pl.BlockSpec((D, tn), lambda j: (0, j)),
            pl.BlockSpec((1, tn), lambda j: (0, j)),
        ],
        out_specs=pl.BlockSpec((rows, tn), lambda j: (0, j)),
        compiler_params=_cparams(("arbitrary",)),
        name="ada_mod",
    )(cp, ada_w, ada_b.reshape(1, N))
    return out[:B].reshape(B, N_ADA, D)


def _norm_mod_value(x, nw, shift, scale):
    ms = jnp.mean(x * x, axis=-1, keepdims=True)
    h = x * lax.rsqrt(ms + EPS) * nw
    return h * (1.0 + scale) + shift


def _norm_mod_kernel(x_ref, nw_ref, mod_ref, o_ref, *, shift_idx, scale_idx):
    h = _norm_mod_value(x_ref[...], nw_ref[...], mod_ref[shift_idx:shift_idx + 1, :],
                        mod_ref[scale_idx:scale_idx + 1, :])
    o_ref[...] = h.astype(BF16)


def _norm_mod(x2, nw, mod, S, shift_idx, scale_idx):
    T, D = x2.shape
    tm = min(256, S)
    return pl.pallas_call(
        functools.partial(_norm_mod_kernel, shift_idx=shift_idx, scale_idx=scale_idx),
        out_shape=jax.ShapeDtypeStruct((T, D), BF16),
        grid=(T // tm,),
        in_specs=[
            pl.BlockSpec((tm, D), lambda i: (i, 0)),
            pl.BlockSpec((1, D), lambda i: (0, 0)),
            pl.BlockSpec((None, N_ADA, D), lambda i: ((i * tm) // S, 0, 0)),
        ],
        out_specs=pl.BlockSpec((tm, D), lambda i: (i, 0)),
        compiler_params=_cparams(("arbitrary",)),
        name="norm_mod",
    )(x2, nw.reshape(1, D), mod)


def _cast_weight_once(w_ref, wb_ref):
    @pl.when(pl.program_id(1) == 0)
    def _():
        wb_ref[...] = w_ref[...].astype(BF16)


def _in_proj_kernel(a_ref, w_ref, cw_ref, bd_ref, o_ref, wb, *, n_norm_tiles):
    j = pl.program_id(0)
    _cast_weight_once(w_ref, wb)

    @pl.when(j < n_norm_tiles)
    def _():
        acc = _dot(a_ref[...], wb[...])
        ms = _dot((acc * acc).astype(BF16), bd_ref[...]) * (1.0 / ATTN_HEAD_DIM)
        o_ref[...] = (acc * lax.rsqrt(ms + EPS) * cw_ref[...]).astype(o_ref.dtype)

    @pl.when(j >= n_norm_tiles)
    def _():
        o_ref[...] = _dot(a_ref[...], wb[...]).astype(o_ref.dtype)


def _in_proj(h, w, q_norm_w, k_norm_w):
    T, K = h.shape
    N = w.shape[1]
    tm = min(1024, T)
    tn = 512
    hd = ATTN_HEAD_DIM
    n_norm = ATTN_Q_W + ATTN_KV_W
    assert n_norm % tn == 0 and tn % hd == 0
    col_w = jnp.concatenate([jnp.tile(q_norm_w.astype(F32) * (hd ** -0.5), ATTN_HEADS),
                             jnp.tile(k_norm_w.astype(F32), ATTN_KV_HEADS)]).reshape(1, n_norm)
    lane_head = jnp.arange(tn, dtype=I32) // hd
    block_ones = (lane_head[:, None] == lane_head[None, :]).astype(BF16)
    last = n_norm // tn - 1
    return pl.pallas_call(
        functools.partial(_in_proj_kernel, n_norm_tiles=n_norm // tn),
        out_shape=jax.ShapeDtypeStruct((T, N), BF16),
        grid=(N // tn, T // tm),
        in_specs=[
            pl.BlockSpec((tm, K), lambda j, i: (i, 0)),
            pl.BlockSpec((K, tn), lambda j, i: (0, j)),
            pl.BlockSpec((1, tn), lambda j, i: (0, jnp.minimum(j, last))),
            pl.BlockSpec((tn, tn), lambda j, i: (0, 0)),
        ],
        out_specs=pl.BlockSpec((tm, tn), lambda j, i: (i, j)),
        scratch_shapes=[pltpu.VMEM((K, tn), BF16)],
        compiler_params=_cparams(("arbitrary", "arbitrary")),
        name="in_proj",
    )(h, w, col_w, block_ones)


def _t5_bucket(rel):
    nb = N_BUCKETS // 2
    max_exact = nb // 2
    base = jnp.where(rel > 0, nb, 0)
    n = jnp.abs(rel)
    nf = jnp.maximum(n, 1).astype(F32)
    large = max_exact + (jnp.log(nf / max_exact) / math.log(MAX_DISTANCE / max_exact) * (nb - max_exact)).astype(I32)
    large = jnp.minimum(large, nb - 1)
    return base + jnp.where(n < max_exact, n, large)


def _attn_bias_table(rel_bias):
    blk = ATTN_BLOCK
    qi = jnp.arange(blk, dtype=I32)[:, None]
    kj = jnp.arange(3 * blk, dtype=I32)[None, :]
    rel = kj - blk - qi
    bucket = _t5_bucket(rel)
    table = rel_bias.astype(F32).T
    bias = jnp.zeros((table.shape[0],) + bucket.shape, F32)
    for b in range(N_BUCKETS):
        bias = jnp.where(bucket[None] == b, table[:, b][:, None, None], bias)
    bias = jnp.where((jnp.abs(rel) <= WINDOW)[None], bias, NEG_INF)
    G = ATTN_HEADS // ATTN_KV_HEADS
    bias = bias.reshape(ATTN_KV_HEADS, G, blk, 3 * blk)
    return jnp.transpose(bias, (0, 3, 1, 2)).reshape(ATTN_KV_HEADS, 3 * blk, G * blk)


def _attn_kernel(sink_ref, q_ref, kp_ref, kc_ref, kn_ref, vp_ref, vc_ref, vn_ref, bias_ref, o_ref):
    n = pl.program_id(1)
    nb = pl.num_programs(1)
    blk, hd = ATTN_BLOCK, ATTN_HEAD_DIM
    G = ATTN_HEADS // ATTN_KV_HEADS
    row = lax.broadcasted_iota(I32, (3 * blk, 1), 0)
    oob = jnp.logical_or(jnp.logical_and(row < blk, n == 0), jnp.logical_and(row >= 2 * blk, n == nb - 1))
    for hk in range(ATTN_KV_HEADS):
        sl = slice(hk * hd, (hk + 1) * hd)
        k3 = jnp.concatenate([kp_ref[:, sl], kc_ref[:, sl], kn_ref[:, sl]], axis=0)
        v3 = jnp.concatenate([vp_ref[:, sl], vc_ref[:, sl], vn_ref[:, sl]], axis=0)
        q4 = jnp.concatenate([q_ref[:, (hk * G + g) * hd:(hk * G + g + 1) * hd] for g in range(G)],
                             axis=0)
        sink = jnp.concatenate([jnp.full((1, blk), sink_ref[hk * G + g], F32) for g in range(G)], axis=1)
        logits = _dot_nt(k3, q4) + bias_ref[hk]
        logits = jnp.where(oob, NEG_INF, logits)
        m = jnp.maximum(jnp.max(logits, axis=0, keepdims=True), sink)
        p = jnp.exp(logits - m)
        denom = jnp.sum(p, axis=0, keepdims=True) + jnp.exp(sink - m)
        out_t = _dot_tn(v3, p.astype(BF16)) * (1.0 / denom)
        out = out_t.T
        for g in range(G):
            h = hk * G + g
            o_ref[:, h * hd:(h + 1) * hd] = out[g * blk:(g + 1) * blk].astype(o_ref.dtype)


def _window_attention(proj, bias_tab, sink, B, S):
    T = proj.shape[0]
    blk = ATTN_BLOCK
    nb = S // blk
    kcol = ATTN_Q_W // ATTN_KV_W
    vcol = kcol + 1

    def kv_spec(col, off):
        return pl.BlockSpec((blk, ATTN_KV_W), lambda b, n, s: (b * nb + jnp.clip(n + off, 0, nb - 1), col))

    grid_spec = pltpu.PrefetchScalarGridSpec(
        num_scalar_prefetch=1,
        grid=(B, nb),
        in_specs=[
            pl.BlockSpec((blk, ATTN_Q_W), lambda b, n, s: (b * nb + n, 0)),
            kv_spec(kcol, -1), kv_spec(kcol, 0), kv_spec(kcol, 1),
            kv_spec(vcol, -1), kv_spec(vcol, 0), kv_spec(vcol, 1),
            pl.BlockSpec((ATTN_KV_HEADS, 3 * blk, (ATTN_HEADS // ATTN_KV_HEADS) * blk), lambda b, n, s: (0, 0, 0)),
        ],
        out_specs=pl.BlockSpec((blk, ATTN_Q_W), lambda b, n, s: (b * nb + n, 0)),
    )
    return pl.pallas_call(
        _attn_kernel,
        out_shape=jax.ShapeDtypeStruct((T, ATTN_Q_W), BF16),
        grid_spec=grid_spec,
        compiler_params=_cparams(("arbitrary", "arbitrary")),
        name="window_attn",
    )(sink.astype(F32), proj, proj, proj, proj, proj, proj, proj, bias_tab)


def _rope_tables(S):
    d = RET_QK_DIM
    inv = ROPE_BASE ** (-jnp.arange(0, d, 2, dtype=F32) / d)
    ang = jnp.arange(S, dtype=F32)[:, None] * inv[None, :]
    cos, sin = jnp.cos(ang), jnp.sin(ang)
    return jnp.concatenate([cos, cos], axis=-1), jnp.concatenate([-sin, sin], axis=-1)


def _ret_kernel(df_ref, db_ref, q_ref, k_ref, v_ref, g_ref, cos_ref, sin_ref, gw_ref, gb_ref, o_ref,
                qb, kb, q2, kv, sprev, *, S, C):
    h = pl.program_id(1)
    nc = S // C
    dk = RET_QK_DIM
    half = dk // 2

    rowf = lax.broadcasted_iota(I32, (C, 1), 0).astype(F32)
    lg_f = -jnp.exp(jnp.full((1, 1), df_ref[h], F32))
    lg_b = -jnp.exp(jnp.full((1, 1), db_ref[h], F32))
    qdec_f = jnp.exp((rowf + 1.0) * lg_f)
    kdec_f = jnp.exp((C - 1.0 - rowf) * lg_f)
    qdec_b = jnp.exp((C - rowf) * lg_b)
    kdec_b = jnp.exp(rowf * lg_b)
    cd_f = jnp.exp(C * lg_f)
    cd_b = jnp.exp(C * lg_b)

    def a_body(n, carry):
        rows = pl.ds(pl.multiple_of(n * C, C), C)
        co = cos_ref[rows, :]
        si = sin_ref[rows, :]
        q = q_ref[rows, :].astype(F32)
        k = k_ref[rows, :].astype(F32)
        qr = q * co + pltpu.roll(q, half, 1) * si
        kr = (k * co + pltpu.roll(k, half, 1) * si) * (dk ** -0.5)
        qb[rows, :] = qr.astype(BF16)
        kb[rows, :] = kr.astype(BF16)
        q2[rows, :] = jnp.concatenate([qr * qdec_f, qr * qdec_b], axis=1).astype(BF16)
        k2 = jnp.concatenate([kr * kdec_f, kr * kdec_b], axis=1).astype(BF16)
        kv[n] = _dot_tn(k2, v_ref[rows, :])
        return carry

    lax.fori_loop(0, nc, a_body, 0, unroll=4)

    def scan_f(n, state):
        sprev[n, 0:dk, :] = state.astype(BF16)
        return state * cd_f + kv[n, 0:dk, :]

    lax.fori_loop(0, nc, scan_f, jnp.zeros((dk, RET_V_DIM), F32))

    def scan_b(t, state):
        n = nc - 1 - t
        sprev[n, dk:2 * dk, :] = state.astype(BF16)
        return state * cd_b + kv[n, dk:2 * dk, :]

    lax.fori_loop(0, nc, scan_b, jnp.zeros((dk, RET_V_DIM), F32))

    ri = lax.broadcasted_iota(I32, (C, C), 0)
    ci = lax.broadcasted_iota(I32, (C, C), 1)
    d = (ri - ci).astype(F32)
    dec = jnp.where(ri >= ci, jnp.exp(jnp.maximum(d, 0.0) * lg_f), jnp.exp(jnp.maximum(-d, 0.0) * lg_b))
    gw = gw_ref[...]
    gb = gb_ref[...]

    def c_body(n, carry):
        rows = pl.ds(pl.multiple_of(n * C, C), C)
        scores = _dot_nt(qb[rows, :], kb[rows, :]) * dec
        y = _dot(scores.astype(BF16), v_ref[rows, :]) + _dot(q2[rows, :], sprev[n])
        mu = jnp.mean(y, axis=-1, keepdims=True)
        yc = y - mu
        var = jnp.mean(yc * yc, axis=-1, keepdims=True)
        yn = yc * lax.rsqrt(var + EPS) * gw + gb
        g = g_ref[rows, :].astype(F32)
        o_ref[rows, :] = (g * _sigmoid(g) * yn).astype(o_ref.dtype)
        return carry

    lax.fori_loop(0, nc, c_body, 0, unroll=4)


def _retention(proj, decay_fwd, decay_bwd, gn_w, gn_b, B, S):
    T = proj.shape[0]
    dk, dv = RET_QK_DIM, RET_V_DIM
    q_off = (ATTN_Q_W + 2 * ATTN_KV_W) // dk
    k_off = q_off + RET_QK_W // dk
    v_off = (ATTN_Q_W + 2 * ATTN_KV_W + 2 * RET_QK_W) // dv
    g_off = v_off + RET_V_W // dv
    cos, sin = _rope_tables(S)
    C = min(RET_TILE, S)
    grid_spec = pltpu.PrefetchScalarGridSpec(
        num_scalar_prefetch=2,
        grid=(B, RET_HEADS),
        in_specs=[
            pl.BlockSpec((S, dk), lambda b, h, *_: (b, q_off + h)),
            pl.BlockSpec((S, dk), lambda b, h, *_: (b, k_off + h)),
            pl.BlockSpec((S, dv), lambda b, h, *_: (b, v_off + h)),
            pl.BlockSpec((S, dv), lambda b, h, *_: (b, g_off + h)),
            pl.BlockSpec((S, dk), lambda b, h, *_: (0, 0)),
            pl.BlockSpec((S, dk), lambda b, h, *_: (0, 0)),
            pl.BlockSpec((1, dv), lambda b, h, *_: (0, h)),
            pl.BlockSpec((1, dv), lambda b, h, *_: (0, h)),
        ],
        out_specs=pl.BlockSpec((S, dv), lambda b, h, *_: (b, h)),
        scratch_shapes=[
            pltpu.VMEM((S, dk), BF16),
            pltpu.VMEM((S, dk), BF16),
            pltpu.VMEM((S, 2 * dk), BF16),
            pltpu.VMEM((S // C, 2 * dk, dv), F32),
            pltpu.VMEM((S // C, 2 * dk, dv), BF16),
        ],
    )
    return pl.pallas_call(
        functools.partial(_ret_kernel, S=S, C=C),
        out_shape=jax.ShapeDtypeStruct((T, RET_V_W), BF16),
        grid_spec=grid_spec,
        compiler_params=_cparams(("arbitrary", "arbitrary")),
        name="retention",
    )(decay_fwd.astype(F32), decay_bwd.astype(F32), proj, proj, proj, proj, cos, sin,
      gn_w.reshape(1, -1), gn_b.reshape(1, -1))


def _merge_kernel(a_ref, r_ref, wa_ref, wr_ref, ga_ref, gr_ref, o_ref, wab, wrb):
    _cast_weight_once(wa_ref, wab)
    _cast_weight_once(wr_ref, wrb)
    ua = _dot(a_ref[...], wab[...])
    ur = _dot(r_ref[...], wrb[...])
    out = _sigmoid(ga_ref[...].astype(F32)) * ua + _sigmoid(gr_ref[...].astype(F32)) * ur
    o_ref[...] = out.astype(o_ref.dtype)


def _merge_up(attn, ret, wa, wr, proj, D):
    T = attn.shape[0]
    tm = min(1024, T)
    tn = min(512, D)
    ga_off = (ATTN_Q_W + 2 * ATTN_KV_W + 2 * RET_QK_W + 2 * RET_V_W) // tn
    gr_off = ga_off + D // tn
    return pl.pallas_call(
        _merge_kernel,
        out_shape=jax.ShapeDtypeStruct((T, D), BF16),
        grid=(D // tn, T // tm),
        in_specs=[
            pl.BlockSpec((tm, ATTN_Q_W), lambda j, i: (i, 0)),
            pl.BlockSpec((tm, RET_V_W), lambda j, i: (i, 0)),
            pl.BlockSpec((ATTN_Q_W, tn), lambda j, i: (0, j)),
            pl.BlockSpec((RET_V_W, tn), lambda j, i: (0, j)),
            pl.BlockSpec((tm, tn), lambda j, i: (i, ga_off + j)),
            pl.BlockSpec((tm, tn), lambda j, i: (i, gr_off + j)),
        ],
        out_specs=pl.BlockSpec((tm, tn), lambda j, i: (i, j)),
        scratch_shapes=[pltpu.VMEM((ATTN_Q_W, tn), BF16), pltpu.VMEM((RET_V_W, tn), BF16)],
        compiler_params=_cparams(("arbitrary", "arbitrary")),
        name="merge_up",
    )(attn, ret, wa, wr, proj, proj)


def _out_proj_kernel(m_ref, w_ref, x_ref, mod_ref, o_ref, wb, *, gate_idx):
    _cast_weight_once(w_ref, wb)
    y = _dot(m_ref[...], wb[...])
    o_ref[...] = x_ref[...] + mod_ref[gate_idx:gate_idx + 1, :] * y


def _out_proj(merged, w, x2, mod, S, gate_idx):
    T, D = x2.shape
    tm = min(1024, S)
    tn = min(512, D)
    return pl.pallas_call(
        functools.partial(_out_proj_kernel, gate_idx=gate_idx),
        out_shape=jax.ShapeDtypeStruct((T, D), F32),
        grid=(D // tn, T // tm),
        in_specs=[
            pl.BlockSpec((tm, D), lambda j, i: (i, 0)),
            pl.BlockSpec((D, tn), lambda j, i: (0, j)),
            pl.BlockSpec((tm, tn), lambda j, i: (i, j)),
            pl.BlockSpec((None, N_ADA, tn), lambda j, i: ((i * tm) // S, 0, j)),
        ],
        out_specs=pl.BlockSpec((tm, tn), lambda j, i: (i, j)),
        scratch_shapes=[pltpu.VMEM((D, tn), BF16)],
        compiler_params=_cparams(("arbitrary", "arbitrary")),
        name="out_proj",
    )(merged, w, x2, mod)


def _pack_bf16_pairs(v):
    n = v.shape[1] // 2
    bits = lax.bitcast_convert_type(v.astype(BF16).astype(F32), U32)
    return jnp.bitwise_or(jnp.bitwise_and(bits[:, n:], jnp.uint32(0xFFFF0000)),
                          jnp.right_shift(bits[:, :n], jnp.uint32(16)))


def _unpack_bf16_pairs(w):
    lo = lax.bitcast_convert_type(jnp.left_shift(w, jnp.uint32(16)), F32)
    hi = lax.bitcast_convert_type(jnp.bitwise_and(w, jnp.uint32(0xFFFF0000)), F32)
    return jnp.concatenate([lo, hi], axis=1)


def _norm_router_kernel(x_ref, nw_ref, mod_ref, rwt_ref, rb_ref, hp_ref, e_ref, w_ref, r_ref, cnt_ref, carry,
                        *, shift_idx, scale_idx):
    i = pl.program_id(0)

    @pl.when(i == 0)
    def _():
        carry[...] = jnp.zeros_like(carry)

    h = _norm_mod_value(x_ref[...], nw_ref[...], mod_ref[shift_idx:shift_idx + 1, :],
                        mod_ref[scale_idx:scale_idx + 1, :])
    hp_ref[...] = _pack_bf16_pairs(h)
    hb = h.astype(BF16)
    tm = hb.shape[0]
    E = rwt_ref.shape[0]
    logits = _dot_nt(rwt_ref[...], hb) + rb_ref[...]
    iota_e = lax.broadcasted_iota(I32, (E, tm), 0)
    vals, idxs = [], []
    work = logits
    sel = jnp.zeros((E, tm), F32)
    for _k in range(TOP_K):
        m = jnp.max(work, axis=0, keepdims=True)
        idx = jnp.min(jnp.where(work == m, iota_e, E), axis=0, keepdims=True)
        hit = iota_e == idx
        vals.append(m)
        idxs.append(idx)
        work = jnp.where(hit, NEG_BIG, work)
        sel = sel + hit.astype(F32)
    ex = [jnp.exp(v - vals[0]) for v in vals]
    tot = ex[0]
    for v in ex[1:]:
        tot = tot + v
    ri = lax.broadcasted_iota(I32, (tm, tm), 0)
    ci = lax.broadcasted_iota(I32, (tm, tm), 1)
    upper = (ri < ci).astype(BF16)
    before = _dot(sel.astype(BF16), upper) + carry[:, 0:1]
    for k in range(TOP_K):
        e_ref[k:k + 1, :] = idxs[k]
        w_ref[k:k + 1, :] = ex[k] / tot
        r_ref[k:k + 1, :] = jnp.sum(jnp.where(iota_e == idxs[k], before, 0.0), axis=0, keepdims=True).astype(I32)
    carry[...] = carry[...] + jnp.sum(sel, axis=1, keepdims=True)
    cnt_ref[...] = carry[...]


def _norm_router(x1, nw, mod, router_w, router_b, S, shift_idx, scale_idx):
    T, D = x1.shape
    E = router_w.shape[1]
    tm = min(256, S)
    outs = pl.pallas_call(
        functools.partial(_norm_router_kernel, shift_idx=shift_idx, scale_idx=scale_idx),
        out_shape=(
            jax.ShapeDtypeStruct((T, D // 2), U32),
            jax.ShapeDtypeStruct((TOP_K, T), I32),
            jax.ShapeDtypeStruct((TOP_K, T), F32),
            jax.ShapeDtypeStruct((TOP_K, T), I32),
            jax.ShapeDtypeStruct((E, 128), F32),
        ),
        grid=(T // tm,),
        in_specs=[
            pl.BlockSpec((tm, D), lambda i: (i, 0)),
            pl.BlockSpec((1, D), lambda i: (0, 0)),
            pl.BlockSpec((None, N_ADA, D), lambda i: ((i * tm) // S, 0, 0)),
            pl.BlockSpec((E, D), lambda i: (0, 0)),
            pl.BlockSpec((E, 1), lambda i: (0, 0)),
        ],
        out_specs=(
            pl.BlockSpec((tm, D // 2), lambda i: (i, 0)),
            pl.BlockSpec((TOP_K, tm), lambda i: (0, i)),
            pl.BlockSpec((TOP_K, tm), lambda i: (0, i)),
            pl.BlockSpec((TOP_K, tm), lambda i: (0, i)),
            pl.BlockSpec((E, 128), lambda i: (0, 0)),
        ),
        scratch_shapes=[pltpu.VMEM((E, 128), F32)],
        compiler_params=_cparams(("arbitrary",)),
        name="norm_router",
    )(x1, nw.reshape(1, D), mod, router_w.T.astype(BF16), router_b.reshape(E, 1).astype(F32))
    return outs


def _dispatch_kernel(pos_ref, pad_ref, hp_ref, xs_ref, zeros, sem, zsem, *, tm, T, E, n_tiles):
    base = pl.program_id(0) * tm

    @pl.when(pl.program_id(0) == 0)
    def _():
        zeros[...] = jnp.zeros_like(zeros)
        def pad_copy(row):
            return pltpu.make_async_copy(zeros.at[pl.ds(0, 1)], xs_ref.at[pl.ds(row, 1)], zsem)

        def tail_copy(t):
            return pltpu.make_async_copy(
                zeros, xs_ref.at[pl.ds(pl.multiple_of(t * MOE_TILE, MOE_TILE), MOE_TILE)], zsem)

        def pad_start(e, carry):
            start = pad_ref[e]
            lax.fori_loop(0, pad_ref[E + e], lambda r, c: (pad_copy(start + r).start(), c)[1], 0)
            return carry

        def pad_wait(e, carry):
            start = pad_ref[e]
            lax.fori_loop(0, pad_ref[E + e], lambda r, c: (pad_copy(start + r).wait(), c)[1], 0)
            return carry

        def tail_start(t, carry):
            tail_copy(t).start()
            return carry

        def tail_wait(t, carry):
            tail_copy(t).wait()
            return carry

        lax.fori_loop(0, E, pad_start, 0)
        lax.fori_loop(pad_ref[2 * E], n_tiles, tail_start, 0)
        lax.fori_loop(0, E, pad_wait, 0)
        lax.fori_loop(pad_ref[2 * E], n_tiles, tail_wait, 0)

    def start_body(t, carry):
        for k in range(TOP_K):
            dst = pos_ref[k * T + base + t]
            pltpu.make_async_copy(hp_ref.at[pl.ds(t, 1)], xs_ref.at[pl.ds(dst, 1)], sem).start(priority=k % 2)
        return carry

    lax.fori_loop(0, tm, start_body, 0, unroll=8)
    for k in range(TOP_K):
        pltpu.make_async_copy(hp_ref, xs_ref.at[pl.ds(0, tm)], sem).wait()


def _dispatch(pos_flat, pad_info, hp, n_tiles, E):
    T, W = hp.shape
    tm = min(512, T)
    grid_spec = pltpu.PrefetchScalarGridSpec(
        num_scalar_prefetch=2,
        grid=(T // tm,),
        in_specs=[pl.BlockSpec((tm, W), lambda i, p, q: (i, 0))],
        out_specs=pl.BlockSpec(memory_space=pl.ANY),
        scratch_shapes=[pltpu.VMEM((MOE_TILE, W), U32), pltpu.SemaphoreType.DMA, pltpu.SemaphoreType.DMA],
    )
    return pl.pallas_call(
        functools.partial(_dispatch_kernel, tm=tm, T=T, E=E, n_tiles=n_tiles),
        out_shape=jax.ShapeDtypeStruct((n_tiles * MOE_TILE, W), U32),
        grid_spec=grid_spec,
        compiler_params=_cparams(("arbitrary",)),
        name="dispatch",
    )(pos_flat, pad_info, hp)


def _expert_changed(te_ref, i):
    return jnp.logical_or(i == 0, te_ref[i] != te_ref[jnp.maximum(i - 1, 0)])


def _cast_rows(src_ref, dst_ref, rows_per_pass=256):
    def body(r, carry):
        rows = pl.ds(pl.multiple_of(r * rows_per_pass, rows_per_pass), rows_per_pass)
        dst_ref[rows, :] = src_ref[rows, :].astype(dst_ref.dtype)
        return carry

    lax.fori_loop(0, src_ref.shape[0] // rows_per_pass, body, 0)


def _expert_weight_stage(te_ref, nx_ref, i, first_step, n, n_passes, copies, slot_ref, consume):
    @pl.when(_expert_changed(te_ref, i))
    def _():
        @pl.when(first_step)
        def _():
            slot_ref[0] = 0
            for cp in copies(te_ref[0], 0, 0):
                cp.start()

        slot = slot_ref[0]
        for cp in copies(te_ref[i], n, slot):
            cp.wait()
        consume(slot)
        same_pass = nx_ref[i] >= 0
        nxt_e = jnp.where(same_pass, nx_ref[i], te_ref[0])
        nxt_n = jnp.where(same_pass, n, n + 1)

        @pl.when(jnp.logical_or(same_pass, n + 1 < n_passes))
        def _():
            for cp in copies(nxt_e, nxt_n, 1 - slot):
                cp.start()

        slot_ref[0] = 1 - slot


def _expert_up_kernel(te_ref, nu_ref, nx_ref, xs_ref, w1_hbm, bg_ref, bu_ref, o_ref, stage, wgb, wub, slot_ref, sem,
                      *, tn, F, nf):
    n = pl.program_id(0)
    i = pl.program_id(1)

    def copies(e, nn, slot):
        c0 = pl.multiple_of(nn * tn, tn)
        return (pltpu.make_async_copy(w1_hbm.at[e, :, pl.ds(c0, tn)], stage.at[slot, 0], sem.at[slot]),
                pltpu.make_async_copy(w1_hbm.at[e, :, pl.ds(F + c0, tn)], stage.at[slot, 1], sem.at[slot]))

    def consume(slot):
        _cast_rows(stage.at[slot, 0], wgb)
        _cast_rows(stage.at[slot, 1], wub)

    _expert_weight_stage(te_ref, nx_ref, i, jnp.logical_and(n == 0, i == 0), n, nf, copies, slot_ref, consume)

    @pl.when(i < nu_ref[0])
    def _():
        x = _unpack_bf16_pairs(xs_ref[...]).astype(BF16)
        gate = jnp.minimum(_dot(x, wgb[...]) + bg_ref[...], SWIGLU_LIMIT)
        up = jnp.clip(_dot(x, wub[...]) + bu_ref[...], -SWIGLU_LIMIT, SWIGLU_LIMIT)
        act = gate * _sigmoid(SWIGLU_ALPHA * gate) * (up + 1.0)
        o_ref[...] = act.astype(o_ref.dtype)

    @pl.when(i >= nu_ref[0])
    def _():
        o_ref[...] = jnp.zeros_like(o_ref)


def _expert_down_kernel(te_ref, nu_ref, nx_ref, a_ref, w2_hbm, b2_ref, o_ref, stage, w2b, slot_ref, sem):
    i = pl.program_id(0)

    def copies(e, nn, slot):
        del nn
        return (pltpu.make_async_copy(w2_hbm.at[e], stage.at[slot], sem.at[slot]),)

    def consume(slot):
        _cast_rows(stage.at[slot], w2b)

    _expert_weight_stage(te_ref, nx_ref, i, i == 0, 0, 1, copies, slot_ref, consume)

    @pl.when(i < nu_ref[0])
    def _():
        y = _dot(a_ref[...], w2b[...]) + b2_ref[...]
        o_ref[...] = _pack_bf16_pairs(y)

    @pl.when(i >= nu_ref[0])
    def _():
        o_ref[...] = jnp.zeros_like(o_ref)


def _expert_ffn(xs, tile_e, n_used, next_e, w1, b1, w2, b2):
    R, W = xs.shape
    E, D, F2 = w1.shape
    F = F2 // 2
    tm = MOE_TILE
    nt = R // tm

    tn = min(512, F)
    nf = F // tn
    b1r = b1.reshape(E, 1, F2)

    act = pl.pallas_call(
        functools.partial(_expert_up_kernel, tn=tn, F=F, nf=nf),
        out_shape=jax.ShapeDtypeStruct((R, F), BF16),
        grid_spec=pltpu.PrefetchScalarGridSpec(
            num_scalar_prefetch=3,
            grid=(nf, nt),
            in_specs=[
                pl.BlockSpec((tm, W), lambda n, i, te, nu, nx: (jnp.minimum(i, nu[0] - 1), 0)),
                pl.BlockSpec(memory_space=pl.ANY),
                pl.BlockSpec((None, 1, tn), lambda n, i, te, nu, nx: (te[i], 0, n)),
                pl.BlockSpec((None, 1, tn), lambda n, i, te, nu, nx: (te[i], 0, nf + n)),
            ],
            out_specs=pl.BlockSpec((tm, tn), lambda n, i, te, nu, nx: (i, n)),
            scratch_shapes=[pltpu.VMEM((2, 2, D, tn), F32), pltpu.VMEM((D, tn), BF16), pltpu.VMEM((D, tn), BF16),
                            pltpu.SMEM((1,), I32), pltpu.SemaphoreType.DMA((2,))],
        ),
        compiler_params=_cparams(("arbitrary", "arbitrary")),
        name="expert_up",
    )(tile_e, n_used, next_e, xs, w1, b1r, b1r)
    y = pl.pallas_call(
        _expert_down_kernel,
        out_shape=jax.ShapeDtypeStruct((R, D // 2), U32),
        grid_spec=pltpu.PrefetchScalarGridSpec(
            num_scalar_prefetch=3,
            grid=(nt,),
            in_specs=[
                pl.BlockSpec((tm, F), lambda i, te, nu, nx: (jnp.minimum(i, nu[0] - 1), 0)),
                pl.BlockSpec(memory_space=pl.ANY),
                pl.BlockSpec((None, 1, D), lambda i, te, nu, nx: (te[i], 0, 0)),
            ],
            out_specs=pl.BlockSpec((tm, D // 2), lambda i, te, nu, nx: (i, 0)),
            scratch_shapes=[pltpu.VMEM((2, F, D), F32), pltpu.VMEM((F, D), BF16),
                            pltpu.SMEM((1,), I32), pltpu.SemaphoreType.DMA((2,))],
        ),
        compiler_params=_cparams(("arbitrary",)),
        name="expert_down",
    )(tile_e, n_used, next_e, act, w2, b2.reshape(E, 1, D))
    return y


def _combine_kernel(pos_ref, x_ref, w_ref, mod_ref, y_ref, o_ref, buf, sem, *, tm, T, gate_idx):
    i = pl.program_id(0)
    n = pl.num_programs(0)
    W = buf.shape[-1]

    def issue_row(step, slot, t):
        for k in range(TOP_K):
            src = pos_ref[k * T + step * tm + t]
            pltpu.make_async_copy(y_ref.at[pl.ds(src, 1)], buf.at[slot, k, pl.ds(t, 1)],
                                  sem.at[slot]).start(priority=k % 2)

    def wait_block(slot):
        for k in range(TOP_K):
            pltpu.make_async_copy(y_ref.at[pl.ds(0, tm)], buf.at[slot, k], sem.at[slot]).wait()

    @pl.when(i == 0)
    def _():
        lax.fori_loop(0, tm, lambda t, c: (issue_row(0, 0, t), c)[1], 0, unroll=8)

    slot = i % 2
    wait_block(slot)
    nxt = jnp.minimum(i + 1, n - 1)
    rc = 8
    cw = min(512, W)

    def rows_body(r, carry):
        rows = pl.ds(pl.multiple_of(r * rc, rc), rc)
        wv = w_ref[rows, :]
        wk = [jnp.broadcast_to(wv[:, k:k + 1], (rc, cw)) for k in range(TOP_K)]
        for c in range(W // cw):
            lo = hi = None
            for k in range(TOP_K):
                u = buf[slot, k, rows, c * cw:(c + 1) * cw]
                l = wk[k] * lax.bitcast_convert_type(jnp.left_shift(u, jnp.uint32(16)), F32)
                h = wk[k] * lax.bitcast_convert_type(jnp.bitwise_and(u, jnp.uint32(0xFFFF0000)), F32)
                lo = l if lo is None else lo + l
                hi = h if hi is None else hi + h
            for half, acc in ((0, lo), (1, hi)):
                cols = slice(half * W + c * cw, half * W + (c + 1) * cw)
                o_ref[rows, cols] = x_ref[rows, cols] + mod_ref[gate_idx:gate_idx + 1, cols] * acc
        for t in range(rc):
            issue_row(nxt, 1 - slot, r * rc + t)
        return carry

    lax.fori_loop(0, tm // rc, rows_body, 0)

    @pl.when(i == n - 1)
    def _():
        wait_block(1 - slot)


def _combine(pos_flat, x1, w_tk, mod, y, S, gate_idx):
    T, D = x1.shape
    tm = min(128, S)
    grid_spec = pltpu.PrefetchScalarGridSpec(
        num_scalar_prefetch=1,
        grid=(T // tm,),
        in_specs=[
            pl.BlockSpec((tm, D), lambda i, p: (i, 0)),
            pl.BlockSpec((tm, TOP_K), lambda i, p: (i, 0)),
            pl.BlockSpec((None, N_ADA, D), lambda i, p: ((i * tm) // S, 0, 0)),
            pl.BlockSpec(memory_space=pl.ANY),
        ],
        out_specs=pl.BlockSpec((tm, D), lambda i, p: (i, 0)),
        scratch_shapes=[pltpu.VMEM((2, TOP_K, tm, D // 2), U32), pltpu.SemaphoreType.DMA((2,))],
    )
    return pl.pallas_call(
        functools.partial(_combine_kernel, tm=tm, T=T, gate_idx=gate_idx),
        out_shape=jax.ShapeDtypeStruct((T, D), F32),
        grid_spec=grid_spec,
        compiler_params=_cparams(("arbitrary",)),
        name="combine",
    )(pos_flat, x1, w_tk, mod, y)


def _layer(x2, mod, B, S, rel_bias, norm_mix_w, w_in, q_norm_w, k_norm_w, attn_sink, ret_decay_fwd, ret_decay_bwd,
           ret_gn_w, ret_gn_b, w_up_attn, w_up_ret, w_out, norm_ffn_w, router_w, router_b,
           expert_w1, expert_b1, expert_w2, expert_b2):
    T, D = x2.shape
    E = router_w.shape[1]
    h = _norm_mod(x2, norm_mix_w, mod, S, 0, 1)
    proj = _in_proj(h, w_in, q_norm_w, k_norm_w)
    attn = _window_attention(proj, _attn_bias_table(rel_bias), attn_sink, B, S)
    ret = _retention(proj, ret_decay_fwd, ret_decay_bwd, ret_gn_w, ret_gn_b, B, S)
    merged = _merge_up(attn, ret, w_up_attn, w_up_ret, proj, D)
    x1 = _out_proj(merged, w_out, x2, mod, S, 2)

    hp, top_e, top_w, rank, cnt = _norm_router(x1, norm_ffn_w, mod, router_w, router_b, S, 3, 4)
    tm = MOE_TILE
    counts = cnt[:, 0].astype(I32)
    tiles_per_e = (counts + tm - 1) // tm
    tile_end = jnp.cumsum(tiles_per_e)
    group_start = (tile_end - tiles_per_e) * tm
    n_tiles = (T * TOP_K) // tm + E
    n_used = tile_end[-1]
    tile_ids = jnp.minimum(jnp.arange(n_tiles, dtype=I32), n_used - 1)
    tile_e = jnp.minimum(jnp.sum((tile_ids[:, None] >= tile_end[None, :]).astype(I32), axis=1), E - 1)
    e_ids = jnp.arange(E, dtype=I32)
    later = jnp.logical_and(e_ids[None, :] > e_ids[:, None], tiles_per_e[None, :] > 0)
    next_of_e = jnp.min(jnp.where(later, e_ids[None, :], E), axis=1)
    next_of_e = jnp.where(next_of_e == E, -1, next_of_e)
    next_e = jnp.sum(jnp.where(tile_e[:, None] == e_ids, next_of_e, 0), axis=1).astype(I32)
    start_of = jnp.sum(jnp.where(top_e[:, :, None] == e_ids, group_start, 0), axis=-1)
    pos_flat = (start_of + rank).astype(I32).reshape(-1)
    pad_info = jnp.concatenate([group_start + counts, tiles_per_e * tm - counts, n_used.reshape(1)]).astype(I32)

    xs = _dispatch(pos_flat, pad_info, hp, n_tiles, E)
    y = _expert_ffn(xs, tile_e, n_used.reshape(1).astype(I32), next_e, expert_w1, expert_b1, expert_w2, expert_b2)
    return _combine(pos_flat, x1, top_w.T, mod, y, S, 5)


def kernel(x, c, rel_bias, ada_w, ada_b, norm_mix_w, w_in, q_norm_w, k_norm_w, attn_sink, ret_decay_fwd,
           ret_decay_bwd, ret_gn_w, ret_gn_b, w_up_attn, w_up_ret, w_out, norm_ffn_w, router_w, router_b,
           expert_w1, expert_b1, expert_w2, expert_b2):
    B, S, D = x.shape
    x2 = x.reshape(B * S, D)
    for l in range(ada_w.shape[0]):
        mod = _ada_mod(c, ada_w[l], ada_b[l])
        x2 = _layer(x2, mod, B, S, rel_bias, norm_mix_w[l], w_in[l], q_norm_w[l], k_norm_w[l], attn_sink[l],
                    ret_decay_fwd[l], ret_decay_bwd[l], ret_gn_w[l], ret_gn_b[l], w_up_attn[l], w_up_ret[l],
                    w_out[l], norm_ffn_w[l], router_w[l], router_b[l], expert_w1[l], expert_b1[l],
                    expert_w2[l], expert_b2[l])
    return x2.reshape(B, S, D)
```

```python
import functools
import math

import jax
import jax.numpy as jnp
from jax import lax
from jax.experimental import pallas as pl
from jax.experimental.pallas import tpu as pltpu

F32 = jnp.float32
BF16 = jnp.bfloat16
U32 = jnp.uint32
I32 = jnp.int32

ATTN_HEADS = 16
ATTN_KV_HEADS = 4
ATTN_HEAD_DIM = 128
WINDOW = 128
ATTN_BLOCK = 128
N_BUCKETS = 32
MAX_DISTANCE = 128
RET_HEADS = 8
RET_QK_DIM = 128
RET_V_DIM = 256
RET_CHUNK = 128
ROPE_BASE = 10000.0
TOP_K = 4
SWIGLU_LIMIT = 7.0
SWIGLU_ALPHA = 1.702
N_ADA = 6
EPS = 1e-6
NEG_INF = -1e30
NEG_BIG = -3.0e38

ATTN_Q_W = ATTN_HEADS * ATTN_HEAD_DIM
ATTN_KV_W = ATTN_KV_HEADS * ATTN_HEAD_DIM
RET_QK_W = RET_HEADS * RET_QK_DIM
RET_V_W = RET_HEADS * RET_V_DIM

VMEM_LIMIT_BYTES = 56 * 1024 * 1024
MOE_TILE = 256
RET_TILE = 256


def _cparams(sem):
    return pltpu.CompilerParams(dimension_semantics=sem, vmem_limit_bytes=VMEM_LIMIT_BYTES)


def _dot(a, b):
    return jnp.dot(a, b, preferred_element_type=F32)


def _dot_nt(a, b):
    return lax.dot_general(a, b, (((1,), (1,)), ((), ())), preferred_element_type=F32)


def _dot_tn(a, b):
    return lax.dot_general(a, b, (((0,), (0,)), ((), ())), preferred_element_type=F32)


def _sigmoid(x):
    return 1.0 / (1.0 + jnp.exp(-x))


def _ada_kernel(c_ref, w_ref, b_ref, o_ref):
    c = c_ref[...]
    cs = (c * _sigmoid(c)).astype(BF16)
    o_ref[...] = _dot(cs, w_ref[...].astype(BF16)) + b_ref[...]


def _ada_mod(c, ada_w, ada_b):
    B, D = c.shape
    N = ada_w.shape[1]
    rows = 8
    cp = jnp.zeros((rows, D), F32).at[:B].set(c)
    tn = min(512, N)
    out = pl.pallas_call(
        _ada_kernel,
        out_shape=jax.ShapeDtypeStruct((rows, N), F32),
        grid=(N // tn,),
        in_specs=[
            pl.BlockSpec((rows, D), lambda j: (0, 0)),
            pl.BlockSpec((D, tn), lambda j: (0, j)),
            pl.BlockSpec((1, tn), lambda j: (0, j)),
        ],
        out_specs=pl.BlockSpec((rows, tn), lambda j: (0, j)),
        compiler_params=_cparams(("arbitrary",)),
        name="ada_mod",
    )(cp, ada_w, ada_b.reshape(1, N))
    return out[:B].reshape(B, N_ADA, D)


def _norm_mod_value(x, nw, shift, scale):
    ms = jnp.mean(x * x, axis=-1, keepdims=True)
    h = x * lax.rsqrt(ms + EPS) * nw
    return h * (1.0 + scale) + shift


def _norm_mod_kernel(x_ref, nw_ref, mod_ref, o_ref, *, shift_idx, scale_idx):
    h = _norm_mod_value(x_ref[...], nw_ref[...], mod_ref[shift_idx:shift_idx + 1, :],
                        mod_ref[scale_idx:scale_idx + 1, :])
    o_ref[...] = h.astype(BF16)


def _norm_mod(x2, nw, mod, S, shift_idx, scale_idx):
    T, D = x2.shape
    tm = min(256, S)
    return pl.pallas_call(
        functools.partial(_norm_mod_kernel, shift_idx=shift_idx, scale_idx=scale_idx),
        out_shape=jax.ShapeDtypeStruct((T, D), BF16),
        grid=(T // tm,),
        in_specs=[
            pl.BlockSpec((tm, D), lambda i: (i, 0)),
            pl.BlockSpec((1, D), lambda i: (0, 0)),
            pl.BlockSpec((None, N_ADA, D), lambda i: ((i * tm) // S, 0, 0)),
        ],
        out_specs=pl.BlockSpec((tm, D), lambda i: (i, 0)),
        compiler_params=_cparams(("arbitrary",)),
        name="norm_mod",
    )(x2, nw.reshape(1, D), mod)


def _cast_weight_once(w_ref, wb_ref):
    @pl.when(pl.program_id(1) == 0)
    def _():
        wb_ref[...] = w_ref[...].astype(BF16)


def _in_proj_kernel(a_ref, w_hbm, cw_ref, cf_ref, bd_ref, o_ref, stage, wb, sem, *, n_norm_tiles):
    j = pl.program_id(0)
    tn = wb.shape[1]

    def w_copy(jj):
        return pltpu.make_async_copy(w_hbm.at[:, pl.ds(pl.multiple_of(jj * tn, tn), tn)], stage, sem)

    @pl.when(pl.program_id(1) == 0)
    def _():
        @pl.when(j == 0)
        def _():
            w_copy(0).start()

        w_copy(j).wait()
        _cast_rows(stage, wb)

        @pl.when(j + 1 < pl.num_programs(0))
        def _():
            w_copy(j + 1).start()

    @pl.when(j < n_norm_tiles)
    def _():
        cb = bd_ref.shape[0]
        for c in range(tn // cb):
            cols = slice(c * cb, (c + 1) * cb)
            acc = _dot(a_ref[...], wb[:, cols])
            ms = _dot((acc * acc).astype(BF16), bd_ref[...]) * (1.0 / ATTN_HEAD_DIM)
            inv = jnp.where(cf_ref[:, cols] > 0.0, lax.rsqrt(ms + EPS), 1.0)
            o_ref[:, cols] = (acc * inv * cw_ref[:, cols]).astype(o_ref.dtype)

    @pl.when(j >= n_norm_tiles)
    def _():
        o_ref[...] = _dot(a_ref[...], wb[...]).astype(o_ref.dtype)


def _in_proj(h, w, q_norm_w, k_norm_w):
    T, K = h.shape
    N = w.shape[1]
    tm = min(1024, T)
    tn = 1024 if N % 1024 == 0 else 512
    hd = ATTN_HEAD_DIM
    n_norm = ATTN_Q_W + ATTN_KV_W
    n_norm_tiles = -(-n_norm // tn)
    pad = n_norm_tiles * tn - n_norm
    col_w = jnp.concatenate([jnp.tile(q_norm_w.astype(F32) * (hd ** -0.5), ATTN_HEADS),
                             jnp.tile(k_norm_w.astype(F32), ATTN_KV_HEADS), jnp.ones((pad,), F32)]).reshape(1, -1)
    col_flag = jnp.concatenate([jnp.ones((n_norm,), F32), jnp.zeros((pad,), F32)]).reshape(1, -1)
    cb = min(512, tn)
    lane_head = jnp.arange(cb, dtype=I32) // hd
    block_ones = (lane_head[:, None] == lane_head[None, :]).astype(BF16)
    last = n_norm_tiles - 1
    return pl.pallas_call(
        functools.partial(_in_proj_kernel, n_norm_tiles=n_norm_tiles),
        out_shape=jax.ShapeDtypeStruct((T, N), BF16),
        grid=(N // tn, T // tm),
        in_specs=[
            pl.BlockSpec((tm, K), lambda j, i: (i, 0)),
            pl.BlockSpec(memory_space=pl.ANY),
            pl.BlockSpec((1, tn), lambda j, i: (0, jnp.minimum(j, last))),
            pl.BlockSpec((1, tn), lambda j, i: (0, jnp.minimum(j, last))),
            pl.BlockSpec((cb, cb), lambda j, i: (0, 0)),
        ],
        out_specs=pl.BlockSpec((tm, tn), lambda j, i: (i, j)),
        scratch_shapes=[pltpu.VMEM((K, tn), F32), pltpu.VMEM((K, tn), BF16), pltpu.SemaphoreType.DMA],
        compiler_params=_cparams(("arbitrary", "arbitrary")),
        name="in_proj",
    )(h, w, col_w, col_flag, block_ones)


def _t5_bucket(rel):
    nb = N_BUCKETS // 2
    max_exact = nb // 2
    base = jnp.where(rel > 0, nb, 0)
    n = jnp.abs(rel)
    nf = jnp.maximum(n, 1).astype(F32)
    large = max_exact + (jnp.log(nf / max_exact) / math.log(MAX_DISTANCE / max_exact) * (nb - max_exact)).astype(I32)
    large = jnp.minimum(large, nb - 1)
    return base + jnp.where(n < max_exact, n, large)


def _attn_bias_table(rel_bias):
    blk = ATTN_BLOCK
    qi = jnp.arange(blk, dtype=I32)[:, None]
    kj = jnp.arange(3 * blk, dtype=I32)[None, :]
    rel = kj - blk - qi
    bucket = _t5_bucket(rel)
    table = rel_bias.astype(F32).T
    bias = jnp.zeros((table.shape[0],) + bucket.shape, F32)
    for b in range(N_BUCKETS):
        bias = jnp.where(bucket[None] == b, table[:, b][:, None, None], bias)
    bias = jnp.where((jnp.abs(rel) <= WINDOW)[None], bias, NEG_INF)
    G = ATTN_HEADS // ATTN_KV_HEADS
    bias = bias.reshape(ATTN_KV_HEADS, G, blk, 3 * blk)
    return jnp.transpose(bias, (0, 3, 1, 2)).reshape(ATTN_KV_HEADS, 3 * blk, G * blk)


def _attn_kernel(sink_ref, q_ref, kp_ref, kc_ref, kn_ref, vp_ref, vc_ref, vn_ref, bias_ref, o_ref):
    n = pl.program_id(1)
    nb = pl.num_programs(1)
    blk, hd = ATTN_BLOCK, ATTN_HEAD_DIM
    G = ATTN_HEADS // ATTN_KV_HEADS
    row = lax.broadcasted_iota(I32, (3 * blk, 1), 0)
    oob = jnp.logical_or(jnp.logical_and(row < blk, n == 0), jnp.logical_and(row >= 2 * blk, n == nb - 1))
    for hk in range(ATTN_KV_HEADS):
        sl = slice(hk * hd, (hk + 1) * hd)
        k3 = jnp.concatenate([kp_ref[:, sl], kc_ref[:, sl], kn_ref[:, sl]], axis=0)
        v3 = jnp.concatenate([vp_ref[:, sl], vc_ref[:, sl], vn_ref[:, sl]], axis=0)
        q4 = jnp.concatenate([q_ref[:, (hk * G + g) * hd:(hk * G + g + 1) * hd] for g in range(G)],
                             axis=0)
        sink = jnp.concatenate([jnp.full((1, blk), sink_ref[hk * G + g], F32) for g in range(G)], axis=1)
        logits = _dot_nt(k3, q4) + bias_ref[hk]
        logits = jnp.where(oob, NEG_INF, logits)
        m = jnp.maximum(jnp.max(logits, axis=0, keepdims=True), sink)
        p = jnp.exp(logits - m)
        denom = jnp.sum(p, axis=0, keepdims=True) + jnp.exp(sink - m)
        out_t = _dot_tn(v3, p.astype(BF16)) * (1.0 / denom)
        out = out_t.T
        for g in range(G):
            h = hk * G + g
            o_ref[:, h * hd:(h + 1) * hd] = out[g * blk:(g + 1) * blk].astype(o_ref.dtype)


def _window_attention(proj, bias_tab, sink, B, S):
    T = proj.shape[0]
    blk = ATTN_BLOCK
    nb = S // blk
    kcol = ATTN_Q_W // ATTN_KV_W
    vcol = kcol + 1

    def kv_spec(col, off):
        return pl.BlockSpec((blk, ATTN_KV_W), lambda b, n, s: (b * nb + jnp.clip(n + off, 0, nb - 1), col))

    grid_spec = pltpu.PrefetchScalarGridSpec(
        num_scalar_prefetch=1,
        grid=(B, nb),
        in_specs=[
            pl.BlockSpec((blk, ATTN_Q_W), lambda b, n, s: (b * nb + n, 0)),
            kv_spec(kcol, -1), kv_spec(kcol, 0), kv_spec(kcol, 1),
            kv_spec(vcol, -1), kv_spec(vcol, 0), kv_spec(vcol, 1),
            pl.BlockSpec((ATTN_KV_HEADS, 3 * blk, (ATTN_HEADS // ATTN_KV_HEADS) * blk), lambda b, n, s: (0, 0, 0)),
        ],
        out_specs=pl.BlockSpec((blk, ATTN_Q_W), lambda b, n, s: (b * nb + n, 0)),
    )
    return pl.pallas_call(
        _attn_kernel,
        out_shape=jax.ShapeDtypeStruct((T, ATTN_Q_W), BF16),
        grid_spec=grid_spec,
        compiler_params=_cparams(("arbitrary", "arbitrary")),
        name="window_attn",
    )(sink.astype(F32), proj, proj, proj, proj, proj, proj, proj, bias_tab)


def _rope_tables(S):
    d = RET_QK_DIM
    inv = ROPE_BASE ** (-jnp.arange(0, d, 2, dtype=F32) / d)
    ang = jnp.arange(S, dtype=F32)[:, None] * inv[None, :]
    cos, sin = jnp.cos(ang), jnp.sin(ang)
    return jnp.concatenate([cos, cos], axis=-1), jnp.concatenate([-sin, sin], axis=-1)


def _ret_kernel(df_ref, db_ref, q_ref, k_ref, v_ref, g_ref, cos_ref, sin_ref, gw_ref, gb_ref, o_ref,
                qb, kb, q2, kv, sprev, *, S, C):
    h = pl.program_id(1)
    nc = S // C
    dk = RET_QK_DIM
    half = dk // 2

    rowf = lax.broadcasted_iota(I32, (C, 1), 0).astype(F32)
    lg_f = -jnp.exp(jnp.full((1, 1), df_ref[h], F32))
    lg_b = -jnp.exp(jnp.full((1, 1), db_ref[h], F32))
    qdec_f = jnp.exp((rowf + 1.0) * lg_f)
    kdec_f = jnp.exp((C - 1.0 - rowf) * lg_f)
    qdec_b = jnp.exp((C - rowf) * lg_b)
    kdec_b = jnp.exp(rowf * lg_b)
    cd_f = jnp.exp(C * lg_f)
    cd_b = jnp.exp(C * lg_b)

    def a_body(n, carry):
        rows = pl.ds(pl.multiple_of(n * C, C), C)
        co = cos_ref[rows, :]
        si = sin_ref[rows, :]
        q = q_ref[rows, :].astype(F32)
        k = k_ref[rows, :].astype(F32)
        qr = q * co + pltpu.roll(q, half, 1) * si
        kr = (k * co + pltpu.roll(k, half, 1) * si) * (dk ** -0.5)
        qb[rows, :] = qr.astype(BF16)
        kb[rows, :] = kr.astype(BF16)
        q2[rows, :] = jnp.concatenate([qr * qdec_f, qr * qdec_b], axis=1).astype(BF16)
        k2 = jnp.concatenate([kr * kdec_f, kr * kdec_b], axis=1).astype(BF16)
        kv[n] = _dot_tn(k2, v_ref[rows, :])
        return carry

    lax.fori_loop(0, nc, a_body, 0, unroll=4)

    def scan_f(n, state):
        sprev[n, 0:dk, :] = state.astype(BF16)
        return state * cd_f + kv[n, 0:dk, :]

    lax.fori_loop(0, nc, scan_f, jnp.zeros((dk, RET_V_DIM), F32))

    def scan_b(t, state):
        n = nc - 1 - t
        sprev[n, dk:2 * dk, :] = state.astype(BF16)
        return state * cd_b + kv[n, dk:2 * dk, :]

    lax.fori_loop(0, nc, scan_b, jnp.zeros((dk, RET_V_DIM), F32))

    ri = lax.broadcasted_iota(I32, (C, C), 0)
    ci = lax.broadcasted_iota(I32, (C, C), 1)
    d = (ri - ci).astype(F32)
    dec = jnp.where(ri >= ci, jnp.exp(jnp.maximum(d, 0.0) * lg_f), jnp.exp(jnp.maximum(-d, 0.0) * lg_b))
    gw = gw_ref[...]
    gb = gb_ref[...]

    def c_body(n, carry):
        rows = pl.ds(pl.multiple_of(n * C, C), C)
        scores = _dot_nt(qb[rows, :], kb[rows, :]) * dec
        y = _dot(scores.astype(BF16), v_ref[rows, :]) + _dot(q2[rows, :], sprev[n])
        mu = jnp.mean(y, axis=-1, keepdims=True)
        yc = y - mu
        var = jnp.mean(yc * yc, axis=-1, keepdims=True)
        yn = yc * lax.rsqrt(var + EPS) * gw + gb
        g = g_ref[rows, :].astype(F32)
        o_ref[rows, :] = (g * _sigmoid(g) * yn).astype(o_ref.dtype)
        return carry

    lax.fori_loop(0, nc, c_body, 0, unroll=4)


def _retention(proj, decay_fwd, decay_bwd, gn_w, gn_b, B, S):
    T = proj.shape[0]
    dk, dv = RET_QK_DIM, RET_V_DIM
    q_off = (ATTN_Q_W + 2 * ATTN_KV_W) // dk
    k_off = q_off + RET_QK_W // dk
    v_off = (ATTN_Q_W + 2 * ATTN_KV_W + 2 * RET_QK_W) // dv
    g_off = v_off + RET_V_W // dv
    cos, sin = _rope_tables(S)
    C = min(RET_TILE, S)
    grid_spec = pltpu.PrefetchScalarGridSpec(
        num_scalar_prefetch=2,
        grid=(B, RET_HEADS),
        in_specs=[
            pl.BlockSpec((S, dk), lambda b, h, *_: (b, q_off + h)),
            pl.BlockSpec((S, dk), lambda b, h, *_: (b, k_off + h)),
            pl.BlockSpec((S, dv), lambda b, h, *_: (b, v_off + h)),
            pl.BlockSpec((S, dv), lambda b, h, *_: (b, g_off + h)),
            pl.BlockSpec((S, dk), lambda b, h, *_: (0, 0)),
            pl.BlockSpec((S, dk), lambda b, h, *_: (0, 0)),
            pl.BlockSpec((1, dv), lambda b, h, *_: (0, h)),
            pl.BlockSpec((1, dv), lambda b, h, *_: (0, h)),
        ],
        out_specs=pl.BlockSpec((S, dv), lambda b, h, *_: (b, h)),
        scratch_shapes=[
            pltpu.VMEM((S, dk), BF16),
            pltpu.VMEM((S, dk), BF16),
            pltpu.VMEM((S, 2 * dk), BF16),
            pltpu.VMEM((S // C, 2 * dk, dv), F32),
            pltpu.VMEM((S // C, 2 * dk, dv), BF16),
        ],
    )
    return pl.pallas_call(
        functools.partial(_ret_kernel, S=S, C=C),
        out_shape=jax.ShapeDtypeStruct((T, RET_V_W), BF16),
        grid_spec=grid_spec,
        compiler_params=_cparams(("arbitrary", "arbitrary")),
        name="retention",
    )(decay_fwd.astype(F32), decay_bwd.astype(F32), proj, proj, proj, proj, cos, sin,
      gn_w.reshape(1, -1), gn_b.reshape(1, -1))


def _merge_kernel(a_ref, r_ref, wa_ref, wr_ref, ga_ref, gr_ref, o_ref, wab, wrb):
    _cast_weight_once(wa_ref, wab)
    _cast_weight_once(wr_ref, wrb)
    ua = _dot(a_ref[...], wab[...])
    ur = _dot(r_ref[...], wrb[...])
    out = _sigmoid(ga_ref[...].astype(F32)) * ua + _sigmoid(gr_ref[...].astype(F32)) * ur
    o_ref[...] = out.astype(o_ref.dtype)


def _merge_up(attn, ret, wa, wr, proj, D):
    T = attn.shape[0]
    tm = min(1024, T)
    tn = min(512, D)
    ga_off = (ATTN_Q_W + 2 * ATTN_KV_W + 2 * RET_QK_W + 2 * RET_V_W) // tn
    gr_off = ga_off + D // tn
    return pl.pallas_call(
        _merge_kernel,
        out_shape=jax.ShapeDtypeStruct((T, D), BF16),
        grid=(D // tn, T // tm),
        in_specs=[
            pl.BlockSpec((tm, ATTN_Q_W), lambda j, i: (i, 0)),
            pl.BlockSpec((tm, RET_V_W), lambda j, i: (i, 0)),
            pl.BlockSpec((ATTN_Q_W, tn), lambda j, i: (0, j)),
            pl.BlockSpec((RET_V_W, tn), lambda j, i: (0, j)),
            pl.BlockSpec((tm, tn), lambda j, i: (i, ga_off + j)),
            pl.BlockSpec((tm, tn), lambda j, i: (i, gr_off + j)),
        ],
        out_specs=pl.BlockSpec((tm, tn), lambda j, i: (i, j)),
        scratch_shapes=[pltpu.VMEM((ATTN_Q_W, tn), BF16), pltpu.VMEM((RET_V_W, tn), BF16)],
        compiler_params=_cparams(("arbitrary", "arbitrary")),
        name="merge_up",
    )(attn, ret, wa, wr, proj, proj)


def _out_proj_kernel(m_ref, w_ref, x_ref, mod_ref, o_ref, wb, *, gate_idx):
    _cast_weight_once(w_ref, wb)
    y = _dot(m_ref[...], wb[...])
    o_ref[...] = x_ref[...] + mod_ref[gate_idx:gate_idx + 1, :] * y


def _out_proj(merged, w, x2, mod, S, gate_idx):
    T, D = x2.shape
    tm = min(1024, S)
    tn = min(512, D)
    return pl.pallas_call(
        functools.partial(_out_proj_kernel, gate_idx=gate_idx),
        out_shape=jax.ShapeDtypeStruct((T, D), F32),
        grid=(D // tn, T // tm),
        in_specs=[
            pl.BlockSpec((tm, D), lambda j, i: (i, 0)),
            pl.BlockSpec((D, tn), lambda j, i: (0, j)),
            pl.BlockSpec((tm, tn), lambda j, i: (i, j)),
            pl.BlockSpec((None, N_ADA, tn), lambda j, i: ((i * tm) // S, 0, j)),
        ],
        out_specs=pl.BlockSpec((tm, tn), lambda j, i: (i, j)),
        scratch_shapes=[pltpu.VMEM((D, tn), BF16)],
        compiler_params=_cparams(("arbitrary", "arbitrary")),
        name="out_proj",
    )(merged, w, x2, mod)


def _pack_bf16_pairs(v):
    n = v.shape[1] // 2
    bits = lax.bitcast_convert_type(v.astype(BF16).astype(F32), U32)
    return jnp.bitwise_or(jnp.bitwise_and(bits[:, n:], jnp.uint32(0xFFFF0000)),
                          jnp.right_shift(bits[:, :n], jnp.uint32(16)))


def _unpack_bf16_pairs(w):
    lo = lax.bitcast_convert_type(jnp.left_shift(w, jnp.uint32(16)), F32)
    hi = lax.bitcast_convert_type(jnp.bitwise_and(w, jnp.uint32(0xFFFF0000)), F32)
    return jnp.concatenate([lo, hi], axis=1)


def _norm_router_kernel(x_ref, nw_ref, mod_ref, rwt_ref, rb_ref, hp_ref, e_ref, w_ref, r_ref, cnt_ref, carry,
                        *, shift_idx, scale_idx):
    i = pl.program_id(0)

    @pl.when(i == 0)
    def _():
        carry[...] = jnp.zeros_like(carry)

    h = _norm_mod_value(x_ref[...], nw_ref[...], mod_ref[shift_idx:shift_idx + 1, :],
                        mod_ref[scale_idx:scale_idx + 1, :])
    hp_ref[...] = _pack_bf16_pairs(h)
    hb = h.astype(BF16)
    tm = hb.shape[0]
    E = rwt_ref.shape[0]
    logits = _dot_nt(rwt_ref[...], hb) + rb_ref[...]
    iota_e = lax.broadcasted_iota(I32, (E, tm), 0)
    vals, idxs = [], []
    work = logits
    sel = jnp.zeros((E, tm), F32)
    for _k in range(TOP_K):
        m = jnp.max(work, axis=0, keepdims=True)
        idx = jnp.min(jnp.where(work == m, iota_e, E), axis=0, keepdims=True)
        hit = iota_e == idx
        vals.append(m)
        idxs.append(idx)
        work = jnp.where(hit, NEG_BIG, work)
        sel = sel + hit.astype(F32)
    ex = [jnp.exp(v - vals[0]) for v in vals]
    tot = ex[0]
    for v in ex[1:]:
        tot = tot + v
    ri = lax.broadcasted_iota(I32, (tm, tm), 0)
    ci = lax.broadcasted_iota(I32, (tm, tm), 1)
    upper = (ri < ci).astype(BF16)
    before = _dot(sel.astype(BF16), upper) + carry[:, 0:1]
    for k in range(TOP_K):
        e_ref[k:k + 1, :] = idxs[k]
        w_ref[k:k + 1, :] = ex[k] / tot
        r_ref[k:k + 1, :] = jnp.sum(jnp.where(iota_e == idxs[k], before, 0.0), axis=0, keepdims=True).astype(I32)
    carry[...] = carry[...] + jnp.sum(sel, axis=1, keepdims=True)
    cnt_ref[...] = carry[...]


def _norm_router(x1, nw, mod, router_w, router_b, S, shift_idx, scale_idx):
    T, D = x1.shape
    E = router_w.shape[1]
    tm = min(256, S)
    outs = pl.pallas_call(
        functools.partial(_norm_router_kernel, shift_idx=shift_idx, scale_idx=scale_idx),
        out_shape=(
            jax.ShapeDtypeStruct((T, D // 2), U32),
            jax.ShapeDtypeStruct((TOP_K, T), I32),
            jax.ShapeDtypeStruct((TOP_K, T), F32),
            jax.ShapeDtypeStruct((TOP_K, T), I32),
            jax.ShapeDtypeStruct((E, 128), F32),
        ),
        grid=(T // tm,),
        in_specs=[
            pl.BlockSpec((tm, D), lambda i: (i, 0)),
            pl.BlockSpec((1, D), lambda i: (0, 0)),
            pl.BlockSpec((None, N_ADA, D), lambda i: ((i * tm) // S, 0, 0)),
            pl.BlockSpec((E, D), lambda i: (0, 0)),
            pl.BlockSpec((E, 1), lambda i: (0, 0)),
        ],
        out_specs=(
            pl.BlockSpec((tm, D // 2), lambda i: (i, 0)),
            pl.BlockSpec((TOP_K, tm), lambda i: (0, i)),
            pl.BlockSpec((TOP_K, tm), lambda i: (0, i)),
            pl.BlockSpec((TOP_K, tm), lambda i: (0, i)),
            pl.BlockSpec((E, 128), lambda i: (0, 0)),
        ),
        scratch_shapes=[pltpu.VMEM((E, 128), F32)],
        compiler_params=_cparams(("arbitrary",)),
        name="norm_router",
    )(x1, nw.reshape(1, D), mod, router_w.T.astype(BF16), router_b.reshape(E, 1).astype(F32))
    return outs


def _dispatch_kernel(pos_ref, pad_ref, hp_ref, xs_ref, zeros, sem, zsem, *, tm, T, E, n_tiles):
    base = pl.program_id(0) * tm

    @pl.when(pl.program_id(0) == 0)
    def _():
        zeros[...] = jnp.zeros_like(zeros)
        def pad_copy(row):
            return pltpu.make_async_copy(zeros.at[pl.ds(0, 1)], xs_ref.at[pl.ds(row, 1)], zsem)

        def tail_copy(t):
            return pltpu.make_async_copy(
                zeros, xs_ref.at[pl.ds(pl.multiple_of(t * MOE_TILE, MOE_TILE), MOE_TILE)], zsem)

        def pad_start(e, carry):
            start = pad_ref[e]
            lax.fori_loop(0, pad_ref[E + e], lambda r, c: (pad_copy(start + r).start(), c)[1], 0)
            return carry

        def pad_wait(e, carry):
            start = pad_ref[e]
            lax.fori_loop(0, pad_ref[E + e], lambda r, c: (pad_copy(start + r).wait(), c)[1], 0)
            return carry

        def tail_start(t, carry):
            tail_copy(t).start()
            return carry

        def tail_wait(t, carry):
            tail_copy(t).wait()
            return carry

        lax.fori_loop(0, E, pad_start, 0)
        lax.fori_loop(pad_ref[2 * E], n_tiles, tail_start, 0)
        lax.fori_loop(0, E, pad_wait, 0)
        lax.fori_loop(pad_ref[2 * E], n_tiles, tail_wait, 0)

    def start_body(t, carry):
        for k in range(TOP_K):
            dst = pos_ref[k * T + base + t]
            pltpu.make_async_copy(hp_ref.at[pl.ds(t, 1)], xs_ref.at[pl.ds(dst, 1)], sem).start(priority=k % 2)
        return carry

    lax.fori_loop(0, tm, start_body, 0, unroll=8)
    for k in range(TOP_K):
        pltpu.make_async_copy(hp_ref, xs_ref.at[pl.ds(0, tm)], sem).wait()


def _dispatch(pos_flat, pad_info, hp, n_tiles, E):
    T, W = hp.shape
    tm = min(512, T)
    grid_spec = pltpu.PrefetchScalarGridSpec(
        num_scalar_prefetch=2,
        grid=(T // tm,),
        in_specs=[pl.BlockSpec((tm, W), lambda i, p, q: (i, 0))],
        out_specs=pl.BlockSpec(memory_space=pl.ANY),
        scratch_shapes=[pltpu.VMEM((MOE_TILE, W), U32), pltpu.SemaphoreType.DMA, pltpu.SemaphoreType.DMA],
    )
    return pl.pallas_call(
        functools.partial(_dispatch_kernel, tm=tm, T=T, E=E, n_tiles=n_tiles),
        out_shape=jax.ShapeDtypeStruct((n_tiles * MOE_TILE, W), U32),
        grid_spec=grid_spec,
        compiler_params=_cparams(("arbitrary",)),
        name="dispatch",
    )(pos_flat, pad_info, hp)


def _expert_changed(te_ref, i):
    return jnp.logical_or(i == 0, te_ref[i] != te_ref[jnp.maximum(i - 1, 0)])


def _cast_rows(src_ref, dst_ref, rows_per_pass=256):
    def body(r, carry):
        rows = pl.ds(pl.multiple_of(r * rows_per_pass, rows_per_pass), rows_per_pass)
        dst_ref[rows, :] = src_ref[rows, :].astype(dst_ref.dtype)
        return carry

    lax.fori_loop(0, src_ref.shape[0] // rows_per_pass, body, 0)


def _expert_weight_stage(te_ref, nx_ref, i, first_step, n, n_passes, copies, slot_ref, consume):
    @pl.when(_expert_changed(te_ref, i))
    def _():
        @pl.when(first_step)
        def _():
            slot_ref[0] = 0
            for cp in copies(te_ref[0], 0, 0):
                cp.start()

        slot = slot_ref[0]
        for cp in copies(te_ref[i], n, slot):
            cp.wait()
        consume(slot)
        same_pass = nx_ref[i] >= 0
        nxt_e = jnp.where(same_pass, nx_ref[i], te_ref[0])
        nxt_n = jnp.where(same_pass, n, n + 1)

        @pl.when(jnp.logical_or(same_pass, n + 1 < n_passes))
        def _():
            for cp in copies(nxt_e, nxt_n, 1 - slot):
                cp.start()

        slot_ref[0] = 1 - slot


def _expert_up_kernel(te_ref, nu_ref, nx_ref, xs_ref, w1_hbm, bg_ref, bu_ref, o_ref, stage, wgb, wub, slot_ref, sem,
                      *, tn, F, nf):
    n = pl.program_id(0)
    i = pl.program_id(1)

    def copies(e, nn, slot):
        c0 = pl.multiple_of(nn * tn, tn)
        return (pltpu.make_async_copy(w1_hbm.at[e, :, pl.ds(c0, tn)], stage.at[slot, 0], sem.at[slot]),
                pltpu.make_async_copy(w1_hbm.at[e, :, pl.ds(F + c0, tn)], stage.at[slot, 1], sem.at[slot]))

    def consume(slot):
        _cast_rows(stage.at[slot, 0], wgb)
        _cast_rows(stage.at[slot, 1], wub)

    _expert_weight_stage(te_ref, nx_ref, i, jnp.logical_and(n == 0, i == 0), n, nf, copies, slot_ref, consume)

    @pl.when(i < nu_ref[0])
    def _():
        x = _unpack_bf16_pairs(xs_ref[...]).astype(BF16)
        gate = jnp.minimum(_dot(x, wgb[...]) + bg_ref[...], SWIGLU_LIMIT)
        up = jnp.clip(_dot(x, wub[...]) + bu_ref[...], -SWIGLU_LIMIT, SWIGLU_LIMIT)
        act = gate * _sigmoid(SWIGLU_ALPHA * gate) * (up + 1.0)
        o_ref[...] = act.astype(o_ref.dtype)

    @pl.when(i >= nu_ref[0])
    def _():
        o_ref[...] = jnp.zeros_like(o_ref)


def _expert_down_kernel(te_ref, nu_ref, nx_ref, a_ref, w2_hbm, b2_ref, o_ref, stage, w2b, slot_ref, sem):
    i = pl.program_id(0)

    def copies(e, nn, slot):
        del nn
        return (pltpu.make_async_copy(w2_hbm.at[e], stage.at[slot], sem.at[slot]),)

    def consume(slot):
        _cast_rows(stage.at[slot], w2b)

    _expert_weight_stage(te_ref, nx_ref, i, i == 0, 0, 1, copies, slot_ref, consume)

    @pl.when(i < nu_ref[0])
    def _():
        y = _dot(a_ref[...], w2b[...]) + b2_ref[...]
        o_ref[...] = _pack_bf16_pairs(y)

    @pl.when(i >= nu_ref[0])
    def _():
        o_ref[...] = jnp.zeros_like(o_ref)


def _expert_ffn(xs, tile_e, n_used, next_e, w1, b1, w2, b2):
    R, W = xs.shape
    E, D, F2 = w1.shape
    F = F2 // 2
    tm = MOE_TILE
    nt = R // tm

    tn = min(512, F)
    nf = F // tn
    b1r = b1.reshape(E, 1, F2)

    act = pl.pallas_call(
        functools.partial(_expert_up_kernel, tn=tn, F=F, nf=nf),
        out_shape=jax.ShapeDtypeStruct((R, F), BF16),
        grid_spec=pltpu.PrefetchScalarGridSpec(
            num_scalar_prefetch=3,
            grid=(nf, nt),
            in_specs=[
                pl.BlockSpec((tm, W), lambda n, i, te, nu, nx: (jnp.minimum(i, nu[0] - 1), 0)),
                pl.BlockSpec(memory_space=pl.ANY),
                pl.BlockSpec((None, 1, tn), lambda n, i, te, nu, nx: (te[i], 0, n)),
                pl.BlockSpec((None, 1, tn), lambda n, i, te, nu, nx: (te[i], 0, nf + n)),
            ],
            out_specs=pl.BlockSpec((tm, tn), lambda n, i, te, nu, nx: (i, n)),
            scratch_shapes=[pltpu.VMEM((2, 2, D, tn), F32), pltpu.VMEM((D, tn), BF16), pltpu.VMEM((D, tn), BF16),
                            pltpu.SMEM((1,), I32), pltpu.SemaphoreType.DMA((2,))],
        ),
        compiler_params=_cparams(("arbitrary", "arbitrary")),
        name="expert_up",
    )(tile_e, n_used, next_e, xs, w1, b1r, b1r)
    y = pl.pallas_call(
        _expert_down_kernel,
        out_shape=jax.ShapeDtypeStruct((R, D // 2), U32),
        grid_spec=pltpu.PrefetchScalarGridSpec(
            num_scalar_prefetch=3,
            grid=(nt,),
            in_specs=[
                pl.BlockSpec((tm, F), lambda i, te, nu, nx: (jnp.minimum(i, nu[0] - 1), 0)),
                pl.BlockSpec(memory_space=pl.ANY),
                pl.BlockSpec((None, 1, D), lambda i, te, nu, nx: (te[i], 0, 0)),
            ],
            out_specs=pl.BlockSpec((tm, D // 2), lambda i, te, nu, nx: (i, 0)),
            scratch_shapes=[pltpu.VMEM((2, F, D), F32), pltpu.VMEM((F, D), BF16),
                            pltpu.SMEM((1,), I32), pltpu.SemaphoreType.DMA((2,))],
        ),
        compiler_params=_cparams(("arbitrary",)),
        name="expert_down",
    )(tile_e, n_used, next_e, act, w2, b2.reshape(E, 1, D))
    return y


def _combine_kernel(pos_ref, x_ref, w_ref, mod_ref, y_ref, o_ref, buf, sem, *, tm, T, gate_idx):
    i = pl.program_id(0)
    n = pl.num_programs(0)
    W = buf.shape[-1]

    def issue_row(step, slot, t):
        for k in range(TOP_K):
            src = pos_ref[k * T + step * tm + t]
            pltpu.make_async_copy(y_ref.at[pl.ds(src, 1)], buf.at[slot, k, pl.ds(t, 1)],
                                  sem.at[slot]).start(priority=k % 2)

    def wait_block(slot):
        for k in range(TOP_K):
            pltpu.make_async_copy(y_ref.at[pl.ds(0, tm)], buf.at[slot, k], sem.at[slot]).wait()

    @pl.when(i == 0)
    def _():
        lax.fori_loop(0, tm, lambda t, c: (issue_row(0, 0, t), c)[1], 0, unroll=8)

    slot = i % 2
    wait_block(slot)
    nxt = jnp.minimum(i + 1, n - 1)
    rc = 8
    cw = min(512, W)

    def rows_body(r, carry):
        rows = pl.ds(pl.multiple_of(r * rc, rc), rc)
        wv = w_ref[rows, :]
        wk = [jnp.broadcast_to(wv[:, k:k + 1], (rc, cw)) for k in range(TOP_K)]
        for c in range(W // cw):
            lo = hi = None
            for k in range(TOP_K):
                u = buf[slot, k, rows, c * cw:(c + 1) * cw]
                l = wk[k] * lax.bitcast_convert_type(jnp.left_shift(u, jnp.uint32(16)), F32)
                h = wk[k] * lax.bitcast_convert_type(jnp.bitwise_and(u, jnp.uint32(0xFFFF0000)), F32)
                lo = l if lo is None else lo + l
                hi = h if hi is None else hi + h
            for half, acc in ((0, lo), (1, hi)):
                cols = slice(half * W + c * cw, half * W + (c + 1) * cw)
                o_ref[rows, cols] = x_ref[rows, cols] + mod_ref[gate_idx:gate_idx + 1, cols] * acc
        for t in range(rc):
            issue_row(nxt, 1 - slot, r * rc + t)
        return carry

    lax.fori_loop(0, tm // rc, rows_body, 0)

    @pl.when(i == n - 1)
    def _():
        wait_block(1 - slot)


def _combine(pos_flat, x1, w_tk, mod, y, S, gate_idx):
    T, D = x1.shape
    tm = min(128, S)
    grid_spec = pltpu.PrefetchScalarGridSpec(
        num_scalar_prefetch=1,
        grid=(T // tm,),
        in_specs=[
            pl.BlockSpec((tm, D), lambda i, p: (i, 0)),
            pl.BlockSpec((tm, TOP_K), lambda i, p: (i, 0)),
            pl.BlockSpec((None, N_ADA, D), lambda i, p: ((i * tm) // S, 0, 0)),
            pl.BlockSpec(memory_space=pl.ANY),
        ],
        out_specs=pl.BlockSpec((tm, D), lambda i, p: (i, 0)),
        scratch_shapes=[pltpu.VMEM((2, TOP_K, tm, D // 2), U32), pltpu.SemaphoreType.DMA((2,))],
    )
    return pl.pallas_call(
        functools.partial(_combine_kernel, tm=tm, T=T, gate_idx=gate_idx),
        out_shape=jax.ShapeDtypeStruct((T, D), F32),
        grid_spec=grid_spec,
        compiler_params=_cparams(("arbitrary",)),
        name="combine",
    )(pos_flat, x1, w_tk, mod, y)


def _layer(x2, mod, B, S, rel_bias, norm_mix_w, w_in, q_norm_w, k_norm_w, attn_sink, ret_decay_fwd, ret_decay_bwd,
           ret_gn_w, ret_gn_b, w_up_attn, w_up_ret, w_out, norm_ffn_w, router_w, router_b,
           expert_w1, expert_b1, expert_w2, expert_b2):
    T, D = x2.shape
    E = router_w.shape[1]
    h = _norm_mod(x2, norm_mix_w, mod, S, 0, 1)
    proj = _in_proj(h, w_in, q_norm_w, k_norm_w)
    attn = _window_attention(proj, _attn_bias_table(rel_bias), attn_sink, B, S)
    ret = _retention(proj, ret_decay_fwd, ret_decay_bwd, ret_gn_w, ret_gn_b, B, S)
    merged = _merge_up(attn, ret, w_up_attn, w_up_ret, proj, D)
    x1 = _out_proj(merged, w_out, x2, mod, S, 2)

    hp, top_e, top_w, rank, cnt = _norm_router(x1, norm_ffn_w, mod, router_w, router_b, S, 3, 4)
    tm = MOE_TILE
    counts = cnt[:, 0].astype(I32)
    tiles_per_e = (counts + tm - 1) // tm
    tile_end = jnp.cumsum(tiles_per_e)
    group_start = (tile_end - tiles_per_e) * tm
    n_tiles = (T * TOP_K) // tm + E
    n_used = tile_end[-1]
    tile_ids = jnp.minimum(jnp.arange(n_tiles, dtype=I32), n_used - 1)
    tile_e = jnp.minimum(jnp.sum((tile_ids[:, None] >= tile_end[None, :]).astype(I32), axis=1), E - 1)
    e_ids = jnp.arange(E, dtype=I32)
    later = jnp.logical_and(e_ids[None, :] > e_ids[:, None], tiles_per_e[None, :] > 0)
    next_of_e = jnp.min(jnp.where(later, e_ids[None, :], E), axis=1)
    next_of_e = jnp.where(next_of_e == E, -1, next_of_e)
    next_e = jnp.sum(jnp.where(tile_e[:, None] == e_ids, next_of_e, 0), axis=1).astype(I32)
    start_of = jnp.sum(jnp.where(top_e[:, :, None] == e_ids, group_start, 0), axis=-1)
    pos_flat = (start_of + rank).astype(I32).reshape(-1)
    pad_info = jnp.concatenate([group_start + counts, tiles_per_e * tm - counts, n_used.reshape(1)]).astype(I32)

    xs = _dispatch(pos_flat, pad_info, hp, n_tiles, E)
    y = _expert_ffn(xs, tile_e, n_used.reshape(1).astype(I32), next_e, expert_w1, expert_b1, expert_w2, expert_b2)
    return _combine(pos_flat, x1, top_w.T, mod, y, S, 5)


def kernel(x, c, rel_bias, ada_w, ada_b, norm_mix_w, w_in, q_norm_w, k_norm_w, attn_sink, ret_decay_fwd,
           ret_decay_bwd, ret_gn_w, ret_gn_b, w_up_attn, w_up_ret, w_out, norm_ffn_w, router_w, router_b,
           expert_w1, expert_b1, expert_w2, expert_b2):
    B, S, D = x.shape
    x2 = x.reshape(B * S, D)
    for l in range(ada_w.shape[0]):
        mod = _ada_mod(c, ada_w[l], ada_b[l])
        x2 = _layer(x2, mod, B, S, rel_bias, norm_mix_w[l], w_in[l], q_norm_w[l], k_norm_w[l], attn_sink[l],
                    ret_decay_fwd[l], ret_decay_bwd[l], ret_gn_w[l], ret_gn_b[l], w_up_attn[l], w_up_ret[l],
                    w_out[l], norm_ffn_w[l], router_w[l], router_b[l], expert_w1[l], expert_b1[l],
                    expert_w2[l], expert_b2[l])
    return x2.reshape(B, S, D)
```

```python
import functools
import math

import jax
import jax.numpy as jnp
from jax import lax
from jax.experimental import pallas as pl
from jax.experimental.pallas import tpu as pltpu

F32 = jnp.float32
BF16 = jnp.bfloat16
U32 = jnp.uint32
I32 = jnp.int32

ATTN_HEADS = 16
ATTN_KV_HEADS = 4
ATTN_HEAD_DIM = 128
WINDOW = 128
ATTN_BLOCK = 128
N_BUCKETS = 32
MAX_DISTANCE = 128
RET_HEADS = 8
RET_QK_DIM = 128
RET_V_DIM = 256
RET_CHUNK = 128
ROPE_BASE = 10000.0
TOP_K = 4
SWIGLU_LIMIT = 7.0
SWIGLU_ALPHA = 1.702
N_ADA = 6
EPS = 1e-6
NEG_INF = -1e30
NEG_BIG = -3.0e38
LOG2E = 1.4426950408889634

ATTN_Q_W = ATTN_HEADS * ATTN_HEAD_DIM
ATTN_KV_W = ATTN_KV_HEADS * ATTN_HEAD_DIM
RET_QK_W = RET_HEADS * RET_QK_DIM
RET_V_W = RET_HEADS * RET_V_DIM

VMEM_LIMIT_BYTES = 56 * 1024 * 1024
MOE_TILE = 256
RET_TILE = 256


def _cparams(sem):
    return pltpu.CompilerParams(dimension_semantics=sem, vmem_limit_bytes=VMEM_LIMIT_BYTES)


def _dot(a, b):
    return jnp.dot(a, b, preferred_element_type=F32)


def _dot_nt(a, b):
    return lax.dot_general(a, b, (((1,), (1,)), ((), ())), preferred_element_type=F32)


def _dot_tn(a, b):
    return lax.dot_general(a, b, (((0,), (0,)), ((), ())), preferred_element_type=F32)


def _sigmoid(x):
    return 1.0 / (1.0 + jnp.exp(-x))


def _ada_kernel(c_ref, w_ref, b_ref, o_ref):
    c = c_ref[...]
    cs = (c * _sigmoid(c)).astype(BF16)
    o_ref[...] = _dot(cs, w_ref[...].astype(BF16)) + b_ref[...]


def _ada_mod(c, ada_w, ada_b):
    B, D = c.shape
    N = ada_w.shape[1]
    rows = 8
    cp = jnp.zeros((rows, D), F32).at[:B].set(c)
    tn = min(512, N)
    out = pl.pallas_call(
        _ada_kernel,
        out_shape=jax.ShapeDtypeStruct((rows, N), F32),
        grid=(N // tn,),
        in_specs=[
            pl.BlockSpec((rows, D), lambda j: (0, 0)),
            pl.BlockSpec((D, tn), lambda j: (0, j)),
            pl.BlockSpec((1, tn), lambda j: (0, j)),
        ],
        out_specs=pl.BlockSpec((rows, tn), lambda j: (0, j)),
        compiler_params=_cparams(("arbitrary",)),
        name="ada_mod",
    )(cp, ada_w, ada_b.reshape(1, N))
    return out[:B].reshape(B, N_ADA, D)


def _norm_mod_value(x, nw, shift, scale):
    ms = jnp.mean(x * x, axis=-1, keepdims=True)
    h = x * lax.rsqrt(ms + EPS) * nw
    return h * (1.0 + scale) + shift


def _norm_mod_kernel(x_ref, nw_ref, mod_ref, o_ref, *, shift_idx, scale_idx):
    h = _norm_mod_value(x_ref[...], nw_ref[...], mod_ref[shift_idx:shift_idx + 1, :],
                        mod_ref[scale_idx:scale_idx + 1, :])
    o_ref[...] = h.astype(BF16)


def _norm_mod(x2, nw, mod, S, shift_idx, scale_idx):
    T, D = x2.shape
    tm = min(256, S)
    return pl.pallas_call(
        functools.partial(_norm_mod_kernel, shift_idx=shift_idx, scale_idx=scale_idx),
        out_shape=jax.ShapeDtypeStruct((T, D), BF16),
        grid=(T // tm,),
        in_specs=[
            pl.BlockSpec((tm, D), lambda i: (i, 0)),
            pl.BlockSpec((1, D), lambda i: (0, 0)),
            pl.BlockSpec((None, N_ADA, D), lambda i: ((i * tm) // S, 0, 0)),
        ],
        out_specs=pl.BlockSpec((tm, D), lambda i: (i, 0)),
        compiler_params=_cparams(("arbitrary",)),
        name="norm_mod",
    )(x2, nw.reshape(1, D), mod)


def _cast_weight_once(w_ref, wb_ref):
    @pl.when(pl.program_id(1) == 0)
    def _():
        wb_ref[...] = w_ref[...].astype(BF16)


def _in_proj_kernel(a_ref, w_hbm, cw_ref, cf_ref, bd_ref, o_ref, stage, wb, sem, *, n_norm_tiles):
    j = pl.program_id(0)
    tn = wb.shape[1]

    def w_copy(jj):
        return pltpu.make_async_copy(w_hbm.at[:, pl.ds(pl.multiple_of(jj * tn, tn), tn)], stage, sem)

    @pl.when(pl.program_id(1) == 0)
    def _():
        @pl.when(j == 0)
        def _():
            w_copy(0).start()

        w_copy(j).wait()
        _cast_rows(stage, wb)

        @pl.when(j + 1 < pl.num_programs(0))
        def _():
            w_copy(j + 1).start()

    @pl.when(j < n_norm_tiles)
    def _():
        cb = bd_ref.shape[0]
        for c in range(tn // cb):
            cols = slice(c * cb, (c + 1) * cb)
            acc = _dot(a_ref[...], wb[:, cols])
            ms = _dot((acc * acc).astype(BF16), bd_ref[...]) * (1.0 / ATTN_HEAD_DIM)
            inv = jnp.where(cf_ref[:, cols] > 0.0, lax.rsqrt(ms + EPS), 1.0)
            o_ref[:, cols] = (acc * inv * cw_ref[:, cols]).astype(o_ref.dtype)

    @pl.when(j >= n_norm_tiles)
    def _():
        o_ref[...] = _dot(a_ref[...], wb[...]).astype(o_ref.dtype)


def _in_proj(h, w, q_norm_w, k_norm_w):
    T, K = h.shape
    N = w.shape[1]
    tm = min(1024, T)
    tn = 1024 if N % 1024 == 0 else 512
    hd = ATTN_HEAD_DIM
    n_norm = ATTN_Q_W + ATTN_KV_W
    n_norm_tiles = -(-n_norm // tn)
    pad = n_norm_tiles * tn - n_norm
    col_w = jnp.concatenate([jnp.tile(q_norm_w.astype(F32) * (hd ** -0.5 * LOG2E), ATTN_HEADS),
                             jnp.tile(k_norm_w.astype(F32), ATTN_KV_HEADS), jnp.ones((pad,), F32)]).reshape(1, -1)
    col_flag = jnp.concatenate([jnp.ones((n_norm,), F32), jnp.zeros((pad,), F32)]).reshape(1, -1)
    cb = min(512, tn)
    lane_head = jnp.arange(cb, dtype=I32) // hd
    block_ones = (lane_head[:, None] == lane_head[None, :]).astype(BF16)
    last = n_norm_tiles - 1
    return pl.pallas_call(
        functools.partial(_in_proj_kernel, n_norm_tiles=n_norm_tiles),
        out_shape=jax.ShapeDtypeStruct((T, N), BF16),
        grid=(N // tn, T // tm),
        in_specs=[
            pl.BlockSpec((tm, K), lambda j, i: (i, 0)),
            pl.BlockSpec(memory_space=pl.ANY),
            pl.BlockSpec((1, tn), lambda j, i: (0, jnp.minimum(j, last))),
            pl.BlockSpec((1, tn), lambda j, i: (0, jnp.minimum(j, last))),
            pl.BlockSpec((cb, cb), lambda j, i: (0, 0)),
        ],
        out_specs=pl.BlockSpec((tm, tn), lambda j, i: (i, j)),
        scratch_shapes=[pltpu.VMEM((K, tn), F32), pltpu.VMEM((K, tn), BF16), pltpu.SemaphoreType.DMA],
        compiler_params=_cparams(("arbitrary", "arbitrary")),
        name="in_proj",
    )(h, w, col_w, col_flag, block_ones)


def _t5_bucket(rel):
    nb = N_BUCKETS // 2
    max_exact = nb // 2
    base = jnp.where(rel > 0, nb, 0)
    n = jnp.abs(rel)
    nf = jnp.maximum(n, 1).astype(F32)
    large = max_exact + (jnp.log(nf / max_exact) / math.log(MAX_DISTANCE / max_exact) * (nb - max_exact)).astype(I32)
    large = jnp.minimum(large, nb - 1)
    return base + jnp.where(n < max_exact, n, large)


def _attn_bias_table(rel_bias):
    blk = ATTN_BLOCK
    qi = jnp.arange(blk, dtype=I32)[:, None]
    kj = jnp.arange(3 * blk, dtype=I32)[None, :]
    rel = kj - blk - qi
    bucket = _t5_bucket(rel)
    table = rel_bias.astype(F32).T
    bias = jnp.zeros((table.shape[0],) + bucket.shape, F32)
    for b in range(N_BUCKETS):
        bias = jnp.where(bucket[None] == b, table[:, b][:, None, None], bias)
    bias = jnp.where((jnp.abs(rel) <= WINDOW)[None], bias * LOG2E, NEG_INF)
    G = ATTN_HEADS // ATTN_KV_HEADS
    bias = bias.reshape(ATTN_KV_HEADS, G, blk, 3 * blk)
    bias = jnp.transpose(bias, (0, 3, 1, 2)).reshape(ATTN_KV_HEADS, 3 * blk, G * blk)
    key = jnp.arange(3 * blk, dtype=I32)[None, :, None]
    no_prev, no_next = key < blk, key >= 2 * blk
    return jnp.stack([bias, jnp.where(no_prev, NEG_INF, bias), jnp.where(no_next, NEG_INF, bias),
                      jnp.where(jnp.logical_or(no_prev, no_next), NEG_INF, bias)])


def _attn_kernel(sink_ref, q_ref, kp_ref, kc_ref, kn_ref, vp_ref, vc_ref, vn_ref, bias_ref, o_ref):
    blk, hd = ATTN_BLOCK, ATTN_HEAD_DIM
    G = ATTN_HEADS // ATTN_KV_HEADS
    for hk in range(ATTN_KV_HEADS):
        sl = slice(hk * hd, (hk + 1) * hd)
        k3 = jnp.concatenate([kp_ref[:, sl], kc_ref[:, sl], kn_ref[:, sl]], axis=0)
        v3 = jnp.concatenate([vp_ref[:, sl], vc_ref[:, sl], vn_ref[:, sl]], axis=0)
        q4 = jnp.concatenate([q_ref[:, (hk * G + g) * hd:(hk * G + g + 1) * hd] for g in range(G)],
                             axis=0)
        sink = jnp.concatenate([jnp.full((1, blk), sink_ref[hk * G + g], F32) for g in range(G)], axis=1)
        logits = _dot_nt(k3, q4) + bias_ref[hk]
        m = jnp.maximum(jnp.max(logits, axis=0, keepdims=True), sink)
        p = jnp.exp2(logits - m)
        denom = jnp.sum(p, axis=0, keepdims=True) + jnp.exp2(sink - m)
        out_t = _dot_tn(v3, p.astype(BF16)) * (1.0 / denom)
        out = out_t.T
        for g in range(G):
            h = hk * G + g
            o_ref[:, h * hd:(h + 1) * hd] = out[g * blk:(g + 1) * blk].astype(o_ref.dtype)


def _window_attention(proj, bias_tab, sink, B, S):
    T = proj.shape[0]
    blk = ATTN_BLOCK
    nb = S // blk
    kcol = ATTN_Q_W // ATTN_KV_W
    vcol = kcol + 1

    def kv_spec(col, off):
        return pl.BlockSpec((blk, ATTN_KV_W), lambda b, n, s: (b * nb + jnp.clip(n + off, 0, nb - 1), col))

    grid_spec = pltpu.PrefetchScalarGridSpec(
        num_scalar_prefetch=1,
        grid=(B, nb),
        in_specs=[
            pl.BlockSpec((blk, ATTN_Q_W), lambda b, n, s: (b * nb + n, 0)),
            kv_spec(kcol, -1), kv_spec(kcol, 0), kv_spec(kcol, 1),
            kv_spec(vcol, -1), kv_spec(vcol, 0), kv_spec(vcol, 1),
            pl.BlockSpec((None, ATTN_KV_HEADS, 3 * blk, (ATTN_HEADS // ATTN_KV_HEADS) * blk),
                         lambda b, n, s: ((n == 0).astype(I32) + 2 * (n == nb - 1).astype(I32), 0, 0, 0)),
        ],
        out_specs=pl.BlockSpec((blk, ATTN_Q_W), lambda b, n, s: (b * nb + n, 0)),
    )
    return pl.pallas_call(
        _attn_kernel,
        out_shape=jax.ShapeDtypeStruct((T, ATTN_Q_W), BF16),
        grid_spec=grid_spec,
        compiler_params=_cparams(("arbitrary", "arbitrary")),
        name="window_attn",
    )(sink.astype(F32) * LOG2E, proj, proj, proj, proj, proj, proj, proj, bias_tab)


def _rope_tables(S):
    d = RET_QK_DIM
    inv = ROPE_BASE ** (-jnp.arange(0, d, 2, dtype=F32) / d)
    ang = jnp.arange(S, dtype=F32)[:, None] * inv[None, :]
    cos, sin = jnp.cos(ang), jnp.sin(ang)
    return jnp.concatenate([cos, cos], axis=-1), jnp.concatenate([-sin, sin], axis=-1)


def _ret_kernel(df_ref, db_ref, q_ref, k_ref, v_ref, g_ref, cos_ref, sin_ref, gw_ref, gb_ref, o_ref,
                qb, kb, q2, kv, sprev, *, S, C):
    h = pl.program_id(1)
    nc = S // C
    dk = RET_QK_DIM
    half = dk // 2

    rowf = lax.broadcasted_iota(I32, (C, 1), 0).astype(F32)
    lg_f = -jnp.exp(jnp.full((1, 1), df_ref[h], F32))
    lg_b = -jnp.exp(jnp.full((1, 1), db_ref[h], F32))
    qdec_f = jnp.exp((rowf + 1.0) * lg_f)
    kdec_f = jnp.exp((C - 1.0 - rowf) * lg_f)
    qdec_b = jnp.exp((C - rowf) * lg_b)
    kdec_b = jnp.exp(rowf * lg_b)
    cd_f = jnp.exp(C * lg_f)
    cd_b = jnp.exp(C * lg_b)

    def a_body(n, carry):
        rows = pl.ds(pl.multiple_of(n * C, C), C)
        co = cos_ref[rows, :]
        si = sin_ref[rows, :]
        q = q_ref[rows, :].astype(F32)
        k = k_ref[rows, :].astype(F32)
        qr = q * co + pltpu.roll(q, half, 1) * si
        kr = (k * co + pltpu.roll(k, half, 1) * si) * (dk ** -0.5)
        qb[rows, :] = qr.astype(BF16)
        kb[rows, :] = kr.astype(BF16)
        q2[rows, :] = jnp.concatenate([qr * qdec_f, qr * qdec_b], axis=1).astype(BF16)
        k2 = jnp.concatenate([kr * kdec_f, kr * kdec_b], axis=1).astype(BF16)
        kv[n] = _dot_tn(k2, v_ref[rows, :])
        return carry

    lax.fori_loop(0, nc, a_body, 0, unroll=4)

    def scan_f(n, state):
        sprev[n, 0:dk, :] = state.astype(BF16)
        return state * cd_f + kv[n, 0:dk, :]

    lax.fori_loop(0, nc, scan_f, jnp.zeros((dk, RET_V_DIM), F32))

    def scan_b(t, state):
        n = nc - 1 - t
        sprev[n, dk:2 * dk, :] = state.astype(BF16)
        return state * cd_b + kv[n, dk:2 * dk, :]

    lax.fori_loop(0, nc, scan_b, jnp.zeros((dk, RET_V_DIM), F32))

    ri = lax.broadcasted_iota(I32, (C, C), 0)
    ci = lax.broadcasted_iota(I32, (C, C), 1)
    d = (ri - ci).astype(F32)
    dec = jnp.where(ri >= ci, jnp.exp(jnp.maximum(d, 0.0) * lg_f), jnp.exp(jnp.maximum(-d, 0.0) * lg_b))
    gw = gw_ref[...]
    gb = gb_ref[...]

    def c_body(n, carry):
        rows = pl.ds(pl.multiple_of(n * C, C), C)
        scores = _dot_nt(qb[rows, :], kb[rows, :]) * dec
        y = _dot(scores.astype(BF16), v_ref[rows, :]) + _dot(q2[rows, :], sprev[n])
        mu = jnp.mean(y, axis=-1, keepdims=True)
        yc = y - mu
        var = jnp.mean(yc * yc, axis=-1, keepdims=True)
        yn = yc * lax.rsqrt(var + EPS) * gw + gb
        g = g_ref[rows, :].astype(F32)
        o_ref[rows, :] = (g * _sigmoid(g) * yn).astype(o_ref.dtype)
        return carry

    lax.fori_loop(0, nc, c_body, 0, unroll=4)


def _retention(proj, decay_fwd, decay_bwd, gn_w, gn_b, B, S):
    T = proj.shape[0]
    dk, dv = RET_QK_DIM, RET_V_DIM
    q_off = (ATTN_Q_W + 2 * ATTN_KV_W) // dk
    k_off = q_off + RET_QK_W // dk
    v_off = (ATTN_Q_W + 2 * ATTN_KV_W + 2 * RET_QK_W) // dv
    g_off = v_off + RET_V_W // dv
    cos, sin = _rope_tables(S)
    C = min(RET_TILE, S)
    grid_spec = pltpu.PrefetchScalarGridSpec(
        num_scalar_prefetch=2,
        grid=(B, RET_HEADS),
        in_specs=[
            pl.BlockSpec((S, dk), lambda b, h, *_: (b, q_off + h)),
            pl.BlockSpec((S, dk), lambda b, h, *_: (b, k_off + h)),
            pl.BlockSpec((S, dv), lambda b, h, *_: (b, v_off + h)),
            pl.BlockSpec((S, dv), lambda b, h, *_: (b, g_off + h)),
            pl.BlockSpec((S, dk), lambda b, h, *_: (0, 0)),
            pl.BlockSpec((S, dk), lambda b, h, *_: (0, 0)),
            pl.BlockSpec((1, dv), lambda b, h, *_: (0, h)),
            pl.BlockSpec((1, dv), lambda b, h, *_: (0, h)),
        ],
        out_specs=pl.BlockSpec((S, dv), lambda b, h, *_: (b, h)),
        scratch_shapes=[
            pltpu.VMEM((S, dk), BF16),
            pltpu.VMEM((S, dk), BF16),
            pltpu.VMEM((S, 2 * dk), BF16),
            pltpu.VMEM((S // C, 2 * dk, dv), F32),
            pltpu.VMEM((S // C, 2 * dk, dv), BF16),
        ],
    )
    return pl.pallas_call(
        functools.partial(_ret_kernel, S=S, C=C),
        out_shape=jax.ShapeDtypeStruct((T, RET_V_W), BF16),
        grid_spec=grid_spec,
        compiler_params=_cparams(("arbitrary", "arbitrary")),
        name="retention",
    )(decay_fwd.astype(F32), decay_bwd.astype(F32), proj, proj, proj, proj, cos, sin,
      gn_w.reshape(1, -1), gn_b.reshape(1, -1))


def _merge_kernel(a_ref, r_ref, wa_ref, wr_ref, ga_ref, gr_ref, o_ref, wab, wrb):
    _cast_weight_once(wa_ref, wab)
    _cast_weight_once(wr_ref, wrb)
    ua = _dot(a_ref[...], wab[...])
    ur = _dot(r_ref[...], wrb[...])
    out = _sigmoid(ga_ref[...].astype(F32)) * ua + _sigmoid(gr_ref[...].astype(F32)) * ur
    o_ref[...] = out.astype(o_ref.dtype)


def _merge_up(attn, ret, wa, wr, proj, D):
    T = attn.shape[0]
    tm = min(1024, T)
    tn = min(512, D)
    ga_off = (ATTN_Q_W + 2 * ATTN_KV_W + 2 * RET_QK_W + 2 * RET_V_W) // tn
    gr_off = ga_off + D // tn
    return pl.pallas_call(
        _merge_kernel,
        out_shape=jax.ShapeDtypeStruct((T, D), BF16),
        grid=(D // tn, T // tm),
        in_specs=[
            pl.BlockSpec((tm, ATTN_Q_W), lambda j, i: (i, 0)),
            pl.BlockSpec((tm, RET_V_W), lambda j, i: (i, 0)),
            pl.BlockSpec((ATTN_Q_W, tn), lambda j, i: (0, j)),
            pl.BlockSpec((RET_V_W, tn), lambda j, i: (0, j)),
            pl.BlockSpec((tm, tn), lambda j, i: (i, ga_off + j)),
            pl.BlockSpec((tm, tn), lambda j, i: (i, gr_off + j)),
        ],
        out_specs=pl.BlockSpec((tm, tn), lambda j, i: (i, j)),
        scratch_shapes=[pltpu.VMEM((ATTN_Q_W, tn), BF16), pltpu.VMEM((RET_V_W, tn), BF16)],
        compiler_params=_cparams(("arbitrary", "arbitrary")),
        name="merge_up",
    )(attn, ret, wa, wr, proj, proj)


def _out_proj_kernel(m_ref, w_ref, x_ref, mod_ref, o_ref, wb, *, gate_idx):
    _cast_weight_once(w_ref, wb)
    y = _dot(m_ref[...], wb[...])
    o_ref[...] = x_ref[...] + mod_ref[gate_idx:gate_idx + 1, :] * y


def _out_proj(merged, w, x2, mod, S, gate_idx):
    T, D = x2.shape
    tm = min(1024, S)
    tn = min(512, D)
    return pl.pallas_call(
        functools.partial(_out_proj_kernel, gate_idx=gate_idx),
        out_shape=jax.ShapeDtypeStruct((T, D), F32),
        grid=(D // tn, T // tm),
        in_specs=[
            pl.BlockSpec((tm, D), lambda j, i: (i, 0)),
            pl.BlockSpec((D, tn), lambda j, i: (0, j)),
            pl.BlockSpec((tm, tn), lambda j, i: (i, j)),
            pl.BlockSpec((None, N_ADA, tn), lambda j, i: ((i * tm) // S, 0, j)),
        ],
        out_specs=pl.BlockSpec((tm, tn), lambda j, i: (i, j)),
        scratch_shapes=[pltpu.VMEM((D, tn), BF16)],
        compiler_params=_cparams(("arbitrary", "arbitrary")),
        name="out_proj",
    )(merged, w, x2, mod)


def _pack_bf16_pairs(v):
    n = v.shape[1] // 2
    bits = lax.bitcast_convert_type(v.astype(BF16).astype(F32), U32)
    return jnp.bitwise_or(jnp.bitwise_and(bits[:, n:], jnp.uint32(0xFFFF0000)),
                          jnp.right_shift(bits[:, :n], jnp.uint32(16)))


def _unpack_bf16_pairs(w):
    lo = lax.bitcast_convert_type(jnp.left_shift(w, jnp.uint32(16)), F32)
    hi = lax.bitcast_convert_type(jnp.bitwise_and(w, jnp.uint32(0xFFFF0000)), F32)
    return jnp.concatenate([lo, hi], axis=1)


def _norm_router_kernel(x_ref, nw_ref, mod_ref, rwt_ref, rb_ref, hp_ref, e_ref, w_ref, r_ref, cnt_ref, carry,
                        *, shift_idx, scale_idx):
    i = pl.program_id(0)

    @pl.when(i == 0)
    def _():
        carry[...] = jnp.zeros_like(carry)

    h = _norm_mod_value(x_ref[...], nw_ref[...], mod_ref[shift_idx:shift_idx + 1, :],
                        mod_ref[scale_idx:scale_idx + 1, :])
    hp_ref[...] = _pack_bf16_pairs(h)
    hb = h.astype(BF16)
    tm = hb.shape[0]
    E = rwt_ref.shape[0]
    logits = _dot_nt(rwt_ref[...], hb) + rb_ref[...]
    iota_e = lax.broadcasted_iota(I32, (E, tm), 0)
    vals, idxs = [], []
    work = logits
    sel = jnp.zeros((E, tm), F32)
    for _k in range(TOP_K):
        m = jnp.max(work, axis=0, keepdims=True)
        idx = jnp.min(jnp.where(work == m, iota_e, E), axis=0, keepdims=True)
        hit = iota_e == idx
        vals.append(m)
        idxs.append(idx)
        work = jnp.where(hit, NEG_BIG, work)
        sel = sel + hit.astype(F32)
    ex = [jnp.exp(v - vals[0]) for v in vals]
    tot = ex[0]
    for v in ex[1:]:
        tot = tot + v
    ri = lax.broadcasted_iota(I32, (tm, tm), 0)
    ci = lax.broadcasted_iota(I32, (tm, tm), 1)
    upper = (ri < ci).astype(BF16)
    before = _dot(sel.astype(BF16), upper) + carry[:, 0:1]
    for k in range(TOP_K):
        e_ref[k:k + 1, :] = idxs[k]
        w_ref[k:k + 1, :] = ex[k] / tot
        r_ref[k:k + 1, :] = jnp.sum(jnp.where(iota_e == idxs[k], before, 0.0), axis=0, keepdims=True).astype(I32)
    carry[...] = carry[...] + jnp.sum(sel, axis=1, keepdims=True)
    cnt_ref[...] = carry[...]


def _norm_router(x1, nw, mod, router_w, router_b, S, shift_idx, scale_idx):
    T, D = x1.shape
    E = router_w.shape[1]
    tm = min(256, S)
    outs = pl.pallas_call(
        functools.partial(_norm_router_kernel, shift_idx=shift_idx, scale_idx=scale_idx),
        out_shape=(
            jax.ShapeDtypeStruct((T, D // 2), U32),
            jax.ShapeDtypeStruct((TOP_K, T), I32),
            jax.ShapeDtypeStruct((TOP_K, T), F32),
            jax.ShapeDtypeStruct((TOP_K, T), I32),
            jax.ShapeDtypeStruct((E, 128), F32),
        ),
        grid=(T // tm,),
        in_specs=[
            pl.BlockSpec((tm, D), lambda i: (i, 0)),
            pl.BlockSpec((1, D), lambda i: (0, 0)),
            pl.BlockSpec((None, N_ADA, D), lambda i: ((i * tm) // S, 0, 0)),
            pl.BlockSpec((E, D), lambda i: (0, 0)),
            pl.BlockSpec((E, 1), lambda i: (0, 0)),
        ],
        out_specs=(
            pl.BlockSpec((tm, D // 2), lambda i: (i, 0)),
            pl.BlockSpec((TOP_K, tm), lambda i: (0, i)),
            pl.BlockSpec((TOP_K, tm), lambda i: (0, i)),
            pl.BlockSpec((TOP_K, tm), lambda i: (0, i)),
            pl.BlockSpec((E, 128), lambda i: (0, 0)),
        ),
        scratch_shapes=[pltpu.VMEM((E, 128), F32)],
        compiler_params=_cparams(("arbitrary",)),
        name="norm_router",
    )(x1, nw.reshape(1, D), mod, router_w.T.astype(BF16), router_b.reshape(E, 1).astype(F32))
    return outs


def _dispatch_kernel(pos_ref, pad_ref, hp_ref, xs_ref, zeros, sem, zsem, *, tm, T, E, n_tiles):
    base = pl.program_id(0) * tm

    @pl.when(pl.program_id(0) == 0)
    def _():
        zeros[...] = jnp.zeros_like(zeros)
        def pad_copy(row):
            return pltpu.make_async_copy(zeros.at[pl.ds(0, 1)], xs_ref.at[pl.ds(row, 1)], zsem)

        def tail_copy(t):
            return pltpu.make_async_copy(
                zeros, xs_ref.at[pl.ds(pl.multiple_of(t * MOE_TILE, MOE_TILE), MOE_TILE)], zsem)

        def pad_start(e, carry):
            start = pad_ref[e]
            lax.fori_loop(0, pad_ref[E + e], lambda r, c: (pad_copy(start + r).start(), c)[1], 0)
            return carry

        def pad_wait(e, carry):
            start = pad_ref[e]
            lax.fori_loop(0, pad_ref[E + e], lambda r, c: (pad_copy(start + r).wait(), c)[1], 0)
            return carry

        def tail_start(t, carry):
            tail_copy(t).start()
            return carry

        def tail_wait(t, carry):
            tail_copy(t).wait()
            return carry

        lax.fori_loop(0, E, pad_start, 0)
        lax.fori_loop(pad_ref[2 * E], n_tiles, tail_start, 0)
        lax.fori_loop(0, E, pad_wait, 0)
        lax.fori_loop(pad_ref[2 * E], n_tiles, tail_wait, 0)

    def start_body(r, carry):
        t0 = pl.multiple_of(r * 8, 8)
        for tt in range(8):
            for k in range(TOP_K):
                dst = pos_ref[k * T + base + t0 + tt]
                pltpu.make_async_copy(hp_ref.at[pl.ds(t0 + tt, 1)], xs_ref.at[pl.ds(dst, 1)],
                                      sem).start(priority=k % 2)
        return carry

    lax.fori_loop(0, tm // 8, start_body, 0)
    for k in range(TOP_K):
        pltpu.make_async_copy(hp_ref, xs_ref.at[pl.ds(0, tm)], sem).wait()


def _dispatch(pos_flat, pad_info, hp, n_tiles, E):
    T, W = hp.shape
    tm = min(512, T)
    grid_spec = pltpu.PrefetchScalarGridSpec(
        num_scalar_prefetch=2,
        grid=(T // tm,),
        in_specs=[pl.BlockSpec((tm, W), lambda i, p, q: (i, 0))],
        out_specs=pl.BlockSpec(memory_space=pl.ANY),
        scratch_shapes=[pltpu.VMEM((MOE_TILE, W), U32), pltpu.SemaphoreType.DMA, pltpu.SemaphoreType.DMA],
    )
    return pl.pallas_call(
        functools.partial(_dispatch_kernel, tm=tm, T=T, E=E, n_tiles=n_tiles),
        out_shape=jax.ShapeDtypeStruct((n_tiles * MOE_TILE, W), U32),
        grid_spec=grid_spec,
        compiler_params=_cparams(("arbitrary",)),
        name="dispatch",
    )(pos_flat, pad_info, hp)


def _expert_changed(te_ref, i):
    return jnp.logical_or(i == 0, te_ref[i] != te_ref[jnp.maximum(i - 1, 0)])


def _cast_rows(src_ref, dst_ref, rows_per_pass=256):
    def body(r, carry):
        rows = pl.ds(pl.multiple_of(r * rows_per_pass, rows_per_pass), rows_per_pass)
        dst_ref[rows, :] = src_ref[rows, :].astype(dst_ref.dtype)
        return carry

    lax.fori_loop(0, src_ref.shape[0] // rows_per_pass, body, 0)


def _expert_weight_stage(te_ref, nx_ref, i, first_step, n, n_passes, copies, slot_ref, consume):
    @pl.when(_expert_changed(te_ref, i))
    def _():
        @pl.when(first_step)
        def _():
            slot_ref[0] = 0
            for cp in copies(te_ref[0], 0, 0):
                cp.start()

        slot = slot_ref[0]
        for cp in copies(te_ref[i], n, slot):
            cp.wait()
        consume(slot)
        same_pass = nx_ref[i] >= 0
        nxt_e = jnp.where(same_pass, nx_ref[i], te_ref[0])
        nxt_n = jnp.where(same_pass, n, n + 1)

        @pl.when(jnp.logical_or(same_pass, n + 1 < n_passes))
        def _():
            for cp in copies(nxt_e, nxt_n, 1 - slot):
                cp.start()

        slot_ref[0] = 1 - slot


def _expert_up_kernel(te_ref, nu_ref, nx_ref, xs_ref, w1_hbm, bg_ref, bu_ref, o_ref, stage, wgb, wub, slot_ref, sem,
                      *, tn, F, nf):
    n = pl.program_id(0)
    i = pl.program_id(1)

    def copies(e, nn, slot):
        c0 = pl.multiple_of(nn * tn, tn)
        return (pltpu.make_async_copy(w1_hbm.at[e, :, pl.ds(c0, tn)], stage.at[slot, 0], sem.at[slot]),
                pltpu.make_async_copy(w1_hbm.at[e, :, pl.ds(F + c0, tn)], stage.at[slot, 1], sem.at[slot]))

    def consume(slot):
        _cast_rows(stage.at[slot, 0], wgb)
        _cast_rows(stage.at[slot, 1], wub)

    _expert_weight_stage(te_ref, nx_ref, i, jnp.logical_and(n == 0, i == 0), n, nf, copies, slot_ref, consume)

    @pl.when(i < nu_ref[0])
    def _():
        x = _unpack_bf16_pairs(xs_ref[...]).astype(BF16)
        gate = jnp.minimum(_dot(x, wgb[...]) + bg_ref[...], SWIGLU_LIMIT)
        up = jnp.clip(_dot(x, wub[...]) + bu_ref[...], -SWIGLU_LIMIT, SWIGLU_LIMIT)
        act = gate * _sigmoid(SWIGLU_ALPHA * gate) * (up + 1.0)
        o_ref[...] = act.astype(o_ref.dtype)

    @pl.when(i >= nu_ref[0])
    def _():
        o_ref[...] = jnp.zeros_like(o_ref)


def _expert_down_kernel(te_ref, nu_ref, nx_ref, a_ref, w2_hbm, b2_ref, o_ref, stage, w2b, slot_ref, sem):
    i = pl.program_id(0)

    def copies(e, nn, slot):
        del nn
        return (pltpu.make_async_copy(w2_hbm.at[e], stage.at[slot], sem.at[slot]),)

    def consume(slot):
        _cast_rows(stage.at[slot], w2b)

    _expert_weight_stage(te_ref, nx_ref, i, i == 0, 0, 1, copies, slot_ref, consume)

    @pl.when(i < nu_ref[0])
    def _():
        y = _dot(a_ref[...], w2b[...]) + b2_ref[...]
        o_ref[...] = _pack_bf16_pairs(y)

    @pl.when(i >= nu_ref[0])
    def _():
        o_ref[...] = jnp.zeros_like(o_ref)


def _expert_ffn(xs, tile_e, n_used, next_e, w1, b1, w2, b2):
    R, W = xs.shape
    E, D, F2 = w1.shape
    F = F2 // 2
    tm = MOE_TILE
    nt = R // tm

    tn = min(512, F)
    nf = F // tn
    b1r = b1.reshape(E, 1, F2)

    act = pl.pallas_call(
        functools.partial(_expert_up_kernel, tn=tn, F=F, nf=nf),
        out_shape=jax.ShapeDtypeStruct((R, F), BF16),
        grid_spec=pltpu.PrefetchScalarGridSpec(
            num_scalar_prefetch=3,
            grid=(nf, nt),
            in_specs=[
                pl.BlockSpec((tm, W), lambda n, i, te, nu, nx: (jnp.minimum(i, nu[0] - 1), 0)),
                pl.BlockSpec(memory_space=pl.ANY),
                pl.BlockSpec((None, 1, tn), lambda n, i, te, nu, nx: (te[i], 0, n)),
                pl.BlockSpec((None, 1, tn), lambda n, i, te, nu, nx: (te[i], 0, nf + n)),
            ],
            out_specs=pl.BlockSpec((tm, tn), lambda n, i, te, nu, nx: (i, n)),
            scratch_shapes=[pltpu.VMEM((2, 2, D, tn), F32), pltpu.VMEM((D, tn), BF16), pltpu.VMEM((D, tn), BF16),
                            pltpu.SMEM((1,), I32), pltpu.SemaphoreType.DMA((2,))],
        ),
        compiler_params=_cparams(("arbitrary", "arbitrary")),
        name="expert_up",
    )(tile_e, n_used, next_e, xs, w1, b1r, b1r)
    y = pl.pallas_call(
        _expert_down_kernel,
        out_shape=jax.ShapeDtypeStruct((R, D // 2), U32),
        grid_spec=pltpu.PrefetchScalarGridSpec(
            num_scalar_prefetch=3,
            grid=(nt,),
            in_specs=[
                pl.BlockSpec((tm, F), lambda i, te, nu, nx: (jnp.minimum(i, nu[0] - 1), 0)),
                pl.BlockSpec(memory_space=pl.ANY),
                pl.BlockSpec((None, 1, D), lambda i, te, nu, nx: (te[i], 0, 0)),
            ],
            out_specs=pl.BlockSpec((tm, D // 2), lambda i, te, nu, nx: (i, 0)),
            scratch_shapes=[pltpu.VMEM((2, F, D), F32), pltpu.VMEM((F, D), BF16),
                            pltpu.SMEM((1,), I32), pltpu.SemaphoreType.DMA((2,))],
        ),
        compiler_params=_cparams(("arbitrary",)),
        name="expert_down",
    )(tile_e, n_used, next_e, act, w2, b2.reshape(E, 1, D))
    return y


def _combine_kernel(pos_ref, x_ref, w_ref, mod_ref, y_ref, o_ref, buf, sem, *, tm, T, gate_idx):
    i = pl.program_id(0)
    n = pl.num_programs(0)
    W = buf.shape[-1]
    th = tm // 2

    def issue_row(first_token, slot, t):
        for k in range(TOP_K):
            src = pos_ref[k * T + first_token + t]
            pltpu.make_async_copy(y_ref.at[pl.ds(src, 1)], buf.at[slot, k, pl.ds(t, 1)],
                                  sem.at[slot]).start(priority=k % 2)

    def wait_half(slot):
        for k in range(TOP_K):
            pltpu.make_async_copy(y_ref.at[pl.ds(0, th)], buf.at[slot, k], sem.at[slot]).wait()

    rc = 8
    cw = min(512, W)

    def consume_half(slot, next_first_token):
        def rows_body(r, carry):
            rows = pl.ds(pl.multiple_of(slot * th + r * rc, rc), rc)
            brow = pl.ds(pl.multiple_of(r * rc, rc), rc)
            wv = w_ref[rows, :]
            wk = [jnp.broadcast_to(wv[:, k:k + 1], (rc, cw)) for k in range(TOP_K)]
            for c in range(W // cw):
                lo = hi = None
                for k in range(TOP_K):
                    u = buf[slot, k, brow, c * cw:(c + 1) * cw]
                    l = wk[k] * lax.bitcast_convert_type(jnp.left_shift(u, jnp.uint32(16)), F32)
                    h = wk[k] * lax.bitcast_convert_type(jnp.bitwise_and(u, jnp.uint32(0xFFFF0000)), F32)
                    lo = l if lo is None else lo + l
                    hi = h if hi is None else hi + h
                for half, acc in ((0, lo), (1, hi)):
                    cols = slice(half * W + c * cw, half * W + (c + 1) * cw)
                    o_ref[rows, cols] = x_ref[rows, cols] + mod_ref[gate_idx:gate_idx + 1, cols] * acc
            for t in range(rc):
                issue_row(next_first_token, 1 - slot, pl.multiple_of(r * rc, rc) + t)
            return carry

        lax.fori_loop(0, th // rc, rows_body, 0)

    base = i * tm

    @pl.when(i == 0)
    def _():
        lax.fori_loop(0, th, lambda t, c: (issue_row(0, 0, t), c)[1], 0, unroll=8)

    wait_half(0)
    consume_half(0, base + th)
    wait_half(1)
    consume_half(1, jnp.minimum(base + tm, T - tm))

    @pl.when(i == n - 1)
    def _():
        wait_half(0)


def _combine(pos_flat, x1, w_tk, mod, y, S, gate_idx):
    T, D = x1.shape
    tm = min(256, S)
    grid_spec = pltpu.PrefetchScalarGridSpec(
        num_scalar_prefetch=1,
        grid=(T // tm,),
        in_specs=[
            pl.BlockSpec((tm, D), lambda i, p: (i, 0)),
            pl.BlockSpec((tm, TOP_K), lambda i, p: (i, 0)),
            pl.BlockSpec((None, N_ADA, D), lambda i, p: ((i * tm) // S, 0, 0)),
            pl.BlockSpec(memory_space=pl.ANY),
        ],
        out_specs=pl.BlockSpec((tm, D), lambda i, p: (i, 0)),
        scratch_shapes=[pltpu.VMEM((2, TOP_K, tm // 2, D // 2), U32), pltpu.SemaphoreType.DMA((2,))],
    )
    return pl.pallas_call(
        functools.partial(_combine_kernel, tm=tm, T=T, gate_idx=gate_idx),
        out_shape=jax.ShapeDtypeStruct((T, D), F32),
        grid_spec=grid_spec,
        compiler_params=_cparams(("arbitrary",)),
        name="combine",
    )(pos_flat, x1, w_tk, mod, y)


def _layer(x2, mod, B, S, rel_bias, norm_mix_w, w_in, q_norm_w, k_norm_w, attn_sink, ret_decay_fwd, ret_decay_bwd,
           ret_gn_w, ret_gn_b, w_up_attn, w_up_ret, w_out, norm_ffn_w, router_w, router_b,
           expert_w1, expert_b1, expert_w2, expert_b2):
    T, D = x2.shape
    E = router_w.shape[1]
    h = _norm_mod(x2, norm_mix_w, mod, S, 0, 1)
    proj = _in_proj(h, w_in, q_norm_w, k_norm_w)
    attn = _window_attention(proj, _attn_bias_table(rel_bias), attn_sink, B, S)
    ret = _retention(proj, ret_decay_fwd, ret_decay_bwd, ret_gn_w, ret_gn_b, B, S)
    merged = _merge_up(attn, ret, w_up_attn, w_up_ret, proj, D)
    x1 = _out_proj(merged, w_out, x2, mod, S, 2)

    hp, top_e, top_w, rank, cnt = _norm_router(x1, norm_ffn_w, mod, router_w, router_b, S, 3, 4)
    tm = MOE_TILE
    counts = cnt[:, 0].astype(I32)
    tiles_per_e = (counts + tm - 1) // tm
    tile_end = jnp.cumsum(tiles_per_e)
    group_start = (tile_end - tiles_per_e) * tm
    n_tiles = (T * TOP_K) // tm + E
    n_used = tile_end[-1]
    tile_ids = jnp.minimum(jnp.arange(n_tiles, dtype=I32), n_used - 1)
    tile_e = jnp.minimum(jnp.sum((tile_ids[:, None] >= tile_end[None, :]).astype(I32), axis=1), E - 1)
    e_ids = jnp.arange(E, dtype=I32)
    later = jnp.logical_and(e_ids[None, :] > e_ids[:, None], tiles_per_e[None, :] > 0)
    next_of_e = jnp.min(jnp.where(later, e_ids[None, :], E), axis=1)
    next_of_e = jnp.where(next_of_e == E, -1, next_of_e)
    next_e = jnp.sum(jnp.where(tile_e[:, None] == e_ids, next_of_e, 0), axis=1).astype(I32)
    start_of = jnp.sum(jnp.where(top_e[:, :, None] == e_ids, group_start, 0), axis=-1)
    pos_flat = (start_of + rank).astype(I32).reshape(-1)
    pad_info = jnp.concatenate([group_start + counts, tiles_per_e * tm - counts, n_used.reshape(1)]).astype(I32)

    xs = _dispatch(pos_flat, pad_info, hp, n_tiles, E)
    y = _expert_ffn(xs, tile_e, n_used.reshape(1).astype(I32), next_e, expert_w1, expert_b1, expert_w2, expert_b2)
    return _combine(pos_flat, x1, top_w.T, mod, y, S, 5)


def kernel(x, c, rel_bias, ada_w, ada_b, norm_mix_w, w_in, q_norm_w, k_norm_w, attn_sink, ret_decay_fwd,
           ret_decay_bwd, ret_gn_w, ret_gn_b, w_up_attn, w_up_ret, w_out, norm_ffn_w, router_w, router_b,
           expert_w1, expert_b1, expert_w2, expert_b2):
    B, S, D = x.shape
    x2 = x.reshape(B * S, D)
    for l in range(ada_w.shape[0]):
        mod = _ada_mod(c, ada_w[l], ada_b[l])
        x2 = _layer(x2, mod, B, S, rel_bias, norm_mix_w[l], w_in[l], q_norm_w[l], k_norm_w[l], attn_sink[l],
                    ret_decay_fwd[l], ret_decay_bwd[l], ret_gn_w[l], ret_gn_b[l], w_up_attn[l], w_up_ret[l],
                    w_out[l], norm_ffn_w[l], router_w[l], router_b[l], expert_w1[l], expert_b1[l],
                    expert_w2[l], expert_b2[l])
    return x2.reshape(B, S, D)
```

```python
import functools
import math

import jax
import jax.numpy as jnp
from jax import lax
from jax.experimental import pallas as pl
from jax.experimental.pallas import tpu as pltpu

F32 = jnp.float32
BF16 = jnp.bfloat16
U32 = jnp.uint32
I32 = jnp.int32

ATTN_HEADS = 16
ATTN_KV_HEADS = 4
ATTN_HEAD_DIM = 128
WINDOW = 128
ATTN_BLOCK = 128
N_BUCKETS = 32
MAX_DISTANCE = 128
RET_HEADS = 8
RET_QK_DIM = 128
RET_V_DIM = 256
RET_CHUNK = 128
ROPE_BASE = 10000.0
TOP_K = 4
SWIGLU_LIMIT = 7.0
SWIGLU_ALPHA = 1.702
N_ADA = 6
EPS = 1e-6
NEG_INF = -1e30
NEG_BIG = -3.0e38
LOG2E = 1.4426950408889634

ATTN_Q_W = ATTN_HEADS * ATTN_HEAD_DIM
ATTN_KV_W = ATTN_KV_HEADS * ATTN_HEAD_DIM
RET_QK_W = RET_HEADS * RET_QK_DIM
RET_V_W = RET_HEADS * RET_V_DIM

VMEM_LIMIT_BYTES = 56 * 1024 * 1024
MOE_TILE = 256
RET_TILE = 256


def _cparams(sem):
    return pltpu.CompilerParams(dimension_semantics=sem, vmem_limit_bytes=VMEM_LIMIT_BYTES)


def _dot(a, b):
    return jnp.dot(a, b, preferred_element_type=F32)


def _dot_nt(a, b):
    return lax.dot_general(a, b, (((1,), (1,)), ((), ())), preferred_element_type=F32)


def _dot_tn(a, b):
    return lax.dot_general(a, b, (((0,), (0,)), ((), ())), preferred_element_type=F32)


def _sigmoid(x):
    return 1.0 / (1.0 + jnp.exp(-x))


def _ada_kernel(c_ref, w_ref, b_ref, o_ref):
    c = c_ref[...]
    cs = (c * _sigmoid(c)).astype(BF16)
    o_ref[...] = _dot(cs, w_ref[...].astype(BF16)) + b_ref[...]


def _ada_mod(c, ada_w, ada_b):
    B, D = c.shape
    N = ada_w.shape[1]
    rows = 8
    cp = jnp.zeros((rows, D), F32).at[:B].set(c)
    tn = min(512, N)
    out = pl.pallas_call(
        _ada_kernel,
        out_shape=jax.ShapeDtypeStruct((rows, N), F32),
        grid=(N // tn,),
        in_specs=[
            pl.BlockSpec((rows, D), lambda j: (0, 0)),
            pl.BlockSpec((D, tn), lambda j: (0, j)),
            pl.BlockSpec((1, tn), lambda j: (0, j)),
        ],
        out_specs=pl.BlockSpec((rows, tn), lambda j: (0, j)),
        compiler_params=_cparams(("arbitrary",)),
        name="ada_mod",
    )(cp, ada_w, ada_b.reshape(1, N))
    return out[:B].reshape(B, N_ADA, D)


def _norm_mod_value(x, nw, shift, scale):
    ms = jnp.mean(x * x, axis=-1, keepdims=True)
    h = x * lax.rsqrt(ms + EPS) * nw
    return h * (1.0 + scale) + shift


def _norm_mod_kernel(x_ref, nw_ref, mod_ref, o_ref, *, shift_idx, scale_idx):
    h = _norm_mod_value(x_ref[...], nw_ref[...], mod_ref[shift_idx:shift_idx + 1, :],
                        mod_ref[scale_idx:scale_idx + 1, :])
    o_ref[...] = h.astype(BF16)


def _norm_mod(x2, nw, mod, S, shift_idx, scale_idx):
    T, D = x2.shape
    tm = min(256, S)
    return pl.pallas_call(
        functools.partial(_norm_mod_kernel, shift_idx=shift_idx, scale_idx=scale_idx),
        out_shape=jax.ShapeDtypeStruct((T, D), BF16),
        grid=(T // tm,),
        in_specs=[
            pl.BlockSpec((tm, D), lambda i: (i, 0)),
            pl.BlockSpec((1, D), lambda i: (0, 0)),
            pl.BlockSpec((None, N_ADA, D), lambda i: ((i * tm) // S, 0, 0)),
        ],
        out_specs=pl.BlockSpec((tm, D), lambda i: (i, 0)),
        compiler_params=_cparams(("arbitrary",)),
        name="norm_mod",
    )(x2, nw.reshape(1, D), mod)


def _cast_weight_once(w_ref, wb_ref):
    @pl.when(pl.program_id(1) == 0)
    def _():
        wb_ref[...] = w_ref[...].astype(BF16)


def _in_proj_kernel(a_ref, w_hbm, cw_ref, cf_ref, bd_ref, o_ref, stage, wb, sem, *, n_norm_tiles):
    j = pl.program_id(0)
    tn = wb.shape[1]

    def w_copy(jj):
        return pltpu.make_async_copy(w_hbm.at[:, pl.ds(pl.multiple_of(jj * tn, tn), tn)], stage, sem)

    @pl.when(pl.program_id(1) == 0)
    def _():
        @pl.when(j == 0)
        def _():
            w_copy(0).start()

        w_copy(j).wait()
        _cast_rows(stage, wb)

        @pl.when(j + 1 < pl.num_programs(0))
        def _():
            w_copy(j + 1).start()

    @pl.when(j < n_norm_tiles)
    def _():
        cb = bd_ref.shape[0]
        for c in range(tn // cb):
            cols = slice(c * cb, (c + 1) * cb)
            acc = _dot(a_ref[...], wb[:, cols])
            ms = _dot((acc * acc).astype(BF16), bd_ref[...]) * (1.0 / ATTN_HEAD_DIM)
            inv = jnp.where(cf_ref[:, cols] > 0.0, lax.rsqrt(ms + EPS), 1.0)
            o_ref[:, cols] = (acc * inv * cw_ref[:, cols]).astype(o_ref.dtype)

    @pl.when(j >= n_norm_tiles)
    def _():
        o_ref[...] = _dot(a_ref[...], wb[...]).astype(o_ref.dtype)


def _in_proj(h, w, q_norm_w, k_norm_w):
    T, K = h.shape
    N = w.shape[1]
    tm = min(1024, T)
    tn = 1024 if N % 1024 == 0 else 512
    hd = ATTN_HEAD_DIM
    n_norm = ATTN_Q_W + ATTN_KV_W
    n_norm_tiles = -(-n_norm // tn)
    pad = n_norm_tiles * tn - n_norm
    col_w = jnp.concatenate([jnp.tile(q_norm_w.astype(F32) * (hd ** -0.5 * LOG2E), ATTN_HEADS),
                             jnp.tile(k_norm_w.astype(F32), ATTN_KV_HEADS), jnp.ones((pad,), F32)]).reshape(1, -1)
    col_flag = jnp.concatenate([jnp.ones((n_norm,), F32), jnp.zeros((pad,), F32)]).reshape(1, -1)
    cb = min(512, tn)
    lane_head = jnp.arange(cb, dtype=I32) // hd
    block_ones = (lane_head[:, None] == lane_head[None, :]).astype(BF16)
    last = n_norm_tiles - 1
    return pl.pallas_call(
        functools.partial(_in_proj_kernel, n_norm_tiles=n_norm_tiles),
        out_shape=jax.ShapeDtypeStruct((T, N), BF16),
        grid=(N // tn, T // tm),
        in_specs=[
            pl.BlockSpec((tm, K), lambda j, i: (i, 0)),
            pl.BlockSpec(memory_space=pl.ANY),
            pl.BlockSpec((1, tn), lambda j, i: (0, jnp.minimum(j, last))),
            pl.BlockSpec((1, tn), lambda j, i: (0, jnp.minimum(j, last))),
            pl.BlockSpec((cb, cb), lambda j, i: (0, 0)),
        ],
        out_specs=pl.BlockSpec((tm, tn), lambda j, i: (i, j)),
        scratch_shapes=[pltpu.VMEM((K, tn), F32), pltpu.VMEM((K, tn), BF16), pltpu.SemaphoreType.DMA],
        compiler_params=_cparams(("arbitrary", "arbitrary")),
        name="in_proj",
    )(h, w, col_w, col_flag, block_ones)


def _t5_bucket(rel):
    nb = N_BUCKETS // 2
    max_exact = nb // 2
    base = jnp.where(rel > 0, nb, 0)
    n = jnp.abs(rel)
    nf = jnp.maximum(n, 1).astype(F32)
    large = max_exact + (jnp.log(nf / max_exact) / math.log(MAX_DISTANCE / max_exact) * (nb - max_exact)).astype(I32)
    large = jnp.minimum(large, nb - 1)
    return base + jnp.where(n < max_exact, n, large)


def _attn_bias_table(rel_bias):
    blk = ATTN_BLOCK
    qi = jnp.arange(blk, dtype=I32)[:, None]
    kj = jnp.arange(3 * blk, dtype=I32)[None, :]
    rel = kj - blk - qi
    bucket = _t5_bucket(rel)
    table = rel_bias.astype(F32).T
    bias = jnp.zeros((table.shape[0],) + bucket.shape, F32)
    for b in range(N_BUCKETS):
        bias = jnp.where(bucket[None] == b, table[:, b][:, None, None], bias)
    bias = jnp.where((jnp.abs(rel) <= WINDOW)[None], bias * LOG2E, NEG_INF)
    G = ATTN_HEADS // ATTN_KV_HEADS
    bias = bias.reshape(ATTN_KV_HEADS, G, blk, 3 * blk)
    bias = jnp.transpose(bias, (0, 3, 1, 2)).reshape(ATTN_KV_HEADS, 3 * blk, G * blk)
    key = jnp.arange(3 * blk, dtype=I32)[None, :, None]
    no_prev, no_next = key < blk, key >= 2 * blk
    return jnp.stack([bias, jnp.where(no_prev, NEG_INF, bias), jnp.where(no_next, NEG_INF, bias),
                      jnp.where(jnp.logical_or(no_prev, no_next), NEG_INF, bias)])


def _attn_kernel(sink_ref, q_ref, kp_ref, kc_ref, kn_ref, vp_ref, vc_ref, vn_ref, bias_ref, o_ref):
    blk, hd = ATTN_BLOCK, ATTN_HEAD_DIM
    G = ATTN_HEADS // ATTN_KV_HEADS
    for hk in range(ATTN_KV_HEADS):
        sl = slice(hk * hd, (hk + 1) * hd)
        k3 = jnp.concatenate([kp_ref[:, sl], kc_ref[:, sl], kn_ref[:, sl]], axis=0)
        v3 = jnp.concatenate([vp_ref[:, sl], vc_ref[:, sl], vn_ref[:, sl]], axis=0)
        q4 = jnp.concatenate([q_ref[:, (hk * G + g) * hd:(hk * G + g + 1) * hd] for g in range(G)],
                             axis=0)
        sink = jnp.concatenate([jnp.full((1, blk), sink_ref[hk * G + g], F32) for g in range(G)], axis=1)
        logits = _dot_nt(k3, q4) + bias_ref[hk]
        m = jnp.maximum(jnp.max(logits, axis=0, keepdims=True), sink)
        p = jnp.exp2(logits - m)
        denom = jnp.sum(p, axis=0, keepdims=True) + jnp.exp2(sink - m)
        out_t = _dot_tn(v3, p.astype(BF16)) * (1.0 / denom)
        out = out_t.T
        for g in range(G):
            h = hk * G + g
            o_ref[:, h * hd:(h + 1) * hd] = out[g * blk:(g + 1) * blk].astype(o_ref.dtype)


def _window_attention(proj, bias_tab, sink, B, S):
    T = proj.shape[0]
    blk = ATTN_BLOCK
    nb = S // blk
    kcol = ATTN_Q_W // ATTN_KV_W
    vcol = kcol + 1

    def kv_spec(col, off):
        return pl.BlockSpec((blk, ATTN_KV_W), lambda b, n, s: (b * nb + jnp.clip(n + off, 0, nb - 1), col))

    grid_spec = pltpu.PrefetchScalarGridSpec(
        num_scalar_prefetch=1,
        grid=(B, nb),
        in_specs=[
            pl.BlockSpec((blk, ATTN_Q_W), lambda b, n, s: (b * nb + n, 0)),
            kv_spec(kcol, -1), kv_spec(kcol, 0), kv_spec(kcol, 1),
            kv_spec(vcol, -1), kv_spec(vcol, 0), kv_spec(vcol, 1),
            pl.BlockSpec((None, ATTN_KV_HEADS, 3 * blk, (ATTN_HEADS // ATTN_KV_HEADS) * blk),
                         lambda b, n, s: ((n == 0).astype(I32) + 2 * (n == nb - 1).astype(I32), 0, 0, 0)),
        ],
        out_specs=pl.BlockSpec((blk, ATTN_Q_W), lambda b, n, s: (b * nb + n, 0)),
    )
    return pl.pallas_call(
        _attn_kernel,
        out_shape=jax.ShapeDtypeStruct((T, ATTN_Q_W), BF16),
        grid_spec=grid_spec,
        compiler_params=_cparams(("arbitrary", "arbitrary")),
        name="window_attn",
    )(sink.astype(F32) * LOG2E, proj, proj, proj, proj, proj, proj, proj, bias_tab)


def _rope_tables(S):
    d = RET_QK_DIM
    inv = ROPE_BASE ** (-jnp.arange(0, d, 2, dtype=F32) / d)
    ang = jnp.arange(S, dtype=F32)[:, None] * inv[None, :]
    cos, sin = jnp.cos(ang), jnp.sin(ang)
    return jnp.concatenate([cos, cos], axis=-1), jnp.concatenate([-sin, sin], axis=-1)


def _ret_kernel(df_ref, db_ref, q_ref, k_ref, v_ref, g_ref, cos_ref, sin_ref, gw_ref, gb_ref, o_ref,
                qb, kb, q2, kv, sprev, *, S, C):
    h = pl.program_id(1)
    nc = S // C
    dk = RET_QK_DIM
    half = dk // 2

    rowf = lax.broadcasted_iota(I32, (C, 1), 0).astype(F32)
    lg_f = -jnp.exp(jnp.full((1, 1), df_ref[h], F32))
    lg_b = -jnp.exp(jnp.full((1, 1), db_ref[h], F32))
    qdec_f = jnp.exp((rowf + 1.0) * lg_f)
    kdec_f = jnp.exp((C - 1.0 - rowf) * lg_f)
    qdec_b = jnp.exp((C - rowf) * lg_b)
    kdec_b = jnp.exp(rowf * lg_b)
    cd_f = jnp.exp(C * lg_f)
    cd_b = jnp.exp(C * lg_b)

    def a_body(n, carry):
        rows = pl.ds(pl.multiple_of(n * C, C), C)
        co = cos_ref[rows, :]
        si = sin_ref[rows, :]
        q = q_ref[rows, :].astype(F32)
        k = k_ref[rows, :].astype(F32)
        qr = q * co + pltpu.roll(q, half, 1) * si
        kr = (k * co + pltpu.roll(k, half, 1) * si) * (dk ** -0.5)
        qb[rows, :] = qr.astype(BF16)
        kb[rows, :] = kr.astype(BF16)
        q2[rows, :] = jnp.concatenate([qr * qdec_f, qr * qdec_b], axis=1).astype(BF16)
        k2 = jnp.concatenate([kr * kdec_f, kr * kdec_b], axis=1).astype(BF16)
        kv[n] = _dot_tn(k2, v_ref[rows, :])
        return carry

    lax.fori_loop(0, nc, a_body, 0, unroll=4)

    def scan_f(n, state):
        sprev[n, 0:dk, :] = state.astype(BF16)
        return state * cd_f + kv[n, 0:dk, :]

    lax.fori_loop(0, nc, scan_f, jnp.zeros((dk, RET_V_DIM), F32))

    def scan_b(t, state):
        n = nc - 1 - t
        sprev[n, dk:2 * dk, :] = state.astype(BF16)
        return state * cd_b + kv[n, dk:2 * dk, :]

    lax.fori_loop(0, nc, scan_b, jnp.zeros((dk, RET_V_DIM), F32))

    ri = lax.broadcasted_iota(I32, (C, C), 0)
    ci = lax.broadcasted_iota(I32, (C, C), 1)
    d = (ri - ci).astype(F32)
    dec = jnp.where(ri >= ci, jnp.exp(jnp.maximum(d, 0.0) * lg_f), jnp.exp(jnp.maximum(-d, 0.0) * lg_b))
    gw = gw_ref[...]
    gb = gb_ref[...]

    def c_body(n, carry):
        rows = pl.ds(pl.multiple_of(n * C, C), C)
        scores = _dot_nt(qb[rows, :], kb[rows, :]) * dec
        y = _dot(scores.astype(BF16), v_ref[rows, :]) + _dot(q2[rows, :], sprev[n])
        mu = jnp.mean(y, axis=-1, keepdims=True)
        yc = y - mu
        var = jnp.mean(yc * yc, axis=-1, keepdims=True)
        yn = yc * lax.rsqrt(var + EPS) * gw + gb
        g = g_ref[rows, :].astype(F32)
        o_ref[rows, :] = (g * _sigmoid(g) * yn).astype(o_ref.dtype)
        return carry

    lax.fori_loop(0, nc, c_body, 0, unroll=4)


def _retention(proj, decay_fwd, decay_bwd, gn_w, gn_b, B, S):
    T = proj.shape[0]
    dk, dv = RET_QK_DIM, RET_V_DIM
    q_off = (ATTN_Q_W + 2 * ATTN_KV_W) // dk
    k_off = q_off + RET_QK_W // dk
    v_off = (ATTN_Q_W + 2 * ATTN_KV_W + 2 * RET_QK_W) // dv
    g_off = v_off + RET_V_W // dv
    cos, sin = _rope_tables(S)
    C = min(RET_TILE, S)
    grid_spec = pltpu.PrefetchScalarGridSpec(
        num_scalar_prefetch=2,
        grid=(B, RET_HEADS),
        in_specs=[
            pl.BlockSpec((S, dk), lambda b, h, *_: (b, q_off + h)),
            pl.BlockSpec((S, dk), lambda b, h, *_: (b, k_off + h)),
            pl.BlockSpec((S, dv), lambda b, h, *_: (b, v_off + h)),
            pl.BlockSpec((S, dv), lambda b, h, *_: (b, g_off + h)),
            pl.BlockSpec((S, dk), lambda b, h, *_: (0, 0)),
            pl.BlockSpec((S, dk), lambda b, h, *_: (0, 0)),
            pl.BlockSpec((1, dv), lambda b, h, *_: (0, h)),
            pl.BlockSpec((1, dv), lambda b, h, *_: (0, h)),
        ],
        out_specs=pl.BlockSpec((S, dv), lambda b, h, *_: (b, h)),
        scratch_shapes=[
            pltpu.VMEM((S, dk), BF16),
            pltpu.VMEM((S, dk), BF16),
            pltpu.VMEM((S, 2 * dk), BF16),
            pltpu.VMEM((S // C, 2 * dk, dv), F32),
            pltpu.VMEM((S // C, 2 * dk, dv), BF16),
        ],
    )
    return pl.pallas_call(
        functools.partial(_ret_kernel, S=S, C=C),
        out_shape=jax.ShapeDtypeStruct((T, RET_V_W), BF16),
        grid_spec=grid_spec,
        compiler_params=_cparams(("arbitrary", "arbitrary")),
        name="retention",
    )(decay_fwd.astype(F32), decay_bwd.astype(F32), proj, proj, proj, proj, cos, sin,
      gn_w.reshape(1, -1), gn_b.reshape(1, -1))


def _merge_kernel(a_ref, r_ref, wa_ref, wr_ref, ga_ref, gr_ref, o_ref, wab, wrb):
    _cast_weight_once(wa_ref, wab)
    _cast_weight_once(wr_ref, wrb)
    ua = _dot(a_ref[...], wab[...])
    ur = _dot(r_ref[...], wrb[...])
    out = _sigmoid(ga_ref[...].astype(F32)) * ua + _sigmoid(gr_ref[...].astype(F32)) * ur
    o_ref[...] = out.astype(o_ref.dtype)


def _merge_up(attn, ret, wa, wr, proj, D):
    T = attn.shape[0]
    tm = min(1024, T)
    tn = min(512, D)
    ga_off = (ATTN_Q_W + 2 * ATTN_KV_W + 2 * RET_QK_W + 2 * RET_V_W) // tn
    gr_off = ga_off + D // tn
    return pl.pallas_call(
        _merge_kernel,
        out_shape=jax.ShapeDtypeStruct((T, D), BF16),
        grid=(D // tn, T // tm),
        in_specs=[
            pl.BlockSpec((tm, ATTN_Q_W), lambda j, i: (i, 0)),
            pl.BlockSpec((tm, RET_V_W), lambda j, i: (i, 0)),
            pl.BlockSpec((ATTN_Q_W, tn), lambda j, i: (0, j)),
            pl.BlockSpec((RET_V_W, tn), lambda j, i: (0, j)),
            pl.BlockSpec((tm, tn), lambda j, i: (i, ga_off + j)),
            pl.BlockSpec((tm, tn), lambda j, i: (i, gr_off + j)),
        ],
        out_specs=pl.BlockSpec((tm, tn), lambda j, i: (i, j)),
        scratch_shapes=[pltpu.VMEM((ATTN_Q_W, tn), BF16), pltpu.VMEM((RET_V_W, tn), BF16)],
        compiler_params=_cparams(("arbitrary", "arbitrary")),
        name="merge_up",
    )(attn, ret, wa, wr, proj, proj)


def _out_proj_kernel(m_ref, w_ref, x_ref, mod_ref, o_ref, wb, *, gate_idx):
    _cast_weight_once(w_ref, wb)
    y = _dot(m_ref[...], wb[...])
    o_ref[...] = x_ref[...] + mod_ref[gate_idx:gate_idx + 1, :] * y


def _out_proj(merged, w, x2, mod, S, gate_idx):
    T, D = x2.shape
    tm = min(1024, S)
    tn = min(512, D)
    return pl.pallas_call(
        functools.partial(_out_proj_kernel, gate_idx=gate_idx),
        out_shape=jax.ShapeDtypeStruct((T, D), F32),
        grid=(D // tn, T // tm),
        in_specs=[
            pl.BlockSpec((tm, D), lambda j, i: (i, 0)),
            pl.BlockSpec((D, tn), lambda j, i: (0, j)),
            pl.BlockSpec((tm, tn), lambda j, i: (i, j)),
            pl.BlockSpec((None, N_ADA, tn), lambda j, i: ((i * tm) // S, 0, j)),
        ],
        out_specs=pl.BlockSpec((tm, tn), lambda j, i: (i, j)),
        scratch_shapes=[pltpu.VMEM((D, tn), BF16)],
        compiler_params=_cparams(("arbitrary", "arbitrary")),
        name="out_proj",
    )(merged, w, x2, mod)


def _pack_bf16_pairs(v):
    n = v.shape[1] // 2
    bits = lax.bitcast_convert_type(v.astype(BF16).astype(F32), U32)
    return jnp.bitwise_or(jnp.bitwise_and(bits[:, n:], jnp.uint32(0xFFFF0000)),
                          jnp.right_shift(bits[:, :n], jnp.uint32(16)))


def _unpack_bf16_pairs(w):
    lo = lax.bitcast_convert_type(jnp.left_shift(w, jnp.uint32(16)), F32)
    hi = lax.bitcast_convert_type(jnp.bitwise_and(w, jnp.uint32(0xFFFF0000)), F32)
    return jnp.concatenate([lo, hi], axis=1)


def _norm_router_kernel(x_ref, nw_ref, mod_ref, rwt_ref, rb_ref, hp_ref, e_ref, w_ref, r_ref, cnt_ref, carry,
                        *, shift_idx, scale_idx):
    i = pl.program_id(0)

    @pl.when(i == 0)
    def _():
        carry[...] = jnp.zeros_like(carry)

    h = _norm_mod_value(x_ref[...], nw_ref[...], mod_ref[shift_idx:shift_idx + 1, :],
                        mod_ref[scale_idx:scale_idx + 1, :])
    hp_ref[...] = _pack_bf16_pairs(h)
    hb = h.astype(BF16)
    tm = hb.shape[0]
    E = rwt_ref.shape[0]
    logits = _dot_nt(rwt_ref[...], hb) + rb_ref[...]
    iota_e = lax.broadcasted_iota(I32, (E, tm), 0)
    vals, idxs = [], []
    work = logits
    sel = jnp.zeros((E, tm), F32)
    for _k in range(TOP_K):
        m = jnp.max(work, axis=0, keepdims=True)
        idx = jnp.min(jnp.where(work == m, iota_e, E), axis=0, keepdims=True)
        hit = iota_e == idx
        vals.append(m)
        idxs.append(idx)
        work = jnp.where(hit, NEG_BIG, work)
        sel = sel + hit.astype(F32)
    ex = [jnp.exp(v - vals[0]) for v in vals]
    tot = ex[0]
    for v in ex[1:]:
        tot = tot + v
    ri = lax.broadcasted_iota(I32, (tm, tm), 0)
    ci = lax.broadcasted_iota(I32, (tm, tm), 1)
    upper = (ri < ci).astype(BF16)
    before = _dot(sel.astype(BF16), upper) + carry[:, 0:1]
    for k in range(TOP_K):
        e_ref[k:k + 1, :] = idxs[k]
        w_ref[k:k + 1, :] = ex[k] / tot
        r_ref[k:k + 1, :] = jnp.sum(jnp.where(iota_e == idxs[k], before, 0.0), axis=0, keepdims=True).astype(I32)
    carry[...] = carry[...] + jnp.sum(sel, axis=1, keepdims=True)
    cnt_ref[...] = carry[...]


def _norm_router(x1, nw, mod, router_w, router_b, S, shift_idx, scale_idx):
    T, D = x1.shape
    E = router_w.shape[1]
    tm = min(256, S)
    outs = pl.pallas_call(
        functools.partial(_norm_router_kernel, shift_idx=shift_idx, scale_idx=scale_idx),
        out_shape=(
            jax.ShapeDtypeStruct((T, D // 2), U32),
            jax.ShapeDtypeStruct((TOP_K, T), I32),
            jax.ShapeDtypeStruct((TOP_K, T), F32),
            jax.ShapeDtypeStruct((TOP_K, T), I32),
            jax.ShapeDtypeStruct((E, 128), F32),
        ),
        grid=(T // tm,),
        in_specs=[
            pl.BlockSpec((tm, D), lambda i: (i, 0)),
            pl.BlockSpec((1, D), lambda i: (0, 0)),
            pl.BlockSpec((None, N_ADA, D), lambda i: ((i * tm) // S, 0, 0)),
            pl.BlockSpec((E, D), lambda i: (0, 0)),
            pl.BlockSpec((E, 1), lambda i: (0, 0)),
        ],
        out_specs=(
            pl.BlockSpec((tm, D // 2), lambda i: (i, 0)),
            pl.BlockSpec((TOP_K, tm), lambda i: (0, i)),
            pl.BlockSpec((TOP_K, tm), lambda i: (0, i)),
            pl.BlockSpec((TOP_K, tm), lambda i: (0, i)),
            pl.BlockSpec((E, 128), lambda i: (0, 0)),
        ),
        scratch_shapes=[pltpu.VMEM((E, 128), F32)],
        compiler_params=_cparams(("arbitrary",)),
        name="norm_router",
    )(x1, nw.reshape(1, D), mod, router_w.T.astype(BF16), router_b.reshape(E, 1).astype(F32))
    return outs


def _dispatch_kernel(pos_ref, pad_ref, hp_ref, xs_ref, zeros, sem, zsem, *, tm, T, E, n_tiles):
    base = pl.program_id(0) * tm

    @pl.when(pl.program_id(0) == 0)
    def _():
        zeros[...] = jnp.zeros_like(zeros)
        def pad_copy(row):
            return pltpu.make_async_copy(zeros.at[pl.ds(0, 1)], xs_ref.at[pl.ds(row, 1)], zsem)

        def tail_copy(t):
            return pltpu.make_async_copy(
                zeros, xs_ref.at[pl.ds(pl.multiple_of(t * MOE_TILE, MOE_TILE), MOE_TILE)], zsem)

        def pad_start(e, carry):
            start = pad_ref[e]
            lax.fori_loop(0, pad_ref[E + e], lambda r, c: (pad_copy(start + r).start(), c)[1], 0)
            return carry

        def pad_wait(e, carry):
            start = pad_ref[e]
            lax.fori_loop(0, pad_ref[E + e], lambda r, c: (pad_copy(start + r).wait(), c)[1], 0)
            return carry

        def tail_start(t, carry):
            tail_copy(t).start()
            return carry

        def tail_wait(t, carry):
            tail_copy(t).wait()
            return carry

        lax.fori_loop(0, E, pad_start, 0)
        lax.fori_loop(pad_ref[2 * E], n_tiles, tail_start, 0)
        lax.fori_loop(0, E, pad_wait, 0)
        lax.fori_loop(pad_ref[2 * E], n_tiles, tail_wait, 0)

    def start_body(t, carry):
        for k in range(TOP_K):
            dst = pos_ref[k * T + base + t]
            pltpu.make_async_copy(hp_ref.at[pl.ds(t, 1)], xs_ref.at[pl.ds(dst, 1)], sem).start(priority=k % 2)
        return carry

    lax.fori_loop(0, tm, start_body, 0, unroll=8)
    for k in range(TOP_K):
        pltpu.make_async_copy(hp_ref, xs_ref.at[pl.ds(0, tm)], sem).wait()


def _dispatch(pos_flat, pad_info, hp, n_tiles, E):
    T, W = hp.shape
    tm = min(512, T)
    grid_spec = pltpu.PrefetchScalarGridSpec(
        num_scalar_prefetch=2,
        grid=(T // tm,),
        in_specs=[pl.BlockSpec((tm, W), lambda i, p, q: (i, 0))],
        out_specs=pl.BlockSpec(memory_space=pl.ANY),
        scratch_shapes=[pltpu.VMEM((MOE_TILE, W), U32), pltpu.SemaphoreType.DMA, pltpu.SemaphoreType.DMA],
    )
    return pl.pallas_call(
        functools.partial(_dispatch_kernel, tm=tm, T=T, E=E, n_tiles=n_tiles),
        out_shape=jax.ShapeDtypeStruct((n_tiles * MOE_TILE, W), U32),
        grid_spec=grid_spec,
        compiler_params=_cparams(("arbitrary",)),
        name="dispatch",
    )(pos_flat, pad_info, hp)


def _expert_changed(te_ref, i):
    return jnp.logical_or(i == 0, te_ref[i] != te_ref[jnp.maximum(i - 1, 0)])


def _cast_rows(src_ref, dst_ref, rows_per_pass=256):
    def body(r, carry):
        rows = pl.ds(pl.multiple_of(r * rows_per_pass, rows_per_pass), rows_per_pass)
        dst_ref[rows, :] = src_ref[rows, :].astype(dst_ref.dtype)
        return carry

    lax.fori_loop(0, src_ref.shape[0] // rows_per_pass, body, 0)


def _expert_weight_stage(te_ref, nx_ref, i, first_step, n, n_passes, copies, slot_ref, consume):
    @pl.when(_expert_changed(te_ref, i))
    def _():
        @pl.when(first_step)
        def _():
            slot_ref[0] = 0
            for cp in copies(te_ref[0], 0, 0):
                cp.start()

        slot = slot_ref[0]
        for cp in copies(te_ref[i], n, slot):
            cp.wait()
        consume(slot)
        same_pass = nx_ref[i] >= 0
        nxt_e = jnp.where(same_pass, nx_ref[i], te_ref[0])
        nxt_n = jnp.where(same_pass, n, n + 1)

        @pl.when(jnp.logical_or(same_pass, n + 1 < n_passes))
        def _():
            for cp in copies(nxt_e, nxt_n, 1 - slot):
                cp.start()

        slot_ref[0] = 1 - slot


def _expert_up_kernel(te_ref, nu_ref, nx_ref, xs_ref, w1_hbm, bg_ref, bu_ref, o_ref, stage, wgb, wub, slot_ref, sem,
                      *, tn, F, nf):
    n = pl.program_id(0)
    i = pl.program_id(1)

    def copies(e, nn, slot):
        c0 = pl.multiple_of(nn * tn, tn)
        return (pltpu.make_async_copy(w1_hbm.at[e, :, pl.ds(c0, tn)], stage.at[slot, 0], sem.at[slot]),
                pltpu.make_async_copy(w1_hbm.at[e, :, pl.ds(F + c0, tn)], stage.at[slot, 1], sem.at[slot]))

    def consume(slot):
        _cast_rows(stage.at[slot, 0], wgb)
        _cast_rows(stage.at[slot, 1], wub)

    _expert_weight_stage(te_ref, nx_ref, i, jnp.logical_and(n == 0, i == 0), n, nf, copies, slot_ref, consume)

    @pl.when(i < nu_ref[0])
    def _():
        x = _unpack_bf16_pairs(xs_ref[...]).astype(BF16)
        gate = jnp.minimum(_dot(x, wgb[...]) + bg_ref[...], SWIGLU_LIMIT)
        up = jnp.clip(_dot(x, wub[...]) + bu_ref[...], -SWIGLU_LIMIT, SWIGLU_LIMIT)
        act = gate * _sigmoid(SWIGLU_ALPHA * gate) * (up + 1.0)
        o_ref[...] = act.astype(o_ref.dtype)

    @pl.when(i >= nu_ref[0])
    def _():
        o_ref[...] = jnp.zeros_like(o_ref)


def _expert_down_kernel(te_ref, nu_ref, nx_ref, a_ref, w2_hbm, b2_ref, o_ref, stage, w2b, slot_ref, sem):
    i = pl.program_id(0)

    def copies(e, nn, slot):
        del nn
        return (pltpu.make_async_copy(w2_hbm.at[e], stage.at[slot], sem.at[slot]),)

    def consume(slot):
        _cast_rows(stage.at[slot], w2b)

    _expert_weight_stage(te_ref, nx_ref, i, i == 0, 0, 1, copies, slot_ref, consume)

    @pl.when(i < nu_ref[0])
    def _():
        y = _dot(a_ref[...], w2b[...]) + b2_ref[...]
        o_ref[...] = _pack_bf16_pairs(y)

    @pl.when(i >= nu_ref[0])
    def _():
        o_ref[...] = jnp.zeros_like(o_ref)


def _expert_ffn(xs, tile_e, n_used, next_e, w1, b1, w2, b2):
    R, W = xs.shape
    E, D, F2 = w1.shape
    F = F2 // 2
    tm = MOE_TILE
    nt = R // tm

    tn = min(512, F)
    nf = F // tn
    b1r = b1.reshape(E, 1, F2)

    act = pl.pallas_call(
        functools.partial(_expert_up_kernel, tn=tn, F=F, nf=nf),
        out_shape=jax.ShapeDtypeStruct((R, F), BF16),
        grid_spec=pltpu.PrefetchScalarGridSpec(
            num_scalar_prefetch=3,
            grid=(nf, nt),
            in_specs=[
                pl.BlockSpec((tm, W), lambda n, i, te, nu, nx: (jnp.minimum(i, nu[0] - 1), 0)),
                pl.BlockSpec(memory_space=pl.ANY),
                pl.BlockSpec((None, 1, tn), lambda n, i, te, nu, nx: (te[i], 0, n)),
                pl.BlockSpec((None, 1, tn), lambda n, i, te, nu, nx: (te[i], 0, nf + n)),
            ],
            out_specs=pl.BlockSpec((tm, tn), lambda n, i, te, nu, nx: (i, n)),
            scratch_shapes=[pltpu.VMEM((2, 2, D, tn), F32), pltpu.VMEM((D, tn), BF16), pltpu.VMEM((D, tn), BF16),
                            pltpu.SMEM((1,), I32), pltpu.SemaphoreType.DMA((2,))],
        ),
        compiler_params=_cparams(("arbitrary", "arbitrary")),
        name="expert_up",
    )(tile_e, n_used, next_e, xs, w1, b1r, b1r)
    y = pl.pallas_call(
        _expert_down_kernel,
        out_shape=jax.ShapeDtypeStruct((R, D // 2), U32),
        grid_spec=pltpu.PrefetchScalarGridSpec(
            num_scalar_prefetch=3,
            grid=(nt,),
            in_specs=[
                pl.BlockSpec((tm, F), lambda i, te, nu, nx: (jnp.minimum(i, nu[0] - 1), 0)),
                pl.BlockSpec(memory_space=pl.ANY),
                pl.BlockSpec((None, 1, D), lambda i, te, nu, nx: (te[i], 0, 0)),
            ],
            out_specs=pl.BlockSpec((tm, D // 2), lambda i, te, nu, nx: (i, 0)),
            scratch_shapes=[pltpu.VMEM((2, F, D), F32), pltpu.VMEM((F, D), BF16),
                            pltpu.SMEM((1,), I32), pltpu.SemaphoreType.DMA((2,))],
        ),
        compiler_params=_cparams(("arbitrary",)),
        name="expert_down",
    )(tile_e, n_used, next_e, act, w2, b2.reshape(E, 1, D))
    return y


def _combine_kernel(pos_ref, x_ref, w_ref, mod_ref, y_ref, o_ref, buf, sem, *, tm, T, gate_idx):
    i = pl.program_id(0)
    n = pl.num_programs(0)
    W = buf.shape[-1]
    th = tm
    ring = buf.shape[0]

    def issue_row(first_token, slot, t):
        for k in range(TOP_K):
            src = pos_ref[k * T + first_token + t]
            pltpu.make_async_copy(y_ref.at[pl.ds(src, 1)], buf.at[slot, k, pl.ds(t, 1)],
                                  sem.at[slot]).start(priority=k % 2)

    def wait_half(slot):
        for k in range(TOP_K):
            pltpu.make_async_copy(y_ref.at[pl.ds(0, th)], buf.at[slot, k], sem.at[slot]).wait()

    rc = 8
    cw = min(512, W)

    def consume(slot, next_first_token, next_slot):
        def rows_body(r, carry):
            rows = pl.ds(pl.multiple_of(r * rc, rc), rc)
            brow = rows
            wv = w_ref[rows, :]
            wk = [jnp.broadcast_to(wv[:, k:k + 1], (rc, cw)) for k in range(TOP_K)]
            for c in range(W // cw):
                lo = hi = None
                for k in range(TOP_K):
                    u = buf[slot, k, brow, c * cw:(c + 1) * cw]
                    l = wk[k] * lax.bitcast_convert_type(jnp.left_shift(u, jnp.uint32(16)), F32)
                    h = wk[k] * lax.bitcast_convert_type(jnp.bitwise_and(u, jnp.uint32(0xFFFF0000)), F32)
                    lo = l if lo is None else lo + l
                    hi = h if hi is None else hi + h
                for half, acc in ((0, lo), (1, hi)):
                    cols = slice(half * W + c * cw, half * W + (c + 1) * cw)
                    o_ref[rows, cols] = x_ref[rows, cols] + mod_ref[gate_idx:gate_idx + 1, cols] * acc
            for t in range(rc):
                issue_row(next_first_token, next_slot, r * rc + t)
            return carry

        lax.fori_loop(0, th // rc, rows_body, 0)

    @pl.when(i == 0)
    def _():
        for b in range(2):
            lax.fori_loop(0, tm, lambda t, c: (issue_row(b * tm, b, t), c)[1], 0, unroll=8)

    wait_half(i % ring)
    consume(i % ring, jnp.minimum((i + 2) * tm, T - tm), (i + 2) % ring)

    @pl.when(i == n - 1)
    def _():
        wait_half((i + 1) % ring)
        wait_half((i + 2) % ring)


def _combine(pos_flat, x1, w_tk, mod, y, S, gate_idx):
    T, D = x1.shape
    tm = min(128, S)
    assert T // tm >= 2
    grid_spec = pltpu.PrefetchScalarGridSpec(
        num_scalar_prefetch=1,
        grid=(T // tm,),
        in_specs=[
            pl.BlockSpec((tm, D), lambda i, p: (i, 0)),
            pl.BlockSpec((tm, TOP_K), lambda i, p: (i, 0)),
            pl.BlockSpec((None, N_ADA, D), lambda i, p: ((i * tm) // S, 0, 0)),
            pl.BlockSpec(memory_space=pl.ANY),
        ],
        out_specs=pl.BlockSpec((tm, D), lambda i, p: (i, 0)),
        scratch_shapes=[pltpu.VMEM((3, TOP_K, tm, D // 2), U32), pltpu.SemaphoreType.DMA((3,))],
    )
    return pl.pallas_call(
        functools.partial(_combine_kernel, tm=tm, T=T, gate_idx=gate_idx),
        out_shape=jax.ShapeDtypeStruct((T, D), F32),
        grid_spec=grid_spec,
        compiler_params=_cparams(("arbitrary",)),
        name="combine",
    )(pos_flat, x1, w_tk, mod, y)


def _layer(x2, mod, B, S, rel_bias, norm_mix_w, w_in, q_norm_w, k_norm_w, attn_sink, ret_decay_fwd, ret_decay_bwd,
           ret_gn_w, ret_gn_b, w_up_attn, w_up_ret, w_out, norm_ffn_w, router_w, router_b,
           expert_w1, expert_b1, expert_w2, expert_b2):
    T, D = x2.shape
    E = router_w.shape[1]
    h = _norm_mod(x2, norm_mix_w, mod, S, 0, 1)
    proj = _in_proj(h, w_in, q_norm_w, k_norm_w)
    attn = _window_attention(proj, _attn_bias_table(rel_bias), attn_sink, B, S)
    ret = _retention(proj, ret_decay_fwd, ret_decay_bwd, ret_gn_w, ret_gn_b, B, S)
    merged = _merge_up(attn, ret, w_up_attn, w_up_ret, proj, D)
    x1 = _out_proj(merged, w_out, x2, mod, S, 2)

    hp, top_e, top_w, rank, cnt = _norm_router(x1, norm_ffn_w, mod, router_w, router_b, S, 3, 4)
    tm = MOE_TILE
    counts = cnt[:, 0].astype(I32)
    tiles_per_e = (counts + tm - 1) // tm
    tile_end = jnp.cumsum(tiles_per_e)
    group_start = (tile_end - tiles_per_e) * tm
    n_tiles = (T * TOP_K) // tm + E
    n_used = tile_end[-1]
    tile_ids = jnp.minimum(jnp.arange(n_tiles, dtype=I32), n_used - 1)
    tile_e = jnp.minimum(jnp.sum((tile_ids[:, None] >= tile_end[None, :]).astype(I32), axis=1), E - 1)
    e_ids = jnp.arange(E, dtype=I32)
    later = jnp.logical_and(e_ids[None, :] > e_ids[:, None], tiles_per_e[None, :] > 0)
    next_of_e = jnp.min(jnp.where(later, e_ids[None, :], E), axis=1)
    next_of_e = jnp.where(next_of_e == E, -1, next_of_e)
    next_e = jnp.sum(jnp.where(tile_e[:, None] == e_ids, next_of_e, 0), axis=1).astype(I32)
    start_of = jnp.sum(jnp.where(top_e[:, :, None] == e_ids, group_start, 0), axis=-1)
    pos_flat = (start_of + rank).astype(I32).reshape(-1)
    pad_info = jnp.concatenate([group_start + counts, tiles_per_e * tm - counts, n_used.reshape(1)]).astype(I32)

    xs = _dispatch(pos_flat, pad_info, hp, n_tiles, E)
    y = _expert_ffn(xs, tile_e, n_used.reshape(1).astype(I32), next_e, expert_w1, expert_b1, expert_w2, expert_b2)
    return _combine(pos_flat, x1, top_w.T, mod, y, S, 5)


def kernel(x, c, rel_bias, ada_w, ada_b, norm_mix_w, w_in, q_norm_w, k_norm_w, attn_sink, ret_decay_fwd,
           ret_decay_bwd, ret_gn_w, ret_gn_b, w_up_attn, w_up_ret, w_out, norm_ffn_w, router_w, router_b,
           expert_w1, expert_b1, expert_w2, expert_b2):
    B, S, D = x.shape
    x2 = x.reshape(B * S, D)
    for l in range(ada_w.shape[0]):
        mod = _ada_mod(c, ada_w[l], ada_b[l])
        x2 = _layer(x2, mod, B, S, rel_bias, norm_mix_w[l], w_in[l], q_norm_w[l], k_norm_w[l], attn_sink[l],
                    ret_decay_fwd[l], ret_decay_bwd[l], ret_gn_w[l], ret_gn_b[l], w_up_attn[l], w_up_ret[l],
                    w_out[l], norm_ffn_w[l], router_w[l], router_b[l], expert_w1[l], expert_b1[l],
                    expert_w2[l], expert_b2[l])
    return x2.reshape(B, S, D)
```

```python
import functools
import math

import jax
import jax.numpy as jnp
from jax import lax
from jax.experimental import pallas as pl
from jax.experimental.pallas import tpu as pltpu

F32 = jnp.float32
BF16 = jnp.bfloat16
U32 = jnp.uint32
I32 = jnp.int32

ATTN_HEADS = 16
ATTN_KV_HEADS = 4
ATTN_HEAD_DIM = 128
WINDOW = 128
ATTN_BLOCK = 128
N_BUCKETS = 32
MAX_DISTANCE = 128
RET_HEADS = 8
RET_QK_DIM = 128
RET_V_DIM = 256
RET_CHUNK = 128
ROPE_BASE = 10000.0
TOP_K = 4
SWIGLU_LIMIT = 7.0
SWIGLU_ALPHA = 1.702
N_ADA = 6
EPS = 1e-6
NEG_INF = -1e30
NEG_BIG = -3.0e38
LOG2E = 1.4426950408889634

ATTN_Q_W = ATTN_HEADS * ATTN_HEAD_DIM
ATTN_KV_W = ATTN_KV_HEADS * ATTN_HEAD_DIM
RET_QK_W = RET_HEADS * RET_QK_DIM
RET_V_W = RET_HEADS * RET_V_DIM

VMEM_LIMIT_BYTES = 56 * 1024 * 1024
MOE_TILE = 256
RET_TILE = 256


def _cparams(sem):
    return pltpu.CompilerParams(dimension_semantics=sem, vmem_limit_bytes=VMEM_LIMIT_BYTES)


def _dot(a, b):
    return jnp.dot(a, b, preferred_element_type=F32)


def _dot_nt(a, b):
    return lax.dot_general(a, b, (((1,), (1,)), ((), ())), preferred_element_type=F32)


def _dot_tn(a, b):
    return lax.dot_general(a, b, (((0,), (0,)), ((), ())), preferred_element_type=F32)


def _sigmoid(x):
    return 1.0 / (1.0 + jnp.exp(-x))


def _ada_kernel(c_ref, w_ref, b_ref, o_ref):
    c = c_ref[...]
    cs = (c * _sigmoid(c)).astype(BF16)
    o_ref[...] = _dot(cs, w_ref[...].astype(BF16)) + b_ref[...]


def _ada_mod(c, ada_w, ada_b):
    B, D = c.shape
    N = ada_w.shape[1]
    rows = 8
    cp = jnp.zeros((rows, D), F32).at[:B].set(c)
    tn = min(512, N)
    out = pl.pallas_call(
        _ada_kernel,
        out_shape=jax.ShapeDtypeStruct((rows, N), F32),
        grid=(N // tn,),
        in_specs=[
            pl.BlockSpec((rows, D), lambda j: (0, 0)),
            pl.BlockSpec((D, tn), lambda j: (0, j)),
            pl.BlockSpec((1, tn), lambda j: (0, j)),
        ],
        out_specs=pl.BlockSpec((rows, tn), lambda j: (0, j)),
        compiler_params=_cparams(("arbitrary",)),
        name="ada_mod",
    )(cp, ada_w, ada_b.reshape(1, N))
    return out[:B].reshape(B, N_ADA, D)


def _norm_mod_value(x, nw, shift, scale):
    ms = jnp.mean(x * x, axis=-1, keepdims=True)
    h = x * lax.rsqrt(ms + EPS) * nw
    return h * (1.0 + scale) + shift


def _norm_mod_kernel(x_ref, nw_ref, mod_ref, o_ref, *, shift_idx, scale_idx):
    h = _norm_mod_value(x_ref[...], nw_ref[...], mod_ref[shift_idx:shift_idx + 1, :],
                        mod_ref[scale_idx:scale_idx + 1, :])
    o_ref[...] = h.astype(BF16)


def _norm_mod(x2, nw, mod, S, shift_idx, scale_idx):
    T, D = x2.shape
    tm = min(256, S)
    return pl.pallas_call(
        functools.partial(_norm_mod_kernel, shift_idx=shift_idx, scale_idx=scale_idx),
        out_shape=jax.ShapeDtypeStruct((T, D), BF16),
        grid=(T // tm,),
        in_specs=[
            pl.BlockSpec((tm, D), lambda i: (i, 0)),
            pl.BlockSpec((1, D), lambda i: (0, 0)),
            pl.BlockSpec((None, N_ADA, D), lambda i: ((i * tm) // S, 0, 0)),
        ],
        out_specs=pl.BlockSpec((tm, D), lambda i: (i, 0)),
        compiler_params=_cparams(("arbitrary",)),
        name="norm_mod",
    )(x2, nw.reshape(1, D), mod)


def _cast_weight_once(w_ref, wb_ref):
    @pl.when(pl.program_id(1) == 0)
    def _():
        wb_ref[...] = w_ref[...].astype(BF16)


def _in_proj_kernel(a_ref, w_hbm, cw_ref, cf_ref, bd_ref, o_ref, stage, wb, sem, *, n_norm_tiles):
    j = pl.program_id(0)
    tn = wb.shape[1]

    def w_copy(jj):
        return pltpu.make_async_copy(w_hbm.at[:, pl.ds(pl.multiple_of(jj * tn, tn), tn)], stage, sem)

    @pl.when(pl.program_id(1) == 0)
    def _():
        @pl.when(j == 0)
        def _():
            w_copy(0).start()

        w_copy(j).wait()
        _cast_rows(stage, wb)

        @pl.when(j + 1 < pl.num_programs(0))
        def _():
            w_copy(j + 1).start()

    @pl.when(j < n_norm_tiles)
    def _():
        cb = bd_ref.shape[0]
        for c in range(tn // cb):
            cols = slice(c * cb, (c + 1) * cb)
            acc = _dot(a_ref[...], wb[:, cols])
            ms = _dot((acc * acc).astype(BF16), bd_ref[...]) * (1.0 / ATTN_HEAD_DIM)
            inv = jnp.where(cf_ref[:, cols] > 0.0, lax.rsqrt(ms + EPS), 1.0)
            o_ref[:, cols] = (acc * inv * cw_ref[:, cols]).astype(o_ref.dtype)

    @pl.when(j >= n_norm_tiles)
    def _():
        o_ref[...] = _dot(a_ref[...], wb[...]).astype(o_ref.dtype)


def _in_proj(h, w, q_norm_w, k_norm_w):
    T, K = h.shape
    N = w.shape[1]
    tm = min(1024, T)
    tn = 1024 if N % 1024 == 0 else 512
    hd = ATTN_HEAD_DIM
    n_norm = ATTN_Q_W + ATTN_KV_W
    n_norm_tiles = -(-n_norm // tn)
    pad = n_norm_tiles * tn - n_norm
    col_w = jnp.concatenate([jnp.tile(q_norm_w.astype(F32) * (hd ** -0.5 * LOG2E), ATTN_HEADS),
                             jnp.tile(k_norm_w.astype(F32), ATTN_KV_HEADS), jnp.ones((pad,), F32)]).reshape(1, -1)
    col_flag = jnp.concatenate([jnp.ones((n_norm,), F32), jnp.zeros((pad,), F32)]).reshape(1, -1)
    cb = min(512, tn)
    lane_head = jnp.arange(cb, dtype=I32) // hd
    block_ones = (lane_head[:, None] == lane_head[None, :]).astype(BF16)
    last = n_norm_tiles - 1
    return pl.pallas_call(
        functools.partial(_in_proj_kernel, n_norm_tiles=n_norm_tiles),
        out_shape=jax.ShapeDtypeStruct((T, N), BF16),
        grid=(N // tn, T // tm),
        in_specs=[
            pl.BlockSpec((tm, K), lambda j, i: (i, 0)),
            pl.BlockSpec(memory_space=pl.ANY),
            pl.BlockSpec((1, tn), lambda j, i: (0, jnp.minimum(j, last))),
            pl.BlockSpec((1, tn), lambda j, i: (0, jnp.minimum(j, last))),
            pl.BlockSpec((cb, cb), lambda j, i: (0, 0)),
        ],
        out_specs=pl.BlockSpec((tm, tn), lambda j, i: (i, j)),
        scratch_shapes=[pltpu.VMEM((K, tn), F32), pltpu.VMEM((K, tn), BF16), pltpu.SemaphoreType.DMA],
        compiler_params=_cparams(("arbitrary", "arbitrary")),
        name="in_proj",
    )(h, w, col_w, col_flag, block_ones)


def _t5_bucket(rel):
    nb = N_BUCKETS // 2
    max_exact = nb // 2
    base = jnp.where(rel > 0, nb, 0)
    n = jnp.abs(rel)
    nf = jnp.maximum(n, 1).astype(F32)
    large = max_exact + (jnp.log(nf / max_exact) / math.log(MAX_DISTANCE / max_exact) * (nb - max_exact)).astype(I32)
    large = jnp.minimum(large, nb - 1)
    return base + jnp.where(n < max_exact, n, large)


def _attn_bias_table(rel_bias):
    blk = ATTN_BLOCK
    qi = jnp.arange(blk, dtype=I32)[:, None]
    kj = jnp.arange(3 * blk, dtype=I32)[None, :]
    rel = kj - blk - qi
    bucket = _t5_bucket(rel)
    table = rel_bias.astype(F32).T
    bias = jnp.zeros((table.shape[0],) + bucket.shape, F32)
    for b in range(N_BUCKETS):
        bias = jnp.where(bucket[None] == b, table[:, b][:, None, None], bias)
    bias = jnp.where((jnp.abs(rel) <= WINDOW)[None], bias * LOG2E, NEG_INF)
    G = ATTN_HEADS // ATTN_KV_HEADS
    bias = bias.reshape(ATTN_KV_HEADS, G, blk, 3 * blk)
    bias = jnp.transpose(bias, (0, 3, 1, 2)).reshape(ATTN_KV_HEADS, 3 * blk, G * blk)
    key = jnp.arange(3 * blk, dtype=I32)[None, :, None]
    no_prev, no_next = key < blk, key >= 2 * blk
    return jnp.stack([bias, jnp.where(no_prev, NEG_INF, bias), jnp.where(no_next, NEG_INF, bias),
                      jnp.where(jnp.logical_or(no_prev, no_next), NEG_INF, bias)])


def _attn_kernel(sink_ref, q_ref, kp_ref, kc_ref, kn_ref, vp_ref, vc_ref, vn_ref, bias_ref, o_ref):
    blk, hd = ATTN_BLOCK, ATTN_HEAD_DIM
    G = ATTN_HEADS // ATTN_KV_HEADS
    for hk in range(ATTN_KV_HEADS):
        sl = slice(hk * hd, (hk + 1) * hd)
        k3 = jnp.concatenate([kp_ref[:, sl], kc_ref[:, sl], kn_ref[:, sl]], axis=0)
        v3 = jnp.concatenate([vp_ref[:, sl], vc_ref[:, sl], vn_ref[:, sl]], axis=0)
        q4 = jnp.concatenate([q_ref[:, (hk * G + g) * hd:(hk * G + g + 1) * hd] for g in range(G)],
                             axis=0)
        sink = jnp.concatenate([jnp.full((1, blk), sink_ref[hk * G + g], F32) for g in range(G)], axis=1)
        logits = _dot_nt(k3, q4) + bias_ref[hk]
        m = jnp.maximum(jnp.max(logits, axis=0, keepdims=True), sink)
        p = jnp.exp2(logits - m)
        denom = jnp.sum(p, axis=0, keepdims=True) + jnp.exp2(sink - m)
        out_t = _dot_tn(v3, p.astype(BF16)) * (1.0 / denom)
        out = out_t.T
        for g in range(G):
            h = hk * G + g
            o_ref[:, h * hd:(h + 1) * hd] = out[g * blk:(g + 1) * blk].astype(o_ref.dtype)


def _window_attention(proj, bias_tab, sink, B, S):
    T = proj.shape[0]
    blk = ATTN_BLOCK
    nb = S // blk
    kcol = ATTN_Q_W // ATTN_KV_W
    vcol = kcol + 1

    def kv_spec(col, off):
        return pl.BlockSpec((blk, ATTN_KV_W), lambda b, n, s: (b * nb + jnp.clip(n + off, 0, nb - 1), col))

    grid_spec = pltpu.PrefetchScalarGridSpec(
        num_scalar_prefetch=1,
        grid=(B, nb),
        in_specs=[
            pl.BlockSpec((blk, ATTN_Q_W), lambda b, n, s: (b * nb + n, 0)),
            kv_spec(kcol, -1), kv_spec(kcol, 0), kv_spec(kcol, 1),
            kv_spec(vcol, -1), kv_spec(vcol, 0), kv_spec(vcol, 1),
            pl.BlockSpec((None, ATTN_KV_HEADS, 3 * blk, (ATTN_HEADS // ATTN_KV_HEADS) * blk),
                         lambda b, n, s: ((n == 0).astype(I32) + 2 * (n == nb - 1).astype(I32), 0, 0, 0)),
        ],
        out_specs=pl.BlockSpec((blk, ATTN_Q_W), lambda b, n, s: (b * nb + n, 0)),
    )
    return pl.pallas_call(
        _attn_kernel,
        out_shape=jax.ShapeDtypeStruct((T, ATTN_Q_W), BF16),
        grid_spec=grid_spec,
        compiler_params=_cparams(("arbitrary", "arbitrary")),
        name="window_attn",
    )(sink.astype(F32) * LOG2E, proj, proj, proj, proj, proj, proj, proj, bias_tab)


def _rope_tables(S):
    d = RET_QK_DIM
    inv = ROPE_BASE ** (-jnp.arange(0, d, 2, dtype=F32) / d)
    ang = jnp.arange(S, dtype=F32)[:, None] * inv[None, :]
    cos, sin = jnp.cos(ang), jnp.sin(ang)
    return jnp.concatenate([cos, cos], axis=-1), jnp.concatenate([-sin, sin], axis=-1)


def _ret_kernel(df_ref, db_ref, q_ref, k_ref, v_ref, g_ref, cos_ref, sin_ref, gw_ref, gb_ref, o_ref,
                qb, kb, q2, kv, sprev, *, S, C):
    h = pl.program_id(1)
    nc = S // C
    dk = RET_QK_DIM
    half = dk // 2

    rowf = lax.broadcasted_iota(I32, (C, 1), 0).astype(F32)
    lg_f = -jnp.exp(jnp.full((1, 1), df_ref[h], F32))
    lg_b = -jnp.exp(jnp.full((1, 1), db_ref[h], F32))
    qdec_f = jnp.exp((rowf + 1.0) * lg_f)
    kdec_f = jnp.exp((C - 1.0 - rowf) * lg_f)
    qdec_b = jnp.exp((C - rowf) * lg_b)
    kdec_b = jnp.exp(rowf * lg_b)
    cd_f = jnp.exp(C * lg_f)
    cd_b = jnp.exp(C * lg_b)

    def a_body(n, carry):
        rows = pl.ds(pl.multiple_of(n * C, C), C)
        co = cos_ref[rows, :]
        si = sin_ref[rows, :]
        q = q_ref[rows, :].astype(F32)
        k = k_ref[rows, :].astype(F32)
        qr = q * co + pltpu.roll(q, half, 1) * si
        kr = (k * co + pltpu.roll(k, half, 1) * si) * (dk ** -0.5)
        qb[rows, :] = qr.astype(BF16)
        kb[rows, :] = kr.astype(BF16)
        q2[rows, :] = jnp.concatenate([qr * qdec_f, qr * qdec_b], axis=1).astype(BF16)
        k2 = jnp.concatenate([kr * kdec_f, kr * kdec_b], axis=1).astype(BF16)
        kv[n] = _dot_tn(k2, v_ref[rows, :])
        return carry

    lax.fori_loop(0, nc, a_body, 0, unroll=4)

    def scan_f(n, state):
        sprev[n, 0:dk, :] = state.astype(BF16)
        return state * cd_f + kv[n, 0:dk, :]

    lax.fori_loop(0, nc, scan_f, jnp.zeros((dk, RET_V_DIM), F32))

    def scan_b(t, state):
        n = nc - 1 - t
        sprev[n, dk:2 * dk, :] = state.astype(BF16)
        return state * cd_b + kv[n, dk:2 * dk, :]

    lax.fori_loop(0, nc, scan_b, jnp.zeros((dk, RET_V_DIM), F32))

    ri = lax.broadcasted_iota(I32, (C, C), 0)
    ci = lax.broadcasted_iota(I32, (C, C), 1)
    d = (ri - ci).astype(F32)
    dec = jnp.where(ri >= ci, jnp.exp(jnp.maximum(d, 0.0) * lg_f), jnp.exp(jnp.maximum(-d, 0.0) * lg_b))
    gw = gw_ref[...]
    gb = gb_ref[...]

    def c_body(n, carry):
        rows = pl.ds(pl.multiple_of(n * C, C), C)
        scores = _dot_nt(qb[rows, :], kb[rows, :]) * dec
        y = _dot(scores.astype(BF16), v_ref[rows, :]) + _dot(q2[rows, :], sprev[n])
        mu = jnp.mean(y, axis=-1, keepdims=True)
        yc = y - mu
        var = jnp.mean(yc * yc, axis=-1, keepdims=True)
        yn = yc * lax.rsqrt(var + EPS) * gw + gb
        g = g_ref[rows, :].astype(F32)
        o_ref[rows, :] = (g * _sigmoid(g) * yn).astype(o_ref.dtype)
        return carry

    lax.fori_loop(0, nc, c_body, 0, unroll=4)


def _retention(proj, decay_fwd, decay_bwd, gn_w, gn_b, B, S):
    T = proj.shape[0]
    dk, dv = RET_QK_DIM, RET_V_DIM
    q_off = (ATTN_Q_W + 2 * ATTN_KV_W) // dk
    k_off = q_off + RET_QK_W // dk
    v_off = (ATTN_Q_W + 2 * ATTN_KV_W + 2 * RET_QK_W) // dv
    g_off = v_off + RET_V_W // dv
    cos, sin = _rope_tables(S)
    C = min(RET_TILE, S)
    grid_spec = pltpu.PrefetchScalarGridSpec(
        num_scalar_prefetch=2,
        grid=(B, RET_HEADS),
        in_specs=[
            pl.BlockSpec((S, dk), lambda b, h, *_: (b, q_off + h)),
            pl.BlockSpec((S, dk), lambda b, h, *_: (b, k_off + h)),
            pl.BlockSpec((S, dv), lambda b, h, *_: (b, v_off + h)),
            pl.BlockSpec((S, dv), lambda b, h, *_: (b, g_off + h)),
            pl.BlockSpec((S, dk), lambda b, h, *_: (0, 0)),
            pl.BlockSpec((S, dk), lambda b, h, *_: (0, 0)),
            pl.BlockSpec((1, dv), lambda b, h, *_: (0, h)),
            pl.BlockSpec((1, dv), lambda b, h, *_: (0, h)),
        ],
        out_specs=pl.BlockSpec((S, dv), lambda b, h, *_: (b, h)),
        scratch_shapes=[
            pltpu.VMEM((S, dk), BF16),
            pltpu.VMEM((S, dk), BF16),
            pltpu.VMEM((S, 2 * dk), BF16),
            pltpu.VMEM((S // C, 2 * dk, dv), F32),
            pltpu.VMEM((S // C, 2 * dk, dv), BF16),
        ],
    )
    return pl.pallas_call(
        functools.partial(_ret_kernel, S=S, C=C),
        out_shape=jax.ShapeDtypeStruct((T, RET_V_W), BF16),
        grid_spec=grid_spec,
        compiler_params=_cparams(("arbitrary", "arbitrary")),
        name="retention",
    )(decay_fwd.astype(F32), decay_bwd.astype(F32), proj, proj, proj, proj, cos, sin,
      gn_w.reshape(1, -1), gn_b.reshape(1, -1))


def _merge_kernel(a_ref, r_ref, wa_ref, wr_ref, ga_ref, gr_ref, o_ref, wab, wrb):
    _cast_weight_once(wa_ref, wab)
    _cast_weight_once(wr_ref, wrb)
    ua = _dot(a_ref[...], wab[...])
    ur = _dot(r_ref[...], wrb[...])
    out = _sigmoid(ga_ref[...].astype(F32)) * ua + _sigmoid(gr_ref[...].astype(F32)) * ur
    o_ref[...] = out.astype(o_ref.dtype)


def _merge_up(attn, ret, wa, wr, proj, D):
    T = attn.shape[0]
    tm = min(1024, T)
    tn = min(512, D)
    ga_off = (ATTN_Q_W + 2 * ATTN_KV_W + 2 * RET_QK_W + 2 * RET_V_W) // tn
    gr_off = ga_off + D // tn
    return pl.pallas_call(
        _merge_kernel,
        out_shape=jax.ShapeDtypeStruct((T, D), BF16),
        grid=(D // tn, T // tm),
        in_specs=[
            pl.BlockSpec((tm, ATTN_Q_W), lambda j, i: (i, 0)),
            pl.BlockSpec((tm, RET_V_W), lambda j, i: (i, 0)),
            pl.BlockSpec((ATTN_Q_W, tn), lambda j, i: (0, j)),
            pl.BlockSpec((RET_V_W, tn), lambda j, i: (0, j)),
            pl.BlockSpec((tm, tn), lambda j, i: (i, ga_off + j)),
            pl.BlockSpec((tm, tn), lambda j, i: (i, gr_off + j)),
        ],
        out_specs=pl.BlockSpec((tm, tn), lambda j, i: (i, j)),
        scratch_shapes=[pltpu.VMEM((ATTN_Q_W, tn), BF16), pltpu.VMEM((RET_V_W, tn), BF16)],
        compiler_params=_cparams(("arbitrary", "arbitrary")),
        name="merge_up",
    )(attn, ret, wa, wr, proj, proj)


def _out_proj_kernel(m_ref, w_ref, x_ref, mod_ref, o_ref, wb, *, gate_idx):
    _cast_weight_once(w_ref, wb)
    y = _dot(m_ref[...], wb[...])
    o_ref[...] = x_ref[...] + mod_ref[gate_idx:gate_idx + 1, :] * y


def _out_proj(merged, w, x2, mod, S, gate_idx):
    T, D = x2.shape
    tm = min(1024, S)
    tn = min(512, D)
    return pl.pallas_call(
        functools.partial(_out_proj_kernel, gate_idx=gate_idx),
        out_shape=jax.ShapeDtypeStruct((T, D), F32),
        grid=(D // tn, T // tm),
        in_specs=[
            pl.BlockSpec((tm, D), lambda j, i: (i, 0)),
            pl.BlockSpec((D, tn), lambda j, i: (0, j)),
            pl.BlockSpec((tm, tn), lambda j, i: (i, j)),
            pl.BlockSpec((None, N_ADA, tn), lambda j, i: ((i * tm) // S, 0, j)),
        ],
        out_specs=pl.BlockSpec((tm, tn), lambda j, i: (i, j)),
        scratch_shapes=[pltpu.VMEM((D, tn), BF16)],
        compiler_params=_cparams(("arbitrary", "arbitrary")),
        name="out_proj",
    )(merged, w, x2, mod)


def _pack_bf16_pairs(v):
    n = v.shape[1] // 2
    bits = lax.bitcast_convert_type(v.astype(BF16).astype(F32), U32)
    return jnp.bitwise_or(jnp.bitwise_and(bits[:, n:], jnp.uint32(0xFFFF0000)),
                          jnp.right_shift(bits[:, :n], jnp.uint32(16)))


def _unpack_bf16_pairs(w):
    lo = lax.bitcast_convert_type(jnp.left_shift(w, jnp.uint32(16)), F32)
    hi = lax.bitcast_convert_type(jnp.bitwise_and(w, jnp.uint32(0xFFFF0000)), F32)
    return jnp.concatenate([lo, hi], axis=1)


def _norm_router_kernel(x_ref, nw_ref, mod_ref, rwt_ref, rb_ref, hp_ref, e_ref, w_ref, r_ref, cnt_ref, carry,
                        *, shift_idx, scale_idx):
    i = pl.program_id(0)

    @pl.when(i == 0)
    def _():
        carry[...] = jnp.zeros_like(carry)

    h = _norm_mod_value(x_ref[...], nw_ref[...], mod_ref[shift_idx:shift_idx + 1, :],
                        mod_ref[scale_idx:scale_idx + 1, :])
    hp_ref[...] = _pack_bf16_pairs(h)
    hb = h.astype(BF16)
    tm = hb.shape[0]
    E = rwt_ref.shape[0]
    logits = _dot_nt(rwt_ref[...], hb) + rb_ref[...]
    iota_e = lax.broadcasted_iota(I32, (E, tm), 0)
    vals, idxs = [], []
    work = logits
    sel = jnp.zeros((E, tm), F32)
    for _k in range(TOP_K):
        m = jnp.max(work, axis=0, keepdims=True)
        idx = jnp.min(jnp.where(work == m, iota_e, E), axis=0, keepdims=True)
        hit = iota_e == idx
        vals.append(m)
        idxs.append(idx)
        work = jnp.where(hit, NEG_BIG, work)
        sel = sel + hit.astype(F32)
    ex = [jnp.exp(v - vals[0]) for v in vals]
    tot = ex[0]
    for v in ex[1:]:
        tot = tot + v
    ri = lax.broadcasted_iota(I32, (tm, tm), 0)
    ci = lax.broadcasted_iota(I32, (tm, tm), 1)
    upper = (ri < ci).astype(BF16)
    before = _dot(sel.astype(BF16), upper) + carry[:, 0:1]
    for k in range(TOP_K):
        e_ref[k:k + 1, :] = idxs[k]
        w_ref[k:k + 1, :] = ex[k] / tot
        r_ref[k:k + 1, :] = jnp.sum(jnp.where(iota_e == idxs[k], before, 0.0), axis=0, keepdims=True).astype(I32)
    carry[...] = carry[...] + jnp.sum(sel, axis=1, keepdims=True)
    cnt_ref[...] = carry[...]


def _norm_router(x1, nw, mod, router_w, router_b, S, shift_idx, scale_idx):
    T, D = x1.shape
    E = router_w.shape[1]
    tm = min(256, S)
    outs = pl.pallas_call(
        functools.partial(_norm_router_kernel, shift_idx=shift_idx, scale_idx=scale_idx),
        out_shape=(
            jax.ShapeDtypeStruct((T, D // 2), U32),
            jax.ShapeDtypeStruct((TOP_K, T), I32),
            jax.ShapeDtypeStruct((TOP_K, T), F32),
            jax.ShapeDtypeStruct((TOP_K, T), I32),
            jax.ShapeDtypeStruct((E, 128), F32),
        ),
        grid=(T // tm,),
        in_specs=[
            pl.BlockSpec((tm, D), lambda i: (i, 0)),
            pl.BlockSpec((1, D), lambda i: (0, 0)),
            pl.BlockSpec((None, N_ADA, D), lambda i: ((i * tm) // S, 0, 0)),
            pl.BlockSpec((E, D), lambda i: (0, 0)),
            pl.BlockSpec((E, 1), lambda i: (0, 0)),
        ],
        out_specs=(
            pl.BlockSpec((tm, D // 2), lambda i: (i, 0)),
            pl.BlockSpec((TOP_K, tm), lambda i: (0, i)),
            pl.BlockSpec((TOP_K, tm), lambda i: (0, i)),
            pl.BlockSpec((TOP_K, tm), lambda i: (0, i)),
            pl.BlockSpec((E, 128), lambda i: (0, 0)),
        ),
        scratch_shapes=[pltpu.VMEM((E, 128), F32)],
        compiler_params=_cparams(("arbitrary",)),
        name="norm_router",
    )(x1, nw.reshape(1, D), mod, router_w.T.astype(BF16), router_b.reshape(E, 1).astype(F32))
    return outs


def _dispatch_kernel(pos_ref, pad_ref, hp_hbm, xs_ref, src, zeros, isem, sem, zsem, *, tm, T, E, n_tiles):
    i = pl.program_id(0)
    n = pl.num_programs(0)
    ring = src.shape[0]
    base = i * tm

    def load(step, slot):
        return pltpu.make_async_copy(hp_hbm.at[pl.ds(pl.multiple_of(step * tm, tm), tm)], src.at[slot],
                                     isem.at[slot])

    def wait_scatters(slot):
        for k in range(TOP_K):
            pltpu.make_async_copy(src.at[slot], xs_ref.at[pl.ds(0, tm)], sem.at[slot]).wait()

    @pl.when(i == 0)
    def _():
        load(0, 0).start()
        zeros[...] = jnp.zeros_like(zeros)
        def pad_copy(row):
            return pltpu.make_async_copy(zeros.at[pl.ds(0, 1)], xs_ref.at[pl.ds(row, 1)], zsem)

        def tail_copy(t):
            return pltpu.make_async_copy(
                zeros, xs_ref.at[pl.ds(pl.multiple_of(t * MOE_TILE, MOE_TILE), MOE_TILE)], zsem)

        def pad_start(e, carry):
            start = pad_ref[e]
            lax.fori_loop(0, pad_ref[E + e], lambda r, c: (pad_copy(start + r).start(), c)[1], 0)
            return carry

        def pad_wait(e, carry):
            start = pad_ref[e]
            lax.fori_loop(0, pad_ref[E + e], lambda r, c: (pad_copy(start + r).wait(), c)[1], 0)
            return carry

        def tail_start(t, carry):
            tail_copy(t).start()
            return carry

        def tail_wait(t, carry):
            tail_copy(t).wait()
            return carry

        lax.fori_loop(0, E, pad_start, 0)
        lax.fori_loop(pad_ref[2 * E], n_tiles, tail_start, 0)
        lax.fori_loop(0, E, pad_wait, 0)
        lax.fori_loop(pad_ref[2 * E], n_tiles, tail_wait, 0)

    slot = i % ring

    @pl.when(i >= 2)
    def _():
        wait_scatters((i + 1) % ring)

    @pl.when(i + 1 < n)
    def _():
        load(i + 1, (i + 1) % ring).start()

    load(i, slot).wait()

    def start_body(t, carry):
        for k in range(TOP_K):
            dst = pos_ref[k * T + base + t]
            pltpu.make_async_copy(src.at[slot, pl.ds(t, 1)], xs_ref.at[pl.ds(dst, 1)],
                                  sem.at[slot]).start(priority=k % 2)
        return carry

    lax.fori_loop(0, tm, start_body, 0, unroll=8)

    @pl.when(i == n - 1)
    def _():
        @pl.when(i >= 1)
        def _():
            wait_scatters((i + 2) % ring)
        wait_scatters(slot)


def _dispatch(pos_flat, pad_info, hp, n_tiles, E):
    T, W = hp.shape
    tm = min(512, T)
    grid_spec = pltpu.PrefetchScalarGridSpec(
        num_scalar_prefetch=2,
        grid=(T // tm,),
        in_specs=[pl.BlockSpec(memory_space=pl.ANY)],
        out_specs=pl.BlockSpec(memory_space=pl.ANY),
        scratch_shapes=[pltpu.VMEM((3, tm, W), U32), pltpu.VMEM((MOE_TILE, W), U32),
                        pltpu.SemaphoreType.DMA((3,)), pltpu.SemaphoreType.DMA((3,)), pltpu.SemaphoreType.DMA],
    )
    return pl.pallas_call(
        functools.partial(_dispatch_kernel, tm=tm, T=T, E=E, n_tiles=n_tiles),
        out_shape=jax.ShapeDtypeStruct((n_tiles * MOE_TILE, W), U32),
        grid_spec=grid_spec,
        compiler_params=_cparams(("arbitrary",)),
        name="dispatch",
    )(pos_flat, pad_info, hp)


def _expert_changed(te_ref, i):
    return jnp.logical_or(i == 0, te_ref[i] != te_ref[jnp.maximum(i - 1, 0)])


def _cast_rows(src_ref, dst_ref, rows_per_pass=256):
    def body(r, carry):
        rows = pl.ds(pl.multiple_of(r * rows_per_pass, rows_per_pass), rows_per_pass)
        dst_ref[rows, :] = src_ref[rows, :].astype(dst_ref.dtype)
        return carry

    lax.fori_loop(0, src_ref.shape[0] // rows_per_pass, body, 0)


def _expert_weight_stage(te_ref, nx_ref, i, first_step, n, n_passes, copies, slot_ref, consume):
    @pl.when(_expert_changed(te_ref, i))
    def _():
        @pl.when(first_step)
        def _():
            slot_ref[0] = 0
            for cp in copies(te_ref[0], 0, 0):
                cp.start()

        slot = slot_ref[0]
        for cp in copies(te_ref[i], n, slot):
            cp.wait()
        consume(slot)
        same_pass = nx_ref[i] >= 0
        nxt_e = jnp.where(same_pass, nx_ref[i], te_ref[0])
        nxt_n = jnp.where(same_pass, n, n + 1)

        @pl.when(jnp.logical_or(same_pass, n + 1 < n_passes))
        def _():
            for cp in copies(nxt_e, nxt_n, 1 - slot):
                cp.start()

        slot_ref[0] = 1 - slot


def _expert_up_kernel(te_ref, nu_ref, nx_ref, xs_ref, w1_hbm, bg_ref, bu_ref, o_ref, stage, wgb, wub, slot_ref, sem,
                      *, tn, F, nf):
    n = pl.program_id(0)
    i = pl.program_id(1)

    def copies(e, nn, slot):
        c0 = pl.multiple_of(nn * tn, tn)
        return (pltpu.make_async_copy(w1_hbm.at[e, :, pl.ds(c0, tn)], stage.at[slot, 0], sem.at[slot]),
                pltpu.make_async_copy(w1_hbm.at[e, :, pl.ds(F + c0, tn)], stage.at[slot, 1], sem.at[slot]))

    def consume(slot):
        _cast_rows(stage.at[slot, 0], wgb)
        _cast_rows(stage.at[slot, 1], wub)

    _expert_weight_stage(te_ref, nx_ref, i, jnp.logical_and(n == 0, i == 0), n, nf, copies, slot_ref, consume)

    @pl.when(i < nu_ref[0])
    def _():
        x = _unpack_bf16_pairs(xs_ref[...]).astype(BF16)
        gate = jnp.minimum(_dot(x, wgb[...]) + bg_ref[...], SWIGLU_LIMIT)
        up = jnp.clip(_dot(x, wub[...]) + bu_ref[...], -SWIGLU_LIMIT, SWIGLU_LIMIT)
        act = gate * _sigmoid(SWIGLU_ALPHA * gate) * (up + 1.0)
        o_ref[...] = act.astype(o_ref.dtype)

    @pl.when(i >= nu_ref[0])
    def _():
        o_ref[...] = jnp.zeros_like(o_ref)


def _expert_down_kernel(te_ref, nu_ref, nx_ref, a_ref, w2_hbm, b2_ref, o_ref, stage, w2b, slot_ref, sem):
    i = pl.program_id(0)

    def copies(e, nn, slot):
        del nn
        return (pltpu.make_async_copy(w2_hbm.at[e], stage.at[slot], sem.at[slot]),)

    def consume(slot):
        _cast_rows(stage.at[slot], w2b)

    _expert_weight_stage(te_ref, nx_ref, i, i == 0, 0, 1, copies, slot_ref, consume)

    @pl.when(i < nu_ref[0])
    def _():
        y = _dot(a_ref[...], w2b[...]) + b2_ref[...]
        o_ref[...] = _pack_bf16_pairs(y)

    @pl.when(i >= nu_ref[0])
    def _():
        o_ref[...] = jnp.zeros_like(o_ref)


def _expert_ffn(xs, tile_e, n_used, next_e, w1, b1, w2, b2):
    R, W = xs.shape
    E, D, F2 = w1.shape
    F = F2 // 2
    tm = MOE_TILE
    nt = R // tm

    tn = min(512, F)
    nf = F // tn
    b1r = b1.reshape(E, 1, F2)

    act = pl.pallas_call(
        functools.partial(_expert_up_kernel, tn=tn, F=F, nf=nf),
        out_shape=jax.ShapeDtypeStruct((R, F), BF16),
        grid_spec=pltpu.PrefetchScalarGridSpec(
            num_scalar_prefetch=3,
            grid=(nf, nt),
            in_specs=[
                pl.BlockSpec((tm, W), lambda n, i, te, nu, nx: (jnp.minimum(i, nu[0] - 1), 0)),
                pl.BlockSpec(memory_space=pl.ANY),
                pl.BlockSpec((None, 1, tn), lambda n, i, te, nu, nx: (te[i], 0, n)),
                pl.BlockSpec((None, 1, tn), lambda n, i, te, nu, nx: (te[i], 0, nf + n)),
            ],
            out_specs=pl.BlockSpec((tm, tn), lambda n, i, te, nu, nx: (i, n)),
            scratch_shapes=[pltpu.VMEM((2, 2, D, tn), F32), pltpu.VMEM((D, tn), BF16), pltpu.VMEM((D, tn), BF16),
                            pltpu.SMEM((1,), I32), pltpu.SemaphoreType.DMA((2,))],
        ),
        compiler_params=_cparams(("arbitrary", "arbitrary")),
        name="expert_up",
    )(tile_e, n_used, next_e, xs, w1, b1r, b1r)
    y = pl.pallas_call(
        _expert_down_kernel,
        out_shape=jax.ShapeDtypeStruct((R, D // 2), U32),
        grid_spec=pltpu.PrefetchScalarGridSpec(
            num_scalar_prefetch=3,
            grid=(nt,),
            in_specs=[
                pl.BlockSpec((tm, F), lambda i, te, nu, nx: (jnp.minimum(i, nu[0] - 1), 0)),
                pl.BlockSpec(memory_space=pl.ANY),
                pl.BlockSpec((None, 1, D), lambda i, te, nu, nx: (te[i], 0, 0)),
            ],
            out_specs=pl.BlockSpec((tm, D // 2), lambda i, te, nu, nx: (i, 0)),
            scratch_shapes=[pltpu.VMEM((2, F, D), F32), pltpu.VMEM((F, D), BF16),
                            pltpu.SMEM((1,), I32), pltpu.SemaphoreType.DMA((2,))],
        ),
        compiler_params=_cparams(("arbitrary",)),
        name="expert_down",
    )(tile_e, n_used, next_e, act, w2, b2.reshape(E, 1, D))
    return y


def _combine_kernel(pos_ref, x_ref, w_ref, mod_ref, y_ref, o_ref, buf, sem, *, tm, T, gate_idx):
    i = pl.program_id(0)
    n = pl.num_programs(0)
    W = buf.shape[-1]
    th = tm
    ring = buf.shape[0]

    def issue_row(first_token, slot, t):
        for k in range(TOP_K):
            src = pos_ref[k * T + first_token + t]
            pltpu.make_async_copy(y_ref.at[pl.ds(src, 1)], buf.at[slot, k, pl.ds(t, 1)],
                                  sem.at[slot]).start(priority=k % 2)

    def wait_half(slot):
        for k in range(TOP_K):
            pltpu.make_async_copy(y_ref.at[pl.ds(0, th)], buf.at[slot, k], sem.at[slot]).wait()

    rc = 8
    cw = min(512, W)

    def consume(slot, next_first_token, next_slot):
        def rows_body(r, carry):
            rows = pl.ds(pl.multiple_of(r * rc, rc), rc)
            brow = rows
            wv = w_ref[rows, :]
            wk = [jnp.broadcast_to(wv[:, k:k + 1], (rc, cw)) for k in range(TOP_K)]
            for c in range(W // cw):
                lo = hi = None
                for k in range(TOP_K):
                    u = buf[slot, k, brow, c * cw:(c + 1) * cw]
                    l = wk[k] * lax.bitcast_convert_type(jnp.left_shift(u, jnp.uint32(16)), F32)
                    h = wk[k] * lax.bitcast_convert_type(jnp.bitwise_and(u, jnp.uint32(0xFFFF0000)), F32)
                    lo = l if lo is None else lo + l
                    hi = h if hi is None else hi + h
                for half, acc in ((0, lo), (1, hi)):
                    cols = slice(half * W + c * cw, half * W + (c + 1) * cw)
                    o_ref[rows, cols] = x_ref[rows, cols] + mod_ref[gate_idx:gate_idx + 1, cols] * acc
            for t in range(rc):
                issue_row(next_first_token, next_slot, r * rc + t)
            return carry

        lax.fori_loop(0, th // rc, rows_body, 0)

    @pl.when(i == 0)
    def _():
        for b in range(2):
            lax.fori_loop(0, tm, lambda t, c: (issue_row(b * tm, b, t), c)[1], 0, unroll=8)

    wait_half(i % ring)
    consume(i % ring, jnp.minimum((i + 2) * tm, T - tm), (i + 2) % ring)

    @pl.when(i == n - 1)
    def _():
        wait_half((i + 1) % ring)
        wait_half((i + 2) % ring)


def _combine(pos_flat, x1, w_tk, mod, y, S, gate_idx):
    T, D = x1.shape
    tm = min(128, S)
    assert T // tm >= 2
    grid_spec = pltpu.PrefetchScalarGridSpec(
        num_scalar_prefetch=1,
        grid=(T // tm,),
        in_specs=[
            pl.BlockSpec((tm, D), lambda i, p: (i, 0)),
            pl.BlockSpec((tm, TOP_K), lambda i, p: (i, 0)),
            pl.BlockSpec((None, N_ADA, D), lambda i, p: ((i * tm) // S, 0, 0)),
            pl.BlockSpec(memory_space=pl.ANY),
        ],
        out_specs=pl.BlockSpec((tm, D), lambda i, p: (i, 0)),
        scratch_shapes=[pltpu.VMEM((3, TOP_K, tm, D // 2), U32), pltpu.SemaphoreType.DMA((3,))],
    )
    return pl.pallas_call(
        functools.partial(_combine_kernel, tm=tm, T=T, gate_idx=gate_idx),
        out_shape=jax.ShapeDtypeStruct((T, D), F32),
        grid_spec=grid_spec,
        compiler_params=_cparams(("arbitrary",)),
        name="combine",
    )(pos_flat, x1, w_tk, mod, y)


def _layer(x2, mod, B, S, rel_bias, norm_mix_w, w_in, q_norm_w, k_norm_w, attn_sink, ret_decay_fwd, ret_decay_bwd,
           ret_gn_w, ret_gn_b, w_up_attn, w_up_ret, w_out, norm_ffn_w, router_w, router_b,
           expert_w1, expert_b1, expert_w2, expert_b2):
    T, D = x2.shape
    E = router_w.shape[1]
    h = _norm_mod(x2, norm_mix_w, mod, S, 0, 1)
    proj = _in_proj(h, w_in, q_norm_w, k_norm_w)
    attn = _window_attention(proj, _attn_bias_table(rel_bias), attn_sink, B, S)
    ret = _retention(proj, ret_decay_fwd, ret_decay_bwd, ret_gn_w, ret_gn_b, B, S)
    merged = _merge_up(attn, ret, w_up_attn, w_up_ret, proj, D)
    x1 = _out_proj(merged, w_out, x2, mod, S, 2)

    hp, top_e, top_w, rank, cnt = _norm_router(x1, norm_ffn_w, mod, router_w, router_b, S, 3, 4)
    tm = MOE_TILE
    counts = cnt[:, 0].astype(I32)
    tiles_per_e = (counts + tm - 1) // tm
    tile_end = jnp.cumsum(tiles_per_e)
    group_start = (tile_end - tiles_per_e) * tm
    n_tiles = (T * TOP_K) // tm + E
    n_used = tile_end[-1]
    tile_ids = jnp.minimum(jnp.arange(n_tiles, dtype=I32), n_used - 1)
    tile_e = jnp.minimum(jnp.sum((tile_ids[:, None] >= tile_end[None, :]).astype(I32), axis=1), E - 1)
    e_ids = jnp.arange(E, dtype=I32)
    later = jnp.logical_and(e_ids[None, :] > e_ids[:, None], tiles_per_e[None, :] > 0)
    next_of_e = jnp.min(jnp.where(later, e_ids[None, :], E), axis=1)
    next_of_e = jnp.where(next_of_e == E, -1, next_of_e)
    next_e = jnp.sum(jnp.where(tile_e[:, None] == e_ids, next_of_e, 0), axis=1).astype(I32)
    start_of = jnp.sum(jnp.where(top_e[:, :, None] == e_ids, group_start, 0), axis=-1)
    pos_flat = (start_of + rank).astype(I32).reshape(-1)
    pad_info = jnp.concatenate([group_start + counts, tiles_per_e * tm - counts, n_used.reshape(1)]).astype(I32)

    xs = _dispatch(pos_flat, pad_info, hp, n_tiles, E)
    y = _expert_ffn(xs, tile_e, n_used.reshape(1).astype(I32), next_e, expert_w1, expert_b1, expert_w2, expert_b2)
    return _combine(pos_flat, x1, top_w.T, mod, y, S, 5)


def kernel(x, c, rel_bias, ada_w, ada_b, norm_mix_w, w_in, q_norm_w, k_norm_w, attn_sink, ret_decay_fwd,
           ret_decay_bwd, ret_gn_w, ret_gn_b, w_up_attn, w_up_ret, w_out, norm_ffn_w, router_w, router_b,
           expert_w1, expert_b1, expert_w2, expert_b2):
    B, S, D = x.shape
    x2 = x.reshape(B * S, D)
    for l in range(ada_w.shape[0]):
        mod = _ada_mod(c, ada_w[l], ada_b[l])
        x2 = _layer(x2, mod, B, S, rel_bias, norm_mix_w[l], w_in[l], q_norm_w[l], k_norm_w[l], attn_sink[l],
                    ret_decay_fwd[l], ret_decay_bwd[l], ret_gn_w[l], ret_gn_b[l], w_up_attn[l], w_up_ret[l],
                    w_out[l], norm_ffn_w[l], router_w[l], router_b[l], expert_w1[l], expert_b1[l],
                    expert_w2[l], expert_b2[l])
    return x2.reshape(B, S, D)
```

```python
import functools
import math

import jax
import jax.numpy as jnp
from jax import lax
from jax.experimental import pallas as pl
from jax.experimental.pallas import tpu as pltpu

F32 = jnp.float32
BF16 = jnp.bfloat16
U32 = jnp.uint32
I32 = jnp.int32

ATTN_HEADS = 16
ATTN_KV_HEADS = 4
ATTN_HEAD_DIM = 128
WINDOW = 128
ATTN_BLOCK = 128
N_BUCKETS = 32
MAX_DISTANCE = 128
RET_HEADS = 8
RET_QK_DIM = 128
RET_V_DIM = 256
RET_CHUNK = 128
ROPE_BASE = 10000.0
TOP_K = 4
SWIGLU_LIMIT = 7.0
SWIGLU_ALPHA = 1.702
N_ADA = 6
EPS = 1e-6
NEG_INF = -1e30
NEG_BIG = -3.0e38
LOG2E = 1.4426950408889634

ATTN_Q_W = ATTN_HEADS * ATTN_HEAD_DIM
ATTN_KV_W = ATTN_KV_HEADS * ATTN_HEAD_DIM
RET_QK_W = RET_HEADS * RET_QK_DIM
RET_V_W = RET_HEADS * RET_V_DIM

VMEM_LIMIT_BYTES = 56 * 1024 * 1024
MOE_TILE = 256
RET_TILE = 256


def _cparams(sem):
    return pltpu.CompilerParams(dimension_semantics=sem, vmem_limit_bytes=VMEM_LIMIT_BYTES)


def _dot(a, b):
    return jnp.dot(a, b, preferred_element_type=F32)


def _dot_nt(a, b):
    return lax.dot_general(a, b, (((1,), (1,)), ((), ())), preferred_element_type=F32)


def _dot_tn(a, b):
    return lax.dot_general(a, b, (((0,), (0,)), ((), ())), preferred_element_type=F32)


def _sigmoid(x):
    return 1.0 / (1.0 + jnp.exp(-x))


def _ada_kernel(c_ref, w_ref, b_ref, o_ref):
    c = c_ref[...]
    cs = (c * _sigmoid(c)).astype(BF16)
    o_ref[...] = _dot(cs, w_ref[...].astype(BF16)) + b_ref[...]


def _ada_mod(c, ada_w, ada_b):
    B, D = c.shape
    N = ada_w.shape[1]
    rows = 8
    cp = jnp.zeros((rows, D), F32).at[:B].set(c)
    tn = min(1024, N)
    out = pl.pallas_call(
        _ada_kernel,
        out_shape=jax.ShapeDtypeStruct((rows, N), F32),
        grid=(N // tn,),
        in_specs=[
            pl.BlockSpec((rows, D), lambda j: (0, 0)),
            pl.BlockSpec((D, tn), lambda j: (0, j)),
            pl.BlockSpec((1, tn), lambda j: (0, j)),
        ],
        out_specs=pl.BlockSpec((rows, tn), lambda j: (0, j)),
        compiler_params=_cparams(("arbitrary",)),
        name="ada_mod",
    )(cp, ada_w, ada_b.reshape(1, N))
    return out[:B].reshape(B, N_ADA, D)


def _norm_mod_value(x, nw, shift, scale):
    ms = jnp.mean(x * x, axis=-1, keepdims=True)
    h = x * lax.rsqrt(ms + EPS) * nw
    return h * (1.0 + scale) + shift


def _norm_mod_kernel(x_ref, nw_ref, mod_ref, o_ref, *, shift_idx, scale_idx):
    h = _norm_mod_value(x_ref[...], nw_ref[...], mod_ref[shift_idx:shift_idx + 1, :],
                        mod_ref[scale_idx:scale_idx + 1, :])
    o_ref[...] = h.astype(BF16)


def _norm_mod(x2, nw, mod, S, shift_idx, scale_idx):
    T, D = x2.shape
    tm = min(512, S)
    return pl.pallas_call(
        functools.partial(_norm_mod_kernel, shift_idx=shift_idx, scale_idx=scale_idx),
        out_shape=jax.ShapeDtypeStruct((T, D), BF16),
        grid=(T // tm,),
        in_specs=[
            pl.BlockSpec((tm, D), lambda i: (i, 0)),
            pl.BlockSpec((1, D), lambda i: (0, 0)),
            pl.BlockSpec((None, N_ADA, D), lambda i: ((i * tm) // S, 0, 0)),
        ],
        out_specs=pl.BlockSpec((tm, D), lambda i: (i, 0)),
        compiler_params=_cparams(("arbitrary",)),
        name="norm_mod",
    )(x2, nw.reshape(1, D), mod)


def _cast_weight_once(w_ref, wb_ref):
    @pl.when(pl.program_id(1) == 0)
    def _():
        wb_ref[...] = w_ref[...].astype(BF16)


def _in_proj_kernel(a_ref, w_hbm, cw_ref, cf_ref, bd_ref, o_ref, stage, wb, sem, *, n_norm_tiles):
    j = pl.program_id(0)
    tn = wb.shape[1]

    def w_copy(jj):
        return pltpu.make_async_copy(w_hbm.at[:, pl.ds(pl.multiple_of(jj * tn, tn), tn)], stage, sem)

    @pl.when(pl.program_id(1) == 0)
    def _():
        @pl.when(j == 0)
        def _():
            w_copy(0).start()

        w_copy(j).wait()
        _cast_rows(stage, wb)

        @pl.when(j + 1 < pl.num_programs(0))
        def _():
            w_copy(j + 1).start()

    @pl.when(j < n_norm_tiles)
    def _():
        cb = bd_ref.shape[0]
        for c in range(tn // cb):
            cols = slice(c * cb, (c + 1) * cb)
            acc = _dot(a_ref[...], wb[:, cols])
            ms = _dot((acc * acc).astype(BF16), bd_ref[...]) * (1.0 / ATTN_HEAD_DIM)
            inv = jnp.where(cf_ref[:, cols] > 0.0, lax.rsqrt(ms + EPS), 1.0)
            o_ref[:, cols] = (acc * inv * cw_ref[:, cols]).astype(o_ref.dtype)

    @pl.when(j >= n_norm_tiles)
    def _():
        o_ref[...] = _dot(a_ref[...], wb[...]).astype(o_ref.dtype)


def _in_proj(h, w, q_norm_w, k_norm_w):
    T, K = h.shape
    N = w.shape[1]
    tm = min(1024, T)
    tn = 1024 if N % 1024 == 0 else 512
    hd = ATTN_HEAD_DIM
    n_norm = ATTN_Q_W + ATTN_KV_W
    n_norm_tiles = -(-n_norm // tn)
    pad = n_norm_tiles * tn - n_norm
    col_w = jnp.concatenate([jnp.tile(q_norm_w.astype(F32) * (hd ** -0.5 * LOG2E), ATTN_HEADS),
                             jnp.tile(k_norm_w.astype(F32), ATTN_KV_HEADS), jnp.ones((pad,), F32)]).reshape(1, -1)
    col_flag = jnp.concatenate([jnp.ones((n_norm,), F32), jnp.zeros((pad,), F32)]).reshape(1, -1)
    cb = min(512, tn)
    lane_head = jnp.arange(cb, dtype=I32) // hd
    block_ones = (lane_head[:, None] == lane_head[None, :]).astype(BF16)
    last = n_norm_tiles - 1
    return pl.pallas_call(
        functools.partial(_in_proj_kernel, n_norm_tiles=n_norm_tiles),
        out_shape=jax.ShapeDtypeStruct((T, N), BF16),
        grid=(N // tn, T // tm),
        in_specs=[
            pl.BlockSpec((tm, K), lambda j, i: (i, 0)),
            pl.BlockSpec(memory_space=pl.ANY),
            pl.BlockSpec((1, tn), lambda j, i: (0, jnp.minimum(j, last))),
            pl.BlockSpec((1, tn), lambda j, i: (0, jnp.minimum(j, last))),
            pl.BlockSpec((cb, cb), lambda j, i: (0, 0)),
        ],
        out_specs=pl.BlockSpec((tm, tn), lambda j, i: (i, j)),
        scratch_shapes=[pltpu.VMEM((K, tn), F32), pltpu.VMEM((K, tn), BF16), pltpu.SemaphoreType.DMA],
        compiler_params=_cparams(("arbitrary", "arbitrary")),
        name="in_proj",
    )(h, w, col_w, col_flag, block_ones)


def _t5_bucket(rel):
    nb = N_BUCKETS // 2
    max_exact = nb // 2
    base = jnp.where(rel > 0, nb, 0)
    n = jnp.abs(rel)
    nf = jnp.maximum(n, 1).astype(F32)
    large = max_exact + (jnp.log(nf / max_exact) / math.log(MAX_DISTANCE / max_exact) * (nb - max_exact)).astype(I32)
    large = jnp.minimum(large, nb - 1)
    return base + jnp.where(n < max_exact, n, large)


def _attn_bias_table(rel_bias):
    blk = ATTN_BLOCK
    qi = jnp.arange(blk, dtype=I32)[:, None]
    kj = jnp.arange(3 * blk, dtype=I32)[None, :]
    rel = kj - blk - qi
    bucket = _t5_bucket(rel)
    table = rel_bias.astype(F32).T
    bias = jnp.zeros((table.shape[0],) + bucket.shape, F32)
    for b in range(N_BUCKETS):
        bias = jnp.where(bucket[None] == b, table[:, b][:, None, None], bias)
    bias = jnp.where((jnp.abs(rel) <= WINDOW)[None], bias * LOG2E, NEG_INF)
    G = ATTN_HEADS // ATTN_KV_HEADS
    bias = bias.reshape(ATTN_KV_HEADS, G, blk, 3 * blk)
    bias = jnp.transpose(bias, (0, 3, 1, 2)).reshape(ATTN_KV_HEADS, 3 * blk, G * blk)
    key = jnp.arange(3 * blk, dtype=I32)[None, :, None]
    no_prev, no_next = key < blk, key >= 2 * blk
    return jnp.stack([bias, jnp.where(no_prev, NEG_INF, bias), jnp.where(no_next, NEG_INF, bias),
                      jnp.where(jnp.logical_or(no_prev, no_next), NEG_INF, bias)])


def _attn_kernel(sink_ref, q_ref, kp_ref, kc_ref, kn_ref, vp_ref, vc_ref, vn_ref, bias_ref, o_ref):
    blk, hd = ATTN_BLOCK, ATTN_HEAD_DIM
    G = ATTN_HEADS // ATTN_KV_HEADS
    for hk in range(ATTN_KV_HEADS):
        sl = slice(hk * hd, (hk + 1) * hd)
        k3 = jnp.concatenate([kp_ref[:, sl], kc_ref[:, sl], kn_ref[:, sl]], axis=0)
        v3 = jnp.concatenate([vp_ref[:, sl], vc_ref[:, sl], vn_ref[:, sl]], axis=0)
        q4 = jnp.concatenate([q_ref[:, (hk * G + g) * hd:(hk * G + g + 1) * hd] for g in range(G)],
                             axis=0)
        sink = jnp.concatenate([jnp.full((1, blk), sink_ref[hk * G + g], F32) for g in range(G)], axis=1)
        logits = _dot_nt(k3, q4) + bias_ref[hk]
        m = jnp.maximum(jnp.max(logits, axis=0, keepdims=True), sink)
        p = jnp.exp2(logits - m)
        denom = jnp.sum(p, axis=0, keepdims=True) + jnp.exp2(sink - m)
        out_t = _dot_tn(v3, p.astype(BF16)) * (1.0 / denom)
        out = out_t.T
        for g in range(G):
            h = hk * G + g
            o_ref[:, h * hd:(h + 1) * hd] = out[g * blk:(g + 1) * blk].astype(o_ref.dtype)


def _window_attention(proj, bias_tab, sink, B, S):
    T = proj.shape[0]
    blk = ATTN_BLOCK
    nb = S // blk
    kcol = ATTN_Q_W // ATTN_KV_W
    vcol = kcol + 1

    def kv_spec(col, off):
        return pl.BlockSpec((blk, ATTN_KV_W), lambda b, n, s: (b * nb + jnp.clip(n + off, 0, nb - 1), col))

    grid_spec = pltpu.PrefetchScalarGridSpec(
        num_scalar_prefetch=1,
        grid=(B, nb),
        in_specs=[
            pl.BlockSpec((blk, ATTN_Q_W), lambda b, n, s: (b * nb + n, 0)),
            kv_spec(kcol, -1), kv_spec(kcol, 0), kv_spec(kcol, 1),
            kv_spec(vcol, -1), kv_spec(vcol, 0), kv_spec(vcol, 1),
            pl.BlockSpec((None, ATTN_KV_HEADS, 3 * blk, (ATTN_HEADS // ATTN_KV_HEADS) * blk),
                         lambda b, n, s: ((n == 0).astype(I32) + 2 * (n == nb - 1).astype(I32), 0, 0, 0)),
        ],
        out_specs=pl.BlockSpec((blk, ATTN_Q_W), lambda b, n, s: (b * nb + n, 0)),
    )
    return pl.pallas_call(
        _attn_kernel,
        out_shape=jax.ShapeDtypeStruct((T, ATTN_Q_W), BF16),
        grid_spec=grid_spec,
        compiler_params=_cparams(("arbitrary", "arbitrary")),
        name="window_attn",
    )(sink.astype(F32) * LOG2E, proj, proj, proj, proj, proj, proj, proj, bias_tab)


def _rope_tables(S):
    d = RET_QK_DIM
    inv = ROPE_BASE ** (-jnp.arange(0, d, 2, dtype=F32) / d)
    ang = jnp.arange(S, dtype=F32)[:, None] * inv[None, :]
    cos, sin = jnp.cos(ang), jnp.sin(ang)
    return jnp.concatenate([cos, cos], axis=-1), jnp.concatenate([-sin, sin], axis=-1)


def _ret_kernel(df_ref, db_ref, q_ref, k_ref, v_ref, g_ref, cos_ref, sin_ref, gw_ref, gb_ref, o_ref,
                qb, kb, q2, kv, sprev, *, S, C):
    h = pl.program_id(1)
    nc = S // C
    dk = RET_QK_DIM
    half = dk // 2

    rowf = lax.broadcasted_iota(I32, (C, 1), 0).astype(F32)
    lg_f = -jnp.exp(jnp.full((1, 1), df_ref[h], F32))
    lg_b = -jnp.exp(jnp.full((1, 1), db_ref[h], F32))
    qdec_f = jnp.exp((rowf + 1.0) * lg_f)
    kdec_f = jnp.exp((C - 1.0 - rowf) * lg_f)
    qdec_b = jnp.exp((C - rowf) * lg_b)
    kdec_b = jnp.exp(rowf * lg_b)
    cd_f = jnp.exp(C * lg_f)
    cd_b = jnp.exp(C * lg_b)

    def a_body(n, carry):
        rows = pl.ds(pl.multiple_of(n * C, C), C)
        co = cos_ref[rows, :]
        si = sin_ref[rows, :]
        q = q_ref[rows, :].astype(F32)
        k = k_ref[rows, :].astype(F32)
        qr = q * co + pltpu.roll(q, half, 1) * si
        kr = (k * co + pltpu.roll(k, half, 1) * si) * (dk ** -0.5)
        qb[rows, :] = qr.astype(BF16)
        kb[rows, :] = kr.astype(BF16)
        q2[rows, :] = jnp.concatenate([qr * qdec_f, qr * qdec_b], axis=1).astype(BF16)
        k2 = jnp.concatenate([kr * kdec_f, kr * kdec_b], axis=1).astype(BF16)
        kv[n] = _dot_tn(k2, v_ref[rows, :])
        return carry

    lax.fori_loop(0, nc, a_body, 0, unroll=4)

    def scan_f(n, state):
        sprev[n, 0:dk, :] = state.astype(BF16)
        return state * cd_f + kv[n, 0:dk, :]

    lax.fori_loop(0, nc, scan_f, jnp.zeros((dk, RET_V_DIM), F32))

    def scan_b(t, state):
        n = nc - 1 - t
        sprev[n, dk:2 * dk, :] = state.astype(BF16)
        return state * cd_b + kv[n, dk:2 * dk, :]

    lax.fori_loop(0, nc, scan_b, jnp.zeros((dk, RET_V_DIM), F32))

    ri = lax.broadcasted_iota(I32, (C, C), 0)
    ci = lax.broadcasted_iota(I32, (C, C), 1)
    d = (ri - ci).astype(F32)
    dec = jnp.where(ri >= ci, jnp.exp(jnp.maximum(d, 0.0) * lg_f), jnp.exp(jnp.maximum(-d, 0.0) * lg_b))
    gw = gw_ref[...]
    gb = gb_ref[...]

    def c_body(n, carry):
        rows = pl.ds(pl.multiple_of(n * C, C), C)
        scores = _dot_nt(qb[rows, :], kb[rows, :]) * dec
        y = _dot(scores.astype(BF16), v_ref[rows, :]) + _dot(q2[rows, :], sprev[n])
        mu = jnp.mean(y, axis=-1, keepdims=True)
        yc = y - mu
        var = jnp.mean(yc * yc, axis=-1, keepdims=True)
        yn = yc * lax.rsqrt(var + EPS) * gw + gb
        g = g_ref[rows, :].astype(F32)
        o_ref[rows, :] = (g * _sigmoid(g) * yn).astype(o_ref.dtype)
        return carry

    lax.fori_loop(0, nc, c_body, 0, unroll=4)


def _retention(proj, decay_fwd, decay_bwd, gn_w, gn_b, B, S):
    T = proj.shape[0]
    dk, dv = RET_QK_DIM, RET_V_DIM
    q_off = (ATTN_Q_W + 2 * ATTN_KV_W) // dk
    k_off = q_off + RET_QK_W // dk
    v_off = (ATTN_Q_W + 2 * ATTN_KV_W + 2 * RET_QK_W) // dv
    g_off = v_off + RET_V_W // dv
    cos, sin = _rope_tables(S)
    C = min(RET_TILE, S)
    grid_spec = pltpu.PrefetchScalarGridSpec(
        num_scalar_prefetch=2,
        grid=(B, RET_HEADS),
        in_specs=[
            pl.BlockSpec((S, dk), lambda b, h, *_: (b, q_off + h)),
            pl.BlockSpec((S, dk), lambda b, h, *_: (b, k_off + h)),
            pl.BlockSpec((S, dv), lambda b, h, *_: (b, v_off + h)),
            pl.BlockSpec((S, dv), lambda b, h, *_: (b, g_off + h)),
            pl.BlockSpec((S, dk), lambda b, h, *_: (0, 0)),
            pl.BlockSpec((S, dk), lambda b, h, *_: (0, 0)),
            pl.BlockSpec((1, dv), lambda b, h, *_: (0, h)),
            pl.BlockSpec((1, dv), lambda b, h, *_: (0, h)),
        ],
        out_specs=pl.BlockSpec((S, dv), lambda b, h, *_: (b, h)),
        scratch_shapes=[
            pltpu.VMEM((S, dk), BF16),
            pltpu.VMEM((S, dk), BF16),
            pltpu.VMEM((S, 2 * dk), BF16),
            pltpu.VMEM((S // C, 2 * dk, dv), F32),
            pltpu.VMEM((S // C, 2 * dk, dv), BF16),
        ],
    )
    return pl.pallas_call(
        functools.partial(_ret_kernel, S=S, C=C),
        out_shape=jax.ShapeDtypeStruct((T, RET_V_W), BF16),
        grid_spec=grid_spec,
        compiler_params=_cparams(("arbitrary", "arbitrary")),
        name="retention",
    )(decay_fwd.astype(F32), decay_bwd.astype(F32), proj, proj, proj, proj, cos, sin,
      gn_w.reshape(1, -1), gn_b.reshape(1, -1))


def _merge_kernel(a_ref, r_ref, wa_ref, wr_ref, ga_ref, gr_ref, o_ref, wab, wrb):
    _cast_weight_once(wa_ref, wab)
    _cast_weight_once(wr_ref, wrb)
    ua = _dot(a_ref[...], wab[...])
    ur = _dot(r_ref[...], wrb[...])
    out = _sigmoid(ga_ref[...].astype(F32)) * ua + _sigmoid(gr_ref[...].astype(F32)) * ur
    o_ref[...] = out.astype(o_ref.dtype)


def _merge_up(attn, ret, wa, wr, proj, D):
    T = attn.shape[0]
    tm = min(1024, T)
    tn = min(512, D)
    ga_off = (ATTN_Q_W + 2 * ATTN_KV_W + 2 * RET_QK_W + 2 * RET_V_W) // tn
    gr_off = ga_off + D // tn
    return pl.pallas_call(
        _merge_kernel,
        out_shape=jax.ShapeDtypeStruct((T, D), BF16),
        grid=(D // tn, T // tm),
        in_specs=[
            pl.BlockSpec((tm, ATTN_Q_W), lambda j, i: (i, 0)),
            pl.BlockSpec((tm, RET_V_W), lambda j, i: (i, 0)),
            pl.BlockSpec((ATTN_Q_W, tn), lambda j, i: (0, j)),
            pl.BlockSpec((RET_V_W, tn), lambda j, i: (0, j)),
            pl.BlockSpec((tm, tn), lambda j, i: (i, ga_off + j)),
            pl.BlockSpec((tm, tn), lambda j, i: (i, gr_off + j)),
        ],
        out_specs=pl.BlockSpec((tm, tn), lambda j, i: (i, j)),
        scratch_shapes=[pltpu.VMEM((ATTN_Q_W, tn), BF16), pltpu.VMEM((RET_V_W, tn), BF16)],
        compiler_params=_cparams(("arbitrary", "arbitrary")),
        name="merge_up",
    )(attn, ret, wa, wr, proj, proj)


def _out_proj_kernel(m_ref, w_ref, x_ref, mod_ref, o_ref, wb, *, gate_idx):
    _cast_weight_once(w_ref, wb)
    y = _dot(m_ref[...], wb[...])
    o_ref[...] = x_ref[...] + mod_ref[gate_idx:gate_idx + 1, :] * y


def _out_proj(merged, w, x2, mod, S, gate_idx):
    T, D = x2.shape
    tm = min(1024, S)
    tn = min(512, D)
    return pl.pallas_call(
        functools.partial(_out_proj_kernel, gate_idx=gate_idx),
        out_shape=jax.ShapeDtypeStruct((T, D), F32),
        grid=(D // tn, T // tm),
        in_specs=[
            pl.BlockSpec((tm, D), lambda j, i: (i, 0)),
            pl.BlockSpec((D, tn), lambda j, i: (0, j)),
            pl.BlockSpec((tm, tn), lambda j, i: (i, j)),
            pl.BlockSpec((None, N_ADA, tn), lambda j, i: ((i * tm) // S, 0, j)),
        ],
        out_specs=pl.BlockSpec((tm, tn), lambda j, i: (i, j)),
        scratch_shapes=[pltpu.VMEM((D, tn), BF16)],
        compiler_params=_cparams(("arbitrary", "arbitrary")),
        name="out_proj",
    )(merged, w, x2, mod)


def _pack_bf16_pairs(v):
    n = v.shape[1] // 2
    bits = lax.bitcast_convert_type(v.astype(BF16).astype(F32), U32)
    return jnp.bitwise_or(jnp.bitwise_and(bits[:, n:], jnp.uint32(0xFFFF0000)),
                          jnp.right_shift(bits[:, :n], jnp.uint32(16)))


def _unpack_bf16_pairs(w):
    lo = lax.bitcast_convert_type(jnp.left_shift(w, jnp.uint32(16)), F32)
    hi = lax.bitcast_convert_type(jnp.bitwise_and(w, jnp.uint32(0xFFFF0000)), F32)
    return jnp.concatenate([lo, hi], axis=1)


def _norm_router_kernel(x_ref, nw_ref, mod_ref, rwt_ref, rb_ref, hp_ref, e_ref, w_ref, r_ref, cnt_ref, carry,
                        *, shift_idx, scale_idx):
    i = pl.program_id(0)

    @pl.when(i == 0)
    def _():
        carry[...] = jnp.zeros_like(carry)

    h = _norm_mod_value(x_ref[...], nw_ref[...], mod_ref[shift_idx:shift_idx + 1, :],
                        mod_ref[scale_idx:scale_idx + 1, :])
    hp_ref[...] = _pack_bf16_pairs(h)
    hb = h.astype(BF16)
    tm = hb.shape[0]
    E = rwt_ref.shape[0]
    logits = _dot_nt(rwt_ref[...], hb) + rb_ref[...]
    iota_e = lax.broadcasted_iota(I32, (E, tm), 0)
    vals, idxs = [], []
    work = logits
    sel = jnp.zeros((E, tm), F32)
    for _k in range(TOP_K):
        m = jnp.max(work, axis=0, keepdims=True)
        idx = jnp.min(jnp.where(work == m, iota_e, E), axis=0, keepdims=True)
        hit = iota_e == idx
        vals.append(m)
        idxs.append(idx)
        work = jnp.where(hit, NEG_BIG, work)
        sel = sel + hit.astype(F32)
    ex = [jnp.exp(v - vals[0]) for v in vals]
    tot = ex[0]
    for v in ex[1:]:
        tot = tot + v
    ri = lax.broadcasted_iota(I32, (tm, tm), 0)
    ci = lax.broadcasted_iota(I32, (tm, tm), 1)
    upper = (ri < ci).astype(BF16)
    before = _dot(sel.astype(BF16), upper) + carry[:, 0:1]
    for k in range(TOP_K):
        e_ref[k:k + 1, :] = idxs[k]
        w_ref[k:k + 1, :] = ex[k] / tot
        r_ref[k:k + 1, :] = jnp.sum(jnp.where(iota_e == idxs[k], before, 0.0), axis=0, keepdims=True).astype(I32)
    carry[...] = carry[...] + jnp.sum(sel, axis=1, keepdims=True)
    cnt_ref[...] = carry[...]


def _norm_router(x1, nw, mod, router_w, router_b, S, shift_idx, scale_idx):
    T, D = x1.shape
    E = router_w.shape[1]
    tm = min(256, S)
    outs = pl.pallas_call(
        functools.partial(_norm_router_kernel, shift_idx=shift_idx, scale_idx=scale_idx),
        out_shape=(
            jax.ShapeDtypeStruct((T, D // 2), U32),
            jax.ShapeDtypeStruct((TOP_K, T), I32),
            jax.ShapeDtypeStruct((TOP_K, T), F32),
            jax.ShapeDtypeStruct((TOP_K, T), I32),
            jax.ShapeDtypeStruct((E, 128), F32),
        ),
        grid=(T // tm,),
        in_specs=[
            pl.BlockSpec((tm, D), lambda i: (i, 0)),
            pl.BlockSpec((1, D), lambda i: (0, 0)),
            pl.BlockSpec((None, N_ADA, D), lambda i: ((i * tm) // S, 0, 0)),
            pl.BlockSpec((E, D), lambda i: (0, 0)),
            pl.BlockSpec((E, 1), lambda i: (0, 0)),
        ],
        out_specs=(
            pl.BlockSpec((tm, D // 2), lambda i: (i, 0)),
            pl.BlockSpec((TOP_K, tm), lambda i: (0, i)),
            pl.BlockSpec((TOP_K, tm), lambda i: (0, i)),
            pl.BlockSpec((TOP_K, tm), lambda i: (0, i)),
            pl.BlockSpec((E, 128), lambda i: (0, 0)),
        ),
        scratch_shapes=[pltpu.VMEM((E, 128), F32)],
        compiler_params=_cparams(("arbitrary",)),
        name="norm_router",
    )(x1, nw.reshape(1, D), mod, router_w.T.astype(BF16), router_b.reshape(E, 1).astype(F32))
    return outs


def _dispatch_kernel(pos_ref, pad_ref, hp_ref, xs_ref, zeros, sem, zsem, *, tm, T, E, n_tiles):
    base = pl.program_id(0) * tm

    @pl.when(pl.program_id(0) == 0)
    def _():
        zeros[...] = jnp.zeros_like(zeros)
        def pad_copy(row):
            return pltpu.make_async_copy(zeros.at[pl.ds(0, 1)], xs_ref.at[pl.ds(row, 1)], zsem)

        def tail_copy(t):
            return pltpu.make_async_copy(
                zeros, xs_ref.at[pl.ds(pl.multiple_of(t * MOE_TILE, MOE_TILE), MOE_TILE)], zsem)

        def pad_start(e, carry):
            start = pad_ref[e]
            lax.fori_loop(0, pad_ref[E + e], lambda r, c: (pad_copy(start + r).start(), c)[1], 0)
            return carry

        def pad_wait(e, carry):
            start = pad_ref[e]
            lax.fori_loop(0, pad_ref[E + e], lambda r, c: (pad_copy(start + r).wait(), c)[1], 0)
            return carry

        def tail_start(t, carry):
            tail_copy(t).start()
            return carry

        def tail_wait(t, carry):
            tail_copy(t).wait()
            return carry

        lax.fori_loop(0, E, pad_start, 0)
        lax.fori_loop(pad_ref[2 * E], n_tiles, tail_start, 0)
        lax.fori_loop(0, E, pad_wait, 0)
        lax.fori_loop(pad_ref[2 * E], n_tiles, tail_wait, 0)

    def start_body(t, carry):
        for k in range(TOP_K):
            dst = pos_ref[k * T + base + t]
            pltpu.make_async_copy(hp_ref.at[pl.ds(t, 1)], xs_ref.at[pl.ds(dst, 1)], sem).start(priority=k % 2)
        return carry

    lax.fori_loop(0, tm, start_body, 0, unroll=8)
    for k in range(TOP_K):
        pltpu.make_async_copy(hp_ref, xs_ref.at[pl.ds(0, tm)], sem).wait()


def _dispatch(pos_flat, pad_info, hp, n_tiles, E):
    T, W = hp.shape
    tm = min(512, T)
    grid_spec = pltpu.PrefetchScalarGridSpec(
        num_scalar_prefetch=2,
        grid=(T // tm,),
        in_specs=[pl.BlockSpec((tm, W), lambda i, p, q: (i, 0))],
        out_specs=pl.BlockSpec(memory_space=pl.ANY),
        scratch_shapes=[pltpu.VMEM((MOE_TILE, W), U32), pltpu.SemaphoreType.DMA, pltpu.SemaphoreType.DMA],
    )
    return pl.pallas_call(
        functools.partial(_dispatch_kernel, tm=tm, T=T, E=E, n_tiles=n_tiles),
        out_shape=jax.ShapeDtypeStruct((n_tiles * MOE_TILE, W), U32),
        grid_spec=grid_spec,
        compiler_params=_cparams(("arbitrary",)),
        name="dispatch",
    )(pos_flat, pad_info, hp)


def _expert_changed(te_ref, i):
    return jnp.logical_or(i == 0, te_ref[i] != te_ref[jnp.maximum(i - 1, 0)])


def _cast_rows(src_ref, dst_ref, rows_per_pass=256):
    def body(r, carry):
        rows = pl.ds(pl.multiple_of(r * rows_per_pass, rows_per_pass), rows_per_pass)
        dst_ref[rows, :] = src_ref[rows, :].astype(dst_ref.dtype)
        return carry

    lax.fori_loop(0, src_ref.shape[0] // rows_per_pass, body, 0)


def _expert_weight_stage(te_ref, nx_ref, i, first_step, n, n_passes, copies, consume):
    @pl.when(_expert_changed(te_ref, i))
    def _():
        @pl.when(first_step)
        def _():
            for cp in copies(te_ref[0], 0):
                cp.start()

        for cp in copies(te_ref[i], n):
            cp.wait()
        consume()
        same_pass = nx_ref[i] >= 0
        nxt_e = jnp.where(same_pass, nx_ref[i], te_ref[0])
        nxt_n = jnp.where(same_pass, n, n + 1)

        @pl.when(jnp.logical_or(same_pass, n + 1 < n_passes))
        def _():
            for cp in copies(nxt_e, nxt_n):
                cp.start()


def _expert_up_kernel(te_ref, nu_ref, nx_ref, xs_ref, w1_hbm, bg_ref, bu_ref, o_ref, stage, wgb, wub, sem,
                      *, tn, F, nf):
    n = pl.program_id(0)
    i = pl.program_id(1)

    def copies(e, nn):
        c0 = pl.multiple_of(nn * tn, tn)
        return (pltpu.make_async_copy(w1_hbm.at[e, :, pl.ds(c0, tn)], stage.at[0], sem),
                pltpu.make_async_copy(w1_hbm.at[e, :, pl.ds(F + c0, tn)], stage.at[1], sem))

    def consume():
        _cast_rows(stage.at[0], wgb)
        _cast_rows(stage.at[1], wub)

    _expert_weight_stage(te_ref, nx_ref, i, jnp.logical_and(n == 0, i == 0), n, nf, copies, consume)

    @pl.when(i < nu_ref[0])
    def _():
        x = _unpack_bf16_pairs(xs_ref[...]).astype(BF16)
        gate = jnp.minimum(_dot(x, wgb[...]) + bg_ref[...], SWIGLU_LIMIT)
        up = jnp.clip(_dot(x, wub[...]) + bu_ref[...], -SWIGLU_LIMIT, SWIGLU_LIMIT)
        act = gate * _sigmoid(SWIGLU_ALPHA * gate) * (up + 1.0)
        o_ref[...] = act.astype(o_ref.dtype)

    @pl.when(i >= nu_ref[0])
    def _():
        o_ref[...] = jnp.zeros_like(o_ref)


def _expert_down_kernel(te_ref, nu_ref, nx_ref, a_ref, w2_hbm, b2_ref, o_ref, stage, w2b, sem):
    i = pl.program_id(0)

    def copies(e, nn):
        del nn
        return (pltpu.make_async_copy(w2_hbm.at[e], stage, sem),)

    def consume():
        _cast_rows(stage, w2b)

    _expert_weight_stage(te_ref, nx_ref, i, i == 0, 0, 1, copies, consume)

    @pl.when(i < nu_ref[0])
    def _():
        y = _dot(a_ref[...], w2b[...]) + b2_ref[...]
        o_ref[...] = _pack_bf16_pairs(y)

    @pl.when(i >= nu_ref[0])
    def _():
        o_ref[...] = jnp.zeros_like(o_ref)


def _expert_ffn(xs, tile_e, n_used, next_e, w1, b1, w2, b2):
    R, W = xs.shape
    E, D, F2 = w1.shape
    F = F2 // 2
    tm = MOE_TILE
    nt = R // tm

    tn = min(512, F)
    nf = F // tn
    b1r = b1.reshape(E, 1, F2)

    act = pl.pallas_call(
        functools.partial(_expert_up_kernel, tn=tn, F=F, nf=nf),
        out_shape=jax.ShapeDtypeStruct((R, F), BF16),
        grid_spec=pltpu.PrefetchScalarGridSpec(
            num_scalar_prefetch=3,
            grid=(nf, nt),
            in_specs=[
                pl.BlockSpec((tm, W), lambda n, i, te, nu, nx: (jnp.minimum(i, nu[0] - 1), 0)),
                pl.BlockSpec(memory_space=pl.ANY),
                pl.BlockSpec((None, 1, tn), lambda n, i, te, nu, nx: (te[i], 0, n)),
                pl.BlockSpec((None, 1, tn), lambda n, i, te, nu, nx: (te[i], 0, nf + n)),
            ],
            out_specs=pl.BlockSpec((tm, tn), lambda n, i, te, nu, nx: (i, n)),
            scratch_shapes=[pltpu.VMEM((2, D, tn), F32), pltpu.VMEM((D, tn), BF16), pltpu.VMEM((D, tn), BF16),
                            pltpu.SemaphoreType.DMA],
        ),
        compiler_params=_cparams(("arbitrary", "arbitrary")),
        name="expert_up",
    )(tile_e, n_used, next_e, xs, w1, b1r, b1r)
    y = pl.pallas_call(
        _expert_down_kernel,
        out_shape=jax.ShapeDtypeStruct((R, D // 2), U32),
        grid_spec=pltpu.PrefetchScalarGridSpec(
            num_scalar_prefetch=3,
            grid=(nt,),
            in_specs=[
                pl.BlockSpec((tm, F), lambda i, te, nu, nx: (jnp.minimum(i, nu[0] - 1), 0)),
                pl.BlockSpec(memory_space=pl.ANY),
                pl.BlockSpec((None, 1, D), lambda i, te, nu, nx: (te[i], 0, 0)),
            ],
            out_specs=pl.BlockSpec((tm, D // 2), lambda i, te, nu, nx: (i, 0)),
            scratch_shapes=[pltpu.VMEM((F, D), F32), pltpu.VMEM((F, D), BF16), pltpu.SemaphoreType.DMA],
        ),
        compiler_params=_cparams(("arbitrary",)),
        name="expert_down",
    )(tile_e, n_used, next_e, act, w2, b2.reshape(E, 1, D))
    return y


def _combine_kernel(pos_ref, x_ref, w_ref, mod_ref, y_ref, o_ref, buf, sem, *, tm, T, gate_idx):
    i = pl.program_id(0)
    n = pl.num_programs(0)
    W = buf.shape[-1]
    th = tm
    ring = buf.shape[0]

    def issue_row(first_token, slot, t):
        for k in range(TOP_K):
            src = pos_ref[k * T + first_token + t]
            pltpu.make_async_copy(y_ref.at[pl.ds(src, 1)], buf.at[slot, k, pl.ds(t, 1)],
                                  sem.at[slot]).start(priority=k % 2)

    def wait_half(slot):
        for k in range(TOP_K):
            pltpu.make_async_copy(y_ref.at[pl.ds(0, th)], buf.at[slot, k], sem.at[slot]).wait()

    rc = 8
    cw = min(512, W)

    def consume(slot, next_first_token, next_slot):
        def rows_body(r, carry):
            rows = pl.ds(pl.multiple_of(r * rc, rc), rc)
            brow = rows
            wv = w_ref[rows, :]
            wk = [jnp.broadcast_to(wv[:, k:k + 1], (rc, cw)) for k in range(TOP_K)]
            for c in range(W // cw):
                lo = hi = None
                for k in range(TOP_K):
                    u = buf[slot, k, brow, c * cw:(c + 1) * cw]
                    l = wk[k] * lax.bitcast_convert_type(jnp.left_shift(u, jnp.uint32(16)), F32)
                    h = wk[k] * lax.bitcast_convert_type(jnp.bitwise_and(u, jnp.uint32(0xFFFF0000)), F32)
                    lo = l if lo is None else lo + l
                    hi = h if hi is None else hi + h
                for half, acc in ((0, lo), (1, hi)):
                    cols = slice(half * W + c * cw, half * W + (c + 1) * cw)
                    o_ref[rows, cols] = x_ref[rows, cols] + mod_ref[gate_idx:gate_idx + 1, cols] * acc
            for t in range(rc):
                issue_row(next_first_token, next_slot, r * rc + t)
            return carry

        lax.fori_loop(0, th // rc, rows_body, 0)

    @pl.when(i == 0)
    def _():
        for b in range(2):
            lax.fori_loop(0, tm, lambda t, c: (issue_row(b * tm, b, t), c)[1], 0, unroll=8)

    wait_half(i % ring)
    consume(i % ring, jnp.minimum((i + 2) * tm, T - tm), (i + 2) % ring)

    @pl.when(i == n - 1)
    def _():
        wait_half((i + 1) % ring)
        wait_half((i + 2) % ring)


def _combine(pos_flat, x1, w_tk, mod, y, S, gate_idx):
    T, D = x1.shape
    tm = min(128, S)
    assert T // tm >= 2
    grid_spec = pltpu.PrefetchScalarGridSpec(
        num_scalar_prefetch=1,
        grid=(T // tm,),
        in_specs=[
            pl.BlockSpec((tm, D), lambda i, p: (i, 0)),
            pl.BlockSpec((tm, TOP_K), lambda i, p: (i, 0)),
            pl.BlockSpec((None, N_ADA, D), lambda i, p: ((i * tm) // S, 0, 0)),
            pl.BlockSpec(memory_space=pl.ANY),
        ],
        out_specs=pl.BlockSpec((tm, D), lambda i, p: (i, 0)),
        scratch_shapes=[pltpu.VMEM((3, TOP_K, tm, D // 2), U32), pltpu.SemaphoreType.DMA((3,))],
    )
    return pl.pallas_call(
        functools.partial(_combine_kernel, tm=tm, T=T, gate_idx=gate_idx),
        out_shape=jax.ShapeDtypeStruct((T, D), F32),
        grid_spec=grid_spec,
        compiler_params=_cparams(("arbitrary",)),
        name="combine",
    )(pos_flat, x1, w_tk, mod, y)


def _layer(x2, mod, B, S, rel_bias, norm_mix_w, w_in, q_norm_w, k_norm_w, attn_sink, ret_decay_fwd, ret_decay_bwd,
           ret_gn_w, ret_gn_b, w_up_attn, w_up_ret, w_out, norm_ffn_w, router_w, router_b,
           expert_w1, expert_b1, expert_w2, expert_b2):
    T, D = x2.shape
    E = router_w.shape[1]
    h = _norm_mod(x2, norm_mix_w, mod, S, 0, 1)
    proj = _in_proj(h, w_in, q_norm_w, k_norm_w)
    attn = _window_attention(proj, _attn_bias_table(rel_bias), attn_sink, B, S)
    ret = _retention(proj, ret_decay_fwd, ret_decay_bwd, ret_gn_w, ret_gn_b, B, S)
    merged = _merge_up(attn, ret, w_up_attn, w_up_ret, proj, D)
    x1 = _out_proj(merged, w_out, x2, mod, S, 2)

    hp, top_e, top_w, rank, cnt = _norm_router(x1, norm_ffn_w, mod, router_w, router_b, S, 3, 4)
    tm = MOE_TILE
    counts = cnt[:, 0].astype(I32)
    tiles_per_e = (counts + tm - 1) // tm
    tile_end = jnp.cumsum(tiles_per_e)
    group_start = (tile_end - tiles_per_e) * tm
    n_tiles = (T * TOP_K) // tm + E
    n_used = tile_end[-1]
    tile_ids = jnp.minimum(jnp.arange(n_tiles, dtype=I32), n_used - 1)
    tile_e = jnp.minimum(jnp.sum((tile_ids[:, None] >= tile_end[None, :]).astype(I32), axis=1), E - 1)
    e_ids = jnp.arange(E, dtype=I32)
    later = jnp.logical_and(e_ids[None, :] > e_ids[:, None], tiles_per_e[None, :] > 0)
    next_of_e = jnp.min(jnp.where(later, e_ids[None, :], E), axis=1)
    next_of_e = jnp.where(next_of_e == E, -1, next_of_e)
    next_e = jnp.sum(jnp.where(tile_e[:, None] == e_ids, next_of_e, 0), axis=1).astype(I32)
    start_of = jnp.sum(jnp.where(top_e[:, :, None] == e_ids, group_start, 0), axis=-1)
    pos_flat = (start_of + rank).astype(I32).reshape(-1)
    pad_info = jnp.concatenate([group_start + counts, tiles_per_e * tm - counts, n_used.reshape(1)]).astype(I32)

    xs = _dispatch(pos_flat, pad_info, hp, n_tiles, E)
    y = _expert_ffn(xs, tile_e, n_used.reshape(1).astype(I32), next_e, expert_w1, expert_b1, expert_w2, expert_b2)
    return _combine(pos_flat, x1, top_w.T, mod, y, S, 5)


def kernel(x, c, rel_bias, ada_w, ada_b, norm_mix_w, w_in, q_norm_w, k_norm_w, attn_sink, ret_decay_fwd,
           ret_decay_bwd, ret_gn_w, ret_gn_b, w_up_attn, w_up_ret, w_out, norm_ffn_w, router_w, router_b,
           expert_w1, expert_b1, expert_w2, expert_b2):
    B, S, D = x.shape
    x2 = x.reshape(B * S, D)
    for l in range(ada_w.shape[0]):
        mod = _ada_mod(c, ada_w[l], ada_b[l])
        x2 = _layer(x2, mod, B, S, rel_bias, norm_mix_w[l], w_in[l], q_norm_w[l], k_norm_w[l], attn_sink[l],
                    ret_decay_fwd[l], ret_decay_bwd[l], ret_gn_w[l], ret_gn_b[l], w_up_attn[l], w_up_ret[l],
                    w_out[l], norm_ffn_w[l], router_w[l], router_b[l], expert_w1[l], expert_b1[l],
                    expert_w2[l], expert_b2[l])
    return x2.reshape(B, S, D)
```

```python
import functools
import math

import jax
import jax.numpy as jnp
from jax import lax
from jax.experimental import pallas as pl
from jax.experimental.pallas import tpu as pltpu

F32 = jnp.float32
BF16 = jnp.bfloat16
U32 = jnp.uint32
I32 = jnp.int32

ATTN_HEADS = 16
ATTN_KV_HEADS = 4
ATTN_HEAD_DIM = 128
WINDOW = 128
ATTN_BLOCK = 128
N_BUCKETS = 32
MAX_DISTANCE = 128
RET_HEADS = 8
RET_QK_DIM = 128
RET_V_DIM = 256
RET_CHUNK = 128
ROPE_BASE = 10000.0
TOP_K = 4
SWIGLU_LIMIT = 7.0
SWIGLU_ALPHA = 1.702
N_ADA = 6
EPS = 1e-6
NEG_INF = -1e30
NEG_BIG = -3.0e38
LOG2E = 1.4426950408889634

ATTN_Q_W = ATTN_HEADS * ATTN_HEAD_DIM
ATTN_KV_W = ATTN_KV_HEADS * ATTN_HEAD_DIM
RET_QK_W = RET_HEADS * RET_QK_DIM
RET_V_W = RET_HEADS * RET_V_DIM

VMEM_LIMIT_BYTES = 56 * 1024 * 1024
MOE_TILE = 512
RET_TILE = 256


def _cparams(sem):
    return pltpu.CompilerParams(dimension_semantics=sem, vmem_limit_bytes=VMEM_LIMIT_BYTES)


def _dot(a, b):
    return jnp.dot(a, b, preferred_element_type=F32)


def _dot_nt(a, b):
    return lax.dot_general(a, b, (((1,), (1,)), ((), ())), preferred_element_type=F32)


def _dot_tn(a, b):
    return lax.dot_general(a, b, (((0,), (0,)), ((), ())), preferred_element_type=F32)


def _sigmoid(x):
    return 1.0 / (1.0 + jnp.exp(-x))


def _ada_kernel(c_ref, w_ref, b_ref, o_ref):
    c = c_ref[...]
    cs = (c * _sigmoid(c)).astype(BF16)
    o_ref[...] = _dot(cs, w_ref[...].astype(BF16)) + b_ref[...]


def _ada_mod(c, ada_w, ada_b):
    B, D = c.shape
    N = ada_w.shape[1]
    rows = 8
    cp = jnp.zeros((rows, D), F32).at[:B].set(c)
    tn = min(1024, N)
    out = pl.pallas_call(
        _ada_kernel,
        out_shape=jax.ShapeDtypeStruct((rows, N), F32),
        grid=(N // tn,),
        in_specs=[
            pl.BlockSpec((rows, D), lambda j: (0, 0)),
            pl.BlockSpec((D, tn), lambda j: (0, j)),
            pl.BlockSpec((1, tn), lambda j: (0, j)),
        ],
        out_specs=pl.BlockSpec((rows, tn), lambda j: (0, j)),
        compiler_params=_cparams(("arbitrary",)),
        name="ada_mod",
    )(cp, ada_w, ada_b.reshape(1, N))
    return out[:B].reshape(B, N_ADA, D)


def _norm_mod_value(x, nw, shift, scale):
    ms = jnp.mean(x * x, axis=-1, keepdims=True)
    h = x * lax.rsqrt(ms + EPS) * nw
    return h * (1.0 + scale) + shift


def _norm_mod_kernel(x_ref, nw_ref, mod_ref, o_ref, *, shift_idx, scale_idx):
    h = _norm_mod_value(x_ref[...], nw_ref[...], mod_ref[shift_idx:shift_idx + 1, :],
                        mod_ref[scale_idx:scale_idx + 1, :])
    o_ref[...] = h.astype(BF16)


def _norm_mod(x2, nw, mod, S, shift_idx, scale_idx):
    T, D = x2.shape
    tm = min(512, S)
    return pl.pallas_call(
        functools.partial(_norm_mod_kernel, shift_idx=shift_idx, scale_idx=scale_idx),
        out_shape=jax.ShapeDtypeStruct((T, D), BF16),
        grid=(T // tm,),
        in_specs=[
            pl.BlockSpec((tm, D), lambda i: (i, 0)),
            pl.BlockSpec((1, D), lambda i: (0, 0)),
            pl.BlockSpec((None, N_ADA, D), lambda i: ((i * tm) // S, 0, 0)),
        ],
        out_specs=pl.BlockSpec((tm, D), lambda i: (i, 0)),
        compiler_params=_cparams(("arbitrary",)),
        name="norm_mod",
    )(x2, nw.reshape(1, D), mod)


def _cast_weight_once(w_ref, wb_ref):
    @pl.when(pl.program_id(1) == 0)
    def _():
        wb_ref[...] = w_ref[...].astype(BF16)


def _in_proj_kernel(a_ref, w_hbm, cw_ref, cf_ref, bd_ref, o_ref, stage, wb, sem, *, n_norm_tiles):
    j = pl.program_id(0)
    tn = wb.shape[1]

    def w_copy(jj):
        return pltpu.make_async_copy(w_hbm.at[:, pl.ds(pl.multiple_of(jj * tn, tn), tn)], stage, sem)

    @pl.when(pl.program_id(1) == 0)
    def _():
        @pl.when(j == 0)
        def _():
            w_copy(0).start()

        w_copy(j).wait()
        _cast_rows(stage, wb)

        @pl.when(j + 1 < pl.num_programs(0))
        def _():
            w_copy(j + 1).start()

    @pl.when(j < n_norm_tiles)
    def _():
        cb = bd_ref.shape[0]
        for c in range(tn // cb):
            cols = slice(c * cb, (c + 1) * cb)
            acc = _dot(a_ref[...], wb[:, cols])
            ms = _dot((acc * acc).astype(BF16), bd_ref[...]) * (1.0 / ATTN_HEAD_DIM)
            inv = jnp.where(cf_ref[:, cols] > 0.0, lax.rsqrt(ms + EPS), 1.0)
            o_ref[:, cols] = (acc * inv * cw_ref[:, cols]).astype(o_ref.dtype)

    @pl.when(j >= n_norm_tiles)
    def _():
        o_ref[...] = _dot(a_ref[...], wb[...]).astype(o_ref.dtype)


def _in_proj(h, w, q_norm_w, k_norm_w):
    T, K = h.shape
    N = w.shape[1]
    tm = min(1024, T)
    tn = 1024 if N % 1024 == 0 else 512
    hd = ATTN_HEAD_DIM
    n_norm = ATTN_Q_W + ATTN_KV_W
    n_norm_tiles = -(-n_norm // tn)
    pad = n_norm_tiles * tn - n_norm
    col_w = jnp.concatenate([jnp.tile(q_norm_w.astype(F32) * (hd ** -0.5 * LOG2E), ATTN_HEADS),
                             jnp.tile(k_norm_w.astype(F32), ATTN_KV_HEADS), jnp.ones((pad,), F32)]).reshape(1, -1)
    col_flag = jnp.concatenate([jnp.ones((n_norm,), F32), jnp.zeros((pad,), F32)]).reshape(1, -1)
    cb = min(512, tn)
    lane_head = jnp.arange(cb, dtype=I32) // hd
    block_ones = (lane_head[:, None] == lane_head[None, :]).astype(BF16)
    last = n_norm_tiles - 1
    return pl.pallas_call(
        functools.partial(_in_proj_kernel, n_norm_tiles=n_norm_tiles),
        out_shape=jax.ShapeDtypeStruct((T, N), BF16),
        grid=(N // tn, T // tm),
        in_specs=[
            pl.BlockSpec((tm, K), lambda j, i: (i, 0)),
            pl.BlockSpec(memory_space=pl.ANY),
            pl.BlockSpec((1, tn), lambda j, i: (0, jnp.minimum(j, last))),
            pl.BlockSpec((1, tn), lambda j, i: (0, jnp.minimum(j, last))),
            pl.BlockSpec((cb, cb), lambda j, i: (0, 0)),
        ],
        out_specs=pl.BlockSpec((tm, tn), lambda j, i: (i, j)),
        scratch_shapes=[pltpu.VMEM((K, tn), F32), pltpu.VMEM((K, tn), BF16), pltpu.SemaphoreType.DMA],
        compiler_params=_cparams(("arbitrary", "arbitrary")),
        name="in_proj",
    )(h, w, col_w, col_flag, block_ones)


def _t5_bucket(rel):
    nb = N_BUCKETS // 2
    max_exact = nb // 2
    base = jnp.where(rel > 0, nb, 0)
    n = jnp.abs(rel)
    nf = jnp.maximum(n, 1).astype(F32)
    large = max_exact + (jnp.log(nf / max_exact) / math.log(MAX_DISTANCE / max_exact) * (nb - max_exact)).astype(I32)
    large = jnp.minimum(large, nb - 1)
    return base + jnp.where(n < max_exact, n, large)


def _attn_bias_table(rel_bias):
    blk = ATTN_BLOCK
    qi = jnp.arange(blk, dtype=I32)[:, None]
    kj = jnp.arange(3 * blk, dtype=I32)[None, :]
    rel = kj - blk - qi
    bucket = _t5_bucket(rel)
    table = rel_bias.astype(F32).T
    bias = jnp.zeros((table.shape[0],) + bucket.shape, F32)
    for b in range(N_BUCKETS):
        bias = jnp.where(bucket[None] == b, table[:, b][:, None, None], bias)
    bias = jnp.where((jnp.abs(rel) <= WINDOW)[None], bias * LOG2E, NEG_INF)
    G = ATTN_HEADS // ATTN_KV_HEADS
    bias = bias.reshape(ATTN_KV_HEADS, G, blk, 3 * blk)
    bias = jnp.transpose(bias, (0, 3, 1, 2)).reshape(ATTN_KV_HEADS, 3 * blk, G * blk)
    key = jnp.arange(3 * blk, dtype=I32)[None, :, None]
    no_prev, no_next = key < blk, key >= 2 * blk
    return jnp.stack([bias, jnp.where(no_prev, NEG_INF, bias), jnp.where(no_next, NEG_INF, bias),
                      jnp.where(jnp.logical_or(no_prev, no_next), NEG_INF, bias)])


def _attn_kernel(sink_ref, q_ref, kp_ref, kc_ref, kn_ref, vp_ref, vc_ref, vn_ref, bias_ref, o_ref):
    blk, hd = ATTN_BLOCK, ATTN_HEAD_DIM
    G = ATTN_HEADS // ATTN_KV_HEADS
    for hk in range(ATTN_KV_HEADS):
        sl = slice(hk * hd, (hk + 1) * hd)
        k3 = jnp.concatenate([kp_ref[:, sl], kc_ref[:, sl], kn_ref[:, sl]], axis=0)
        v3 = jnp.concatenate([vp_ref[:, sl], vc_ref[:, sl], vn_ref[:, sl]], axis=0)
        q4 = jnp.concatenate([q_ref[:, (hk * G + g) * hd:(hk * G + g + 1) * hd] for g in range(G)],
                             axis=0)
        sink = jnp.concatenate([jnp.full((1, blk), sink_ref[hk * G + g], F32) for g in range(G)], axis=1)
        logits = _dot_nt(k3, q4) + bias_ref[hk]
        m = jnp.maximum(jnp.max(logits, axis=0, keepdims=True), sink)
        p = jnp.exp2(logits - m)
        denom = jnp.sum(p, axis=0, keepdims=True) + jnp.exp2(sink - m)
        out_t = _dot_tn(v3, p.astype(BF16)) * (1.0 / denom)
        out = out_t.T
        for g in range(G):
            h = hk * G + g
            o_ref[:, h * hd:(h + 1) * hd] = out[g * blk:(g + 1) * blk].astype(o_ref.dtype)


def _window_attention(proj, bias_tab, sink, B, S):
    T = proj.shape[0]
    blk = ATTN_BLOCK
    nb = S // blk
    kcol = ATTN_Q_W // ATTN_KV_W
    vcol = kcol + 1

    def kv_spec(col, off):
        return pl.BlockSpec((blk, ATTN_KV_W), lambda b, n, s: (b * nb + jnp.clip(n + off, 0, nb - 1), col))

    grid_spec = pltpu.PrefetchScalarGridSpec(
        num_scalar_prefetch=1,
        grid=(B, nb),
        in_specs=[
            pl.BlockSpec((blk, ATTN_Q_W), lambda b, n, s: (b * nb + n, 0)),
            kv_spec(kcol, -1), kv_spec(kcol, 0), kv_spec(kcol, 1),
            kv_spec(vcol, -1), kv_spec(vcol, 0), kv_spec(vcol, 1),
            pl.BlockSpec((None, ATTN_KV_HEADS, 3 * blk, (ATTN_HEADS // ATTN_KV_HEADS) * blk),
                         lambda b, n, s: ((n == 0).astype(I32) + 2 * (n == nb - 1).astype(I32), 0, 0, 0)),
        ],
        out_specs=pl.BlockSpec((blk, ATTN_Q_W), lambda b, n, s: (b * nb + n, 0)),
    )
    return pl.pallas_call(
        _attn_kernel,
        out_shape=jax.ShapeDtypeStruct((T, ATTN_Q_W), BF16),
        grid_spec=grid_spec,
        compiler_params=_cparams(("arbitrary", "arbitrary")),
        name="window_attn",
    )(sink.astype(F32) * LOG2E, proj, proj, proj, proj, proj, proj, proj, bias_tab)


def _rope_tables(S):
    d = RET_QK_DIM
    inv = ROPE_BASE ** (-jnp.arange(0, d, 2, dtype=F32) / d)
    ang = jnp.arange(S, dtype=F32)[:, None] * inv[None, :]
    cos, sin = jnp.cos(ang), jnp.sin(ang)
    return jnp.concatenate([cos, cos], axis=-1), jnp.concatenate([-sin, sin], axis=-1)


def _ret_kernel(df_ref, db_ref, q_ref, k_ref, v_ref, g_ref, cos_ref, sin_ref, gw_ref, gb_ref, o_ref,
                qb, kb, q2, kv, sprev, *, S, C):
    h = pl.program_id(1)
    nc = S // C
    dk = RET_QK_DIM
    half = dk // 2

    rowf = lax.broadcasted_iota(I32, (C, 1), 0).astype(F32)
    lg_f = -jnp.exp(jnp.full((1, 1), df_ref[h], F32))
    lg_b = -jnp.exp(jnp.full((1, 1), db_ref[h], F32))
    qdec_f = jnp.exp((rowf + 1.0) * lg_f)
    kdec_f = jnp.exp((C - 1.0 - rowf) * lg_f)
    qdec_b = jnp.exp((C - rowf) * lg_b)
    kdec_b = jnp.exp(rowf * lg_b)
    cd_f = jnp.exp(C * lg_f)
    cd_b = jnp.exp(C * lg_b)

    def a_body(n, carry):
        rows = pl.ds(pl.multiple_of(n * C, C), C)
        co = cos_ref[rows, :]
        si = sin_ref[rows, :]
        q = q_ref[rows, :].astype(F32)
        k = k_ref[rows, :].astype(F32)
        qr = q * co + pltpu.roll(q, half, 1) * si
        kr = (k * co + pltpu.roll(k, half, 1) * si) * (dk ** -0.5)
        qb[rows, :] = qr.astype(BF16)
        kb[rows, :] = kr.astype(BF16)
        q2[rows, :] = jnp.concatenate([qr * qdec_f, qr * qdec_b], axis=1).astype(BF16)
        k2 = jnp.concatenate([kr * kdec_f, kr * kdec_b], axis=1).astype(BF16)
        kv[n] = _dot_tn(k2, v_ref[rows, :])
        return carry

    lax.fori_loop(0, nc, a_body, 0, unroll=4)

    def scan_f(n, state):
        sprev[n, 0:dk, :] = state.astype(BF16)
        return state * cd_f + kv[n, 0:dk, :]

    lax.fori_loop(0, nc, scan_f, jnp.zeros((dk, RET_V_DIM), F32))

    def scan_b(t, state):
        n = nc - 1 - t
        sprev[n, dk:2 * dk, :] = state.astype(BF16)
        return state * cd_b + kv[n, dk:2 * dk, :]

    lax.fori_loop(0, nc, scan_b, jnp.zeros((dk, RET_V_DIM), F32))

    ri = lax.broadcasted_iota(I32, (C, C), 0)
    ci = lax.broadcasted_iota(I32, (C, C), 1)
    d = (ri - ci).astype(F32)
    dec = jnp.where(ri >= ci, jnp.exp(jnp.maximum(d, 0.0) * lg_f), jnp.exp(jnp.maximum(-d, 0.0) * lg_b))
    gw = gw_ref[...]
    gb = gb_ref[...]

    def c_body(n, carry):
        rows = pl.ds(pl.multiple_of(n * C, C), C)
        scores = _dot_nt(qb[rows, :], kb[rows, :]) * dec
        y = _dot(scores.astype(BF16), v_ref[rows, :]) + _dot(q2[rows, :], sprev[n])
        mu = jnp.mean(y, axis=-1, keepdims=True)
        yc = y - mu
        var = jnp.mean(yc * yc, axis=-1, keepdims=True)
        yn = yc * lax.rsqrt(var + EPS) * gw + gb
        g = g_ref[rows, :].astype(F32)
        o_ref[rows, :] = (g * _sigmoid(g) * yn).astype(o_ref.dtype)
        return carry

    lax.fori_loop(0, nc, c_body, 0, unroll=4)


def _retention(proj, decay_fwd, decay_bwd, gn_w, gn_b, B, S):
    T = proj.shape[0]
    dk, dv = RET_QK_DIM, RET_V_DIM
    q_off = (ATTN_Q_W + 2 * ATTN_KV_W) // dk
    k_off = q_off + RET_QK_W // dk
    v_off = (ATTN_Q_W + 2 * ATTN_KV_W + 2 * RET_QK_W) // dv
    g_off = v_off + RET_V_W // dv
    cos, sin = _rope_tables(S)
    C = min(RET_TILE, S)
    grid_spec = pltpu.PrefetchScalarGridSpec(
        num_scalar_prefetch=2,
        grid=(B, RET_HEADS),
        in_specs=[
            pl.BlockSpec((S, dk), lambda b, h, *_: (b, q_off + h)),
            pl.BlockSpec((S, dk), lambda b, h, *_: (b, k_off + h)),
            pl.BlockSpec((S, dv), lambda b, h, *_: (b, v_off + h)),
            pl.BlockSpec((S, dv), lambda b, h, *_: (b, g_off + h)),
            pl.BlockSpec((S, dk), lambda b, h, *_: (0, 0)),
            pl.BlockSpec((S, dk), lambda b, h, *_: (0, 0)),
            pl.BlockSpec((1, dv), lambda b, h, *_: (0, h)),
            pl.BlockSpec((1, dv), lambda b, h, *_: (0, h)),
        ],
        out_specs=pl.BlockSpec((S, dv), lambda b, h, *_: (b, h)),
        scratch_shapes=[
            pltpu.VMEM((S, dk), BF16),
            pltpu.VMEM((S, dk), BF16),
            pltpu.VMEM((S, 2 * dk), BF16),
            pltpu.VMEM((S // C, 2 * dk, dv), F32),
            pltpu.VMEM((S // C, 2 * dk, dv), BF16),
        ],
    )
    return pl.pallas_call(
        functools.partial(_ret_kernel, S=S, C=C),
        out_shape=jax.ShapeDtypeStruct((T, RET_V_W), BF16),
        grid_spec=grid_spec,
        compiler_params=_cparams(("arbitrary", "arbitrary")),
        name="retention",
    )(decay_fwd.astype(F32), decay_bwd.astype(F32), proj, proj, proj, proj, cos, sin,
      gn_w.reshape(1, -1), gn_b.reshape(1, -1))


def _merge_kernel(a_ref, r_ref, wa_ref, wr_ref, ga_ref, gr_ref, o_ref, wab, wrb):
    _cast_weight_once(wa_ref, wab)
    _cast_weight_once(wr_ref, wrb)
    ua = _dot(a_ref[...], wab[...])
    ur = _dot(r_ref[...], wrb[...])
    out = _sigmoid(ga_ref[...].astype(F32)) * ua + _sigmoid(gr_ref[...].astype(F32)) * ur
    o_ref[...] = out.astype(o_ref.dtype)


def _merge_up(attn, ret, wa, wr, proj, D):
    T = attn.shape[0]
    tm = min(1024, T)
    tn = min(512, D)
    ga_off = (ATTN_Q_W + 2 * ATTN_KV_W + 2 * RET_QK_W + 2 * RET_V_W) // tn
    gr_off = ga_off + D // tn
    return pl.pallas_call(
        _merge_kernel,
        out_shape=jax.ShapeDtypeStruct((T, D), BF16),
        grid=(D // tn, T // tm),
        in_specs=[
            pl.BlockSpec((tm, ATTN_Q_W), lambda j, i: (i, 0)),
            pl.BlockSpec((tm, RET_V_W), lambda j, i: (i, 0)),
            pl.BlockSpec((ATTN_Q_W, tn), lambda j, i: (0, j)),
            pl.BlockSpec((RET_V_W, tn), lambda j, i: (0, j)),
            pl.BlockSpec((tm, tn), lambda j, i: (i, ga_off + j)),
            pl.BlockSpec((tm, tn), lambda j, i: (i, gr_off + j)),
        ],
        out_specs=pl.BlockSpec((tm, tn), lambda j, i: (i, j)),
        scratch_shapes=[pltpu.VMEM((ATTN_Q_W, tn), BF16), pltpu.VMEM((RET_V_W, tn), BF16)],
        compiler_params=_cparams(("arbitrary", "arbitrary")),
        name="merge_up",
    )(attn, ret, wa, wr, proj, proj)


def _out_proj_kernel(m_ref, w_ref, x_ref, mod_ref, o_ref, wb, *, gate_idx):
    _cast_weight_once(w_ref, wb)
    y = _dot(m_ref[...], wb[...])
    o_ref[...] = x_ref[...] + mod_ref[gate_idx:gate_idx + 1, :] * y


def _out_proj(merged, w, x2, mod, S, gate_idx):
    T, D = x2.shape
    tm = min(1024, S)
    tn = min(512, D)
    return pl.pallas_call(
        functools.partial(_out_proj_kernel, gate_idx=gate_idx),
        out_shape=jax.ShapeDtypeStruct((T, D), F32),
        grid=(D // tn, T // tm),
        in_specs=[
            pl.BlockSpec((tm, D), lambda j, i: (i, 0)),
            pl.BlockSpec((D, tn), lambda j, i: (0, j)),
            pl.BlockSpec((tm, tn), lambda j, i: (i, j)),
            pl.BlockSpec((None, N_ADA, tn), lambda j, i: ((i * tm) // S, 0, j)),
        ],
        out_specs=pl.BlockSpec((tm, tn), lambda j, i: (i, j)),
        scratch_shapes=[pltpu.VMEM((D, tn), BF16)],
        compiler_params=_cparams(("arbitrary", "arbitrary")),
        name="out_proj",
    )(merged, w, x2, mod)


def _pack_bf16_pairs(v):
    n = v.shape[1] // 2
    bits = lax.bitcast_convert_type(v.astype(BF16).astype(F32), U32)
    return jnp.bitwise_or(jnp.bitwise_and(bits[:, n:], jnp.uint32(0xFFFF0000)),
                          jnp.right_shift(bits[:, :n], jnp.uint32(16)))


def _unpack_bf16_pairs(w):
    lo = lax.bitcast_convert_type(jnp.left_shift(w, jnp.uint32(16)), F32)
    hi = lax.bitcast_convert_type(jnp.bitwise_and(w, jnp.uint32(0xFFFF0000)), F32)
    return jnp.concatenate([lo, hi], axis=1)


def _norm_router_kernel(x_ref, nw_ref, mod_ref, rwt_ref, rb_ref, hp_ref, e_ref, w_ref, r_ref, cnt_ref, carry,
                        *, shift_idx, scale_idx):
    i = pl.program_id(0)

    @pl.when(i == 0)
    def _():
        carry[...] = jnp.zeros_like(carry)

    h = _norm_mod_value(x_ref[...], nw_ref[...], mod_ref[shift_idx:shift_idx + 1, :],
                        mod_ref[scale_idx:scale_idx + 1, :])
    hp_ref[...] = _pack_bf16_pairs(h)
    hb = h.astype(BF16)
    tm = hb.shape[0]
    E = rwt_ref.shape[0]
    logits = _dot_nt(rwt_ref[...], hb) + rb_ref[...]
    iota_e = lax.broadcasted_iota(I32, (E, tm), 0)
    vals, idxs = [], []
    work = logits
    sel = jnp.zeros((E, tm), F32)
    for _k in range(TOP_K):
        m = jnp.max(work, axis=0, keepdims=True)
        idx = jnp.min(jnp.where(work == m, iota_e, E), axis=0, keepdims=True)
        hit = iota_e == idx
        vals.append(m)
        idxs.append(idx)
        work = jnp.where(hit, NEG_BIG, work)
        sel = sel + hit.astype(F32)
    ex = [jnp.exp(v - vals[0]) for v in vals]
    tot = ex[0]
    for v in ex[1:]:
        tot = tot + v
    ri = lax.broadcasted_iota(I32, (tm, tm), 0)
    ci = lax.broadcasted_iota(I32, (tm, tm), 1)
    upper = (ri < ci).astype(BF16)
    before = _dot(sel.astype(BF16), upper) + carry[:, 0:1]
    for k in range(TOP_K):
        e_ref[k:k + 1, :] = idxs[k]
        w_ref[k:k + 1, :] = ex[k] / tot
        r_ref[k:k + 1, :] = jnp.sum(jnp.where(iota_e == idxs[k], before, 0.0), axis=0, keepdims=True).astype(I32)
    carry[...] = carry[...] + jnp.sum(sel, axis=1, keepdims=True)
    cnt_ref[...] = carry[...]


def _norm_router(x1, nw, mod, router_w, router_b, S, shift_idx, scale_idx):
    T, D = x1.shape
    E = router_w.shape[1]
    tm = min(256, S)
    outs = pl.pallas_call(
        functools.partial(_norm_router_kernel, shift_idx=shift_idx, scale_idx=scale_idx),
        out_shape=(
            jax.ShapeDtypeStruct((T, D // 2), U32),
            jax.ShapeDtypeStruct((TOP_K, T), I32),
            jax.ShapeDtypeStruct((TOP_K, T), F32),
            jax.ShapeDtypeStruct((TOP_K, T), I32),
            jax.ShapeDtypeStruct((E, 128), F32),
        ),
        grid=(T // tm,),
        in_specs=[
            pl.BlockSpec((tm, D), lambda i: (i, 0)),
            pl.BlockSpec((1, D), lambda i: (0, 0)),
            pl.BlockSpec((None, N_ADA, D), lambda i: ((i * tm) // S, 0, 0)),
            pl.BlockSpec((E, D), lambda i: (0, 0)),
            pl.BlockSpec((E, 1), lambda i: (0, 0)),
        ],
        out_specs=(
            pl.BlockSpec((tm, D // 2), lambda i: (i, 0)),
            pl.BlockSpec((TOP_K, tm), lambda i: (0, i)),
            pl.BlockSpec((TOP_K, tm), lambda i: (0, i)),
            pl.BlockSpec((TOP_K, tm), lambda i: (0, i)),
            pl.BlockSpec((E, 128), lambda i: (0, 0)),
        ),
        scratch_shapes=[pltpu.VMEM((E, 128), F32)],
        compiler_params=_cparams(("arbitrary",)),
        name="norm_router",
    )(x1, nw.reshape(1, D), mod, router_w.T.astype(BF16), router_b.reshape(E, 1).astype(F32))
    return outs


def _dispatch_kernel(pos_ref, pad_ref, hp_ref, xs_ref, zeros, sem, zsem, *, tm, T, E, n_tiles):
    base = pl.program_id(0) * tm

    @pl.when(pl.program_id(0) == 0)
    def _():
        zeros[...] = jnp.zeros_like(zeros)
        def pad_copy(row):
            return pltpu.make_async_copy(zeros.at[pl.ds(0, 1)], xs_ref.at[pl.ds(row, 1)], zsem)

        def tail_copy(t):
            return pltpu.make_async_copy(
                zeros, xs_ref.at[pl.ds(pl.multiple_of(t * MOE_TILE, MOE_TILE), MOE_TILE)], zsem)

        def pad_start(e, carry):
            start = pad_ref[e]
            lax.fori_loop(0, pad_ref[E + e], lambda r, c: (pad_copy(start + r).start(), c)[1], 0)
            return carry

        def pad_wait(e, carry):
            start = pad_ref[e]
            lax.fori_loop(0, pad_ref[E + e], lambda r, c: (pad_copy(start + r).wait(), c)[1], 0)
            return carry

        def tail_start(t, carry):
            tail_copy(t).start()
            return carry

        def tail_wait(t, carry):
            tail_copy(t).wait()
            return carry

        lax.fori_loop(0, E, pad_start, 0)
        lax.fori_loop(pad_ref[2 * E], n_tiles, tail_start, 0)
        lax.fori_loop(0, E, pad_wait, 0)
        lax.fori_loop(pad_ref[2 * E], n_tiles, tail_wait, 0)

    def start_body(t, carry):
        for k in range(TOP_K):
            dst = pos_ref[k * T + base + t]
            pltpu.make_async_copy(hp_ref.at[pl.ds(t, 1)], xs_ref.at[pl.ds(dst, 1)], sem).start(priority=k % 2)
        return carry

    lax.fori_loop(0, tm, start_body, 0, unroll=8)
    for k in range(TOP_K):
        pltpu.make_async_copy(hp_ref, xs_ref.at[pl.ds(0, tm)], sem).wait()


def _dispatch(pos_flat, pad_info, hp, n_tiles, E):
    T, W = hp.shape
    tm = min(512, T)
    grid_spec = pltpu.PrefetchScalarGridSpec(
        num_scalar_prefetch=2,
        grid=(T // tm,),
        in_specs=[pl.BlockSpec((tm, W), lambda i, p, q: (i, 0))],
        out_specs=pl.BlockSpec(memory_space=pl.ANY),
        scratch_shapes=[pltpu.VMEM((MOE_TILE, W), U32), pltpu.SemaphoreType.DMA, pltpu.SemaphoreType.DMA],
    )
    return pl.pallas_call(
        functools.partial(_dispatch_kernel, tm=tm, T=T, E=E, n_tiles=n_tiles),
        out_shape=jax.ShapeDtypeStruct((n_tiles * MOE_TILE, W), U32),
        grid_spec=grid_spec,
        compiler_params=_cparams(("arbitrary",)),
        name="dispatch",
    )(pos_flat, pad_info, hp)


def _expert_changed(te_ref, i):
    return jnp.logical_or(i == 0, te_ref[i] != te_ref[jnp.maximum(i - 1, 0)])


def _cast_rows(src_ref, dst_ref, rows_per_pass=256):
    def body(r, carry):
        rows = pl.ds(pl.multiple_of(r * rows_per_pass, rows_per_pass), rows_per_pass)
        dst_ref[rows, :] = src_ref[rows, :].astype(dst_ref.dtype)
        return carry

    lax.fori_loop(0, src_ref.shape[0] // rows_per_pass, body, 0)


def _expert_weight_stage(te_ref, nx_ref, i, first_step, n, n_passes, copies, consume):
    @pl.when(_expert_changed(te_ref, i))
    def _():
        @pl.when(first_step)
        def _():
            for cp in copies(te_ref[0], 0):
                cp.start()

        for cp in copies(te_ref[i], n):
            cp.wait()
        consume()
        same_pass = nx_ref[i] >= 0
        nxt_e = jnp.where(same_pass, nx_ref[i], te_ref[0])
        nxt_n = jnp.where(same_pass, n, n + 1)

        @pl.when(jnp.logical_or(same_pass, n + 1 < n_passes))
        def _():
            for cp in copies(nxt_e, nxt_n):
                cp.start()


def _expert_up_kernel(te_ref, nu_ref, nx_ref, xs_ref, w1_hbm, bg_ref, bu_ref, o_ref, stage, wgb, wub, sem,
                      *, tn, F, nf):
    n = pl.program_id(0)
    i = pl.program_id(1)

    def copies(e, nn):
        c0 = pl.multiple_of(nn * tn, tn)
        return (pltpu.make_async_copy(w1_hbm.at[e, :, pl.ds(c0, tn)], stage.at[0], sem),
                pltpu.make_async_copy(w1_hbm.at[e, :, pl.ds(F + c0, tn)], stage.at[1], sem))

    def consume():
        _cast_rows(stage.at[0], wgb)
        _cast_rows(stage.at[1], wub)

    _expert_weight_stage(te_ref, nx_ref, i, jnp.logical_and(n == 0, i == 0), n, nf, copies, consume)

    @pl.when(i < nu_ref[0])
    def _():
        x = _unpack_bf16_pairs(xs_ref[...]).astype(BF16)
        gate = jnp.minimum(_dot(x, wgb[...]) + bg_ref[...], SWIGLU_LIMIT)
        up = jnp.clip(_dot(x, wub[...]) + bu_ref[...], -SWIGLU_LIMIT, SWIGLU_LIMIT)
        act = gate * _sigmoid(SWIGLU_ALPHA * gate) * (up + 1.0)
        o_ref[...] = act.astype(o_ref.dtype)

    @pl.when(i >= nu_ref[0])
    def _():
        o_ref[...] = jnp.zeros_like(o_ref)


def _expert_down_kernel(te_ref, nu_ref, nx_ref, a_ref, w2_hbm, b2_ref, o_ref, stage, w2b, sem):
    i = pl.program_id(0)

    def copies(e, nn):
        del nn
        return (pltpu.make_async_copy(w2_hbm.at[e], stage, sem),)

    def consume():
        _cast_rows(stage, w2b)

    _expert_weight_stage(te_ref, nx_ref, i, i == 0, 0, 1, copies, consume)

    @pl.when(i < nu_ref[0])
    def _():
        y = _dot(a_ref[...], w2b[...]) + b2_ref[...]
        o_ref[...] = _pack_bf16_pairs(y)

    @pl.when(i >= nu_ref[0])
    def _():
        o_ref[...] = jnp.zeros_like(o_ref)


def _expert_ffn(xs, tile_e, n_used, next_e, w1, b1, w2, b2):
    R, W = xs.shape
    E, D, F2 = w1.shape
    F = F2 // 2
    tm = MOE_TILE
    nt = R // tm

    tn = min(512, F)
    nf = F // tn
    b1r = b1.reshape(E, 1, F2)

    act = pl.pallas_call(
        functools.partial(_expert_up_kernel, tn=tn, F=F, nf=nf),
        out_shape=jax.ShapeDtypeStruct((R, F), BF16),
        grid_spec=pltpu.PrefetchScalarGridSpec(
            num_scalar_prefetch=3,
            grid=(nf, nt),
            in_specs=[
                pl.BlockSpec((tm, W), lambda n, i, te, nu, nx: (jnp.minimum(i, nu[0] - 1), 0)),
                pl.BlockSpec(memory_space=pl.ANY),
                pl.BlockSpec((None, 1, tn), lambda n, i, te, nu, nx: (te[i], 0, n)),
                pl.BlockSpec((None, 1, tn), lambda n, i, te, nu, nx: (te[i], 0, nf + n)),
            ],
            out_specs=pl.BlockSpec((tm, tn), lambda n, i, te, nu, nx: (i, n)),
            scratch_shapes=[pltpu.VMEM((2, D, tn), F32), pltpu.VMEM((D, tn), BF16), pltpu.VMEM((D, tn), BF16),
                            pltpu.SemaphoreType.DMA],
        ),
        compiler_params=_cparams(("arbitrary", "arbitrary")),
        name="expert_up",
    )(tile_e, n_used, next_e, xs, w1, b1r, b1r)
    y = pl.pallas_call(
        _expert_down_kernel,
        out_shape=jax.ShapeDtypeStruct((R, D // 2), U32),
        grid_spec=pltpu.PrefetchScalarGridSpec(
            num_scalar_prefetch=3,
            grid=(nt,),
            in_specs=[
                pl.BlockSpec((tm, F), lambda i, te, nu, nx: (jnp.minimum(i, nu[0] - 1), 0)),
                pl.BlockSpec(memory_space=pl.ANY),
                pl.BlockSpec((None, 1, D), lambda i, te, nu, nx: (te[i], 0, 0)),
            ],
            out_specs=pl.BlockSpec((tm, D // 2), lambda i, te, nu, nx: (i, 0)),
            scratch_shapes=[pltpu.VMEM((F, D), F32), pltpu.VMEM((F, D), BF16), pltpu.SemaphoreType.DMA],
        ),
        compiler_params=_cparams(("arbitrary",)),
        name="expert_down",
    )(tile_e, n_used, next_e, act, w2, b2.reshape(E, 1, D))
    return y


def _combine_kernel(pos_ref, x_ref, w_ref, mod_ref, y_ref, o_ref, buf, sem, *, tm, T, gate_idx):
    i = pl.program_id(0)
    n = pl.num_programs(0)
    W = buf.shape[-1]
    th = tm
    ring = buf.shape[0]

    def issue_row(first_token, slot, t):
        for k in range(TOP_K):
            src = pos_ref[k * T + first_token + t]
            pltpu.make_async_copy(y_ref.at[pl.ds(src, 1)], buf.at[slot, k, pl.ds(t, 1)],
                                  sem.at[slot]).start(priority=k % 2)

    def wait_half(slot):
        for k in range(TOP_K):
            pltpu.make_async_copy(y_ref.at[pl.ds(0, th)], buf.at[slot, k], sem.at[slot]).wait()

    rc = 8
    cw = min(512, W)

    def consume(slot, next_first_token, next_slot):
        def rows_body(r, carry):
            rows = pl.ds(pl.multiple_of(r * rc, rc), rc)
            brow = rows
            wv = w_ref[rows, :]
            wk = [jnp.broadcast_to(wv[:, k:k + 1], (rc, cw)) for k in range(TOP_K)]
            for c in range(W // cw):
                lo = hi = None
                for k in range(TOP_K):
                    u = buf[slot, k, brow, c * cw:(c + 1) * cw]
                    l = wk[k] * lax.bitcast_convert_type(jnp.left_shift(u, jnp.uint32(16)), F32)
                    h = wk[k] * lax.bitcast_convert_type(jnp.bitwise_and(u, jnp.uint32(0xFFFF0000)), F32)
                    lo = l if lo is None else lo + l
                    hi = h if hi is None else hi + h
                for half, acc in ((0, lo), (1, hi)):
                    cols = slice(half * W + c * cw, half * W + (c + 1) * cw)
                    o_ref[rows, cols] = x_ref[rows, cols] + mod_ref[gate_idx:gate_idx + 1, cols] * acc
            for t in range(rc):
                issue_row(next_first_token, next_slot, r * rc + t)
            return carry

        lax.fori_loop(0, th // rc, rows_body, 0)

    @pl.when(i == 0)
    def _():
        for b in range(2):
            lax.fori_loop(0, tm, lambda t, c: (issue_row(b * tm, b, t), c)[1], 0, unroll=8)

    wait_half(i % ring)
    consume(i % ring, jnp.minimum((i + 2) * tm, T - tm), (i + 2) % ring)

    @pl.when(i == n - 1)
    def _():
        wait_half((i + 1) % ring)
        wait_half((i + 2) % ring)


def _combine(pos_flat, x1, w_tk, mod, y, S, gate_idx):
    T, D = x1.shape
    tm = min(128, S)
    assert T // tm >= 2
    grid_spec = pltpu.PrefetchScalarGridSpec(
        num_scalar_prefetch=1,
        grid=(T // tm,),
        in_specs=[
            pl.BlockSpec((tm, D), lambda i, p: (i, 0)),
            pl.BlockSpec((tm, TOP_K), lambda i, p: (i, 0)),
            pl.BlockSpec((None, N_ADA, D), lambda i, p: ((i * tm) // S, 0, 0)),
            pl.BlockSpec(memory_space=pl.ANY),
        ],
        out_specs=pl.BlockSpec((tm, D), lambda i, p: (i, 0)),
        scratch_shapes=[pltpu.VMEM((3, TOP_K, tm, D // 2), U32), pltpu.SemaphoreType.DMA((3,))],
    )
    return pl.pallas_call(
        functools.partial(_combine_kernel, tm=tm, T=T, gate_idx=gate_idx),
        out_shape=jax.ShapeDtypeStruct((T, D), F32),
        grid_spec=grid_spec,
        compiler_params=_cparams(("arbitrary",)),
        name="combine",
    )(pos_flat, x1, w_tk, mod, y)


def _layer(x2, mod, B, S, rel_bias, norm_mix_w, w_in, q_norm_w, k_norm_w, attn_sink, ret_decay_fwd, ret_decay_bwd,
           ret_gn_w, ret_gn_b, w_up_attn, w_up_ret, w_out, norm_ffn_w, router_w, router_b,
           expert_w1, expert_b1, expert_w2, expert_b2):
    T, D = x2.shape
    E = router_w.shape[1]
    h = _norm_mod(x2, norm_mix_w, mod, S, 0, 1)
    proj = _in_proj(h, w_in, q_norm_w, k_norm_w)
    attn = _window_attention(proj, _attn_bias_table(rel_bias), attn_sink, B, S)
    ret = _retention(proj, ret_decay_fwd, ret_decay_bwd, ret_gn_w, ret_gn_b, B, S)
    merged = _merge_up(attn, ret, w_up_attn, w_up_ret, proj, D)
    x1 = _out_proj(merged, w_out, x2, mod, S, 2)

    hp, top_e, top_w, rank, cnt = _norm_router(x1, norm_ffn_w, mod, router_w, router_b, S, 3, 4)
    tm = MOE_TILE
    counts = cnt[:, 0].astype(I32)
    tiles_per_e = (counts + tm - 1) // tm
    tile_end = jnp.cumsum(tiles_per_e)
    group_start = (tile_end - tiles_per_e) * tm
    n_tiles = (T * TOP_K) // tm + E
    n_used = tile_end[-1]
    tile_ids = jnp.minimum(jnp.arange(n_tiles, dtype=I32), n_used - 1)
    tile_e = jnp.minimum(jnp.sum((tile_ids[:, None] >= tile_end[None, :]).astype(I32), axis=1), E - 1)
    e_ids = jnp.arange(E, dtype=I32)
    later = jnp.logical_and(e_ids[None, :] > e_ids[:, None], tiles_per_e[None, :] > 0)
    next_of_e = jnp.min(jnp.where(later, e_ids[None, :], E), axis=1)
    next_of_e = jnp.where(next_of_e == E, -1, next_of_e)
    next_e = jnp.sum(jnp.where(tile_e[:, None] == e_ids, next_of_e, 0), axis=1).astype(I32)
    start_of = jnp.sum(jnp.where(top_e[:, :, None] == e_ids, group_start, 0), axis=-1)
    pos_flat = (start_of + rank).astype(I32).reshape(-1)
    pad_info = jnp.concatenate([group_start + counts, tiles_per_e * tm - counts, n_used.reshape(1)]).astype(I32)

    xs = _dispatch(pos_flat, pad_info, hp, n_tiles, E)
    y = _expert_ffn(xs, tile_e, n_used.reshape(1).astype(I32), next_e, expert_w1, expert_b1, expert_w2, expert_b2)
    return _combine(pos_flat, x1, top_w.T, mod, y, S, 5)


def kernel(x, c, rel_bias, ada_w, ada_b, norm_mix_w, w_in, q_norm_w, k_norm_w, attn_sink, ret_decay_fwd,
           ret_decay_bwd, ret_gn_w, ret_gn_b, w_up_attn, w_up_ret, w_out, norm_ffn_w, router_w, router_b,
           expert_w1, expert_b1, expert_w2, expert_b2):
    B, S, D = x.shape
    x2 = x.reshape(B * S, D)
    for l in range(ada_w.shape[0]):
        mod = _ada_mod(c, ada_w[l], ada_b[l])
        x2 = _layer(x2, mod, B, S, rel_bias, norm_mix_w[l], w_in[l], q_norm_w[l], k_norm_w[l], attn_sink[l],
                    ret_decay_fwd[l], ret_decay_bwd[l], ret_gn_w[l], ret_gn_b[l], w_up_attn[l], w_up_ret[l],
                    w_out[l], norm_ffn_w[l], router_w[l], router_b[l], expert_w1[l], expert_b1[l],
                    expert_w2[l], expert_b2[l])
    return x2.reshape(B, S, D)
```

```python
import functools
import math

import jax
import jax.numpy as jnp
from jax import lax
from jax.experimental import pallas as pl
from jax.experimental.pallas import tpu as pltpu

F32 = jnp.float32
BF16 = jnp.bfloat16
U32 = jnp.uint32
I32 = jnp.int32

ATTN_HEADS = 16
ATTN_KV_HEADS = 4
ATTN_HEAD_DIM = 128
WINDOW = 128
ATTN_BLOCK = 128
N_BUCKETS = 32
MAX_DISTANCE = 128
RET_HEADS = 8
RET_QK_DIM = 128
RET_V_DIM = 256
RET_CHUNK = 128
ROPE_BASE = 10000.0
TOP_K = 4
SWIGLU_LIMIT = 7.0
SWIGLU_ALPHA = 1.702
N_ADA = 6
EPS = 1e-6
NEG_INF = -1e30
NEG_BIG = -3.0e38
LOG2E = 1.4426950408889634

ATTN_Q_W = ATTN_HEADS * ATTN_HEAD_DIM
ATTN_KV_W = ATTN_KV_HEADS * ATTN_HEAD_DIM
RET_QK_W = RET_HEADS * RET_QK_DIM
RET_V_W = RET_HEADS * RET_V_DIM

VMEM_LIMIT_BYTES = 56 * 1024 * 1024
MOE_TILE = 512
RET_TILE = 256


def _cparams(sem):
    return pltpu.CompilerParams(dimension_semantics=sem, vmem_limit_bytes=VMEM_LIMIT_BYTES)


def _dot(a, b):
    return jnp.dot(a, b, preferred_element_type=F32)


def _dot_nt(a, b):
    return lax.dot_general(a, b, (((1,), (1,)), ((), ())), preferred_element_type=F32)


def _dot_tn(a, b):
    return lax.dot_general(a, b, (((0,), (0,)), ((), ())), preferred_element_type=F32)


def _sigmoid(x):
    return 1.0 / (1.0 + jnp.exp(-x))


def _ada_kernel(c_ref, w_ref, b_ref, o_ref):
    c = c_ref[...]
    cs = (c * _sigmoid(c)).astype(BF16)
    o_ref[...] = _dot(cs, w_ref[...].astype(BF16)) + b_ref[...]


def _ada_mod(c, ada_w, ada_b):
    B, D = c.shape
    N = ada_w.shape[1]
    rows = 8
    cp = jnp.zeros((rows, D), F32).at[:B].set(c)
    tn = min(1024, N)
    out = pl.pallas_call(
        _ada_kernel,
        out_shape=jax.ShapeDtypeStruct((rows, N), F32),
        grid=(N // tn,),
        in_specs=[
            pl.BlockSpec((rows, D), lambda j: (0, 0)),
            pl.BlockSpec((D, tn), lambda j: (0, j)),
            pl.BlockSpec((1, tn), lambda j: (0, j)),
        ],
        out_specs=pl.BlockSpec((rows, tn), lambda j: (0, j)),
        compiler_params=_cparams(("arbitrary",)),
        name="ada_mod",
    )(cp, ada_w, ada_b.reshape(1, N))
    return out[:B].reshape(B, N_ADA, D)


def _norm_mod_value(x, nw, shift, scale):
    ms = jnp.mean(x * x, axis=-1, keepdims=True)
    h = x * lax.rsqrt(ms + EPS) * nw
    return h * (1.0 + scale) + shift


def _norm_mod_kernel(x_ref, nw_ref, mod_ref, o_ref, *, shift_idx, scale_idx):
    h = _norm_mod_value(x_ref[...], nw_ref[...], mod_ref[shift_idx:shift_idx + 1, :],
                        mod_ref[scale_idx:scale_idx + 1, :])
    o_ref[...] = h.astype(BF16)


def _norm_mod(x2, nw, mod, S, shift_idx, scale_idx):
    T, D = x2.shape
    tm = min(512, S)
    return pl.pallas_call(
        functools.partial(_norm_mod_kernel, shift_idx=shift_idx, scale_idx=scale_idx),
        out_shape=jax.ShapeDtypeStruct((T, D), BF16),
        grid=(T // tm,),
        in_specs=[
            pl.BlockSpec((tm, D), lambda i: (i, 0)),
            pl.BlockSpec((1, D), lambda i: (0, 0)),
            pl.BlockSpec((None, N_ADA, D), lambda i: ((i * tm) // S, 0, 0)),
        ],
        out_specs=pl.BlockSpec((tm, D), lambda i: (i, 0)),
        compiler_params=_cparams(("arbitrary",)),
        name="norm_mod",
    )(x2, nw.reshape(1, D), mod)


def _stream_weight_tile(w_hbm, stage, wb, sem):
    j = pl.program_id(0)
    tn = wb.shape[1]

    def w_copy(jj):
        return pltpu.make_async_copy(w_hbm.at[:, pl.ds(pl.multiple_of(jj * tn, tn), tn)], stage, sem)

    @pl.when(pl.program_id(1) == 0)
    def _():
        @pl.when(j == 0)
        def _():
            w_copy(0).start()

        w_copy(j).wait()
        _cast_rows(stage, wb)

        @pl.when(j + 1 < pl.num_programs(0))
        def _():
            w_copy(j + 1).start()


def _in_proj_kernel(a_ref, w_hbm, cw_ref, cf_ref, bd_ref, o_ref, stage, wb, sem, *, n_norm_tiles):
    j = pl.program_id(0)
    tn = wb.shape[1]
    _stream_weight_tile(w_hbm, stage, wb, sem)

    @pl.when(j < n_norm_tiles)
    def _():
        cb = bd_ref.shape[0]
        for c in range(tn // cb):
            cols = slice(c * cb, (c + 1) * cb)
            acc = _dot(a_ref[...], wb[:, cols])
            ms = _dot((acc * acc).astype(BF16), bd_ref[...]) * (1.0 / ATTN_HEAD_DIM)
            inv = jnp.where(cf_ref[:, cols] > 0.0, lax.rsqrt(ms + EPS), 1.0)
            o_ref[:, cols] = (acc * inv * cw_ref[:, cols]).astype(o_ref.dtype)

    @pl.when(j >= n_norm_tiles)
    def _():
        o_ref[...] = _dot(a_ref[...], wb[...]).astype(o_ref.dtype)


def _in_proj(h, w, q_norm_w, k_norm_w):
    T, K = h.shape
    N = w.shape[1]
    tm = min(1024, T)
    tn = 1024 if N % 1024 == 0 else 512
    hd = ATTN_HEAD_DIM
    n_norm = ATTN_Q_W + ATTN_KV_W
    n_norm_tiles = -(-n_norm // tn)
    pad = n_norm_tiles * tn - n_norm
    col_w = jnp.concatenate([jnp.tile(q_norm_w.astype(F32) * (hd ** -0.5 * LOG2E), ATTN_HEADS),
                             jnp.tile(k_norm_w.astype(F32), ATTN_KV_HEADS), jnp.ones((pad,), F32)]).reshape(1, -1)
    col_flag = jnp.concatenate([jnp.ones((n_norm,), F32), jnp.zeros((pad,), F32)]).reshape(1, -1)
    cb = min(512, tn)
    lane_head = jnp.arange(cb, dtype=I32) // hd
    block_ones = (lane_head[:, None] == lane_head[None, :]).astype(BF16)
    last = n_norm_tiles - 1
    return pl.pallas_call(
        functools.partial(_in_proj_kernel, n_norm_tiles=n_norm_tiles),
        out_shape=jax.ShapeDtypeStruct((T, N), BF16),
        grid=(N // tn, T // tm),
        in_specs=[
            pl.BlockSpec((tm, K), lambda j, i: (i, 0)),
            pl.BlockSpec(memory_space=pl.ANY),
            pl.BlockSpec((1, tn), lambda j, i: (0, jnp.minimum(j, last))),
            pl.BlockSpec((1, tn), lambda j, i: (0, jnp.minimum(j, last))),
            pl.BlockSpec((cb, cb), lambda j, i: (0, 0)),
        ],
        out_specs=pl.BlockSpec((tm, tn), lambda j, i: (i, j)),
        scratch_shapes=[pltpu.VMEM((K, tn), F32), pltpu.VMEM((K, tn), BF16), pltpu.SemaphoreType.DMA],
        compiler_params=_cparams(("arbitrary", "arbitrary")),
        name="in_proj",
    )(h, w, col_w, col_flag, block_ones)


def _t5_bucket(rel):
    nb = N_BUCKETS // 2
    max_exact = nb // 2
    base = jnp.where(rel > 0, nb, 0)
    n = jnp.abs(rel)
    nf = jnp.maximum(n, 1).astype(F32)
    large = max_exact + (jnp.log(nf / max_exact) / math.log(MAX_DISTANCE / max_exact) * (nb - max_exact)).astype(I32)
    large = jnp.minimum(large, nb - 1)
    return base + jnp.where(n < max_exact, n, large)


def _attn_bias_table(rel_bias):
    blk = ATTN_BLOCK
    qi = jnp.arange(blk, dtype=I32)[:, None]
    kj = jnp.arange(3 * blk, dtype=I32)[None, :]
    rel = kj - blk - qi
    bucket = _t5_bucket(rel)
    table = rel_bias.astype(F32).T
    bias = jnp.zeros((table.shape[0],) + bucket.shape, F32)
    for b in range(N_BUCKETS):
        bias = jnp.where(bucket[None] == b, table[:, b][:, None, None], bias)
    bias = jnp.where((jnp.abs(rel) <= WINDOW)[None], bias * LOG2E, NEG_INF)
    G = ATTN_HEADS // ATTN_KV_HEADS
    bias = bias.reshape(ATTN_KV_HEADS, G, blk, 3 * blk)
    bias = jnp.transpose(bias, (0, 3, 1, 2)).reshape(ATTN_KV_HEADS, 3 * blk, G * blk)
    key = jnp.arange(3 * blk, dtype=I32)[None, :, None]
    no_prev, no_next = key < blk, key >= 2 * blk
    return jnp.stack([bias, jnp.where(no_prev, NEG_INF, bias), jnp.where(no_next, NEG_INF, bias),
                      jnp.where(jnp.logical_or(no_prev, no_next), NEG_INF, bias)])


def _attn_kernel(sink_ref, q_ref, kp_ref, kc_ref, kn_ref, vp_ref, vc_ref, vn_ref, bias_ref, o_ref):
    blk, hd = ATTN_BLOCK, ATTN_HEAD_DIM
    G = ATTN_HEADS // ATTN_KV_HEADS
    for hk in range(ATTN_KV_HEADS):
        sl = slice(hk * hd, (hk + 1) * hd)
        k3 = jnp.concatenate([kp_ref[:, sl], kc_ref[:, sl], kn_ref[:, sl]], axis=0)
        v3 = jnp.concatenate([vp_ref[:, sl], vc_ref[:, sl], vn_ref[:, sl]], axis=0)
        q4 = jnp.concatenate([q_ref[:, (hk * G + g) * hd:(hk * G + g + 1) * hd] for g in range(G)],
                             axis=0)
        sink = jnp.concatenate([jnp.full((1, blk), sink_ref[hk * G + g], F32) for g in range(G)], axis=1)
        logits = _dot_nt(k3, q4) + bias_ref[hk]
        m = jnp.maximum(jnp.max(logits, axis=0, keepdims=True), sink)
        p = jnp.exp2(logits - m)
        denom = jnp.sum(p, axis=0, keepdims=True) + jnp.exp2(sink - m)
        out_t = _dot_tn(v3, p.astype(BF16)) * (1.0 / denom)
        out = out_t.T
        for g in range(G):
            h = hk * G + g
            o_ref[:, h * hd:(h + 1) * hd] = out[g * blk:(g + 1) * blk].astype(o_ref.dtype)


def _window_attention(proj, bias_tab, sink, B, S):
    T = proj.shape[0]
    blk = ATTN_BLOCK
    nb = S // blk
    kcol = ATTN_Q_W // ATTN_KV_W
    vcol = kcol + 1

    def kv_spec(col, off):
        return pl.BlockSpec((blk, ATTN_KV_W), lambda b, n, s: (b * nb + jnp.clip(n + off, 0, nb - 1), col))

    grid_spec = pltpu.PrefetchScalarGridSpec(
        num_scalar_prefetch=1,
        grid=(B, nb),
        in_specs=[
            pl.BlockSpec((blk, ATTN_Q_W), lambda b, n, s: (b * nb + n, 0)),
            kv_spec(kcol, -1), kv_spec(kcol, 0), kv_spec(kcol, 1),
            kv_spec(vcol, -1), kv_spec(vcol, 0), kv_spec(vcol, 1),
            pl.BlockSpec((None, ATTN_KV_HEADS, 3 * blk, (ATTN_HEADS // ATTN_KV_HEADS) * blk),
                         lambda b, n, s: ((n == 0).astype(I32) + 2 * (n == nb - 1).astype(I32), 0, 0, 0)),
        ],
        out_specs=pl.BlockSpec((blk, ATTN_Q_W), lambda b, n, s: (b * nb + n, 0)),
    )
    return pl.pallas_call(
        _attn_kernel,
        out_shape=jax.ShapeDtypeStruct((T, ATTN_Q_W), BF16),
        grid_spec=grid_spec,
        compiler_params=_cparams(("arbitrary", "arbitrary")),
        name="window_attn",
    )(sink.astype(F32) * LOG2E, proj, proj, proj, proj, proj, proj, proj, bias_tab)


def _rope_tables(S):
    d = RET_QK_DIM
    inv = ROPE_BASE ** (-jnp.arange(0, d, 2, dtype=F32) / d)
    ang = jnp.arange(S, dtype=F32)[:, None] * inv[None, :]
    cos, sin = jnp.cos(ang), jnp.sin(ang)
    return jnp.concatenate([cos, cos], axis=-1), jnp.concatenate([-sin, sin], axis=-1)


def _ret_kernel(df_ref, db_ref, q_ref, k_ref, v_ref, g_ref, cos_ref, sin_ref, gw_ref, gb_ref, o_ref,
                qb, kb, q2, kv, sprev, *, S, C):
    h = pl.program_id(1)
    nc = S // C
    dk = RET_QK_DIM
    half = dk // 2

    rowf = lax.broadcasted_iota(I32, (C, 1), 0).astype(F32)
    lg_f = -jnp.exp(jnp.full((1, 1), df_ref[h], F32))
    lg_b = -jnp.exp(jnp.full((1, 1), db_ref[h], F32))
    qdec_f = jnp.exp((rowf + 1.0) * lg_f)
    kdec_f = jnp.exp((C - 1.0 - rowf) * lg_f)
    qdec_b = jnp.exp((C - rowf) * lg_b)
    kdec_b = jnp.exp(rowf * lg_b)
    cd_f = jnp.exp(C * lg_f)
    cd_b = jnp.exp(C * lg_b)

    def a_body(n, carry):
        rows = pl.ds(pl.multiple_of(n * C, C), C)
        co = cos_ref[rows, :]
        si = sin_ref[rows, :]
        q = q_ref[rows, :].astype(F32)
        k = k_ref[rows, :].astype(F32)
        qr = q * co + pltpu.roll(q, half, 1) * si
        kr = (k * co + pltpu.roll(k, half, 1) * si) * (dk ** -0.5)
        qb[rows, :] = qr.astype(BF16)
        kb[rows, :] = kr.astype(BF16)
        q2[rows, :] = jnp.concatenate([qr * qdec_f, qr * qdec_b], axis=1).astype(BF16)
        k2 = jnp.concatenate([kr * kdec_f, kr * kdec_b], axis=1).astype(BF16)
        kv[n] = _dot_tn(k2, v_ref[rows, :])
        return carry

    lax.fori_loop(0, nc, a_body, 0, unroll=4)

    def scan_f(n, state):
        sprev[n, 0:dk, :] = state.astype(BF16)
        return state * cd_f + kv[n, 0:dk, :]

    lax.fori_loop(0, nc, scan_f, jnp.zeros((dk, RET_V_DIM), F32))

    def scan_b(t, state):
        n = nc - 1 - t
        sprev[n, dk:2 * dk, :] = state.astype(BF16)
        return state * cd_b + kv[n, dk:2 * dk, :]

    lax.fori_loop(0, nc, scan_b, jnp.zeros((dk, RET_V_DIM), F32))

    ri = lax.broadcasted_iota(I32, (C, C), 0)
    ci = lax.broadcasted_iota(I32, (C, C), 1)
    d = (ri - ci).astype(F32)
    dec = jnp.where(ri >= ci, jnp.exp(jnp.maximum(d, 0.0) * lg_f), jnp.exp(jnp.maximum(-d, 0.0) * lg_b))
    gw = gw_ref[...]
    gb = gb_ref[...]

    def c_body(n, carry):
        rows = pl.ds(pl.multiple_of(n * C, C), C)
        scores = _dot_nt(qb[rows, :], kb[rows, :]) * dec
        y = _dot(scores.astype(BF16), v_ref[rows, :]) + _dot(q2[rows, :], sprev[n])
        mu = jnp.mean(y, axis=-1, keepdims=True)
        yc = y - mu
        var = jnp.mean(yc * yc, axis=-1, keepdims=True)
        yn = yc * lax.rsqrt(var + EPS) * gw + gb
        g = g_ref[rows, :].astype(F32)
        o_ref[rows, :] = (g * _sigmoid(g) * yn).astype(o_ref.dtype)
        return carry

    lax.fori_loop(0, nc, c_body, 0, unroll=4)


def _retention(proj, decay_fwd, decay_bwd, gn_w, gn_b, B, S):
    T = proj.shape[0]
    dk, dv = RET_QK_DIM, RET_V_DIM
    q_off = (ATTN_Q_W + 2 * ATTN_KV_W) // dk
    k_off = q_off + RET_QK_W // dk
    v_off = (ATTN_Q_W + 2 * ATTN_KV_W + 2 * RET_QK_W) // dv
    g_off = v_off + RET_V_W // dv
    cos, sin = _rope_tables(S)
    C = min(RET_TILE, S)
    grid_spec = pltpu.PrefetchScalarGridSpec(
        num_scalar_prefetch=2,
        grid=(B, RET_HEADS),
        in_specs=[
            pl.BlockSpec((S, dk), lambda b, h, *_: (b, q_off + h)),
            pl.BlockSpec((S, dk), lambda b, h, *_: (b, k_off + h)),
            pl.BlockSpec((S, dv), lambda b, h, *_: (b, v_off + h)),
            pl.BlockSpec((S, dv), lambda b, h, *_: (b, g_off + h)),
            pl.BlockSpec((S, dk), lambda b, h, *_: (0, 0)),
            pl.BlockSpec((S, dk), lambda b, h, *_: (0, 0)),
            pl.BlockSpec((1, dv), lambda b, h, *_: (0, h)),
            pl.BlockSpec((1, dv), lambda b, h, *_: (0, h)),
        ],
        out_specs=pl.BlockSpec((S, dv), lambda b, h, *_: (b, h)),
        scratch_shapes=[
            pltpu.VMEM((S, dk), BF16),
            pltpu.VMEM((S, dk), BF16),
            pltpu.VMEM((S, 2 * dk), BF16),
            pltpu.VMEM((S // C, 2 * dk, dv), F32),
            pltpu.VMEM((S // C, 2 * dk, dv), BF16),
        ],
    )
    return pl.pallas_call(
        functools.partial(_ret_kernel, S=S, C=C),
        out_shape=jax.ShapeDtypeStruct((T, RET_V_W), BF16),
        grid_spec=grid_spec,
        compiler_params=_cparams(("arbitrary", "arbitrary")),
        name="retention",
    )(decay_fwd.astype(F32), decay_bwd.astype(F32), proj, proj, proj, proj, cos, sin,
      gn_w.reshape(1, -1), gn_b.reshape(1, -1))


def _merge_kernel(a_ref, r_ref, wa_hbm, wr_hbm, ga_ref, gr_ref, o_ref, sta, stb, wab, wrb, sem):
    _stream_weight_tile(wa_hbm, sta, wab, sem.at[0])
    _stream_weight_tile(wr_hbm, stb, wrb, sem.at[1])
    ua = _dot(a_ref[...], wab[...])
    ur = _dot(r_ref[...], wrb[...])
    out = _sigmoid(ga_ref[...].astype(F32)) * ua + _sigmoid(gr_ref[...].astype(F32)) * ur
    o_ref[...] = out.astype(o_ref.dtype)


def _merge_up(attn, ret, wa, wr, proj, D):
    T = attn.shape[0]
    tm = min(512, T)
    tn = min(1024, D)
    ga_off = (ATTN_Q_W + 2 * ATTN_KV_W + 2 * RET_QK_W + 2 * RET_V_W) // tn
    gr_off = ga_off + D // tn
    return pl.pallas_call(
        _merge_kernel,
        out_shape=jax.ShapeDtypeStruct((T, D), BF16),
        grid=(D // tn, T // tm),
        in_specs=[
            pl.BlockSpec((tm, ATTN_Q_W), lambda j, i: (i, 0)),
            pl.BlockSpec((tm, RET_V_W), lambda j, i: (i, 0)),
            pl.BlockSpec(memory_space=pl.ANY),
            pl.BlockSpec(memory_space=pl.ANY),
            pl.BlockSpec((tm, tn), lambda j, i: (i, ga_off + j)),
            pl.BlockSpec((tm, tn), lambda j, i: (i, gr_off + j)),
        ],
        out_specs=pl.BlockSpec((tm, tn), lambda j, i: (i, j)),
        scratch_shapes=[pltpu.VMEM((ATTN_Q_W, tn), F32), pltpu.VMEM((RET_V_W, tn), F32),
                        pltpu.VMEM((ATTN_Q_W, tn), BF16), pltpu.VMEM((RET_V_W, tn), BF16),
                        pltpu.SemaphoreType.DMA((2,))],
        compiler_params=_cparams(("arbitrary", "arbitrary")),
        name="merge_up",
    )(attn, ret, wa, wr, proj, proj)


def _out_proj_kernel(m_ref, w_hbm, x_ref, mod_ref, o_ref, stage, wb, sem, *, gate_idx):
    _stream_weight_tile(w_hbm, stage, wb, sem)
    y = _dot(m_ref[...], wb[...])
    o_ref[...] = x_ref[...] + mod_ref[gate_idx:gate_idx + 1, :] * y


def _out_proj(merged, w, x2, mod, S, gate_idx):
    T, D = x2.shape
    tm = min(512, S)
    tn = min(1024, D)
    return pl.pallas_call(
        functools.partial(_out_proj_kernel, gate_idx=gate_idx),
        out_shape=jax.ShapeDtypeStruct((T, D), F32),
        grid=(D // tn, T // tm),
        in_specs=[
            pl.BlockSpec((tm, D), lambda j, i: (i, 0)),
            pl.BlockSpec(memory_space=pl.ANY),
            pl.BlockSpec((tm, tn), lambda j, i: (i, j)),
            pl.BlockSpec((None, N_ADA, tn), lambda j, i: ((i * tm) // S, 0, j)),
        ],
        out_specs=pl.BlockSpec((tm, tn), lambda j, i: (i, j)),
        scratch_shapes=[pltpu.VMEM((D, tn), F32), pltpu.VMEM((D, tn), BF16), pltpu.SemaphoreType.DMA],
        compiler_params=_cparams(("arbitrary", "arbitrary")),
        name="out_proj",
    )(merged, w, x2, mod)


def _pack_bf16_pairs(v):
    n = v.shape[1] // 2
    bits = lax.bitcast_convert_type(v.astype(BF16).astype(F32), U32)
    return jnp.bitwise_or(jnp.bitwise_and(bits[:, n:], jnp.uint32(0xFFFF0000)),
                          jnp.right_shift(bits[:, :n], jnp.uint32(16)))


def _unpack_bf16_pairs(w):
    lo = lax.bitcast_convert_type(jnp.left_shift(w, jnp.uint32(16)), F32)
    hi = lax.bitcast_convert_type(jnp.bitwise_and(w, jnp.uint32(0xFFFF0000)), F32)
    return jnp.concatenate([lo, hi], axis=1)


def _norm_router_kernel(x_ref, nw_ref, mod_ref, rwt_ref, rb_ref, hp_ref, e_ref, w_ref, r_ref, cnt_ref, carry,
                        *, shift_idx, scale_idx):
    i = pl.program_id(0)

    @pl.when(i == 0)
    def _():
        carry[...] = jnp.zeros_like(carry)

    h = _norm_mod_value(x_ref[...], nw_ref[...], mod_ref[shift_idx:shift_idx + 1, :],
                        mod_ref[scale_idx:scale_idx + 1, :])
    hp_ref[...] = _pack_bf16_pairs(h)
    hb = h.astype(BF16)
    tm = hb.shape[0]
    E = rwt_ref.shape[0]
    logits = _dot_nt(rwt_ref[...], hb) + rb_ref[...]
    iota_e = lax.broadcasted_iota(I32, (E, tm), 0)
    vals, idxs = [], []
    work = logits
    sel = jnp.zeros((E, tm), F32)
    for _k in range(TOP_K):
        m = jnp.max(work, axis=0, keepdims=True)
        idx = jnp.min(jnp.where(work == m, iota_e, E), axis=0, keepdims=True)
        hit = iota_e == idx
        vals.append(m)
        idxs.append(idx)
        work = jnp.where(hit, NEG_BIG, work)
        sel = sel + hit.astype(F32)
    ex = [jnp.exp(v - vals[0]) for v in vals]
    tot = ex[0]
    for v in ex[1:]:
        tot = tot + v
    ri = lax.broadcasted_iota(I32, (tm, tm), 0)
    ci = lax.broadcasted_iota(I32, (tm, tm), 1)
    upper = (ri < ci).astype(BF16)
    before = _dot(sel.astype(BF16), upper) + carry[:, 0:1]
    for k in range(TOP_K):
        e_ref[k:k + 1, :] = idxs[k]
        w_ref[k:k + 1, :] = ex[k] / tot
        r_ref[k:k + 1, :] = jnp.sum(jnp.where(iota_e == idxs[k], before, 0.0), axis=0, keepdims=True).astype(I32)
    carry[...] = carry[...] + jnp.sum(sel, axis=1, keepdims=True)
    cnt_ref[...] = carry[...]


def _norm_router(x1, nw, mod, router_w, router_b, S, shift_idx, scale_idx):
    T, D = x1.shape
    E = router_w.shape[1]
    tm = min(256, S)
    outs = pl.pallas_call(
        functools.partial(_norm_router_kernel, shift_idx=shift_idx, scale_idx=scale_idx),
        out_shape=(
            jax.ShapeDtypeStruct((T, D // 2), U32),
            jax.ShapeDtypeStruct((TOP_K, T), I32),
            jax.ShapeDtypeStruct((TOP_K, T), F32),
            jax.ShapeDtypeStruct((TOP_K, T), I32),
            jax.ShapeDtypeStruct((E, 128), F32),
        ),
        grid=(T // tm,),
        in_specs=[
            pl.BlockSpec((tm, D), lambda i: (i, 0)),
            pl.BlockSpec((1, D), lambda i: (0, 0)),
            pl.BlockSpec((None, N_ADA, D), lambda i: ((i * tm) // S, 0, 0)),
            pl.BlockSpec((E, D), lambda i: (0, 0)),
            pl.BlockSpec((E, 1), lambda i: (0, 0)),
        ],
        out_specs=(
            pl.BlockSpec((tm, D // 2), lambda i: (i, 0)),
            pl.BlockSpec((TOP_K, tm), lambda i: (0, i)),
            pl.BlockSpec((TOP_K, tm), lambda i: (0, i)),
            pl.BlockSpec((TOP_K, tm), lambda i: (0, i)),
            pl.BlockSpec((E, 128), lambda i: (0, 0)),
        ),
        scratch_shapes=[pltpu.VMEM((E, 128), F32)],
        compiler_params=_cparams(("arbitrary",)),
        name="norm_router",
    )(x1, nw.reshape(1, D), mod, router_w.T.astype(BF16), router_b.reshape(E, 1).astype(F32))
    return outs


def _dispatch_kernel(pos_ref, pad_ref, hp_ref, xs_ref, zeros, sem, zsem, *, tm, T, E, n_tiles):
    base = pl.program_id(0) * tm

    @pl.when(pl.program_id(0) == 0)
    def _():
        zeros[...] = jnp.zeros_like(zeros)
        def pad_copy(row):
            return pltpu.make_async_copy(zeros.at[pl.ds(0, 1)], xs_ref.at[pl.ds(row, 1)], zsem)

        def tail_copy(t):
            return pltpu.make_async_copy(
                zeros, xs_ref.at[pl.ds(pl.multiple_of(t * MOE_TILE, MOE_TILE), MOE_TILE)], zsem)

        def pad8_copy(row):
            return pltpu.make_async_copy(zeros.at[pl.ds(0, 8)], xs_ref.at[pl.ds(pl.multiple_of(row, 8), 8)], zsem)

        def pad_pieces(e):
            start, count = pad_ref[e], pad_ref[E + e]
            head = jnp.minimum(count, jnp.bitwise_and(-start, 7))
            return start, head, start + head, (count - head) // 8

        def pad_start(e, carry):
            start, head, body, n8 = pad_pieces(e)
            lax.fori_loop(0, head, lambda r, c: (pad_copy(start + r).start(), c)[1], 0)
            lax.fori_loop(0, n8, lambda r, c: (pad8_copy(body + 8 * r).start(), c)[1], 0)
            return carry

        def pad_wait(e, carry):
            start, head, body, n8 = pad_pieces(e)
            lax.fori_loop(0, head, lambda r, c: (pad_copy(start + r).wait(), c)[1], 0)
            lax.fori_loop(0, n8, lambda r, c: (pad8_copy(body + 8 * r).wait(), c)[1], 0)
            return carry

        def tail_start(t, carry):
            tail_copy(t).start()
            return carry

        def tail_wait(t, carry):
            tail_copy(t).wait()
            return carry

        lax.fori_loop(0, E, pad_start, 0)
        lax.fori_loop(pad_ref[2 * E], n_tiles, tail_start, 0)
        lax.fori_loop(0, E, pad_wait, 0)
        lax.fori_loop(pad_ref[2 * E], n_tiles, tail_wait, 0)

    def start_body(t, carry):
        for k in range(TOP_K):
            dst = pos_ref[k * T + base + t]
            pltpu.make_async_copy(hp_ref.at[pl.ds(t, 1)], xs_ref.at[pl.ds(dst, 1)], sem).start(priority=k % 2)
        return carry

    lax.fori_loop(0, tm, start_body, 0, unroll=8)
    for k in range(TOP_K):
        pltpu.make_async_copy(hp_ref, xs_ref.at[pl.ds(0, tm)], sem).wait()


def _dispatch(pos_flat, pad_info, hp, n_tiles, E):
    T, W = hp.shape
    tm = min(512, T)
    grid_spec = pltpu.PrefetchScalarGridSpec(
        num_scalar_prefetch=2,
        grid=(T // tm,),
        in_specs=[pl.BlockSpec((tm, W), lambda i, p, q: (i, 0))],
        out_specs=pl.BlockSpec(memory_space=pl.ANY),
        scratch_shapes=[pltpu.VMEM((MOE_TILE, W), U32), pltpu.SemaphoreType.DMA, pltpu.SemaphoreType.DMA],
    )
    return pl.pallas_call(
        functools.partial(_dispatch_kernel, tm=tm, T=T, E=E, n_tiles=n_tiles),
        out_shape=jax.ShapeDtypeStruct((n_tiles * MOE_TILE, W), U32),
        grid_spec=grid_spec,
        compiler_params=_cparams(("arbitrary",)),
        name="dispatch",
    )(pos_flat, pad_info, hp)


def _expert_changed(te_ref, i):
    return jnp.logical_or(i == 0, te_ref[i] != te_ref[jnp.maximum(i - 1, 0)])


def _cast_rows(src_ref, dst_ref, rows_per_pass=256):
    def body(r, carry):
        rows = pl.ds(pl.multiple_of(r * rows_per_pass, rows_per_pass), rows_per_pass)
        dst_ref[rows, :] = src_ref[rows, :].astype(dst_ref.dtype)
        return carry

    lax.fori_loop(0, src_ref.shape[0] // rows_per_pass, body, 0)


def _expert_weight_stage(te_ref, nx_ref, i, first_step, n, n_passes, copies, consume):
    @pl.when(_expert_changed(te_ref, i))
    def _():
        @pl.when(first_step)
        def _():
            for cp in copies(te_ref[0], 0):
                cp.start()

        for cp in copies(te_ref[i], n):
            cp.wait()
        consume()
        same_pass = nx_ref[i] >= 0
        nxt_e = jnp.where(same_pass, nx_ref[i], te_ref[0])
        nxt_n = jnp.where(same_pass, n, n + 1)

        @pl.when(jnp.logical_or(same_pass, n + 1 < n_passes))
        def _():
            for cp in copies(nxt_e, nxt_n):
                cp.start()


def _expert_up_kernel(te_ref, nu_ref, nx_ref, xs_ref, w1_hbm, bg_ref, bu_ref, o_ref, stage, wgb, wub, sem,
                      *, tn, F, nf):
    n = pl.program_id(0)
    i = pl.program_id(1)

    def copies(e, nn):
        c0 = pl.multiple_of(nn * tn, tn)
        return (pltpu.make_async_copy(w1_hbm.at[e, :, pl.ds(c0, tn)], stage.at[0], sem),
                pltpu.make_async_copy(w1_hbm.at[e, :, pl.ds(F + c0, tn)], stage.at[1], sem))

    def consume():
        _cast_rows(stage.at[0], wgb)
        _cast_rows(stage.at[1], wub)

    _expert_weight_stage(te_ref, nx_ref, i, jnp.logical_and(n == 0, i == 0), n, nf, copies, consume)

    @pl.when(i < nu_ref[0])
    def _():
        x = _unpack_bf16_pairs(xs_ref[...]).astype(BF16)
        gate = jnp.minimum(_dot(x, wgb[...]) + bg_ref[...], SWIGLU_LIMIT)
        up = jnp.clip(_dot(x, wub[...]) + bu_ref[...], -SWIGLU_LIMIT, SWIGLU_LIMIT)
        act = gate * _sigmoid(SWIGLU_ALPHA * gate) * (up + 1.0)
        o_ref[...] = act.astype(o_ref.dtype)

    @pl.when(i >= nu_ref[0])
    def _():
        o_ref[...] = jnp.zeros_like(o_ref)


def _expert_down_kernel(te_ref, nu_ref, nx_ref, a_ref, w2_hbm, b2_ref, o_ref, stage, w2b, sem):
    i = pl.program_id(0)

    def copies(e, nn):
        del nn
        return (pltpu.make_async_copy(w2_hbm.at[e], stage, sem),)

    def consume():
        _cast_rows(stage, w2b)

    _expert_weight_stage(te_ref, nx_ref, i, i == 0, 0, 1, copies, consume)

    @pl.when(i < nu_ref[0])
    def _():
        y = _dot(a_ref[...], w2b[...]) + b2_ref[...]
        o_ref[...] = _pack_bf16_pairs(y)

    @pl.when(i >= nu_ref[0])
    def _():
        o_ref[...] = jnp.zeros_like(o_ref)


def _expert_ffn(xs, tile_e, n_used, next_e, w1, b1, w2, b2):
    R, W = xs.shape
    E, D, F2 = w1.shape
    F = F2 // 2
    tm = MOE_TILE
    nt = R // tm

    tn = min(512, F)
    nf = F // tn
    b1r = b1.reshape(E, 1, F2)

    act = pl.pallas_call(
        functools.partial(_expert_up_kernel, tn=tn, F=F, nf=nf),
        out_shape=jax.ShapeDtypeStruct((R, F), BF16),
        grid_spec=pltpu.PrefetchScalarGridSpec(
            num_scalar_prefetch=3,
            grid=(nf, nt),
            in_specs=[
                pl.BlockSpec((tm, W), lambda n, i, te, nu, nx: (jnp.minimum(i, nu[0] - 1), 0)),
                pl.BlockSpec(memory_space=pl.ANY),
                pl.BlockSpec((None, 1, tn), lambda n, i, te, nu, nx: (te[i], 0, n)),
                pl.BlockSpec((None, 1, tn), lambda n, i, te, nu, nx: (te[i], 0, nf + n)),
            ],
            out_specs=pl.BlockSpec((tm, tn), lambda n, i, te, nu, nx: (i, n)),
            scratch_shapes=[pltpu.VMEM((2, D, tn), F32), pltpu.VMEM((D, tn), BF16), pltpu.VMEM((D, tn), BF16),
                            pltpu.SemaphoreType.DMA],
        ),
        compiler_params=_cparams(("arbitrary", "arbitrary")),
        name="expert_up",
    )(tile_e, n_used, next_e, xs, w1, b1r, b1r)
    y = pl.pallas_call(
        _expert_down_kernel,
        out_shape=jax.ShapeDtypeStruct((R, D // 2), U32),
        grid_spec=pltpu.PrefetchScalarGridSpec(
            num_scalar_prefetch=3,
            grid=(nt,),
            in_specs=[
                pl.BlockSpec((tm, F), lambda i, te, nu, nx: (jnp.minimum(i, nu[0] - 1), 0)),
                pl.BlockSpec(memory_space=pl.ANY),
                pl.BlockSpec((None, 1, D), lambda i, te, nu, nx: (te[i], 0, 0)),
            ],
            out_specs=pl.BlockSpec((tm, D // 2), lambda i, te, nu, nx: (i, 0)),
            scratch_shapes=[pltpu.VMEM((F, D), F32), pltpu.VMEM((F, D), BF16), pltpu.SemaphoreType.DMA],
        ),
        compiler_params=_cparams(("arbitrary",)),
        name="expert_down",
    )(tile_e, n_used, next_e, act, w2, b2.reshape(E, 1, D))
    return y


def _combine_kernel(pos_ref, x_ref, w_ref, mod_ref, y_ref, o_ref, buf, sem, *, tm, T, gate_idx):
    i = pl.program_id(0)
    n = pl.num_programs(0)
    W = buf.shape[-1]
    th = tm
    ring = buf.shape[0]

    def issue_row(first_token, slot, t):
        for k in range(TOP_K):
            src = pos_ref[k * T + first_token + t]
            pltpu.make_async_copy(y_ref.at[pl.ds(src, 1)], buf.at[slot, k, pl.ds(t, 1)],
                                  sem.at[slot]).start(priority=k % 2)

    def wait_half(slot):
        for k in range(TOP_K):
            pltpu.make_async_copy(y_ref.at[pl.ds(0, th)], buf.at[slot, k], sem.at[slot]).wait()

    rc = 8
    cw = min(512, W)

    def consume(slot, next_first_token, next_slot):
        def rows_body(r, carry):
            rows = pl.ds(pl.multiple_of(r * rc, rc), rc)
            brow = rows
            wv = w_ref[rows, :]
            wk = [jnp.broadcast_to(wv[:, k:k + 1], (rc, cw)) for k in range(TOP_K)]
            for c in range(W // cw):
                lo = hi = None
                for k in range(TOP_K):
                    u = buf[slot, k, brow, c * cw:(c + 1) * cw]
                    l = wk[k] * lax.bitcast_convert_type(jnp.left_shift(u, jnp.uint32(16)), F32)
                    h = wk[k] * lax.bitcast_convert_type(jnp.bitwise_and(u, jnp.uint32(0xFFFF0000)), F32)
                    lo = l if lo is None else lo + l
                    hi = h if hi is None else hi + h
                for half, acc in ((0, lo), (1, hi)):
                    cols = slice(half * W + c * cw, half * W + (c + 1) * cw)
                    o_ref[rows, cols] = x_ref[rows, cols] + mod_ref[gate_idx:gate_idx + 1, cols] * acc
            for t in range(rc):
                issue_row(next_first_token, next_slot, r * rc + t)
            return carry

        lax.fori_loop(0, th // rc, rows_body, 0)

    @pl.when(i == 0)
    def _():
        for b in range(2):
            lax.fori_loop(0, tm, lambda t, c: (issue_row(b * tm, b, t), c)[1], 0, unroll=8)

    wait_half(i % ring)
    consume(i % ring, jnp.minimum((i + 2) * tm, T - tm), (i + 2) % ring)

    @pl.when(i == n - 1)
    def _():
        wait_half((i + 1) % ring)
        wait_half((i + 2) % ring)


def _combine(pos_flat, x1, w_tk, mod, y, S, gate_idx):
    T, D = x1.shape
    tm = min(128, S)
    assert T // tm >= 2
    grid_spec = pltpu.PrefetchScalarGridSpec(
        num_scalar_prefetch=1,
        grid=(T // tm,),
        in_specs=[
            pl.BlockSpec((tm, D), lambda i, p: (i, 0)),
            pl.BlockSpec((tm, TOP_K), lambda i, p: (i, 0)),
            pl.BlockSpec((None, N_ADA, D), lambda i, p: ((i * tm) // S, 0, 0)),
            pl.BlockSpec(memory_space=pl.ANY),
        ],
        out_specs=pl.BlockSpec((tm, D), lambda i, p: (i, 0)),
        scratch_shapes=[pltpu.VMEM((3, TOP_K, tm, D // 2), U32), pltpu.SemaphoreType.DMA((3,))],
    )
    return pl.pallas_call(
        functools.partial(_combine_kernel, tm=tm, T=T, gate_idx=gate_idx),
        out_shape=jax.ShapeDtypeStruct((T, D), F32),
        grid_spec=grid_spec,
        compiler_params=_cparams(("arbitrary",)),
        name="combine",
    )(pos_flat, x1, w_tk, mod, y)


def _layer(x2, mod, B, S, rel_bias, norm_mix_w, w_in, q_norm_w, k_norm_w, attn_sink, ret_decay_fwd, ret_decay_bwd,
           ret_gn_w, ret_gn_b, w_up_attn, w_up_ret, w_out, norm_ffn_w, router_w, router_b,
           expert_w1, expert_b1, expert_w2, expert_b2):
    T, D = x2.shape
    E = router_w.shape[1]
    h = _norm_mod(x2, norm_mix_w, mod, S, 0, 1)
    proj = _in_proj(h, w_in, q_norm_w, k_norm_w)
    attn = _window_attention(proj, _attn_bias_table(rel_bias), attn_sink, B, S)
    ret = _retention(proj, ret_decay_fwd, ret_decay_bwd, ret_gn_w, ret_gn_b, B, S)
    merged = _merge_up(attn, ret, w_up_attn, w_up_ret, proj, D)
    x1 = _out_proj(merged, w_out, x2, mod, S, 2)

    hp, top_e, top_w, rank, cnt = _norm_router(x1, norm_ffn_w, mod, router_w, router_b, S, 3, 4)
    tm = MOE_TILE
    counts = cnt[:, 0].astype(I32)
    tiles_per_e = (counts + tm - 1) // tm
    tile_end = jnp.cumsum(tiles_per_e)
    group_start = (tile_end - tiles_per_e) * tm
    n_tiles = (T * TOP_K) // tm + E
    n_used = tile_end[-1]
    tile_ids = jnp.minimum(jnp.arange(n_tiles, dtype=I32), n_used - 1)
    tile_e = jnp.minimum(jnp.sum((tile_ids[:, None] >= tile_end[None, :]).astype(I32), axis=1), E - 1)
    e_ids = jnp.arange(E, dtype=I32)
    later = jnp.logical_and(e_ids[None, :] > e_ids[:, None], tiles_per_e[None, :] > 0)
    next_of_e = jnp.min(jnp.where(later, e_ids[None, :], E), axis=1)
    next_of_e = jnp.where(next_of_e == E, -1, next_of_e)
    next_e = jnp.sum(jnp.where(tile_e[:, None] == e_ids, next_of_e, 0), axis=1).astype(I32)
    start_of = jnp.sum(jnp.where(top_e[:, :, None] == e_ids, group_start, 0), axis=-1)
    pos_flat = (start_of + rank).astype(I32).reshape(-1)
    pad_info = jnp.concatenate([group_start + counts, tiles_per_e * tm - counts, n_used.reshape(1)]).astype(I32)

    xs = _dispatch(pos_flat, pad_info, hp, n_tiles, E)
    y = _expert_ffn(xs, tile_e, n_used.reshape(1).astype(I32), next_e, expert_w1, expert_b1, expert_w2, expert_b2)
    return _combine(pos_flat, x1, top_w.T, mod, y, S, 5)


def kernel(x, c, rel_bias, ada_w, ada_b, norm_mix_w, w_in, q_norm_w, k_norm_w, attn_sink, ret_decay_fwd,
           ret_decay_bwd, ret_gn_w, ret_gn_b, w_up_attn, w_up_ret, w_out, norm_ffn_w, router_w, router_b,
           expert_w1, expert_b1, expert_w2, expert_b2):
    B, S, D = x.shape
    x2 = x.reshape(B * S, D)
    for l in range(ada_w.shape[0]):
        mod = _ada_mod(c, ada_w[l], ada_b[l])
        x2 = _layer(x2, mod, B, S, rel_bias, norm_mix_w[l], w_in[l], q_norm_w[l], k_norm_w[l], attn_sink[l],
                    ret_decay_fwd[l], ret_decay_bwd[l], ret_gn_w[l], ret_gn_b[l], w_up_attn[l], w_up_ret[l],
                    w_out[l], norm_ffn_w[l], router_w[l], router_b[l], expert_w1[l], expert_b1[l],
                    expert_w2[l], expert_b2[l])
    return x2.reshape(B, S, D)
```

```python
import functools
import math

import jax
import jax.numpy as jnp
from jax import lax
from jax.experimental import pallas as pl
from jax.experimental.pallas import tpu as pltpu

F32 = jnp.float32
BF16 = jnp.bfloat16
U32 = jnp.uint32
I32 = jnp.int32

ATTN_HEADS = 16
ATTN_KV_HEADS = 4
ATTN_HEAD_DIM = 128
WINDOW = 128
ATTN_BLOCK = 128
N_BUCKETS = 32
MAX_DISTANCE = 128
RET_HEADS = 8
RET_QK_DIM = 128
RET_V_DIM = 256
RET_CHUNK = 128
ROPE_BASE = 10000.0
TOP_K = 4
SWIGLU_LIMIT = 7.0
SWIGLU_ALPHA = 1.702
N_ADA = 6
EPS = 1e-6
NEG_INF = -1e30
NEG_BIG = -3.0e38
LOG2E = 1.4426950408889634

ATTN_Q_W = ATTN_HEADS * ATTN_HEAD_DIM
ATTN_KV_W = ATTN_KV_HEADS * ATTN_HEAD_DIM
RET_QK_W = RET_HEADS * RET_QK_DIM
RET_V_W = RET_HEADS * RET_V_DIM

VMEM_LIMIT_BYTES = 56 * 1024 * 1024
MOE_TILE = 512
RET_TILE = 256


def _cparams(sem):
    return pltpu.CompilerParams(dimension_semantics=sem, vmem_limit_bytes=VMEM_LIMIT_BYTES)


def _dot(a, b):
    return jnp.dot(a, b, preferred_element_type=F32)


def _dot_nt(a, b):
    return lax.dot_general(a, b, (((1,), (1,)), ((), ())), preferred_element_type=F32)


def _dot_tn(a, b):
    return lax.dot_general(a, b, (((0,), (0,)), ((), ())), preferred_element_type=F32)


def _sigmoid(x):
    return 1.0 / (1.0 + jnp.exp(-x))


def _ada_kernel(c_ref, w_ref, b_ref, o_ref):
    c = c_ref[...]
    cs = (c * _sigmoid(c)).astype(BF16)
    o_ref[...] = _dot(cs, w_ref[...].astype(BF16)) + b_ref[...]


def _ada_mod(c, ada_w, ada_b):
    B, D = c.shape
    N = ada_w.shape[1]
    rows = 8
    cp = jnp.zeros((rows, D), F32).at[:B].set(c)
    tn = min(1024, N)
    out = pl.pallas_call(
        _ada_kernel,
        out_shape=jax.ShapeDtypeStruct((rows, N), F32),
        grid=(N // tn,),
        in_specs=[
            pl.BlockSpec((rows, D), lambda j: (0, 0)),
            pl.BlockSpec((D, tn), lambda j: (0, j)),
            pl.BlockSpec((1, tn), lambda j: (0, j)),
        ],
        out_specs=pl.BlockSpec((rows, tn), lambda j: (0, j)),
        compiler_params=_cparams(("arbitrary",)),
        name="ada_mod",
    )(cp, ada_w, ada_b.reshape(1, N))
    return out[:B].reshape(B, N_ADA, D)


def _norm_mod_value(x, nw, shift, scale):
    ms = jnp.mean(x * x, axis=-1, keepdims=True)
    h = x * lax.rsqrt(ms + EPS) * nw
    return h * (1.0 + scale) + shift


def _norm_mod_kernel(x_ref, nw_ref, mod_ref, o_ref, *, shift_idx, scale_idx):
    h = _norm_mod_value(x_ref[...], nw_ref[...], mod_ref[shift_idx:shift_idx + 1, :],
                        mod_ref[scale_idx:scale_idx + 1, :])
    o_ref[...] = h.astype(BF16)


def _norm_mod(x2, nw, mod, S, shift_idx, scale_idx):
    T, D = x2.shape
    tm = min(512, S)
    return pl.pallas_call(
        functools.partial(_norm_mod_kernel, shift_idx=shift_idx, scale_idx=scale_idx),
        out_shape=jax.ShapeDtypeStruct((T, D), BF16),
        grid=(T // tm,),
        in_specs=[
            pl.BlockSpec((tm, D), lambda i: (i, 0)),
            pl.BlockSpec((1, D), lambda i: (0, 0)),
            pl.BlockSpec((None, N_ADA, D), lambda i: ((i * tm) // S, 0, 0)),
        ],
        out_specs=pl.BlockSpec((tm, D), lambda i: (i, 0)),
        compiler_params=_cparams(("arbitrary",)),
        name="norm_mod",
    )(x2, nw.reshape(1, D), mod)


def _stream_weight_tile(w_hbm, stage, wb, sem):
    j = pl.program_id(0)
    tn = wb.shape[1]

    def w_copy(jj):
        return pltpu.make_async_copy(w_hbm.at[:, pl.ds(pl.multiple_of(jj * tn, tn), tn)], stage, sem)

    @pl.when(pl.program_id(1) == 0)
    def _():
        @pl.when(j == 0)
        def _():
            w_copy(0).start()

        w_copy(j).wait()
        _cast_rows(stage, wb)

        @pl.when(j + 1 < pl.num_programs(0))
        def _():
            w_copy(j + 1).start()


def _in_proj_kernel(a_ref, w_hbm, cw_ref, cf_ref, bd_ref, o_ref, stage, wb, sem, *, n_norm_tiles):
    j = pl.program_id(0)
    tn = wb.shape[1]
    _stream_weight_tile(w_hbm, stage, wb, sem)

    @pl.when(j < n_norm_tiles)
    def _():
        cb = bd_ref.shape[0]
        for c in range(tn // cb):
            cols = slice(c * cb, (c + 1) * cb)
            acc = _dot(a_ref[...], wb[:, cols])
            ms = _dot((acc * acc).astype(BF16), bd_ref[...]) * (1.0 / ATTN_HEAD_DIM)
            inv = jnp.where(cf_ref[:, cols] > 0.0, lax.rsqrt(ms + EPS), 1.0)
            o_ref[:, cols] = (acc * inv * cw_ref[:, cols]).astype(o_ref.dtype)

    @pl.when(j >= n_norm_tiles)
    def _():
        o_ref[...] = _dot(a_ref[...], wb[...]).astype(o_ref.dtype)


def _in_proj(h, w, q_norm_w, k_norm_w):
    T, K = h.shape
    N = w.shape[1]
    tm = min(1024, T)
    tn = 1024 if N % 1024 == 0 else 512
    hd = ATTN_HEAD_DIM
    n_norm = ATTN_Q_W + ATTN_KV_W
    n_norm_tiles = -(-n_norm // tn)
    pad = n_norm_tiles * tn - n_norm
    col_w = jnp.concatenate([jnp.tile(q_norm_w.astype(F32) * (hd ** -0.5 * LOG2E), ATTN_HEADS),
                             jnp.tile(k_norm_w.astype(F32), ATTN_KV_HEADS), jnp.ones((pad,), F32)]).reshape(1, -1)
    col_flag = jnp.concatenate([jnp.ones((n_norm,), F32), jnp.zeros((pad,), F32)]).reshape(1, -1)
    cb = min(512, tn)
    lane_head = jnp.arange(cb, dtype=I32) // hd
    block_ones = (lane_head[:, None] == lane_head[None, :]).astype(BF16)
    last = n_norm_tiles - 1
    return pl.pallas_call(
        functools.partial(_in_proj_kernel, n_norm_tiles=n_norm_tiles),
        out_shape=jax.ShapeDtypeStruct((T, N), BF16),
        grid=(N // tn, T // tm),
        in_specs=[
            pl.BlockSpec((tm, K), lambda j, i: (i, 0)),
            pl.BlockSpec(memory_space=pl.ANY),
            pl.BlockSpec((1, tn), lambda j, i: (0, jnp.minimum(j, last))),
            pl.BlockSpec((1, tn), lambda j, i: (0, jnp.minimum(j, last))),
            pl.BlockSpec((cb, cb), lambda j, i: (0, 0)),
        ],
        out_specs=pl.BlockSpec((tm, tn), lambda j, i: (i, j)),
        scratch_shapes=[pltpu.VMEM((K, tn), F32), pltpu.VMEM((K, tn), BF16), pltpu.SemaphoreType.DMA],
        compiler_params=_cparams(("arbitrary", "arbitrary")),
        name="in_proj",
    )(h, w, col_w, col_flag, block_ones)


def _t5_bucket(rel):
    nb = N_BUCKETS // 2
    max_exact = nb // 2
    base = jnp.where(rel > 0, nb, 0)
    n = jnp.abs(rel)
    nf = jnp.maximum(n, 1).astype(F32)
    large = max_exact + (jnp.log(nf / max_exact) / math.log(MAX_DISTANCE / max_exact) * (nb - max_exact)).astype(I32)
    large = jnp.minimum(large, nb - 1)
    return base + jnp.where(n < max_exact, n, large)


def _attn_bias_table(rel_bias):
    blk = ATTN_BLOCK
    qi = jnp.arange(blk, dtype=I32)[:, None]
    kj = jnp.arange(3 * blk, dtype=I32)[None, :]
    rel = kj - blk - qi
    bucket = _t5_bucket(rel)
    table = rel_bias.astype(F32).T
    bias = jnp.zeros((table.shape[0],) + bucket.shape, F32)
    for b in range(N_BUCKETS):
        bias = jnp.where(bucket[None] == b, table[:, b][:, None, None], bias)
    bias = jnp.where((jnp.abs(rel) <= WINDOW)[None], bias * LOG2E, NEG_INF)
    G = ATTN_HEADS // ATTN_KV_HEADS
    bias = bias.reshape(ATTN_KV_HEADS, G, blk, 3 * blk)
    bias = jnp.transpose(bias, (0, 3, 1, 2)).reshape(ATTN_KV_HEADS, 3 * blk, G * blk)
    key = jnp.arange(3 * blk, dtype=I32)[None, :, None]
    no_prev, no_next = key < blk, key >= 2 * blk
    return jnp.stack([bias, jnp.where(no_prev, NEG_INF, bias), jnp.where(no_next, NEG_INF, bias),
                      jnp.where(jnp.logical_or(no_prev, no_next), NEG_INF, bias)])


def _attn_kernel(sink_ref, q_ref, kp_ref, kc_ref, kn_ref, vp_ref, vc_ref, vn_ref, bias_ref, o_ref):
    blk, hd = ATTN_BLOCK, ATTN_HEAD_DIM
    G = ATTN_HEADS // ATTN_KV_HEADS
    for hk in range(ATTN_KV_HEADS):
        sl = slice(hk * hd, (hk + 1) * hd)
        k3 = jnp.concatenate([kp_ref[:, sl], kc_ref[:, sl], kn_ref[:, sl]], axis=0)
        v3 = jnp.concatenate([vp_ref[:, sl], vc_ref[:, sl], vn_ref[:, sl]], axis=0)
        q4 = jnp.concatenate([q_ref[:, (hk * G + g) * hd:(hk * G + g + 1) * hd] for g in range(G)],
                             axis=0)
        sink = jnp.concatenate([jnp.full((1, blk), sink_ref[hk * G + g], F32) for g in range(G)], axis=1)
        logits = _dot_nt(k3, q4) + bias_ref[hk]
        m = jnp.maximum(jnp.max(logits, axis=0, keepdims=True), sink)
        p = jnp.exp2(logits - m)
        denom = jnp.sum(p, axis=0, keepdims=True) + jnp.exp2(sink - m)
        out_t = _dot_tn(v3, p.astype(BF16)) * (1.0 / denom)
        out = out_t.T
        for g in range(G):
            h = hk * G + g
            o_ref[:, h * hd:(h + 1) * hd] = out[g * blk:(g + 1) * blk].astype(o_ref.dtype)


def _window_attention(proj, bias_tab, sink, B, S):
    T = proj.shape[0]
    blk = ATTN_BLOCK
    nb = S // blk
    kcol = ATTN_Q_W // ATTN_KV_W
    vcol = kcol + 1

    def kv_spec(col, off):
        return pl.BlockSpec((blk, ATTN_KV_W), lambda b, n, s: (b * nb + jnp.clip(n + off, 0, nb - 1), col))

    grid_spec = pltpu.PrefetchScalarGridSpec(
        num_scalar_prefetch=1,
        grid=(B, nb),
        in_specs=[
            pl.BlockSpec((blk, ATTN_Q_W), lambda b, n, s: (b * nb + n, 0)),
            kv_spec(kcol, -1), kv_spec(kcol, 0), kv_spec(kcol, 1),
            kv_spec(vcol, -1), kv_spec(vcol, 0), kv_spec(vcol, 1),
            pl.BlockSpec((None, ATTN_KV_HEADS, 3 * blk, (ATTN_HEADS // ATTN_KV_HEADS) * blk),
                         lambda b, n, s: ((n == 0).astype(I32) + 2 * (n == nb - 1).astype(I32), 0, 0, 0)),
        ],
        out_specs=pl.BlockSpec((blk, ATTN_Q_W), lambda b, n, s: (b * nb + n, 0)),
    )
    return pl.pallas_call(
        _attn_kernel,
        out_shape=jax.ShapeDtypeStruct((T, ATTN_Q_W), BF16),
        grid_spec=grid_spec,
        compiler_params=_cparams(("arbitrary", "arbitrary")),
        name="window_attn",
    )(sink.astype(F32) * LOG2E, proj, proj, proj, proj, proj, proj, proj, bias_tab)


def _rope_tables(S):
    d = RET_QK_DIM
    inv = ROPE_BASE ** (-jnp.arange(0, d, 2, dtype=F32) / d)
    ang = jnp.arange(S, dtype=F32)[:, None] * inv[None, :]
    cos, sin = jnp.cos(ang), jnp.sin(ang)
    return jnp.concatenate([cos, cos], axis=-1), jnp.concatenate([-sin, sin], axis=-1)


def _ret_kernel(df_ref, db_ref, q_ref, k_ref, v_ref, g_ref, cos_ref, sin_ref, gw_ref, gb_ref, o_ref,
                qb, kb, q2, kv, sprev, *, S, C):
    h = pl.program_id(1)
    nc = S // C
    dk = RET_QK_DIM
    half = dk // 2

    rowf = lax.broadcasted_iota(I32, (C, 1), 0).astype(F32)
    lg_f = -jnp.exp(jnp.full((1, 1), df_ref[h], F32))
    lg_b = -jnp.exp(jnp.full((1, 1), db_ref[h], F32))
    qdec_f = jnp.exp((rowf + 1.0) * lg_f)
    kdec_f = jnp.exp((C - 1.0 - rowf) * lg_f)
    qdec_b = jnp.exp((C - rowf) * lg_b)
    kdec_b = jnp.exp(rowf * lg_b)
    cd_f = jnp.exp(C * lg_f)
    cd_b = jnp.exp(C * lg_b)

    def a_body(n, carry):
        rows = pl.ds(pl.multiple_of(n * C, C), C)
        co = cos_ref[rows, :]
        si = sin_ref[rows, :]
        q = q_ref[rows, :].astype(F32)
        k = k_ref[rows, :].astype(F32)
        qr = q * co + pltpu.roll(q, half, 1) * si
        kr = (k * co + pltpu.roll(k, half, 1) * si) * (dk ** -0.5)
        qb[rows, :] = qr.astype(BF16)
        kb[rows, :] = kr.astype(BF16)
        q2[rows, :] = jnp.concatenate([qr * qdec_f, qr * qdec_b], axis=1).astype(BF16)
        k2 = jnp.concatenate([kr * kdec_f, kr * kdec_b], axis=1).astype(BF16)
        kv[n] = _dot_tn(k2, v_ref[rows, :])
        return carry

    lax.fori_loop(0, nc, a_body, 0, unroll=4)

    def scan_f(n, state):
        sprev[n, 0:dk, :] = state.astype(BF16)
        return state * cd_f + kv[n, 0:dk, :]

    lax.fori_loop(0, nc, scan_f, jnp.zeros((dk, RET_V_DIM), F32))

    def scan_b(t, state):
        n = nc - 1 - t
        sprev[n, dk:2 * dk, :] = state.astype(BF16)
        return state * cd_b + kv[n, dk:2 * dk, :]

    lax.fori_loop(0, nc, scan_b, jnp.zeros((dk, RET_V_DIM), F32))

    ri = lax.broadcasted_iota(I32, (C, C), 0)
    ci = lax.broadcasted_iota(I32, (C, C), 1)
    d = (ri - ci).astype(F32)
    dec = jnp.where(ri >= ci, jnp.exp(jnp.maximum(d, 0.0) * lg_f), jnp.exp(jnp.maximum(-d, 0.0) * lg_b))
    gw = gw_ref[...]
    gb = gb_ref[...]

    def c_body(n, carry):
        rows = pl.ds(pl.multiple_of(n * C, C), C)
        scores = _dot_nt(qb[rows, :], kb[rows, :]) * dec
        y = _dot(scores.astype(BF16), v_ref[rows, :]) + _dot(q2[rows, :], sprev[n])
        mu = jnp.mean(y, axis=-1, keepdims=True)
        yc = y - mu
        var = jnp.mean(yc * yc, axis=-1, keepdims=True)
        yn = yc * lax.rsqrt(var + EPS) * gw + gb
        g = g_ref[rows, :].astype(F32)
        o_ref[rows, :] = (g * _sigmoid(g) * yn).astype(o_ref.dtype)
        return carry

    lax.fori_loop(0, nc, c_body, 0, unroll=4)


def _retention(proj, decay_fwd, decay_bwd, gn_w, gn_b, B, S):
    T = proj.shape[0]
    dk, dv = RET_QK_DIM, RET_V_DIM
    q_off = (ATTN_Q_W + 2 * ATTN_KV_W) // dk
    k_off = q_off + RET_QK_W // dk
    v_off = (ATTN_Q_W + 2 * ATTN_KV_W + 2 * RET_QK_W) // dv
    g_off = v_off + RET_V_W // dv
    cos, sin = _rope_tables(S)
    C = min(RET_TILE, S)
    grid_spec = pltpu.PrefetchScalarGridSpec(
        num_scalar_prefetch=2,
        grid=(B, RET_HEADS),
        in_specs=[
            pl.BlockSpec((S, dk), lambda b, h, *_: (b, q_off + h)),
            pl.BlockSpec((S, dk), lambda b, h, *_: (b, k_off + h)),
            pl.BlockSpec((S, dv), lambda b, h, *_: (b, v_off + h)),
            pl.BlockSpec((S, dv), lambda b, h, *_: (b, g_off + h)),
            pl.BlockSpec((S, dk), lambda b, h, *_: (0, 0)),
            pl.BlockSpec((S, dk), lambda b, h, *_: (0, 0)),
            pl.BlockSpec((1, dv), lambda b, h, *_: (0, h)),
            pl.BlockSpec((1, dv), lambda b, h, *_: (0, h)),
        ],
        out_specs=pl.BlockSpec((S, dv), lambda b, h, *_: (b, h)),
        scratch_shapes=[
            pltpu.VMEM((S, dk), BF16),
            pltpu.VMEM((S, dk), BF16),
            pltpu.VMEM((S, 2 * dk), BF16),
            pltpu.VMEM((S // C, 2 * dk, dv), F32),
            pltpu.VMEM((S // C, 2 * dk, dv), BF16),
        ],
    )
    return pl.pallas_call(
        functools.partial(_ret_kernel, S=S, C=C),
        out_shape=jax.ShapeDtypeStruct((T, RET_V_W), BF16),
        grid_spec=grid_spec,
        compiler_params=_cparams(("arbitrary", "arbitrary")),
        name="retention",
    )(decay_fwd.astype(F32), decay_bwd.astype(F32), proj, proj, proj, proj, cos, sin,
      gn_w.reshape(1, -1), gn_b.reshape(1, -1))


def _merge_kernel(a_ref, r_ref, wa_hbm, wr_hbm, ga_ref, gr_ref, o_ref, sta, stb, wab, wrb, sem):
    _stream_weight_tile(wa_hbm, sta, wab, sem.at[0])
    _stream_weight_tile(wr_hbm, stb, wrb, sem.at[1])
    ua = _dot(a_ref[...], wab[...])
    ur = _dot(r_ref[...], wrb[...])
    out = _sigmoid(ga_ref[...].astype(F32)) * ua + _sigmoid(gr_ref[...].astype(F32)) * ur
    o_ref[...] = out.astype(o_ref.dtype)


def _merge_up(attn, ret, wa, wr, proj, D):
    T = attn.shape[0]
    tm = min(512, T)
    tn = min(1024, D)
    ga_off = (ATTN_Q_W + 2 * ATTN_KV_W + 2 * RET_QK_W + 2 * RET_V_W) // tn
    gr_off = ga_off + D // tn
    return pl.pallas_call(
        _merge_kernel,
        out_shape=jax.ShapeDtypeStruct((T, D), BF16),
        grid=(D // tn, T // tm),
        in_specs=[
            pl.BlockSpec((tm, ATTN_Q_W), lambda j, i: (i, 0)),
            pl.BlockSpec((tm, RET_V_W), lambda j, i: (i, 0)),
            pl.BlockSpec(memory_space=pl.ANY),
            pl.BlockSpec(memory_space=pl.ANY),
            pl.BlockSpec((tm, tn), lambda j, i: (i, ga_off + j)),
            pl.BlockSpec((tm, tn), lambda j, i: (i, gr_off + j)),
        ],
        out_specs=pl.BlockSpec((tm, tn), lambda j, i: (i, j)),
        scratch_shapes=[pltpu.VMEM((ATTN_Q_W, tn), F32), pltpu.VMEM((RET_V_W, tn), F32),
                        pltpu.VMEM((ATTN_Q_W, tn), BF16), pltpu.VMEM((RET_V_W, tn), BF16),
                        pltpu.SemaphoreType.DMA((2,))],
        compiler_params=_cparams(("arbitrary", "arbitrary")),
        name="merge_up",
    )(attn, ret, wa, wr, proj, proj)


def _out_proj_kernel(m_ref, w_hbm, x_ref, mod_ref, o_ref, stage, wb, sem, *, gate_idx):
    _stream_weight_tile(w_hbm, stage, wb, sem)
    y = _dot(m_ref[...], wb[...])
    o_ref[...] = x_ref[...] + mod_ref[gate_idx:gate_idx + 1, :] * y


def _out_proj(merged, w, x2, mod, S, gate_idx):
    T, D = x2.shape
    tm = min(512, S)
    tn = min(1024, D)
    return pl.pallas_call(
        functools.partial(_out_proj_kernel, gate_idx=gate_idx),
        out_shape=jax.ShapeDtypeStruct((T, D), F32),
        grid=(D // tn, T // tm),
        in_specs=[
            pl.BlockSpec((tm, D), lambda j, i: (i, 0)),
            pl.BlockSpec(memory_space=pl.ANY),
            pl.BlockSpec((tm, tn), lambda j, i: (i, j)),
            pl.BlockSpec((None, N_ADA, tn), lambda j, i: ((i * tm) // S, 0, j)),
        ],
        out_specs=pl.BlockSpec((tm, tn), lambda j, i: (i, j)),
        scratch_shapes=[pltpu.VMEM((D, tn), F32), pltpu.VMEM((D, tn), BF16), pltpu.SemaphoreType.DMA],
        compiler_params=_cparams(("arbitrary", "arbitrary")),
        name="out_proj",
    )(merged, w, x2, mod)


def _pack_bf16_pairs(v):
    n = v.shape[1] // 2
    bits = lax.bitcast_convert_type(v.astype(BF16).astype(F32), U32)
    return jnp.bitwise_or(jnp.bitwise_and(bits[:, n:], jnp.uint32(0xFFFF0000)),
                          jnp.right_shift(bits[:, :n], jnp.uint32(16)))


def _unpack_bf16_pairs(w):
    lo = lax.bitcast_convert_type(jnp.left_shift(w, jnp.uint32(16)), F32)
    hi = lax.bitcast_convert_type(jnp.bitwise_and(w, jnp.uint32(0xFFFF0000)), F32)
    return jnp.concatenate([lo, hi], axis=1)


def _norm_router_kernel(x_ref, nw_ref, mod_ref, rwt_ref, rb_ref, hp_ref, e_ref, w_ref, r_ref, cnt_ref, carry,
                        *, shift_idx, scale_idx):
    i = pl.program_id(0)

    @pl.when(i == 0)
    def _():
        carry[...] = jnp.zeros_like(carry)

    h = _norm_mod_value(x_ref[...], nw_ref[...], mod_ref[shift_idx:shift_idx + 1, :],
                        mod_ref[scale_idx:scale_idx + 1, :])
    hp_ref[...] = _pack_bf16_pairs(h)
    hb = h.astype(BF16)
    tm = hb.shape[0]
    E = rwt_ref.shape[0]
    logits = _dot_nt(rwt_ref[...], hb) + rb_ref[...]
    iota_e = lax.broadcasted_iota(I32, (E, tm), 0)
    vals, idxs = [], []
    work = logits
    sel = jnp.zeros((E, tm), F32)
    for _k in range(TOP_K):
        m = jnp.max(work, axis=0, keepdims=True)
        idx = jnp.min(jnp.where(work == m, iota_e, E), axis=0, keepdims=True)
        hit = iota_e == idx
        vals.append(m)
        idxs.append(idx)
        work = jnp.where(hit, NEG_BIG, work)
        sel = sel + hit.astype(F32)
    ex = [jnp.exp(v - vals[0]) for v in vals]
    tot = ex[0]
    for v in ex[1:]:
        tot = tot + v
    ri = lax.broadcasted_iota(I32, (tm, tm), 0)
    ci = lax.broadcasted_iota(I32, (tm, tm), 1)
    upper = (ri < ci).astype(BF16)
    before = _dot(sel.astype(BF16), upper) + carry[:, 0:1]
    for k in range(TOP_K):
        e_ref[k:k + 1, :] = idxs[k]
        w_ref[k:k + 1, :] = ex[k] / tot
        r_ref[k:k + 1, :] = jnp.sum(jnp.where(iota_e == idxs[k], before, 0.0), axis=0, keepdims=True).astype(I32)
    carry[...] = carry[...] + jnp.sum(sel, axis=1, keepdims=True)
    cnt_ref[...] = carry[...]


def _norm_router(x1, nw, mod, router_w, router_b, S, shift_idx, scale_idx):
    T, D = x1.shape
    E = router_w.shape[1]
    tm = min(256, S)
    outs = pl.pallas_call(
        functools.partial(_norm_router_kernel, shift_idx=shift_idx, scale_idx=scale_idx),
        out_shape=(
            jax.ShapeDtypeStruct((T, D // 2), U32),
            jax.ShapeDtypeStruct((TOP_K, T), I32),
            jax.ShapeDtypeStruct((TOP_K, T), F32),
            jax.ShapeDtypeStruct((TOP_K, T), I32),
            jax.ShapeDtypeStruct((E, 128), F32),
        ),
        grid=(T // tm,),
        in_specs=[
            pl.BlockSpec((tm, D), lambda i: (i, 0)),
            pl.BlockSpec((1, D), lambda i: (0, 0)),
            pl.BlockSpec((None, N_ADA, D), lambda i: ((i * tm) // S, 0, 0)),
            pl.BlockSpec((E, D), lambda i: (0, 0)),
            pl.BlockSpec((E, 1), lambda i: (0, 0)),
        ],
        out_specs=(
            pl.BlockSpec((tm, D // 2), lambda i: (i, 0)),
            pl.BlockSpec((TOP_K, tm), lambda i: (0, i)),
            pl.BlockSpec((TOP_K, tm), lambda i: (0, i)),
            pl.BlockSpec((TOP_K, tm), lambda i: (0, i)),
            pl.BlockSpec((E, 128), lambda i: (0, 0)),
        ),
        scratch_shapes=[pltpu.VMEM((E, 128), F32)],
        compiler_params=_cparams(("arbitrary",)),
        name="norm_router",
    )(x1, nw.reshape(1, D), mod, router_w.T.astype(BF16), router_b.reshape(E, 1).astype(F32))
    return outs


def _dispatch_kernel(pos_ref, pad_ref, hp_ref, xs_ref, zeros, sem, zsem, *, tm, T, E, n_tiles):
    base = pl.program_id(0) * tm

    @pl.when(pl.program_id(0) == 0)
    def _():
        zeros[...] = jnp.zeros_like(zeros)
        def pad_copy(row):
            return pltpu.make_async_copy(zeros.at[pl.ds(0, 1)], xs_ref.at[pl.ds(row, 1)], zsem)

        def tail_copy(t):
            return pltpu.make_async_copy(
                zeros, xs_ref.at[pl.ds(pl.multiple_of(t * MOE_TILE, MOE_TILE), MOE_TILE)], zsem)

        def pad8_copy(row):
            return pltpu.make_async_copy(zeros.at[pl.ds(0, 8)], xs_ref.at[pl.ds(pl.multiple_of(row, 8), 8)], zsem)

        def pad_pieces(e):
            start, count = pad_ref[e], pad_ref[E + e]
            head = jnp.minimum(count, jnp.bitwise_and(-start, 7))
            return start, head, start + head, (count - head) // 8

        def pad_start(e, carry):
            start, head, body, n8 = pad_pieces(e)
            lax.fori_loop(0, head, lambda r, c: (pad_copy(start + r).start(), c)[1], 0)
            lax.fori_loop(0, n8, lambda r, c: (pad8_copy(body + 8 * r).start(), c)[1], 0)
            return carry

        def pad_wait(e, carry):
            start, head, body, n8 = pad_pieces(e)
            lax.fori_loop(0, head, lambda r, c: (pad_copy(start + r).wait(), c)[1], 0)
            lax.fori_loop(0, n8, lambda r, c: (pad8_copy(body + 8 * r).wait(), c)[1], 0)
            return carry

        def tail_start(t, carry):
            tail_copy(t).start()
            return carry

        def tail_wait(t, carry):
            tail_copy(t).wait()
            return carry

        lax.fori_loop(0, E, pad_start, 0)
        lax.fori_loop(pad_ref[2 * E], n_tiles, tail_start, 0)
        lax.fori_loop(0, E, pad_wait, 0)
        lax.fori_loop(pad_ref[2 * E], n_tiles, tail_wait, 0)

    def start_body(t, carry):
        for k in range(TOP_K):
            dst = pos_ref[k * T + base + t]
            pltpu.make_async_copy(hp_ref.at[pl.ds(t, 1)], xs_ref.at[pl.ds(dst, 1)], sem).start(priority=k % 2)
        return carry

    lax.fori_loop(0, tm, start_body, 0, unroll=8)
    for k in range(TOP_K):
        pltpu.make_async_copy(hp_ref, xs_ref.at[pl.ds(0, tm)], sem).wait()


def _dispatch(pos_flat, pad_info, hp, n_tiles, E):
    T, W = hp.shape
    tm = min(512, T)
    grid_spec = pltpu.PrefetchScalarGridSpec(
        num_scalar_prefetch=2,
        grid=(T // tm,),
        in_specs=[pl.BlockSpec((tm, W), lambda i, p, q: (i, 0))],
        out_specs=pl.BlockSpec(memory_space=pl.ANY),
        scratch_shapes=[pltpu.VMEM((MOE_TILE, W), U32), pltpu.SemaphoreType.DMA, pltpu.SemaphoreType.DMA],
    )
    return pl.pallas_call(
        functools.partial(_dispatch_kernel, tm=tm, T=T, E=E, n_tiles=n_tiles),
        out_shape=jax.ShapeDtypeStruct((n_tiles * MOE_TILE, W), U32),
        grid_spec=grid_spec,
        compiler_params=_cparams(("arbitrary",)),
        name="dispatch",
    )(pos_flat, pad_info, hp)


def _expert_changed(te_ref, i):
    return jnp.logical_or(i == 0, te_ref[i] != te_ref[jnp.maximum(i - 1, 0)])


def _cast_rows(src_ref, dst_ref, rows_per_pass=256):
    def body(r, carry):
        rows = pl.ds(pl.multiple_of(r * rows_per_pass, rows_per_pass), rows_per_pass)
        dst_ref[rows, :] = src_ref[rows, :].astype(dst_ref.dtype)
        return carry

    lax.fori_loop(0, src_ref.shape[0] // rows_per_pass, body, 0)


def _expert_weight_stage(te_ref, nx_ref, i, first_step, n, n_passes, copies, consume):
    @pl.when(_expert_changed(te_ref, i))
    def _():
        @pl.when(first_step)
        def _():
            for cp in copies(te_ref[0], 0):
                cp.start()

        for cp in copies(te_ref[i], n):
            cp.wait()
        consume()
        same_pass = nx_ref[i] >= 0
        nxt_e = jnp.where(same_pass, nx_ref[i], te_ref[0])
        nxt_n = jnp.where(same_pass, n, n + 1)

        @pl.when(jnp.logical_or(same_pass, n + 1 < n_passes))
        def _():
            for cp in copies(nxt_e, nxt_n):
                cp.start()


def _expert_up_kernel(te_ref, nu_ref, nx_ref, tr_ref, xs_ref, w1_hbm, bg_ref, bu_ref, o_ref, stage, wgb, wub, sem,
                      *, tn, F, nf):
    n = pl.program_id(0)
    i = pl.program_id(1)

    def copies(e, nn):
        c0 = pl.multiple_of(nn * tn, tn)
        return (pltpu.make_async_copy(w1_hbm.at[e, :, pl.ds(c0, tn)], stage.at[0], sem),
                pltpu.make_async_copy(w1_hbm.at[e, :, pl.ds(F + c0, tn)], stage.at[1], sem))

    def consume():
        _cast_rows(stage.at[0], wgb)
        _cast_rows(stage.at[1], wub)

    _expert_weight_stage(te_ref, nx_ref, i, jnp.logical_and(n == 0, i == 0), n, nf, copies, consume)

    def ffn_up(rows):
        x = _unpack_bf16_pairs(xs_ref[rows, :]).astype(BF16)
        gate = jnp.minimum(_dot(x, wgb[...]) + bg_ref[...], SWIGLU_LIMIT)
        up = jnp.clip(_dot(x, wub[...]) + bu_ref[...], -SWIGLU_LIMIT, SWIGLU_LIMIT)
        act = gate * _sigmoid(SWIGLU_ALPHA * gate) * (up + 1.0)
        o_ref[rows, :] = act.astype(o_ref.dtype)

    _per_tile_rows(tr_ref[i], o_ref, ffn_up)


def _per_tile_rows(valid_rows, o_ref, compute):
    tm = o_ref.shape[0]
    half = tm // 2

    @pl.when(valid_rows > half)
    def _():
        compute(slice(0, tm))

    @pl.when(jnp.logical_and(valid_rows > 0, valid_rows <= half))
    def _():
        compute(slice(0, half))
        o_ref[half:, :] = jnp.zeros((tm - half, o_ref.shape[1]), o_ref.dtype)

    @pl.when(valid_rows <= 0)
    def _():
        o_ref[...] = jnp.zeros_like(o_ref)


def _expert_down_kernel(te_ref, nu_ref, nx_ref, tr_ref, a_ref, w2_hbm, b2_ref, o_ref, stage, w2b, sem):
    i = pl.program_id(0)

    def copies(e, nn):
        del nn
        return (pltpu.make_async_copy(w2_hbm.at[e], stage, sem),)

    def consume():
        _cast_rows(stage, w2b)

    _expert_weight_stage(te_ref, nx_ref, i, i == 0, 0, 1, copies, consume)

    def ffn_down(rows):
        y = _dot(a_ref[rows, :], w2b[...]) + b2_ref[...]
        o_ref[rows, :] = _pack_bf16_pairs(y)

    _per_tile_rows(tr_ref[i], o_ref, ffn_down)


def _expert_ffn(xs, tile_e, n_used, next_e, tile_rows, w1, b1, w2, b2):
    R, W = xs.shape
    E, D, F2 = w1.shape
    F = F2 // 2
    tm = MOE_TILE
    nt = R // tm

    tn = min(512, F)
    nf = F // tn
    b1r = b1.reshape(E, 1, F2)

    act = pl.pallas_call(
        functools.partial(_expert_up_kernel, tn=tn, F=F, nf=nf),
        out_shape=jax.ShapeDtypeStruct((R, F), BF16),
        grid_spec=pltpu.PrefetchScalarGridSpec(
            num_scalar_prefetch=4,
            grid=(nf, nt),
            in_specs=[
                pl.BlockSpec((tm, W), lambda n, i, te, nu, nx, tr: (jnp.minimum(i, nu[0] - 1), 0)),
                pl.BlockSpec(memory_space=pl.ANY),
                pl.BlockSpec((None, 1, tn), lambda n, i, te, nu, nx, tr: (te[i], 0, n)),
                pl.BlockSpec((None, 1, tn), lambda n, i, te, nu, nx, tr: (te[i], 0, nf + n)),
            ],
            out_specs=pl.BlockSpec((tm, tn), lambda n, i, te, nu, nx, tr: (i, n)),
            scratch_shapes=[pltpu.VMEM((2, D, tn), F32), pltpu.VMEM((D, tn), BF16), pltpu.VMEM((D, tn), BF16),
                            pltpu.SemaphoreType.DMA],
        ),
        compiler_params=_cparams(("arbitrary", "arbitrary")),
        name="expert_up",
    )(tile_e, n_used, next_e, tile_rows, xs, w1, b1r, b1r)
    y = pl.pallas_call(
        _expert_down_kernel,
        out_shape=jax.ShapeDtypeStruct((R, D // 2), U32),
        grid_spec=pltpu.PrefetchScalarGridSpec(
            num_scalar_prefetch=4,
            grid=(nt,),
            in_specs=[
                pl.BlockSpec((tm, F), lambda i, te, nu, nx, tr: (jnp.minimum(i, nu[0] - 1), 0)),
                pl.BlockSpec(memory_space=pl.ANY),
                pl.BlockSpec((None, 1, D), lambda i, te, nu, nx, tr: (te[i], 0, 0)),
            ],
            out_specs=pl.BlockSpec((tm, D // 2), lambda i, te, nu, nx, tr: (i, 0)),
            scratch_shapes=[pltpu.VMEM((F, D), F32), pltpu.VMEM((F, D), BF16), pltpu.SemaphoreType.DMA],
        ),
        compiler_params=_cparams(("arbitrary",)),
        name="expert_down",
    )(tile_e, n_used, next_e, tile_rows, act, w2, b2.reshape(E, 1, D))
    return y


def _combine_kernel(pos_ref, x_ref, w_ref, mod_ref, y_ref, o_ref, buf, sem, *, tm, T, gate_idx):
    i = pl.program_id(0)
    n = pl.num_programs(0)
    W = buf.shape[-1]
    th = tm
    ring = buf.shape[0]

    def issue_row(first_token, slot, t):
        for k in range(TOP_K):
            src = pos_ref[k * T + first_token + t]
            pltpu.make_async_copy(y_ref.at[pl.ds(src, 1)], buf.at[slot, k, pl.ds(t, 1)],
                                  sem.at[slot]).start(priority=k % 2)

    def wait_half(slot):
        for k in range(TOP_K):
            pltpu.make_async_copy(y_ref.at[pl.ds(0, th)], buf.at[slot, k], sem.at[slot]).wait()

    rc = 8
    cw = min(512, W)

    def consume(slot, next_first_token, next_slot):
        def rows_body(r, carry):
            rows = pl.ds(pl.multiple_of(r * rc, rc), rc)
            brow = rows
            wv = w_ref[rows, :]
            wk = [jnp.broadcast_to(wv[:, k:k + 1], (rc, cw)) for k in range(TOP_K)]
            for c in range(W // cw):
                lo = hi = None
                for k in range(TOP_K):
                    u = buf[slot, k, brow, c * cw:(c + 1) * cw]
                    l = wk[k] * lax.bitcast_convert_type(jnp.left_shift(u, jnp.uint32(16)), F32)
                    h = wk[k] * lax.bitcast_convert_type(jnp.bitwise_and(u, jnp.uint32(0xFFFF0000)), F32)
                    lo = l if lo is None else lo + l
                    hi = h if hi is None else hi + h
                for half, acc in ((0, lo), (1, hi)):
                    cols = slice(half * W + c * cw, half * W + (c + 1) * cw)
                    o_ref[rows, cols] = x_ref[rows, cols] + mod_ref[gate_idx:gate_idx + 1, cols] * acc
            for t in range(rc):
                issue_row(next_first_token, next_slot, r * rc + t)
            return carry

        lax.fori_loop(0, th // rc, rows_body, 0)

    @pl.when(i == 0)
    def _():
        for b in range(2):
            lax.fori_loop(0, tm, lambda t, c: (issue_row(b * tm, b, t), c)[1], 0, unroll=8)

    wait_half(i % ring)
    consume(i % ring, jnp.minimum((i + 2) * tm, T - tm), (i + 2) % ring)

    @pl.when(i == n - 1)
    def _():
        wait_half((i + 1) % ring)
        wait_half((i + 2) % ring)


def _combine(pos_flat, x1, w_tk, mod, y, S, gate_idx):
    T, D = x1.shape
    tm = min(128, S)
    assert T // tm >= 2
    grid_spec = pltpu.PrefetchScalarGridSpec(
        num_scalar_prefetch=1,
        grid=(T // tm,),
        in_specs=[
            pl.BlockSpec((tm, D), lambda i, p: (i, 0)),
            pl.BlockSpec((tm, TOP_K), lambda i, p: (i, 0)),
            pl.BlockSpec((None, N_ADA, D), lambda i, p: ((i * tm) // S, 0, 0)),
            pl.BlockSpec(memory_space=pl.ANY),
        ],
        out_specs=pl.BlockSpec((tm, D), lambda i, p: (i, 0)),
        scratch_shapes=[pltpu.VMEM((3, TOP_K, tm, D // 2), U32), pltpu.SemaphoreType.DMA((3,))],
    )
    return pl.pallas_call(
        functools.partial(_combine_kernel, tm=tm, T=T, gate_idx=gate_idx),
        out_shape=jax.ShapeDtypeStruct((T, D), F32),
        grid_spec=grid_spec,
        compiler_params=_cparams(("arbitrary",)),
        name="combine",
    )(pos_flat, x1, w_tk, mod, y)


def _layer(x2, mod, B, S, rel_bias, norm_mix_w, w_in, q_norm_w, k_norm_w, attn_sink, ret_decay_fwd, ret_decay_bwd,
           ret_gn_w, ret_gn_b, w_up_attn, w_up_ret, w_out, norm_ffn_w, router_w, router_b,
           expert_w1, expert_b1, expert_w2, expert_b2):
    T, D = x2.shape
    E = router_w.shape[1]
    h = _norm_mod(x2, norm_mix_w, mod, S, 0, 1)
    proj = _in_proj(h, w_in, q_norm_w, k_norm_w)
    attn = _window_attention(proj, _attn_bias_table(rel_bias), attn_sink, B, S)
    ret = _retention(proj, ret_decay_fwd, ret_decay_bwd, ret_gn_w, ret_gn_b, B, S)
    merged = _merge_up(attn, ret, w_up_attn, w_up_ret, proj, D)
    x1 = _out_proj(merged, w_out, x2, mod, S, 2)

    hp, top_e, top_w, rank, cnt = _norm_router(x1, norm_ffn_w, mod, router_w, router_b, S, 3, 4)
    tm = MOE_TILE
    counts = cnt[:, 0].astype(I32)
    tiles_per_e = (counts + tm - 1) // tm
    tile_end = jnp.cumsum(tiles_per_e)
    group_start = (tile_end - tiles_per_e) * tm
    n_tiles = (T * TOP_K) // tm + E
    n_used = tile_end[-1]
    tile_ids = jnp.minimum(jnp.arange(n_tiles, dtype=I32), n_used - 1)
    tile_e = jnp.minimum(jnp.sum((tile_ids[:, None] >= tile_end[None, :]).astype(I32), axis=1), E - 1)
    e_ids = jnp.arange(E, dtype=I32)
    later = jnp.logical_and(e_ids[None, :] > e_ids[:, None], tiles_per_e[None, :] > 0)
    next_of_e = jnp.min(jnp.where(later, e_ids[None, :], E), axis=1)
    next_of_e = jnp.where(next_of_e == E, -1, next_of_e)
    next_e = jnp.sum(jnp.where(tile_e[:, None] == e_ids, next_of_e, 0), axis=1).astype(I32)
    all_tiles = jnp.arange(n_tiles, dtype=I32)
    in_e = tile_e[:, None] == e_ids
    rows_left = jnp.sum(jnp.where(in_e, counts + group_start, 0), axis=1) - all_tiles * tm
    tile_rows = jnp.where(all_tiles < n_used, jnp.clip(rows_left, 0, tm), 0).astype(I32)
    start_of = jnp.sum(jnp.where(top_e[:, :, None] == e_ids, group_start, 0), axis=-1)
    pos_flat = (start_of + rank).astype(I32).reshape(-1)
    pad_info = jnp.concatenate([group_start + counts, tiles_per_e * tm - counts, n_used.reshape(1)]).astype(I32)

    xs = _dispatch(pos_flat, pad_info, hp, n_tiles, E)
    y = _expert_ffn(xs, tile_e, n_used.reshape(1).astype(I32), next_e, tile_rows, expert_w1, expert_b1,
                    expert_w2, expert_b2)
    return _combine(pos_flat, x1, top_w.T, mod, y, S, 5)


def kernel(x, c, rel_bias, ada_w, ada_b, norm_mix_w, w_in, q_norm_w, k_norm_w, attn_sink, ret_decay_fwd,
           ret_decay_bwd, ret_gn_w, ret_gn_b, w_up_attn, w_up_ret, w_out, norm_ffn_w, router_w, router_b,
           expert_w1, expert_b1, expert_w2, expert_b2):
    B, S, D = x.shape
    x2 = x.reshape(B * S, D)
    for l in range(ada_w.shape[0]):
        mod = _ada_mod(c, ada_w[l], ada_b[l])
        x2 = _layer(x2, mod, B, S, rel_bias, norm_mix_w[l], w_in[l], q_norm_w[l], k_norm_w[l], attn_sink[l],
                    ret_decay_fwd[l], ret_decay_bwd[l], ret_gn_w[l], ret_gn_b[l], w_up_attn[l], w_up_ret[l],
                    w_out[l], norm_ffn_w[l], router_w[l], router_b[l], expert_w1[l], expert_b1[l],
                    expert_w2[l], expert_b2[l])
    return x2.reshape(B, S, D)
```

```python
import functools
import math

import jax
import jax.numpy as jnp
from jax import lax
from jax.experimental import pallas as pl
from jax.experimental.pallas import tpu as pltpu

F32 = jnp.float32
BF16 = jnp.bfloat16
U32 = jnp.uint32
I32 = jnp.int32

ATTN_HEADS = 16
ATTN_KV_HEADS = 4
ATTN_HEAD_DIM = 128
WINDOW = 128
ATTN_BLOCK = 128
N_BUCKETS = 32
MAX_DISTANCE = 128
RET_HEADS = 8
RET_QK_DIM = 128
RET_V_DIM = 256
RET_CHUNK = 128
ROPE_BASE = 10000.0
TOP_K = 4
SWIGLU_LIMIT = 7.0
SWIGLU_ALPHA = 1.702
N_ADA = 6
EPS = 1e-6
NEG_INF = -1e30
NEG_BIG = -3.0e38
LOG2E = 1.4426950408889634

ATTN_Q_W = ATTN_HEADS * ATTN_HEAD_DIM
ATTN_KV_W = ATTN_KV_HEADS * ATTN_HEAD_DIM
RET_QK_W = RET_HEADS * RET_QK_DIM
RET_V_W = RET_HEADS * RET_V_DIM

VMEM_LIMIT_BYTES = 56 * 1024 * 1024
MOE_TILE = 512
RET_TILE = 256


def _cparams(sem):
    return pltpu.CompilerParams(dimension_semantics=sem, vmem_limit_bytes=VMEM_LIMIT_BYTES)


def _dot(a, b):
    return jnp.dot(a, b, preferred_element_type=F32)


def _dot_nt(a, b):
    return lax.dot_general(a, b, (((1,), (1,)), ((), ())), preferred_element_type=F32)


def _dot_tn(a, b):
    return lax.dot_general(a, b, (((0,), (0,)), ((), ())), preferred_element_type=F32)


def _sigmoid(x):
    return 1.0 / (1.0 + jnp.exp(-x))


def _ada_kernel(c_ref, w_ref, b_ref, o_ref):
    c = c_ref[...]
    cs = (c * _sigmoid(c)).astype(BF16)
    o_ref[...] = _dot(cs, w_ref[...].astype(BF16)) + b_ref[...]


def _ada_mod(c, ada_w, ada_b):
    B, D = c.shape
    N = ada_w.shape[1]
    rows = 8
    cp = jnp.zeros((rows, D), F32).at[:B].set(c)
    tn = min(1024, N)
    out = pl.pallas_call(
        _ada_kernel,
        out_shape=jax.ShapeDtypeStruct((rows, N), F32),
        grid=(N // tn,),
        in_specs=[
            pl.BlockSpec((rows, D), lambda j: (0, 0)),
            pl.BlockSpec((D, tn), lambda j: (0, j)),
            pl.BlockSpec((1, tn), lambda j: (0, j)),
        ],
        out_specs=pl.BlockSpec((rows, tn), lambda j: (0, j)),
        compiler_params=_cparams(("arbitrary",)),
        name="ada_mod",
    )(cp, ada_w, ada_b.reshape(1, N))
    return out[:B].reshape(B, N_ADA, D)


def _norm_mod_value(x, nw, shift, scale):
    ms = jnp.mean(x * x, axis=-1, keepdims=True)
    h = x * lax.rsqrt(ms + EPS) * nw
    return h * (1.0 + scale) + shift


def _norm_mod_kernel(x_ref, nw_ref, mod_ref, o_ref, *, shift_idx, scale_idx):
    h = _norm_mod_value(x_ref[...], nw_ref[...], mod_ref[shift_idx:shift_idx + 1, :],
                        mod_ref[scale_idx:scale_idx + 1, :])
    o_ref[...] = h.astype(BF16)


def _norm_mod(x2, nw, mod, S, shift_idx, scale_idx):
    T, D = x2.shape
    tm = min(512, S)
    return pl.pallas_call(
        functools.partial(_norm_mod_kernel, shift_idx=shift_idx, scale_idx=scale_idx),
        out_shape=jax.ShapeDtypeStruct((T, D), BF16),
        grid=(T // tm,),
        in_specs=[
            pl.BlockSpec((tm, D), lambda i: (i, 0)),
            pl.BlockSpec((1, D), lambda i: (0, 0)),
            pl.BlockSpec((None, N_ADA, D), lambda i: ((i * tm) // S, 0, 0)),
        ],
        out_specs=pl.BlockSpec((tm, D), lambda i: (i, 0)),
        compiler_params=_cparams(("arbitrary",)),
        name="norm_mod",
    )(x2, nw.reshape(1, D), mod)


def _stream_weight_tile(w_hbm, stage, wb, sem):
    j = pl.program_id(0)
    tn = wb.shape[1]

    def w_copy(jj):
        return pltpu.make_async_copy(w_hbm.at[:, pl.ds(pl.multiple_of(jj * tn, tn), tn)], stage, sem)

    @pl.when(pl.program_id(1) == 0)
    def _():
        @pl.when(j == 0)
        def _():
            w_copy(0).start()

        w_copy(j).wait()
        _cast_rows(stage, wb)

        @pl.when(j + 1 < pl.num_programs(0))
        def _():
            w_copy(j + 1).start()


def _in_proj_kernel(a_ref, w_hbm, cw_ref, cf_ref, bd_ref, o_ref, stage, wb, sem, *, n_norm_tiles):
    j = pl.program_id(0)
    tn = wb.shape[1]
    _stream_weight_tile(w_hbm, stage, wb, sem)

    @pl.when(j < n_norm_tiles)
    def _():
        cb = bd_ref.shape[0]
        for c in range(tn // cb):
            cols = slice(c * cb, (c + 1) * cb)
            acc = _dot(a_ref[...], wb[:, cols])
            ms = _dot((acc * acc).astype(BF16), bd_ref[...]) * (1.0 / ATTN_HEAD_DIM)
            inv = jnp.where(cf_ref[:, cols] > 0.0, lax.rsqrt(ms + EPS), 1.0)
            o_ref[:, cols] = (acc * inv * cw_ref[:, cols]).astype(o_ref.dtype)

    @pl.when(j >= n_norm_tiles)
    def _():
        o_ref[...] = _dot(a_ref[...], wb[...]).astype(o_ref.dtype)


def _in_proj(h, w, q_norm_w, k_norm_w):
    T, K = h.shape
    N = w.shape[1]
    tm = min(1024, T)
    tn = 1024 if N % 1024 == 0 else 512
    hd = ATTN_HEAD_DIM
    n_norm = ATTN_Q_W + ATTN_KV_W
    n_norm_tiles = -(-n_norm // tn)
    pad = n_norm_tiles * tn - n_norm
    col_w = jnp.concatenate([jnp.tile(q_norm_w.astype(F32) * (hd ** -0.5 * LOG2E), ATTN_HEADS),
                             jnp.tile(k_norm_w.astype(F32), ATTN_KV_HEADS), jnp.ones((pad,), F32)]).reshape(1, -1)
    col_flag = jnp.concatenate([jnp.ones((n_norm,), F32), jnp.zeros((pad,), F32)]).reshape(1, -1)
    cb = min(512, tn)
    lane_head = jnp.arange(cb, dtype=I32) // hd
    block_ones = (lane_head[:, None] == lane_head[None, :]).astype(BF16)
    last = n_norm_tiles - 1
    return pl.pallas_call(
        functools.partial(_in_proj_kernel, n_norm_tiles=n_norm_tiles),
        out_shape=jax.ShapeDtypeStruct((T, N), BF16),
        grid=(N // tn, T // tm),
        in_specs=[
            pl.BlockSpec((tm, K), lambda j, i: (i, 0)),
            pl.BlockSpec(memory_space=pl.ANY),
            pl.BlockSpec((1, tn), lambda j, i: (0, jnp.minimum(j, last))),
            pl.BlockSpec((1, tn), lambda j, i: (0, jnp.minimum(j, last))),
            pl.BlockSpec((cb, cb), lambda j, i: (0, 0)),
        ],
        out_specs=pl.BlockSpec((tm, tn), lambda j, i: (i, j)),
        scratch_shapes=[pltpu.VMEM((K, tn), F32), pltpu.VMEM((K, tn), BF16), pltpu.SemaphoreType.DMA],
        compiler_params=_cparams(("arbitrary", "arbitrary")),
        name="in_proj",
    )(h, w, col_w, col_flag, block_ones)


def _t5_bucket(rel):
    nb = N_BUCKETS // 2
    max_exact = nb // 2
    base = jnp.where(rel > 0, nb, 0)
    n = jnp.abs(rel)
    nf = jnp.maximum(n, 1).astype(F32)
    large = max_exact + (jnp.log(nf / max_exact) / math.log(MAX_DISTANCE / max_exact) * (nb - max_exact)).astype(I32)
    large = jnp.minimum(large, nb - 1)
    return base + jnp.where(n < max_exact, n, large)


def _attn_bias_table(rel_bias):
    blk = ATTN_BLOCK
    qi = jnp.arange(blk, dtype=I32)[:, None]
    kj = jnp.arange(3 * blk, dtype=I32)[None, :]
    rel = kj - blk - qi
    bucket = _t5_bucket(rel)
    table = rel_bias.astype(F32).T
    bias = jnp.zeros((table.shape[0],) + bucket.shape, F32)
    for b in range(N_BUCKETS):
        bias = jnp.where(bucket[None] == b, table[:, b][:, None, None], bias)
    bias = jnp.where((jnp.abs(rel) <= WINDOW)[None], bias * LOG2E, NEG_INF)
    G = ATTN_HEADS // ATTN_KV_HEADS
    bias = bias.reshape(ATTN_KV_HEADS, G, blk, 3 * blk)
    bias = jnp.transpose(bias, (0, 3, 1, 2)).reshape(ATTN_KV_HEADS, 3 * blk, G * blk)
    key = jnp.arange(3 * blk, dtype=I32)[None, :, None]
    no_prev, no_next = key < blk, key >= 2 * blk
    return jnp.stack([bias, jnp.where(no_prev, NEG_INF, bias), jnp.where(no_next, NEG_INF, bias),
                      jnp.where(jnp.logical_or(no_prev, no_next), NEG_INF, bias)])


def _attn_kernel(sink_ref, q_ref, kp_ref, kc_ref, kn_ref, vp_ref, vc_ref, vn_ref, bias_ref, o_ref):
    blk, hd = ATTN_BLOCK, ATTN_HEAD_DIM
    G = ATTN_HEADS // ATTN_KV_HEADS
    qb = q_ref.shape[0] // blk
    n = pl.program_id(1)
    last = pl.num_programs(1) - 1
    for s in range(qb):
        first = jnp.logical_and(n == 0, s == 0) if s == 0 else False
        final = jnp.logical_and(n == last, s == qb - 1) if s == qb - 1 else False
        var = jnp.where(first, 1, 0) + jnp.where(final, 2, 0)
        rows = slice(s * blk, (s + 1) * blk)
        for hk in range(ATTN_KV_HEADS):
            sl = slice(hk * hd, (hk + 1) * hd)
            kwin = [kp_ref[:, sl]] + [kc_ref[j * blk:(j + 1) * blk, sl] for j in range(qb)] + [kn_ref[:, sl]]
            vwin = [vp_ref[:, sl]] + [vc_ref[j * blk:(j + 1) * blk, sl] for j in range(qb)] + [vn_ref[:, sl]]
            k3 = jnp.concatenate(kwin[s:s + 3], axis=0)
            v3 = jnp.concatenate(vwin[s:s + 3], axis=0)
            q4 = jnp.concatenate([q_ref[rows, (hk * G + g) * hd:(hk * G + g + 1) * hd] for g in range(G)],
                                 axis=0)
            sink = jnp.concatenate([jnp.full((1, blk), sink_ref[hk * G + g], F32) for g in range(G)], axis=1)
            logits = _dot_nt(k3, q4) + bias_ref[var, hk]
            m = jnp.maximum(jnp.max(logits, axis=0, keepdims=True), sink)
            p = jnp.exp2(logits - m)
            denom = jnp.sum(p, axis=0, keepdims=True) + jnp.exp2(sink - m)
            out_t = _dot_tn(v3, p.astype(BF16)) * (1.0 / denom)
            out = out_t.T
            for g in range(G):
                h = hk * G + g
                o_ref[rows, h * hd:(h + 1) * hd] = out[g * blk:(g + 1) * blk].astype(o_ref.dtype)


def _window_attention(proj, bias_tab, sink, B, S):
    T = proj.shape[0]
    blk = ATTN_BLOCK
    nb = S // blk
    qb = 2 if nb % 2 == 0 else 1
    ns = nb // qb
    kcol = ATTN_Q_W // ATTN_KV_W
    vcol = kcol + 1

    def cur_spec(col):
        return pl.BlockSpec((qb * blk, ATTN_KV_W), lambda b, n, s: (b * ns + n, col))

    def side_spec(col, off):
        return pl.BlockSpec((blk, ATTN_KV_W), lambda b, n, s: (b * nb + jnp.clip(qb * n + off, 0, nb - 1), col))

    grid_spec = pltpu.PrefetchScalarGridSpec(
        num_scalar_prefetch=1,
        grid=(B, ns),
        in_specs=[
            pl.BlockSpec((qb * blk, ATTN_Q_W), lambda b, n, s: (b * ns + n, 0)),
            side_spec(kcol, -1), cur_spec(kcol), side_spec(kcol, qb),
            side_spec(vcol, -1), cur_spec(vcol), side_spec(vcol, qb),
            pl.BlockSpec(bias_tab.shape, lambda b, n, s: (0, 0, 0, 0)),
        ],
        out_specs=pl.BlockSpec((qb * blk, ATTN_Q_W), lambda b, n, s: (b * ns + n, 0)),
    )
    return pl.pallas_call(
        _attn_kernel,
        out_shape=jax.ShapeDtypeStruct((T, ATTN_Q_W), BF16),
        grid_spec=grid_spec,
        compiler_params=_cparams(("arbitrary", "arbitrary")),
        name="window_attn",
    )(sink.astype(F32) * LOG2E, proj, proj, proj, proj, proj, proj, proj, bias_tab)


def _rope_tables(S):
    d = RET_QK_DIM
    inv = ROPE_BASE ** (-jnp.arange(0, d, 2, dtype=F32) / d)
    ang = jnp.arange(S, dtype=F32)[:, None] * inv[None, :]
    cos, sin = jnp.cos(ang), jnp.sin(ang)
    return jnp.concatenate([cos, cos], axis=-1), jnp.concatenate([-sin, sin], axis=-1)


def _ret_kernel(df_ref, db_ref, q_ref, k_ref, v_ref, g_ref, cos_ref, sin_ref, gw_ref, gb_ref, o_ref,
                qb, kb, q2, kv, sprev, *, S, C):
    h = pl.program_id(1)
    nc = S // C
    dk = RET_QK_DIM
    half = dk // 2

    rowf = lax.broadcasted_iota(I32, (C, 1), 0).astype(F32)
    lg_f = -jnp.exp(jnp.full((1, 1), df_ref[h], F32))
    lg_b = -jnp.exp(jnp.full((1, 1), db_ref[h], F32))
    qdec_f = jnp.exp((rowf + 1.0) * lg_f)
    kdec_f = jnp.exp((C - 1.0 - rowf) * lg_f)
    qdec_b = jnp.exp((C - rowf) * lg_b)
    kdec_b = jnp.exp(rowf * lg_b)
    cd_f = jnp.exp(C * lg_f)
    cd_b = jnp.exp(C * lg_b)

    def a_body(n, carry):
        rows = pl.ds(pl.multiple_of(n * C, C), C)
        co = cos_ref[rows, :]
        si = sin_ref[rows, :]
        q = q_ref[rows, :].astype(F32)
        k = k_ref[rows, :].astype(F32)
        qr = q * co + pltpu.roll(q, half, 1) * si
        kr = (k * co + pltpu.roll(k, half, 1) * si) * (dk ** -0.5)
        qb[rows, :] = qr.astype(BF16)
        kb[rows, :] = kr.astype(BF16)
        q2[rows, :] = jnp.concatenate([qr * qdec_f, qr * qdec_b], axis=1).astype(BF16)
        k2 = jnp.concatenate([kr * kdec_f, kr * kdec_b], axis=1).astype(BF16)
        kv[n] = _dot_tn(k2, v_ref[rows, :])
        return carry

    lax.fori_loop(0, nc, a_body, 0, unroll=4)

    def scan_f(n, state):
        sprev[n, 0:dk, :] = state.astype(BF16)
        return state * cd_f + kv[n, 0:dk, :]

    lax.fori_loop(0, nc, scan_f, jnp.zeros((dk, RET_V_DIM), F32))

    def scan_b(t, state):
        n = nc - 1 - t
        sprev[n, dk:2 * dk, :] = state.astype(BF16)
        return state * cd_b + kv[n, dk:2 * dk, :]

    lax.fori_loop(0, nc, scan_b, jnp.zeros((dk, RET_V_DIM), F32))

    ri = lax.broadcasted_iota(I32, (C, C), 0)
    ci = lax.broadcasted_iota(I32, (C, C), 1)
    d = (ri - ci).astype(F32)
    dec = jnp.where(ri >= ci, jnp.exp(jnp.maximum(d, 0.0) * lg_f), jnp.exp(jnp.maximum(-d, 0.0) * lg_b))
    gw = gw_ref[...]
    gb = gb_ref[...]

    def c_body(n, carry):
        rows = pl.ds(pl.multiple_of(n * C, C), C)
        scores = _dot_nt(qb[rows, :], kb[rows, :]) * dec
        y = _dot(scores.astype(BF16), v_ref[rows, :]) + _dot(q2[rows, :], sprev[n])
        mu = jnp.mean(y, axis=-1, keepdims=True)
        yc = y - mu
        var = jnp.mean(yc * yc, axis=-1, keepdims=True)
        yn = yc * lax.rsqrt(var + EPS) * gw + gb
        g = g_ref[rows, :].astype(F32)
        o_ref[rows, :] = (g * _sigmoid(g) * yn).astype(o_ref.dtype)
        return carry

    lax.fori_loop(0, nc, c_body, 0, unroll=4)


def _retention(proj, decay_fwd, decay_bwd, gn_w, gn_b, B, S):
    T = proj.shape[0]
    dk, dv = RET_QK_DIM, RET_V_DIM
    q_off = (ATTN_Q_W + 2 * ATTN_KV_W) // dk
    k_off = q_off + RET_QK_W // dk
    v_off = (ATTN_Q_W + 2 * ATTN_KV_W + 2 * RET_QK_W) // dv
    g_off = v_off + RET_V_W // dv
    cos, sin = _rope_tables(S)
    C = min(RET_TILE, S)
    grid_spec = pltpu.PrefetchScalarGridSpec(
        num_scalar_prefetch=2,
        grid=(B, RET_HEADS),
        in_specs=[
            pl.BlockSpec((S, dk), lambda b, h, *_: (b, q_off + h)),
            pl.BlockSpec((S, dk), lambda b, h, *_: (b, k_off + h)),
            pl.BlockSpec((S, dv), lambda b, h, *_: (b, v_off + h)),
            pl.BlockSpec((S, dv), lambda b, h, *_: (b, g_off + h)),
            pl.BlockSpec((S, dk), lambda b, h, *_: (0, 0)),
            pl.BlockSpec((S, dk), lambda b, h, *_: (0, 0)),
            pl.BlockSpec((1, dv), lambda b, h, *_: (0, h)),
            pl.BlockSpec((1, dv), lambda b, h, *_: (0, h)),
        ],
        out_specs=pl.BlockSpec((S, dv), lambda b, h, *_: (b, h)),
        scratch_shapes=[
            pltpu.VMEM((S, dk), BF16),
            pltpu.VMEM((S, dk), BF16),
            pltpu.VMEM((S, 2 * dk), BF16),
            pltpu.VMEM((S // C, 2 * dk, dv), F32),
            pltpu.VMEM((S // C, 2 * dk, dv), BF16),
        ],
    )
    return pl.pallas_call(
        functools.partial(_ret_kernel, S=S, C=C),
        out_shape=jax.ShapeDtypeStruct((T, RET_V_W), BF16),
        grid_spec=grid_spec,
        compiler_params=_cparams(("arbitrary", "arbitrary")),
        name="retention",
    )(decay_fwd.astype(F32), decay_bwd.astype(F32), proj, proj, proj, proj, cos, sin,
      gn_w.reshape(1, -1), gn_b.reshape(1, -1))


def _merge_kernel(a_ref, r_ref, wa_hbm, wr_hbm, ga_ref, gr_ref, o_ref, sta, stb, wab, wrb, sem):
    _stream_weight_tile(wa_hbm, sta, wab, sem.at[0])
    _stream_weight_tile(wr_hbm, stb, wrb, sem.at[1])
    ua = _dot(a_ref[...], wab[...])
    ur = _dot(r_ref[...], wrb[...])
    out = _sigmoid(ga_ref[...].astype(F32)) * ua + _sigmoid(gr_ref[...].astype(F32)) * ur
    o_ref[...] = out.astype(o_ref.dtype)


def _merge_up(attn, ret, wa, wr, proj, D):
    T = attn.shape[0]
    tm = min(512, T)
    tn = min(1024, D)
    ga_off = (ATTN_Q_W + 2 * ATTN_KV_W + 2 * RET_QK_W + 2 * RET_V_W) // tn
    gr_off = ga_off + D // tn
    return pl.pallas_call(
        _merge_kernel,
        out_shape=jax.ShapeDtypeStruct((T, D), BF16),
        grid=(D // tn, T // tm),
        in_specs=[
            pl.BlockSpec((tm, ATTN_Q_W), lambda j, i: (i, 0)),
            pl.BlockSpec((tm, RET_V_W), lambda j, i: (i, 0)),
            pl.BlockSpec(memory_space=pl.ANY),
            pl.BlockSpec(memory_space=pl.ANY),
            pl.BlockSpec((tm, tn), lambda j, i: (i, ga_off + j)),
            pl.BlockSpec((tm, tn), lambda j, i: (i, gr_off + j)),
        ],
        out_specs=pl.BlockSpec((tm, tn), lambda j, i: (i, j)),
        scratch_shapes=[pltpu.VMEM((ATTN_Q_W, tn), F32), pltpu.VMEM((RET_V_W, tn), F32),
                        pltpu.VMEM((ATTN_Q_W, tn), BF16), pltpu.VMEM((RET_V_W, tn), BF16),
                        pltpu.SemaphoreType.DMA((2,))],
        compiler_params=_cparams(("arbitrary", "arbitrary")),
        name="merge_up",
    )(attn, ret, wa, wr, proj, proj)


def _out_proj_kernel(m_ref, w_hbm, x_ref, mod_ref, o_ref, stage, wb, sem, *, gate_idx):
    _stream_weight_tile(w_hbm, stage, wb, sem)
    y = _dot(m_ref[...], wb[...])
    o_ref[...] = x_ref[...] + mod_ref[gate_idx:gate_idx + 1, :] * y


def _out_proj(merged, w, x2, mod, S, gate_idx):
    T, D = x2.shape
    tm = min(512, S)
    tn = min(1024, D)
    return pl.pallas_call(
        functools.partial(_out_proj_kernel, gate_idx=gate_idx),
        out_shape=jax.ShapeDtypeStruct((T, D), F32),
        grid=(D // tn, T // tm),
        in_specs=[
            pl.BlockSpec((tm, D), lambda j, i: (i, 0)),
            pl.BlockSpec(memory_space=pl.ANY),
            pl.BlockSpec((tm, tn), lambda j, i: (i, j)),
            pl.BlockSpec((None, N_ADA, tn), lambda j, i: ((i * tm) // S, 0, j)),
        ],
        out_specs=pl.BlockSpec((tm, tn), lambda j, i: (i, j)),
        scratch_shapes=[pltpu.VMEM((D, tn), F32), pltpu.VMEM((D, tn), BF16), pltpu.SemaphoreType.DMA],
        compiler_params=_cparams(("arbitrary", "arbitrary")),
        name="out_proj",
    )(merged, w, x2, mod)


def _pack_bf16_pairs(v):
    n = v.shape[1] // 2
    bits = lax.bitcast_convert_type(v.astype(BF16).astype(F32), U32)
    return jnp.bitwise_or(jnp.bitwise_and(bits[:, n:], jnp.uint32(0xFFFF0000)),
                          jnp.right_shift(bits[:, :n], jnp.uint32(16)))


def _unpack_bf16_pairs(w):
    lo = lax.bitcast_convert_type(jnp.left_shift(w, jnp.uint32(16)), F32)
    hi = lax.bitcast_convert_type(jnp.bitwise_and(w, jnp.uint32(0xFFFF0000)), F32)
    return jnp.concatenate([lo, hi], axis=1)


def _norm_router_kernel(x_ref, nw_ref, mod_ref, rwt_ref, rb_ref, hp_ref, e_ref, w_ref, r_ref, cnt_ref, carry,
                        *, shift_idx, scale_idx):
    i = pl.program_id(0)

    @pl.when(i == 0)
    def _():
        carry[...] = jnp.zeros_like(carry)

    h = _norm_mod_value(x_ref[...], nw_ref[...], mod_ref[shift_idx:shift_idx + 1, :],
                        mod_ref[scale_idx:scale_idx + 1, :])
    hp_ref[...] = _pack_bf16_pairs(h)
    hb = h.astype(BF16)
    tm = hb.shape[0]
    E = rwt_ref.shape[0]
    logits = _dot_nt(rwt_ref[...], hb) + rb_ref[...]
    iota_e = lax.broadcasted_iota(I32, (E, tm), 0)
    vals, idxs = [], []
    work = logits
    sel = jnp.zeros((E, tm), F32)
    for _k in range(TOP_K):
        m = jnp.max(work, axis=0, keepdims=True)
        idx = jnp.min(jnp.where(work == m, iota_e, E), axis=0, keepdims=True)
        hit = iota_e == idx
        vals.append(m)
        idxs.append(idx)
        work = jnp.where(hit, NEG_BIG, work)
        sel = sel + hit.astype(F32)
    ex = [jnp.exp(v - vals[0]) for v in vals]
    tot = ex[0]
    for v in ex[1:]:
        tot = tot + v
    ri = lax.broadcasted_iota(I32, (tm, tm), 0)
    ci = lax.broadcasted_iota(I32, (tm, tm), 1)
    upper = (ri < ci).astype(BF16)
    before = _dot(sel.astype(BF16), upper) + carry[:, 0:1]
    for k in range(TOP_K):
        e_ref[k:k + 1, :] = idxs[k]
        w_ref[k:k + 1, :] = ex[k] / tot
        r_ref[k:k + 1, :] = jnp.sum(jnp.where(iota_e == idxs[k], before, 0.0), axis=0, keepdims=True).astype(I32)
    carry[...] = carry[...] + jnp.sum(sel, axis=1, keepdims=True)
    cnt_ref[...] = carry[...]


def _norm_router(x1, nw, mod, router_w, router_b, S, shift_idx, scale_idx):
    T, D = x1.shape
    E = router_w.shape[1]
    tm = min(512, S)
    outs = pl.pallas_call(
        functools.partial(_norm_router_kernel, shift_idx=shift_idx, scale_idx=scale_idx),
        out_shape=(
            jax.ShapeDtypeStruct((T, D // 2), U32),
            jax.ShapeDtypeStruct((TOP_K, T), I32),
            jax.ShapeDtypeStruct((TOP_K, T), F32),
            jax.ShapeDtypeStruct((TOP_K, T), I32),
            jax.ShapeDtypeStruct((E, 128), F32),
        ),
        grid=(T // tm,),
        in_specs=[
            pl.BlockSpec((tm, D), lambda i: (i, 0)),
            pl.BlockSpec((1, D), lambda i: (0, 0)),
            pl.BlockSpec((None, N_ADA, D), lambda i: ((i * tm) // S, 0, 0)),
            pl.BlockSpec((E, D), lambda i: (0, 0)),
            pl.BlockSpec((E, 1), lambda i: (0, 0)),
        ],
        out_specs=(
            pl.BlockSpec((tm, D // 2), lambda i: (i, 0)),
            pl.BlockSpec((TOP_K, tm), lambda i: (0, i)),
            pl.BlockSpec((TOP_K, tm), lambda i: (0, i)),
            pl.BlockSpec((TOP_K, tm), lambda i: (0, i)),
            pl.BlockSpec((E, 128), lambda i: (0, 0)),
        ),
        scratch_shapes=[pltpu.VMEM((E, 128), F32)],
        compiler_params=_cparams(("arbitrary",)),
        name="norm_router",
    )(x1, nw.reshape(1, D), mod, router_w.T.astype(BF16), router_b.reshape(E, 1).astype(F32))
    return outs


def _dispatch_kernel(pos_ref, pad_ref, hp_ref, xs_ref, zeros, sem, zsem, *, tm, T, E, n_tiles):
    base = pl.program_id(0) * tm

    @pl.when(pl.program_id(0) == 0)
    def _():
        zeros[...] = jnp.zeros_like(zeros)
        def pad_copy(row):
            return pltpu.make_async_copy(zeros.at[pl.ds(0, 1)], xs_ref.at[pl.ds(row, 1)], zsem)

        def tail_copy(t):
            return pltpu.make_async_copy(
                zeros, xs_ref.at[pl.ds(pl.multiple_of(t * MOE_TILE, MOE_TILE), MOE_TILE)], zsem)

        def pad8_copy(row):
            return pltpu.make_async_copy(zeros.at[pl.ds(0, 8)], xs_ref.at[pl.ds(pl.multiple_of(row, 8), 8)], zsem)

        def pad_pieces(e):
            start, count = pad_ref[e], pad_ref[E + e]
            head = jnp.minimum(count, jnp.bitwise_and(-start, 7))
            return start, head, start + head, (count - head) // 8

        def pad_start(e, carry):
            start, head, body, n8 = pad_pieces(e)
            lax.fori_loop(0, head, lambda r, c: (pad_copy(start + r).start(), c)[1], 0)
            lax.fori_loop(0, n8, lambda r, c: (pad8_copy(body + 8 * r).start(), c)[1], 0)
            return carry

        def pad_wait(e, carry):
            start, head, body, n8 = pad_pieces(e)
            lax.fori_loop(0, head, lambda r, c: (pad_copy(start + r).wait(), c)[1], 0)
            lax.fori_loop(0, n8, lambda r, c: (pad8_copy(body + 8 * r).wait(), c)[1], 0)
            return carry

        def tail_start(t, carry):
            tail_copy(t).start()
            return carry

        def tail_wait(t, carry):
            tail_copy(t).wait()
            return carry

        lax.fori_loop(0, E, pad_start, 0)
        lax.fori_loop(pad_ref[2 * E], n_tiles, tail_start, 0)
        lax.fori_loop(0, E, pad_wait, 0)
        lax.fori_loop(pad_ref[2 * E], n_tiles, tail_wait, 0)

    def start_body(t, carry):
        for k in range(TOP_K):
            dst = pos_ref[k * T + base + t]
            pltpu.make_async_copy(hp_ref.at[pl.ds(t, 1)], xs_ref.at[pl.ds(dst, 1)], sem).start(priority=k % 2)
        return carry

    lax.fori_loop(0, tm, start_body, 0, unroll=8)
    for k in range(TOP_K):
        pltpu.make_async_copy(hp_ref, xs_ref.at[pl.ds(0, tm)], sem).wait()


def _dispatch(pos_flat, pad_info, hp, n_tiles, E):
    T, W = hp.shape
    tm = min(512, T)
    grid_spec = pltpu.PrefetchScalarGridSpec(
        num_scalar_prefetch=2,
        grid=(T // tm,),
        in_specs=[pl.BlockSpec((tm, W), lambda i, p, q: (i, 0))],
        out_specs=pl.BlockSpec(memory_space=pl.ANY),
        scratch_shapes=[pltpu.VMEM((MOE_TILE, W), U32), pltpu.SemaphoreType.DMA, pltpu.SemaphoreType.DMA],
    )
    return pl.pallas_call(
        functools.partial(_dispatch_kernel, tm=tm, T=T, E=E, n_tiles=n_tiles),
        out_shape=jax.ShapeDtypeStruct((n_tiles * MOE_TILE, W), U32),
        grid_spec=grid_spec,
        compiler_params=_cparams(("arbitrary",)),
        name="dispatch",
    )(pos_flat, pad_info, hp)


def _expert_changed(te_ref, i):
    return jnp.logical_or(i == 0, te_ref[i] != te_ref[jnp.maximum(i - 1, 0)])


def _cast_rows(src_ref, dst_ref, rows_per_pass=256):
    def body(r, carry):
        rows = pl.ds(pl.multiple_of(r * rows_per_pass, rows_per_pass), rows_per_pass)
        dst_ref[rows, :] = src_ref[rows, :].astype(dst_ref.dtype)
        return carry

    lax.fori_loop(0, src_ref.shape[0] // rows_per_pass, body, 0)


def _expert_weight_stage(te_ref, nx_ref, i, first_step, n, n_passes, copies, consume):
    @pl.when(_expert_changed(te_ref, i))
    def _():
        @pl.when(first_step)
        def _():
            for cp in copies(te_ref[0], 0):
                cp.start()

        for cp in copies(te_ref[i], n):
            cp.wait()
        consume()
        same_pass = nx_ref[i] >= 0
        nxt_e = jnp.where(same_pass, nx_ref[i], te_ref[0])
        nxt_n = jnp.where(same_pass, n, n + 1)

        @pl.when(jnp.logical_or(same_pass, n + 1 < n_passes))
        def _():
            for cp in copies(nxt_e, nxt_n):
                cp.start()


def _expert_up_kernel(te_ref, nu_ref, nx_ref, tr_ref, xs_ref, w1_hbm, bg_ref, bu_ref, o_ref, stage, wgb, wub, sem,
                      *, tn, F, nf):
    n = pl.program_id(0)
    i = pl.program_id(1)

    def copies(e, nn):
        c0 = pl.multiple_of(nn * tn, tn)
        return (pltpu.make_async_copy(w1_hbm.at[e, :, pl.ds(c0, tn)], stage.at[0], sem),
                pltpu.make_async_copy(w1_hbm.at[e, :, pl.ds(F + c0, tn)], stage.at[1], sem))

    def consume():
        _cast_rows(stage.at[0], wgb)
        _cast_rows(stage.at[1], wub)

    _expert_weight_stage(te_ref, nx_ref, i, jnp.logical_and(n == 0, i == 0), n, nf, copies, consume)

    def ffn_up(rows):
        x = _unpack_bf16_pairs(xs_ref[rows, :]).astype(BF16)
        gate = jnp.minimum(_dot(x, wgb[...]) + bg_ref[...], SWIGLU_LIMIT)
        up = jnp.clip(_dot(x, wub[...]) + bu_ref[...], -SWIGLU_LIMIT, SWIGLU_LIMIT)
        act = gate * _sigmoid(SWIGLU_ALPHA * gate) * (up + 1.0)
        o_ref[rows, :] = act.astype(o_ref.dtype)

    _per_tile_rows(tr_ref[i], o_ref, ffn_up)


def _per_tile_rows(valid_rows, o_ref, compute):
    tm = o_ref.shape[0]
    half = tm // 2

    @pl.when(valid_rows > half)
    def _():
        compute(slice(0, tm))

    @pl.when(jnp.logical_and(valid_rows > 0, valid_rows <= half))
    def _():
        compute(slice(0, half))
        o_ref[half:, :] = jnp.zeros((tm - half, o_ref.shape[1]), o_ref.dtype)

    @pl.when(valid_rows <= 0)
    def _():
        o_ref[...] = jnp.zeros_like(o_ref)


def _expert_down_kernel(te_ref, nu_ref, nx_ref, tr_ref, a_ref, w2_hbm, b2_ref, o_ref, stage, w2b, sem):
    i = pl.program_id(0)

    def copies(e, nn):
        del nn
        return (pltpu.make_async_copy(w2_hbm.at[e], stage, sem),)

    def consume():
        _cast_rows(stage, w2b)

    _expert_weight_stage(te_ref, nx_ref, i, i == 0, 0, 1, copies, consume)

    def ffn_down(rows):
        y = _dot(a_ref[rows, :], w2b[...]) + b2_ref[...]
        o_ref[rows, :] = _pack_bf16_pairs(y)

    _per_tile_rows(tr_ref[i], o_ref, ffn_down)


def _expert_ffn(xs, tile_e, n_used, next_e, tile_rows, w1, b1, w2, b2):
    R, W = xs.shape
    E, D, F2 = w1.shape
    F = F2 // 2
    tm = MOE_TILE
    nt = R // tm

    tn = min(512, F)
    nf = F // tn
    b1r = b1.reshape(E, 1, F2)

    act = pl.pallas_call(
        functools.partial(_expert_up_kernel, tn=tn, F=F, nf=nf),
        out_shape=jax.ShapeDtypeStruct((R, F), BF16),
        grid_spec=pltpu.PrefetchScalarGridSpec(
            num_scalar_prefetch=4,
            grid=(nf, nt),
            in_specs=[
                pl.BlockSpec((tm, W), lambda n, i, te, nu, nx, tr: (jnp.minimum(i, nu[0] - 1), 0)),
                pl.BlockSpec(memory_space=pl.ANY),
                pl.BlockSpec((None, 1, tn), lambda n, i, te, nu, nx, tr: (te[i], 0, n)),
                pl.BlockSpec((None, 1, tn), lambda n, i, te, nu, nx, tr: (te[i], 0, nf + n)),
            ],
            out_specs=pl.BlockSpec((tm, tn), lambda n, i, te, nu, nx, tr: (i, n)),
            scratch_shapes=[pltpu.VMEM((2, D, tn), F32), pltpu.VMEM((D, tn), BF16), pltpu.VMEM((D, tn), BF16),
                            pltpu.SemaphoreType.DMA],
        ),
        compiler_params=_cparams(("arbitrary", "arbitrary")),
        name="expert_up",
    )(tile_e, n_used, next_e, tile_rows, xs, w1, b1r, b1r)
    y = pl.pallas_call(
        _expert_down_kernel,
        out_shape=jax.ShapeDtypeStruct((R, D // 2), U32),
        grid_spec=pltpu.PrefetchScalarGridSpec(
            num_scalar_prefetch=4,
            grid=(nt,),
            in_specs=[
                pl.BlockSpec((tm, F), lambda i, te, nu, nx, tr: (jnp.minimum(i, nu[0] - 1), 0)),
                pl.BlockSpec(memory_space=pl.ANY),
                pl.BlockSpec((None, 1, D), lambda i, te, nu, nx, tr: (te[i], 0, 0)),
            ],
            out_specs=pl.BlockSpec((tm, D // 2), lambda i, te, nu, nx, tr: (i, 0)),
            scratch_shapes=[pltpu.VMEM((F, D), F32), pltpu.VMEM((F, D), BF16), pltpu.SemaphoreType.DMA],
        ),
        compiler_params=_cparams(("arbitrary",)),
        name="expert_down",
    )(tile_e, n_used, next_e, tile_rows, act, w2, b2.reshape(E, 1, D))
    return y


def _combine_kernel(pos_ref, x_ref, w_ref, mod_ref, y_ref, o_ref, buf, sem, *, tm, T, gate_idx):
    i = pl.program_id(0)
    n = pl.num_programs(0)
    W = buf.shape[-1]
    th = tm
    ring = buf.shape[0]

    def issue_row(first_token, slot, t):
        for k in range(TOP_K):
            src = pos_ref[k * T + first_token + t]
            pltpu.make_async_copy(y_ref.at[pl.ds(src, 1)], buf.at[slot, k, pl.ds(t, 1)],
                                  sem.at[slot]).start(priority=k % 2)

    def wait_half(slot):
        for k in range(TOP_K):
            pltpu.make_async_copy(y_ref.at[pl.ds(0, th)], buf.at[slot, k], sem.at[slot]).wait()

    rc = 8
    cw = min(512, W)

    def consume(slot, next_first_token, next_slot):
        def rows_body(r, carry):
            rows = pl.ds(pl.multiple_of(r * rc, rc), rc)
            brow = rows
            wv = w_ref[rows, :]
            wk = [jnp.broadcast_to(wv[:, k:k + 1], (rc, cw)) for k in range(TOP_K)]
            for c in range(W // cw):
                lo = hi = None
                for k in range(TOP_K):
                    u = buf[slot, k, brow, c * cw:(c + 1) * cw]
                    l = wk[k] * lax.bitcast_convert_type(jnp.left_shift(u, jnp.uint32(16)), F32)
                    h = wk[k] * lax.bitcast_convert_type(jnp.bitwise_and(u, jnp.uint32(0xFFFF0000)), F32)
                    lo = l if lo is None else lo + l
                    hi = h if hi is None else hi + h
                for half, acc in ((0, lo), (1, hi)):
                    cols = slice(half * W + c * cw, half * W + (c + 1) * cw)
                    o_ref[rows, cols] = x_ref[rows, cols] + mod_ref[gate_idx:gate_idx + 1, cols] * acc
            for t in range(rc):
                issue_row(next_first_token, next_slot, r * rc + t)
            return carry

        lax.fori_loop(0, th // rc, rows_body, 0)

    @pl.when(i == 0)
    def _():
        for b in range(2):
            lax.fori_loop(0, tm, lambda t, c: (issue_row(b * tm, b, t), c)[1], 0, unroll=8)

    wait_half(i % ring)
    consume(i % ring, jnp.minimum((i + 2) * tm, T - tm), (i + 2) % ring)

    @pl.when(i == n - 1)
    def _():
        wait_half((i + 1) % ring)
        wait_half((i + 2) % ring)


def _combine(pos_flat, x1, w_tk, mod, y, S, gate_idx):
    T, D = x1.shape
    tm = min(128, S)
    assert T // tm >= 2
    grid_spec = pltpu.PrefetchScalarGridSpec(
        num_scalar_prefetch=1,
        grid=(T // tm,),
        in_specs=[
            pl.BlockSpec((tm, D), lambda i, p: (i, 0)),
            pl.BlockSpec((tm, TOP_K), lambda i, p: (i, 0)),
            pl.BlockSpec((None, N_ADA, D), lambda i, p: ((i * tm) // S, 0, 0)),
            pl.BlockSpec(memory_space=pl.ANY),
        ],
        out_specs=pl.BlockSpec((tm, D), lambda i, p: (i, 0)),
        scratch_shapes=[pltpu.VMEM((3, TOP_K, tm, D // 2), U32), pltpu.SemaphoreType.DMA((3,))],
    )
    return pl.pallas_call(
        functools.partial(_combine_kernel, tm=tm, T=T, gate_idx=gate_idx),
        out_shape=jax.ShapeDtypeStruct((T, D), F32),
        grid_spec=grid_spec,
        compiler_params=_cparams(("arbitrary",)),
        name="combine",
    )(pos_flat, x1, w_tk, mod, y)


def _layer(x2, mod, B, S, rel_bias, norm_mix_w, w_in, q_norm_w, k_norm_w, attn_sink, ret_decay_fwd, ret_decay_bwd,
           ret_gn_w, ret_gn_b, w_up_attn, w_up_ret, w_out, norm_ffn_w, router_w, router_b,
           expert_w1, expert_b1, expert_w2, expert_b2):
    T, D = x2.shape
    E = router_w.shape[1]
    h = _norm_mod(x2, norm_mix_w, mod, S, 0, 1)
    proj = _in_proj(h, w_in, q_norm_w, k_norm_w)
    attn = _window_attention(proj, _attn_bias_table(rel_bias), attn_sink, B, S)
    ret = _retention(proj, ret_decay_fwd, ret_decay_bwd, ret_gn_w, ret_gn_b, B, S)
    merged = _merge_up(attn, ret, w_up_attn, w_up_ret, proj, D)
    x1 = _out_proj(merged, w_out, x2, mod, S, 2)

    hp, top_e, top_w, rank, cnt = _norm_router(x1, norm_ffn_w, mod, router_w, router_b, S, 3, 4)
    tm = MOE_TILE
    counts = cnt[:, 0].astype(I32)
    tiles_per_e = (counts + tm - 1) // tm
    tile_end = jnp.cumsum(tiles_per_e)
    group_start = (tile_end - tiles_per_e) * tm
    n_tiles = (T * TOP_K) // tm + E
    n_used = tile_end[-1]
    tile_ids = jnp.minimum(jnp.arange(n_tiles, dtype=I32), n_used - 1)
    tile_e = jnp.minimum(jnp.sum((tile_ids[:, None] >= tile_end[None, :]).astype(I32), axis=1), E - 1)
    e_ids = jnp.arange(E, dtype=I32)
    later = jnp.logical_and(e_ids[None, :] > e_ids[:, None], tiles_per_e[None, :] > 0)
    next_of_e = jnp.min(jnp.where(later, e_ids[None, :], E), axis=1)
    next_of_e = jnp.where(next_of_e == E, -1, next_of_e)
    next_e = jnp.sum(jnp.where(tile_e[:, None] == e_ids, next_of_e, 0), axis=1).astype(I32)
    all_tiles = jnp.arange(n_tiles, dtype=I32)
    in_e = tile_e[:, None] == e_ids
    rows_left = jnp.sum(jnp.where(in_e, counts + group_start, 0), axis=1) - all_tiles * tm
    tile_rows = jnp.where(all_tiles < n_used, jnp.clip(rows_left, 0, tm), 0).astype(I32)
    start_of = jnp.sum(jnp.where(top_e[:, :, None] == e_ids, group_start, 0), axis=-1)
    pos_flat = (start_of + rank).astype(I32).reshape(-1)
    pad_info = jnp.concatenate([group_start + counts, tiles_per_e * tm - counts, n_used.reshape(1)]).astype(I32)

    xs = _dispatch(pos_flat, pad_info, hp, n_tiles, E)
    y = _expert_ffn(xs, tile_e, n_used.reshape(1).astype(I32), next_e, tile_rows, expert_w1, expert_b1,
                    expert_w2, expert_b2)
    return _combine(pos_flat, x1, top_w.T, mod, y, S, 5)


def kernel(x, c, rel_bias, ada_w, ada_b, norm_mix_w, w_in, q_norm_w, k_norm_w, attn_sink, ret_decay_fwd,
           ret_decay_bwd, ret_gn_w, ret_gn_b, w_up_attn, w_up_ret, w_out, norm_ffn_w, router_w, router_b,
           expert_w1, expert_b1, expert_w2, expert_b2):
    B, S, D = x.shape
    x2 = x.reshape(B * S, D)
    for l in range(ada_w.shape[0]):
        mod = _ada_mod(c, ada_w[l], ada_b[l])
        x2 = _layer(x2, mod, B, S, rel_bias, norm_mix_w[l], w_in[l], q_norm_w[l], k_norm_w[l], attn_sink[l],
                    ret_decay_fwd[l], ret_decay_bwd[l], ret_gn_w[l], ret_gn_b[l], w_up_attn[l], w_up_ret[l],
                    w_out[l], norm_ffn_w[l], router_w[l], router_b[l], expert_w1[l], expert_b1[l],
                    expert_w2[l], expert_b2[l])
    return x2.reshape(B, S, D)
```

```python
import functools
import math

import jax
import jax.numpy as jnp
from jax import lax
from jax.experimental import pallas as pl
from jax.experimental.pallas import tpu as pltpu

F32 = jnp.float32
BF16 = jnp.bfloat16
U32 = jnp.uint32
I32 = jnp.int32

ATTN_HEADS = 16
ATTN_KV_HEADS = 4
ATTN_HEAD_DIM = 128
WINDOW = 128
ATTN_BLOCK = 128
N_BUCKETS = 32
MAX_DISTANCE = 128
RET_HEADS = 8
RET_QK_DIM = 128
RET_V_DIM = 256
RET_CHUNK = 128
ROPE_BASE = 10000.0
TOP_K = 4
SWIGLU_LIMIT = 7.0
SWIGLU_ALPHA = 1.702
N_ADA = 6
EPS = 1e-6
NEG_INF = -1e30
NEG_BIG = -3.0e38
LOG2E = 1.4426950408889634

ATTN_Q_W = ATTN_HEADS * ATTN_HEAD_DIM
ATTN_KV_W = ATTN_KV_HEADS * ATTN_HEAD_DIM
RET_QK_W = RET_HEADS * RET_QK_DIM
RET_V_W = RET_HEADS * RET_V_DIM

VMEM_LIMIT_BYTES = 56 * 1024 * 1024
MOE_TILE = 512
RET_TILE = 256


def _cparams(sem):
    return pltpu.CompilerParams(dimension_semantics=sem, vmem_limit_bytes=VMEM_LIMIT_BYTES)


def _dot(a, b):
    return jnp.dot(a, b, preferred_element_type=F32)


def _dot_nt(a, b):
    return lax.dot_general(a, b, (((1,), (1,)), ((), ())), preferred_element_type=F32)


def _dot_tn(a, b):
    return lax.dot_general(a, b, (((0,), (0,)), ((), ())), preferred_element_type=F32)


def _sigmoid(x):
    return 1.0 / (1.0 + jnp.exp(-x))


def _ada_kernel(c_ref, w_ref, b_ref, o_ref):
    c = c_ref[...]
    cs = (c * _sigmoid(c)).astype(BF16)
    o_ref[...] = _dot(cs, w_ref[...].astype(BF16)) + b_ref[...]


def _ada_mod(c, ada_w, ada_b):
    B, D = c.shape
    N = ada_w.shape[1]
    rows = 8
    cp = jnp.zeros((rows, D), F32).at[:B].set(c)
    tn = min(512, N)
    out = pl.pallas_call(
        _ada_kernel,
        out_shape=jax.ShapeDtypeStruct((rows, N), F32),
        grid=(N // tn,),
        in_specs=[
            pl.BlockSpec((rows, D), lambda j: (0, 0)),
            pl.BlockSpec((D, tn), lambda j: (0, j)),
            pl.BlockSpec((1, tn), lambda j: (0, j)),
        ],
        out_specs=pl.BlockSpec((rows, tn), lambda j: (0, j)),
        compiler_params=_cparams(("arbitrary",)),
        name="ada_mod",
    )(cp, ada_w, ada_b.reshape(1, N))
    return out[:B].reshape(B, N_ADA, D)


def _norm_mod_value(x, nw, shift, scale):
    ms = jnp.mean(x * x, axis=-1, keepdims=True)
    h = x * lax.rsqrt(ms + EPS) * nw
    return h * (1.0 + scale) + shift


def _norm_mod_kernel(x_ref, nw_ref, mod_ref, o_ref, *, shift_idx, scale_idx):
    h = _norm_mod_value(x_ref[...], nw_ref[...], mod_ref[shift_idx:shift_idx + 1, :],
                        mod_ref[scale_idx:scale_idx + 1, :])
    o_ref[...] = h.astype(BF16)


def _norm_mod(x2, nw, mod, S, shift_idx, scale_idx):
    T, D = x2.shape
    tm = min(512, S)
    return pl.pallas_call(
        functools.partial(_norm_mod_kernel, shift_idx=shift_idx, scale_idx=scale_idx),
        out_shape=jax.ShapeDtypeStruct((T, D), BF16),
        grid=(T // tm,),
        in_specs=[
            pl.BlockSpec((tm, D), lambda i: (i, 0)),
            pl.BlockSpec((1, D), lambda i: (0, 0)),
            pl.BlockSpec((None, N_ADA, D), lambda i: ((i * tm) // S, 0, 0)),
        ],
        out_specs=pl.BlockSpec((tm, D), lambda i: (i, 0)),
        compiler_params=_cparams(("arbitrary",)),
        name="norm_mod",
    )(x2, nw.reshape(1, D), mod)


def _stream_weight_tile(w_hbm, stage, wb, sem):
    j = pl.program_id(0)
    tn = wb.shape[1]

    def w_copy(jj):
        return pltpu.make_async_copy(w_hbm.at[:, pl.ds(pl.multiple_of(jj * tn, tn), tn)], stage, sem)

    @pl.when(pl.program_id(1) == 0)
    def _():
        @pl.when(j == 0)
        def _():
            w_copy(0).start()

        w_copy(j).wait()
        _cast_rows(stage, wb)

        @pl.when(j + 1 < pl.num_programs(0))
        def _():
            w_copy(j + 1).start()


def _in_proj_kernel(a_ref, w_hbm, cw_ref, cf_ref, bd_ref, o_ref, stage, wb, sem, *, n_norm_tiles):
    j = pl.program_id(0)
    tn = wb.shape[1]
    _stream_weight_tile(w_hbm, stage, wb, sem)

    @pl.when(j < n_norm_tiles)
    def _():
        cb = bd_ref.shape[0]
        for c in range(tn // cb):
            cols = slice(c * cb, (c + 1) * cb)
            acc = _dot(a_ref[...], wb[:, cols])
            ms = _dot((acc * acc).astype(BF16), bd_ref[...]) * (1.0 / ATTN_HEAD_DIM)
            inv = jnp.where(cf_ref[:, cols] > 0.0, lax.rsqrt(ms + EPS), 1.0)
            o_ref[:, cols] = (acc * inv * cw_ref[:, cols]).astype(o_ref.dtype)

    @pl.when(j >= n_norm_tiles)
    def _():
        o_ref[...] = _dot(a_ref[...], wb[...]).astype(o_ref.dtype)


def _in_proj(h, w, q_norm_w, k_norm_w):
    T, K = h.shape
    N = w.shape[1]
    tm = min(1024, T)
    tn = 1024 if N % 1024 == 0 else 512
    hd = ATTN_HEAD_DIM
    n_norm = ATTN_Q_W + ATTN_KV_W
    n_norm_tiles = -(-n_norm // tn)
    pad = n_norm_tiles * tn - n_norm
    col_w = jnp.concatenate([jnp.tile(q_norm_w.astype(F32) * (hd ** -0.5 * LOG2E), ATTN_HEADS),
                             jnp.tile(k_norm_w.astype(F32), ATTN_KV_HEADS), jnp.ones((pad,), F32)]).reshape(1, -1)
    col_flag = jnp.concatenate([jnp.ones((n_norm,), F32), jnp.zeros((pad,), F32)]).reshape(1, -1)
    cb = min(512, tn)
    lane_head = jnp.arange(cb, dtype=I32) // hd
    block_ones = (lane_head[:, None] == lane_head[None, :]).astype(BF16)
    last = n_norm_tiles - 1
    return pl.pallas_call(
        functools.partial(_in_proj_kernel, n_norm_tiles=n_norm_tiles),
        out_shape=jax.ShapeDtypeStruct((T, N), BF16),
        grid=(N // tn, T // tm),
        in_specs=[
            pl.BlockSpec((tm, K), lambda j, i: (i, 0)),
            pl.BlockSpec(memory_space=pl.ANY),
            pl.BlockSpec((1, tn), lambda j, i: (0, jnp.minimum(j, last))),
            pl.BlockSpec((1, tn), lambda j, i: (0, jnp.minimum(j, last))),
            pl.BlockSpec((cb, cb), lambda j, i: (0, 0)),
        ],
        out_specs=pl.BlockSpec((tm, tn), lambda j, i: (i, j)),
        scratch_shapes=[pltpu.VMEM((K, tn), F32), pltpu.VMEM((K, tn), BF16), pltpu.SemaphoreType.DMA],
        compiler_params=_cparams(("arbitrary", "arbitrary")),
        name="in_proj",
    )(h, w, col_w, col_flag, block_ones)


def _t5_bucket(rel):
    nb = N_BUCKETS // 2
    max_exact = nb // 2
    base = jnp.where(rel > 0, nb, 0)
    n = jnp.abs(rel)
    nf = jnp.maximum(n, 1).astype(F32)
    large = max_exact + (jnp.log(nf / max_exact) / math.log(MAX_DISTANCE / max_exact) * (nb - max_exact)).astype(I32)
    large = jnp.minimum(large, nb - 1)
    return base + jnp.where(n < max_exact, n, large)


def _attn_bias_table(rel_bias):
    blk = ATTN_BLOCK
    qi = jnp.arange(blk, dtype=I32)[:, None]
    kj = jnp.arange(3 * blk, dtype=I32)[None, :]
    rel = kj - blk - qi
    bucket = _t5_bucket(rel)
    table = rel_bias.astype(F32).T
    bias = jnp.zeros((table.shape[0],) + bucket.shape, F32)
    for b in range(N_BUCKETS):
        bias = jnp.where(bucket[None] == b, table[:, b][:, None, None], bias)
    bias = jnp.where((jnp.abs(rel) <= WINDOW)[None], bias * LOG2E, NEG_INF)
    G = ATTN_HEADS // ATTN_KV_HEADS
    bias = bias.reshape(ATTN_KV_HEADS, G, blk, 3 * blk)
    bias = jnp.transpose(bias, (0, 3, 1, 2)).reshape(ATTN_KV_HEADS, 3 * blk, G * blk)
    key = jnp.arange(3 * blk, dtype=I32)[None, :, None]
    no_prev, no_next = key < blk, key >= 2 * blk
    return jnp.stack([bias, jnp.where(no_prev, NEG_INF, bias), jnp.where(no_next, NEG_INF, bias),
                      jnp.where(jnp.logical_or(no_prev, no_next), NEG_INF, bias)])


def _attn_kernel(sink_ref, q_ref, kp_ref, kc_ref, kn_ref, vp_ref, vc_ref, vn_ref, bias_ref, o_ref):
    blk, hd = ATTN_BLOCK, ATTN_HEAD_DIM
    G = ATTN_HEADS // ATTN_KV_HEADS
    qb = q_ref.shape[0] // blk
    n = pl.program_id(1)
    last = pl.num_programs(1) - 1
    for s in range(qb):
        first = jnp.logical_and(n == 0, s == 0) if s == 0 else False
        final = jnp.logical_and(n == last, s == qb - 1) if s == qb - 1 else False
        var = jnp.where(first, 1, 0) + jnp.where(final, 2, 0)
        rows = slice(s * blk, (s + 1) * blk)
        for hk in range(ATTN_KV_HEADS):
            sl = slice(hk * hd, (hk + 1) * hd)
            kwin = [kp_ref[:, sl]] + [kc_ref[j * blk:(j + 1) * blk, sl] for j in range(qb)] + [kn_ref[:, sl]]
            vwin = [vp_ref[:, sl]] + [vc_ref[j * blk:(j + 1) * blk, sl] for j in range(qb)] + [vn_ref[:, sl]]
            k3 = jnp.concatenate(kwin[s:s + 3], axis=0)
            v3 = jnp.concatenate(vwin[s:s + 3], axis=0)
            q4 = jnp.concatenate([q_ref[rows, (hk * G + g) * hd:(hk * G + g + 1) * hd] for g in range(G)],
                                 axis=0)
            sink = jnp.concatenate([jnp.full((1, blk), sink_ref[hk * G + g], F32) for g in range(G)], axis=1)
            logits = _dot_nt(k3, q4) + bias_ref[var, hk]
            m = jnp.maximum(jnp.max(logits, axis=0, keepdims=True), sink)
            p = jnp.exp2(logits - m)
            denom = jnp.sum(p, axis=0, keepdims=True) + jnp.exp2(sink - m)
            out_t = _dot_tn(v3, p.astype(BF16)) * (1.0 / denom)
            out = out_t.T
            for g in range(G):
                h = hk * G + g
                o_ref[rows, h * hd:(h + 1) * hd] = out[g * blk:(g + 1) * blk].astype(o_ref.dtype)


def _window_attention(proj, bias_tab, sink, B, S):
    T = proj.shape[0]
    blk = ATTN_BLOCK
    nb = S // blk
    qb = next(c for c in (4, 2, 1) if nb % c == 0)
    ns = nb // qb
    kcol = ATTN_Q_W // ATTN_KV_W
    vcol = kcol + 1

    def cur_spec(col):
        return pl.BlockSpec((qb * blk, ATTN_KV_W), lambda b, n, s: (b * ns + n, col))

    def side_spec(col, off):
        return pl.BlockSpec((blk, ATTN_KV_W), lambda b, n, s: (b * nb + jnp.clip(qb * n + off, 0, nb - 1), col))

    grid_spec = pltpu.PrefetchScalarGridSpec(
        num_scalar_prefetch=1,
        grid=(B, ns),
        in_specs=[
            pl.BlockSpec((qb * blk, ATTN_Q_W), lambda b, n, s: (b * ns + n, 0)),
            side_spec(kcol, -1), cur_spec(kcol), side_spec(kcol, qb),
            side_spec(vcol, -1), cur_spec(vcol), side_spec(vcol, qb),
            pl.BlockSpec(bias_tab.shape, lambda b, n, s: (0, 0, 0, 0)),
        ],
        out_specs=pl.BlockSpec((qb * blk, ATTN_Q_W), lambda b, n, s: (b * ns + n, 0)),
    )
    return pl.pallas_call(
        _attn_kernel,
        out_shape=jax.ShapeDtypeStruct((T, ATTN_Q_W), BF16),
        grid_spec=grid_spec,
        compiler_params=_cparams(("arbitrary", "arbitrary")),
        name="window_attn",
    )(sink.astype(F32) * LOG2E, proj, proj, proj, proj, proj, proj, proj, bias_tab)


def _rope_tables(S):
    d = RET_QK_DIM
    inv = ROPE_BASE ** (-jnp.arange(0, d, 2, dtype=F32) / d)
    ang = jnp.arange(S, dtype=F32)[:, None] * inv[None, :]
    cos, sin = jnp.cos(ang), jnp.sin(ang)
    return jnp.concatenate([cos, cos], axis=-1), jnp.concatenate([-sin, sin], axis=-1)


def _ret_kernel(df_ref, db_ref, q_ref, k_ref, v_ref, g_ref, cos_ref, sin_ref, gw_ref, gb_ref, o_ref,
                qb, kb, q2, kv, sprev, *, S, C):
    h = pl.program_id(1)
    nc = S // C
    dk = RET_QK_DIM
    half = dk // 2

    rowf = lax.broadcasted_iota(I32, (C, 1), 0).astype(F32)
    lg_f = -jnp.exp(jnp.full((1, 1), df_ref[h], F32))
    lg_b = -jnp.exp(jnp.full((1, 1), db_ref[h], F32))
    qdec_f = jnp.exp((rowf + 1.0) * lg_f)
    kdec_f = jnp.exp((C - 1.0 - rowf) * lg_f)
    qdec_b = jnp.exp((C - rowf) * lg_b)
    kdec_b = jnp.exp(rowf * lg_b)
    cd_f = jnp.exp(C * lg_f)
    cd_b = jnp.exp(C * lg_b)

    def a_body(n, carry):
        rows = pl.ds(pl.multiple_of(n * C, C), C)
        co = cos_ref[rows, :]
        si = sin_ref[rows, :]
        q = q_ref[rows, :].astype(F32)
        k = k_ref[rows, :].astype(F32)
        qr = q * co + pltpu.roll(q, half, 1) * si
        kr = (k * co + pltpu.roll(k, half, 1) * si) * (dk ** -0.5)
        qb[rows, :] = qr.astype(BF16)
        kb[rows, :] = kr.astype(BF16)
        q2[rows, :] = jnp.concatenate([qr * qdec_f, qr * qdec_b], axis=1).astype(BF16)
        k2 = jnp.concatenate([kr * kdec_f, kr * kdec_b], axis=1).astype(BF16)
        kv[n] = _dot_tn(k2, v_ref[rows, :])
        return carry

    lax.fori_loop(0, nc, a_body, 0, unroll=4)

    def scan_f(n, state):
        sprev[n, 0:dk, :] = state.astype(BF16)
        return state * cd_f + kv[n, 0:dk, :]

    lax.fori_loop(0, nc, scan_f, jnp.zeros((dk, RET_V_DIM), F32))

    def scan_b(t, state):
        n = nc - 1 - t
        sprev[n, dk:2 * dk, :] = state.astype(BF16)
        return state * cd_b + kv[n, dk:2 * dk, :]

    lax.fori_loop(0, nc, scan_b, jnp.zeros((dk, RET_V_DIM), F32))

    ri = lax.broadcasted_iota(I32, (C, C), 0)
    ci = lax.broadcasted_iota(I32, (C, C), 1)
    d = (ri - ci).astype(F32)
    dec = jnp.where(ri >= ci, jnp.exp(jnp.maximum(d, 0.0) * lg_f), jnp.exp(jnp.maximum(-d, 0.0) * lg_b))
    gw = gw_ref[...]
    gb = gb_ref[...]

    def c_body(n, carry):
        rows = pl.ds(pl.multiple_of(n * C, C), C)
        scores = _dot_nt(qb[rows, :], kb[rows, :]) * dec
        y = _dot(scores.astype(BF16), v_ref[rows, :]) + _dot(q2[rows, :], sprev[n])
        mu = jnp.mean(y, axis=-1, keepdims=True)
        yc = y - mu
        var = jnp.mean(yc * yc, axis=-1, keepdims=True)
        yn = yc * lax.rsqrt(var + EPS) * gw + gb
        g = g_ref[rows, :].astype(F32)
        o_ref[rows, :] = (g * _sigmoid(g) * yn).astype(o_ref.dtype)
        return carry

    lax.fori_loop(0, nc, c_body, 0, unroll=4)


def _retention(proj, decay_fwd, decay_bwd, gn_w, gn_b, B, S):
    T = proj.shape[0]
    dk, dv = RET_QK_DIM, RET_V_DIM
    q_off = (ATTN_Q_W + 2 * ATTN_KV_W) // dk
    k_off = q_off + RET_QK_W // dk
    v_off = (ATTN_Q_W + 2 * ATTN_KV_W + 2 * RET_QK_W) // dv
    g_off = v_off + RET_V_W // dv
    cos, sin = _rope_tables(S)
    C = min(RET_TILE, S)
    grid_spec = pltpu.PrefetchScalarGridSpec(
        num_scalar_prefetch=2,
        grid=(B, RET_HEADS),
        in_specs=[
            pl.BlockSpec((S, dk), lambda b, h, *_: (b, q_off + h)),
            pl.BlockSpec((S, dk), lambda b, h, *_: (b, k_off + h)),
            pl.BlockSpec((S, dv), lambda b, h, *_: (b, v_off + h)),
            pl.BlockSpec((S, dv), lambda b, h, *_: (b, g_off + h)),
            pl.BlockSpec((S, dk), lambda b, h, *_: (0, 0)),
            pl.BlockSpec((S, dk), lambda b, h, *_: (0, 0)),
            pl.BlockSpec((1, dv), lambda b, h, *_: (0, h)),
            pl.BlockSpec((1, dv), lambda b, h, *_: (0, h)),
        ],
        out_specs=pl.BlockSpec((S, dv), lambda b, h, *_: (b, h)),
        scratch_shapes=[
            pltpu.VMEM((S, dk), BF16),
            pltpu.VMEM((S, dk), BF16),
            pltpu.VMEM((S, 2 * dk), BF16),
            pltpu.VMEM((S // C, 2 * dk, dv), F32),
            pltpu.VMEM((S // C, 2 * dk, dv), BF16),
        ],
    )
    return pl.pallas_call(
        functools.partial(_ret_kernel, S=S, C=C),
        out_shape=jax.ShapeDtypeStruct((T, RET_V_W), BF16),
        grid_spec=grid_spec,
        compiler_params=_cparams(("arbitrary", "arbitrary")),
        name="retention",
    )(decay_fwd.astype(F32), decay_bwd.astype(F32), proj, proj, proj, proj, cos, sin,
      gn_w.reshape(1, -1), gn_b.reshape(1, -1))


def _merge_kernel(a_ref, r_ref, wa_hbm, wr_hbm, ga_ref, gr_ref, o_ref, sta, stb, wab, wrb, sem):
    _stream_weight_tile(wa_hbm, sta, wab, sem.at[0])
    _stream_weight_tile(wr_hbm, stb, wrb, sem.at[1])
    ua = _dot(a_ref[...], wab[...])
    ur = _dot(r_ref[...], wrb[...])
    out = _sigmoid(ga_ref[...].astype(F32)) * ua + _sigmoid(gr_ref[...].astype(F32)) * ur
    o_ref[...] = out.astype(o_ref.dtype)


def _merge_up(attn, ret, wa, wr, proj, D):
    T = attn.shape[0]
    tm = min(512, T)
    tn = min(1024, D)
    ga_off = (ATTN_Q_W + 2 * ATTN_KV_W + 2 * RET_QK_W + 2 * RET_V_W) // tn
    gr_off = ga_off + D // tn
    return pl.pallas_call(
        _merge_kernel,
        out_shape=jax.ShapeDtypeStruct((T, D), BF16),
        grid=(D // tn, T // tm),
        in_specs=[
            pl.BlockSpec((tm, ATTN_Q_W), lambda j, i: (i, 0)),
            pl.BlockSpec((tm, RET_V_W), lambda j, i: (i, 0)),
            pl.BlockSpec(memory_space=pl.ANY),
            pl.BlockSpec(memory_space=pl.ANY),
            pl.BlockSpec((tm, tn), lambda j, i: (i, ga_off + j)),
            pl.BlockSpec((tm, tn), lambda j, i: (i, gr_off + j)),
        ],
        out_specs=pl.BlockSpec((tm, tn), lambda j, i: (i, j)),
        scratch_shapes=[pltpu.VMEM((ATTN_Q_W, tn), F32), pltpu.VMEM((RET_V_W, tn), F32),
                        pltpu.VMEM((ATTN_Q_W, tn), BF16), pltpu.VMEM((RET_V_W, tn), BF16),
                        pltpu.SemaphoreType.DMA((2,))],
        compiler_params=_cparams(("arbitrary", "arbitrary")),
        name="merge_up",
    )(attn, ret, wa, wr, proj, proj)


def _out_proj_kernel(m_ref, w_hbm, x_ref, mod_ref, o_ref, stage, wb, sem, *, gate_idx):
    _stream_weight_tile(w_hbm, stage, wb, sem)
    y = _dot(m_ref[...], wb[...])
    o_ref[...] = x_ref[...] + mod_ref[gate_idx:gate_idx + 1, :] * y


def _out_proj(merged, w, x2, mod, S, gate_idx):
    T, D = x2.shape
    tm = min(512, S)
    tn = min(1024, D)
    return pl.pallas_call(
        functools.partial(_out_proj_kernel, gate_idx=gate_idx),
        out_shape=jax.ShapeDtypeStruct((T, D), F32),
        grid=(D // tn, T // tm),
        in_specs=[
            pl.BlockSpec((tm, D), lambda j, i: (i, 0)),
            pl.BlockSpec(memory_space=pl.ANY),
            pl.BlockSpec((tm, tn), lambda j, i: (i, j)),
            pl.BlockSpec((None, N_ADA, tn), lambda j, i: ((i * tm) // S, 0, j)),
        ],
        out_specs=pl.BlockSpec((tm, tn), lambda j, i: (i, j)),
        scratch_shapes=[pltpu.VMEM((D, tn), F32), pltpu.VMEM((D, tn), BF16), pltpu.SemaphoreType.DMA],
        compiler_params=_cparams(("arbitrary", "arbitrary")),
        name="out_proj",
    )(merged, w, x2, mod)


def _pack_bf16_pairs(v):
    n = v.shape[1] // 2
    bits = lax.bitcast_convert_type(v.astype(BF16).astype(F32), U32)
    return jnp.bitwise_or(jnp.bitwise_and(bits[:, n:], jnp.uint32(0xFFFF0000)),
                          jnp.right_shift(bits[:, :n], jnp.uint32(16)))


def _unpack_bf16_pairs(w):
    lo = lax.bitcast_convert_type(jnp.left_shift(w, jnp.uint32(16)), F32)
    hi = lax.bitcast_convert_type(jnp.bitwise_and(w, jnp.uint32(0xFFFF0000)), F32)
    return jnp.concatenate([lo, hi], axis=1)


def _norm_router_kernel(x_ref, nw_ref, mod_ref, rwt_ref, rb_ref, hp_ref, e_ref, w_ref, r_ref, cnt_ref, carry,
                        *, shift_idx, scale_idx):
    i = pl.program_id(0)

    @pl.when(i == 0)
    def _():
        carry[...] = jnp.zeros_like(carry)

    h = _norm_mod_value(x_ref[...], nw_ref[...], mod_ref[shift_idx:shift_idx + 1, :],
                        mod_ref[scale_idx:scale_idx + 1, :])
    hp_ref[...] = _pack_bf16_pairs(h)
    hb = h.astype(BF16)
    tm = hb.shape[0]
    E = rwt_ref.shape[0]
    logits = _dot_nt(rwt_ref[...], hb) + rb_ref[...]
    iota_e = lax.broadcasted_iota(I32, (E, tm), 0)
    vals, idxs = [], []
    work = logits
    sel = jnp.zeros((E, tm), F32)
    for _k in range(TOP_K):
        m = jnp.max(work, axis=0, keepdims=True)
        idx = jnp.min(jnp.where(work == m, iota_e, E), axis=0, keepdims=True)
        hit = iota_e == idx
        vals.append(m)
        idxs.append(idx)
        work = jnp.where(hit, NEG_BIG, work)
        sel = sel + hit.astype(F32)
    ex = [jnp.exp(v - vals[0]) for v in vals]
    tot = ex[0]
    for v in ex[1:]:
        tot = tot + v
    ri = lax.broadcasted_iota(I32, (tm, tm), 0)
    ci = lax.broadcasted_iota(I32, (tm, tm), 1)
    upper = (ri < ci).astype(BF16)
    before = _dot(sel.astype(BF16), upper) + carry[:, 0:1]
    for k in range(TOP_K):
        e_ref[k:k + 1, :] = idxs[k]
        w_ref[k:k + 1, :] = ex[k] / tot
        r_ref[k:k + 1, :] = jnp.sum(jnp.where(iota_e == idxs[k], before, 0.0), axis=0, keepdims=True).astype(I32)
    carry[...] = carry[...] + jnp.sum(sel, axis=1, keepdims=True)
    cnt_ref[...] = carry[...]


def _norm_router(x1, nw, mod, router_w, router_b, S, shift_idx, scale_idx):
    T, D = x1.shape
    E = router_w.shape[1]
    tm = min(512, S)
    outs = pl.pallas_call(
        functools.partial(_norm_router_kernel, shift_idx=shift_idx, scale_idx=scale_idx),
        out_shape=(
            jax.ShapeDtypeStruct((T, D // 2), U32),
            jax.ShapeDtypeStruct((TOP_K, T), I32),
            jax.ShapeDtypeStruct((TOP_K, T), F32),
            jax.ShapeDtypeStruct((TOP_K, T), I32),
            jax.ShapeDtypeStruct((E, 128), F32),
        ),
        grid=(T // tm,),
        in_specs=[
            pl.BlockSpec((tm, D), lambda i: (i, 0)),
            pl.BlockSpec((1, D), lambda i: (0, 0)),
            pl.BlockSpec((None, N_ADA, D), lambda i: ((i * tm) // S, 0, 0)),
            pl.BlockSpec((E, D), lambda i: (0, 0)),
            pl.BlockSpec((E, 1), lambda i: (0, 0)),
        ],
        out_specs=(
            pl.BlockSpec((tm, D // 2), lambda i: (i, 0)),
            pl.BlockSpec((TOP_K, tm), lambda i: (0, i)),
            pl.BlockSpec((TOP_K, tm), lambda i: (0, i)),
            pl.BlockSpec((TOP_K, tm), lambda i: (0, i)),
            pl.BlockSpec((E, 128), lambda i: (0, 0)),
        ),
        scratch_shapes=[pltpu.VMEM((E, 128), F32)],
        compiler_params=_cparams(("arbitrary",)),
        name="norm_router",
    )(x1, nw.reshape(1, D), mod, router_w.T.astype(BF16), router_b.reshape(E, 1).astype(F32))
    return outs


def _dispatch_kernel(pos_ref, pad_ref, hp_ref, xs_ref, zeros, sem, zsem, *, tm, T, E, n_tiles):
    base = pl.program_id(0) * tm

    @pl.when(pl.program_id(0) == 0)
    def _():
        zeros[...] = jnp.zeros_like(zeros)
        def pad_copy(row):
            return pltpu.make_async_copy(zeros.at[pl.ds(0, 1)], xs_ref.at[pl.ds(row, 1)], zsem)

        def tail_copy(t):
            return pltpu.make_async_copy(
                zeros, xs_ref.at[pl.ds(pl.multiple_of(t * MOE_TILE, MOE_TILE), MOE_TILE)], zsem)

        def pad8_copy(row):
            return pltpu.make_async_copy(zeros.at[pl.ds(0, 8)], xs_ref.at[pl.ds(pl.multiple_of(row, 8), 8)], zsem)

        def pad_pieces(e):
            start, count = pad_ref[e], pad_ref[E + e]
            head = jnp.minimum(count, jnp.bitwise_and(-start, 7))
            return start, head, start + head, (count - head) // 8

        def pad_start(e, carry):
            start, head, body, n8 = pad_pieces(e)
            lax.fori_loop(0, head, lambda r, c: (pad_copy(start + r).start(), c)[1], 0)
            lax.fori_loop(0, n8, lambda r, c: (pad8_copy(body + 8 * r).start(), c)[1], 0)
            return carry

        def pad_wait(e, carry):
            start, head, body, n8 = pad_pieces(e)
            lax.fori_loop(0, head, lambda r, c: (pad_copy(start + r).wait(), c)[1], 0)
            lax.fori_loop(0, n8, lambda r, c: (pad8_copy(body + 8 * r).wait(), c)[1], 0)
            return carry

        def tail_start(t, carry):
            tail_copy(t).start()
            return carry

        def tail_wait(t, carry):
            tail_copy(t).wait()
            return carry

        lax.fori_loop(0, E, pad_start, 0)
        lax.fori_loop(pad_ref[2 * E], n_tiles, tail_start, 0)
        lax.fori_loop(0, E, pad_wait, 0)
        lax.fori_loop(pad_ref[2 * E], n_tiles, tail_wait, 0)

    def start_body(t, carry):
        for k in range(TOP_K):
            dst = pos_ref[k * T + base + t]
            pltpu.make_async_copy(hp_ref.at[pl.ds(t, 1)], xs_ref.at[pl.ds(dst, 1)], sem).start(priority=k % 2)
        return carry

    lax.fori_loop(0, tm, start_body, 0, unroll=8)
    for k in range(TOP_K):
        pltpu.make_async_copy(hp_ref, xs_ref.at[pl.ds(0, tm)], sem).wait()


def _dispatch(pos_flat, pad_info, hp, n_tiles, E):
    T, W = hp.shape
    tm = min(512, T)
    grid_spec = pltpu.PrefetchScalarGridSpec(
        num_scalar_prefetch=2,
        grid=(T // tm,),
        in_specs=[pl.BlockSpec((tm, W), lambda i, p, q: (i, 0))],
        out_specs=pl.BlockSpec(memory_space=pl.ANY),
        scratch_shapes=[pltpu.VMEM((MOE_TILE, W), U32), pltpu.SemaphoreType.DMA, pltpu.SemaphoreType.DMA],
    )
    return pl.pallas_call(
        functools.partial(_dispatch_kernel, tm=tm, T=T, E=E, n_tiles=n_tiles),
        out_shape=jax.ShapeDtypeStruct((n_tiles * MOE_TILE, W), U32),
        grid_spec=grid_spec,
        compiler_params=_cparams(("arbitrary",)),
        name="dispatch",
    )(pos_flat, pad_info, hp)


def _expert_changed(te_ref, i):
    return jnp.logical_or(i == 0, te_ref[i] != te_ref[jnp.maximum(i - 1, 0)])


def _cast_rows(src_ref, dst_ref, rows_per_pass=256):
    def body(r, carry):
        rows = pl.ds(pl.multiple_of(r * rows_per_pass, rows_per_pass), rows_per_pass)
        dst_ref[rows, :] = src_ref[rows, :].astype(dst_ref.dtype)
        return carry

    lax.fori_loop(0, src_ref.shape[0] // rows_per_pass, body, 0)


def _expert_weight_stage(te_ref, nx_ref, i, first_step, n, n_passes, copies, consume):
    @pl.when(_expert_changed(te_ref, i))
    def _():
        @pl.when(first_step)
        def _():
            for cp in copies(te_ref[0], 0):
                cp.start()

        for cp in copies(te_ref[i], n):
            cp.wait()
        consume()
        same_pass = nx_ref[i] >= 0
        nxt_e = jnp.where(same_pass, nx_ref[i], te_ref[0])
        nxt_n = jnp.where(same_pass, n, n + 1)

        @pl.when(jnp.logical_or(same_pass, n + 1 < n_passes))
        def _():
            for cp in copies(nxt_e, nxt_n):
                cp.start()


def _expert_up_kernel(te_ref, nu_ref, nx_ref, tr_ref, xs_ref, w1_hbm, bg_ref, bu_ref, o_ref, stage, wgb, wub, sem,
                      *, tn, F, nf):
    n = pl.program_id(0)
    i = pl.program_id(1)

    def copies(e, nn):
        c0 = pl.multiple_of(nn * tn, tn)
        return (pltpu.make_async_copy(w1_hbm.at[e, :, pl.ds(c0, tn)], stage.at[0], sem),
                pltpu.make_async_copy(w1_hbm.at[e, :, pl.ds(F + c0, tn)], stage.at[1], sem))

    def consume():
        _cast_rows(stage.at[0], wgb)
        _cast_rows(stage.at[1], wub)

    _expert_weight_stage(te_ref, nx_ref, i, jnp.logical_and(n == 0, i == 0), n, nf, copies, consume)

    def ffn_up(rows):
        x = _unpack_bf16_pairs(xs_ref[rows, :]).astype(BF16)
        gate = jnp.minimum(_dot(x, wgb[...]) + bg_ref[...], SWIGLU_LIMIT)
        up = jnp.clip(_dot(x, wub[...]) + bu_ref[...], -SWIGLU_LIMIT, SWIGLU_LIMIT)
        act = gate * _sigmoid(SWIGLU_ALPHA * gate) * (up + 1.0)
        o_ref[rows, :] = act.astype(o_ref.dtype)

    _per_tile_rows(tr_ref[i], o_ref, ffn_up)


def _per_tile_rows(valid_rows, o_ref, compute):
    tm = o_ref.shape[0]
    half = tm // 2

    @pl.when(valid_rows > half)
    def _():
        compute(slice(0, tm))

    @pl.when(jnp.logical_and(valid_rows > 0, valid_rows <= half))
    def _():
        compute(slice(0, half))
        o_ref[half:, :] = jnp.zeros((tm - half, o_ref.shape[1]), o_ref.dtype)

    @pl.when(valid_rows <= 0)
    def _():
        o_ref[...] = jnp.zeros_like(o_ref)


def _expert_down_kernel(te_ref, nu_ref, nx_ref, tr_ref, a_ref, w2_hbm, b2_ref, o_ref, stage, w2b, sem):
    i = pl.program_id(0)

    def copies(e, nn):
        del nn
        return (pltpu.make_async_copy(w2_hbm.at[e], stage, sem),)

    def consume():
        _cast_rows(stage, w2b)

    _expert_weight_stage(te_ref, nx_ref, i, i == 0, 0, 1, copies, consume)

    def ffn_down(rows):
        y = _dot(a_ref[rows, :], w2b[...]) + b2_ref[...]
        o_ref[rows, :] = _pack_bf16_pairs(y)

    _per_tile_rows(tr_ref[i], o_ref, ffn_down)


def _expert_ffn(xs, tile_e, n_used, next_e, tile_rows, w1, b1, w2, b2):
    R, W = xs.shape
    E, D, F2 = w1.shape
    F = F2 // 2
    tm = MOE_TILE
    nt = R // tm

    tn = min(512, F)
    nf = F // tn
    b1r = b1.reshape(E, 1, F2)

    act = pl.pallas_call(
        functools.partial(_expert_up_kernel, tn=tn, F=F, nf=nf),
        out_shape=jax.ShapeDtypeStruct((R, F), BF16),
        grid_spec=pltpu.PrefetchScalarGridSpec(
            num_scalar_prefetch=4,
            grid=(nf, nt),
            in_specs=[
                pl.BlockSpec((tm, W), lambda n, i, te, nu, nx, tr: (jnp.minimum(i, nu[0] - 1), 0)),
                pl.BlockSpec(memory_space=pl.ANY),
                pl.BlockSpec((None, 1, tn), lambda n, i, te, nu, nx, tr: (te[i], 0, n)),
                pl.BlockSpec((None, 1, tn), lambda n, i, te, nu, nx, tr: (te[i], 0, nf + n)),
            ],
            out_specs=pl.BlockSpec((tm, tn), lambda n, i, te, nu, nx, tr: (i, n)),
            scratch_shapes=[pltpu.VMEM((2, D, tn), F32), pltpu.VMEM((D, tn), BF16), pltpu.VMEM((D, tn), BF16),
                            pltpu.SemaphoreType.DMA],
        ),
        compiler_params=_cparams(("arbitrary", "arbitrary")),
        name="expert_up",
    )(tile_e, n_used, next_e, tile_rows, xs, w1, b1r, b1r)
    y = pl.pallas_call(
        _expert_down_kernel,
        out_shape=jax.ShapeDtypeStruct((R, D // 2), U32),
        grid_spec=pltpu.PrefetchScalarGridSpec(
            num_scalar_prefetch=4,
            grid=(nt,),
            in_specs=[
                pl.BlockSpec((tm, F), lambda i, te, nu, nx, tr: (jnp.minimum(i, nu[0] - 1), 0)),
                pl.BlockSpec(memory_space=pl.ANY),
                pl.BlockSpec((None, 1, D), lambda i, te, nu, nx, tr: (te[i], 0, 0)),
            ],
            out_specs=pl.BlockSpec((tm, D // 2), lambda i, te, nu, nx, tr: (i, 0)),
            scratch_shapes=[pltpu.VMEM((F, D), F32), pltpu.VMEM((F, D), BF16), pltpu.SemaphoreType.DMA],
        ),
        compiler_params=_cparams(("arbitrary",)),
        name="expert_down",
    )(tile_e, n_used, next_e, tile_rows, act, w2, b2.reshape(E, 1, D))
    return y


def _combine_kernel(pos_ref, x_ref, w_ref, mod_ref, y_ref, o_ref, buf, sem, *, tm, T, gate_idx):
    i = pl.program_id(0)
    n = pl.num_programs(0)
    W = buf.shape[-1]
    th = tm
    ring = buf.shape[0]

    def issue_row(first_token, slot, t):
        for k in range(TOP_K):
            src = pos_ref[k * T + first_token + t]
            pltpu.make_async_copy(y_ref.at[pl.ds(src, 1)], buf.at[slot, k, pl.ds(t, 1)],
                                  sem.at[slot]).start(priority=k % 2)

    def wait_half(slot):
        for k in range(TOP_K):
            pltpu.make_async_copy(y_ref.at[pl.ds(0, th)], buf.at[slot, k], sem.at[slot]).wait()

    rc = 8
    cw = min(512, W)

    def consume(slot, next_first_token, next_slot):
        def rows_body(r, carry):
            rows = pl.ds(pl.multiple_of(r * rc, rc), rc)
            brow = rows
            wv = w_ref[rows, :]
            wk = [jnp.broadcast_to(wv[:, k:k + 1], (rc, cw)) for k in range(TOP_K)]
            for c in range(W // cw):
                lo = hi = None
                for k in range(TOP_K):
                    u = buf[slot, k, brow, c * cw:(c + 1) * cw]
                    l = wk[k] * lax.bitcast_convert_type(jnp.left_shift(u, jnp.uint32(16)), F32)
                    h = wk[k] * lax.bitcast_convert_type(jnp.bitwise_and(u, jnp.uint32(0xFFFF0000)), F32)
                    lo = l if lo is None else lo + l
                    hi = h if hi is None else hi + h
                for half, acc in ((0, lo), (1, hi)):
                    cols = slice(half * W + c * cw, half * W + (c + 1) * cw)
                    o_ref[rows, cols] = x_ref[rows, cols] + mod_ref[gate_idx:gate_idx + 1, cols] * acc
            for t in range(rc):
                issue_row(next_first_token, next_slot, r * rc + t)
            return carry

        lax.fori_loop(0, th // rc, rows_body, 0)

    @pl.when(i == 0)
    def _():
        for b in range(2):
            lax.fori_loop(0, tm, lambda t, c: (issue_row(b * tm, b, t), c)[1], 0, unroll=8)

    wait_half(i % ring)
    consume(i % ring, jnp.minimum((i + 2) * tm, T - tm), (i + 2) % ring)

    @pl.when(i == n - 1)
    def _():
        wait_half((i + 1) % ring)
        wait_half((i + 2) % ring)


def _combine(pos_flat, x1, w_tk, mod, y, S, gate_idx):
    T, D = x1.shape
    tm = min(128, S)
    assert T // tm >= 2
    grid_spec = pltpu.PrefetchScalarGridSpec(
        num_scalar_prefetch=1,
        grid=(T // tm,),
        in_specs=[
            pl.BlockSpec((tm, D), lambda i, p: (i, 0)),
            pl.BlockSpec((tm, TOP_K), lambda i, p: (i, 0)),
            pl.BlockSpec((None, N_ADA, D), lambda i, p: ((i * tm) // S, 0, 0)),
            pl.BlockSpec(memory_space=pl.ANY),
        ],
        out_specs=pl.BlockSpec((tm, D), lambda i, p: (i, 0)),
        scratch_shapes=[pltpu.VMEM((3, TOP_K, tm, D // 2), U32), pltpu.SemaphoreType.DMA((3,))],
    )
    return pl.pallas_call(
        functools.partial(_combine_kernel, tm=tm, T=T, gate_idx=gate_idx),
        out_shape=jax.ShapeDtypeStruct((T, D), F32),
        grid_spec=grid_spec,
        compiler_params=_cparams(("arbitrary",)),
        name="combine",
    )(pos_flat, x1, w_tk, mod, y)


def _layer(x2, mod, B, S, rel_bias, norm_mix_w, w_in, q_norm_w, k_norm_w, attn_sink, ret_decay_fwd, ret_decay_bwd,
           ret_gn_w, ret_gn_b, w_up_attn, w_up_ret, w_out, norm_ffn_w, router_w, router_b,
           expert_w1, expert_b1, expert_w2, expert_b2):
    T, D = x2.shape
    E = router_w.shape[1]
    h = _norm_mod(x2, norm_mix_w, mod, S, 0, 1)
    proj = _in_proj(h, w_in, q_norm_w, k_norm_w)
    attn = _window_attention(proj, _attn_bias_table(rel_bias), attn_sink, B, S)
    ret = _retention(proj, ret_decay_fwd, ret_decay_bwd, ret_gn_w, ret_gn_b, B, S)
    merged = _merge_up(attn, ret, w_up_attn, w_up_ret, proj, D)
    x1 = _out_proj(merged, w_out, x2, mod, S, 2)

    hp, top_e, top_w, rank, cnt = _norm_router(x1, norm_ffn_w, mod, router_w, router_b, S, 3, 4)
    tm = MOE_TILE
    counts = cnt[:, 0].astype(I32)
    tiles_per_e = (counts + tm - 1) // tm
    tile_end = jnp.cumsum(tiles_per_e)
    group_start = (tile_end - tiles_per_e) * tm
    n_tiles = (T * TOP_K) // tm + E
    n_used = tile_end[-1]
    tile_ids = jnp.minimum(jnp.arange(n_tiles, dtype=I32), n_used - 1)
    tile_e = jnp.minimum(jnp.sum((tile_ids[:, None] >= tile_end[None, :]).astype(I32), axis=1), E - 1)
    e_ids = jnp.arange(E, dtype=I32)
    later = jnp.logical_and(e_ids[None, :] > e_ids[:, None], tiles_per_e[None, :] > 0)
    next_of_e = jnp.min(jnp.where(later, e_ids[None, :], E), axis=1)
    next_of_e = jnp.where(next_of_e == E, -1, next_of_e)
    next_e = jnp.sum(jnp.where(tile_e[:, None] == e_ids, next_of_e, 0), axis=1).astype(I32)
    all_tiles = jnp.arange(n_tiles, dtype=I32)
    in_e = tile_e[:, None] == e_ids
    rows_left = jnp.sum(jnp.where(in_e, counts + group_start, 0), axis=1) - all_tiles * tm
    tile_rows = jnp.where(all_tiles < n_used, jnp.clip(rows_left, 0, tm), 0).astype(I32)
    start_of = jnp.sum(jnp.where(top_e[:, :, None] == e_ids, group_start, 0), axis=-1)
    pos_flat = (start_of + rank).astype(I32).reshape(-1)
    pad_info = jnp.concatenate([group_start + counts, tiles_per_e * tm - counts, n_used.reshape(1)]).astype(I32)

    xs = _dispatch(pos_flat, pad_info, hp, n_tiles, E)
    y = _expert_ffn(xs, tile_e, n_used.reshape(1).astype(I32), next_e, tile_rows, expert_w1, expert_b1,
                    expert_w2, expert_b2)
    return _combine(pos_flat, x1, top_w.T, mod, y, S, 5)


def kernel(x, c, rel_bias, ada_w, ada_b, norm_mix_w, w_in, q_norm_w, k_norm_w, attn_sink, ret_decay_fwd,
           ret_decay_bwd, ret_gn_w, ret_gn_b, w_up_attn, w_up_ret, w_out, norm_ffn_w, router_w, router_b,
           expert_w1, expert_b1, expert_w2, expert_b2):
    B, S, D = x.shape
    x2 = x.reshape(B * S, D)
    for l in range(ada_w.shape[0]):
        mod = _ada_mod(c, ada_w[l], ada_b[l])
        x2 = _layer(x2, mod, B, S, rel_bias, norm_mix_w[l], w_in[l], q_norm_w[l], k_norm_w[l], attn_sink[l],
                    ret_decay_fwd[l], ret_decay_bwd[l], ret_gn_w[l], ret_gn_b[l], w_up_attn[l], w_up_ret[l],
                    w_out[l], norm_ffn_w[l], router_w[l], router_b[l], expert_w1[l], expert_b1[l],
                    expert_w2[l], expert_b2[l])
    return x2.reshape(B, S, D)
```

```python
import functools
import math

import jax
import jax.numpy as jnp
from jax import lax
from jax.experimental import pallas as pl
from jax.experimental.pallas import tpu as pltpu

F32 = jnp.float32
BF16 = jnp.bfloat16
U32 = jnp.uint32
I32 = jnp.int32

ATTN_HEADS = 16
ATTN_KV_HEADS = 4
ATTN_HEAD_DIM = 128
WINDOW = 128
ATTN_BLOCK = 128
N_BUCKETS = 32
MAX_DISTANCE = 128
RET_HEADS = 8
RET_QK_DIM = 128
RET_V_DIM = 256
RET_CHUNK = 128
ROPE_BASE = 10000.0
TOP_K = 4
SWIGLU_LIMIT = 7.0
SWIGLU_ALPHA = 1.702
N_ADA = 6
EPS = 1e-6
NEG_INF = -1e30
NEG_BIG = -3.0e38
LOG2E = 1.4426950408889634

ATTN_Q_W = ATTN_HEADS * ATTN_HEAD_DIM
ATTN_KV_W = ATTN_KV_HEADS * ATTN_HEAD_DIM
RET_QK_W = RET_HEADS * RET_QK_DIM
RET_V_W = RET_HEADS * RET_V_DIM

VMEM_LIMIT_BYTES = 56 * 1024 * 1024
MOE_TILE = 512
RET_TILE = 256


def _cparams(sem):
    return pltpu.CompilerParams(dimension_semantics=sem, vmem_limit_bytes=VMEM_LIMIT_BYTES)


def _dot(a, b):
    return jnp.dot(a, b, preferred_element_type=F32)


def _dot_nt(a, b):
    return lax.dot_general(a, b, (((1,), (1,)), ((), ())), preferred_element_type=F32)


def _dot_tn(a, b):
    return lax.dot_general(a, b, (((0,), (0,)), ((), ())), preferred_element_type=F32)


def _sigmoid(x):
    return 1.0 / (1.0 + jnp.exp(-x))


def _ada_kernel(c_ref, w_ref, b_ref, o_ref):
    c = c_ref[...]
    cs = (c * _sigmoid(c)).astype(BF16)
    o_ref[...] = _dot(cs, w_ref[...].astype(BF16)) + b_ref[...]


def _ada_mod(c, ada_w, ada_b):
    B, D = c.shape
    N = ada_w.shape[1]
    rows = 8
    cp = jnp.zeros((rows, D), F32).at[:B].set(c)
    tn = min(1024, N)
    out = pl.pallas_call(
        _ada_kernel,
        out_shape=jax.ShapeDtypeStruct((rows, N), F32),
        grid=(N // tn,),
        in_specs=[
            pl.BlockSpec((rows, D), lambda j: (0, 0)),
            pl.BlockSpec((D, tn), lambda j: (0, j)),
            pl.BlockSpec((1, tn), lambda j: (0, j)),
        ],
        out_specs=pl.BlockSpec((rows, tn), lambda j: (0, j)),
        compiler_params=_cparams(("arbitrary",)),
        name="ada_mod",
    )(cp, ada_w, ada_b.reshape(1, N))
    return out[:B].reshape(B, N_ADA, D)


def _norm_mod_value(x, nw, shift, scale):
    ms = jnp.mean(x * x, axis=-1, keepdims=True)
    h = x * lax.rsqrt(ms + EPS) * nw
    return h * (1.0 + scale) + shift


def _norm_mod_kernel(x_ref, nw_ref, mod_ref, o_ref, *, shift_idx, scale_idx):
    h = _norm_mod_value(x_ref[...], nw_ref[...], mod_ref[shift_idx:shift_idx + 1, :],
                        mod_ref[scale_idx:scale_idx + 1, :])
    o_ref[...] = h.astype(BF16)


def _norm_mod(x2, nw, mod, S, shift_idx, scale_idx):
    T, D = x2.shape
    tm = min(512, S)
    return pl.pallas_call(
        functools.partial(_norm_mod_kernel, shift_idx=shift_idx, scale_idx=scale_idx),
        out_shape=jax.ShapeDtypeStruct((T, D), BF16),
        grid=(T // tm,),
        in_specs=[
            pl.BlockSpec((tm, D), lambda i: (i, 0)),
            pl.BlockSpec((1, D), lambda i: (0, 0)),
            pl.BlockSpec((None, N_ADA, D), lambda i: ((i * tm) // S, 0, 0)),
        ],
        out_specs=pl.BlockSpec((tm, D), lambda i: (i, 0)),
        compiler_params=_cparams(("arbitrary",)),
        name="norm_mod",
    )(x2, nw.reshape(1, D), mod)


def _stream_weight_tile(w_hbm, stage, wb, sem):
    j = pl.program_id(0)
    tn = wb.shape[1]

    def w_copy(jj):
        return pltpu.make_async_copy(w_hbm.at[:, pl.ds(pl.multiple_of(jj * tn, tn), tn)], stage, sem)

    @pl.when(pl.program_id(1) == 0)
    def _():
        @pl.when(j == 0)
        def _():
            w_copy(0).start()

        w_copy(j).wait()
        _cast_rows(stage, wb)

        @pl.when(j + 1 < pl.num_programs(0))
        def _():
            w_copy(j + 1).start()


def _in_proj_kernel(a_ref, w_hbm, cw_ref, cf_ref, bd_ref, o_ref, stage, wb, sem, *, n_norm_tiles):
    j = pl.program_id(0)
    tn = wb.shape[1]
    _stream_weight_tile(w_hbm, stage, wb, sem)

    @pl.when(j < n_norm_tiles)
    def _():
        cb = bd_ref.shape[0]
        for c in range(tn // cb):
            cols = slice(c * cb, (c + 1) * cb)
            acc = _dot(a_ref[...], wb[:, cols])
            ms = _dot((acc * acc).astype(BF16), bd_ref[...]) * (1.0 / ATTN_HEAD_DIM)
            inv = jnp.where(cf_ref[:, cols] > 0.0, lax.rsqrt(ms + EPS), 1.0)
            o_ref[:, cols] = (acc * inv * cw_ref[:, cols]).astype(o_ref.dtype)

    @pl.when(j >= n_norm_tiles)
    def _():
        o_ref[...] = _dot(a_ref[...], wb[...]).astype(o_ref.dtype)


def _in_proj(h, w, q_norm_w, k_norm_w):
    T, K = h.shape
    N = w.shape[1]
    tm = min(1024, T)
    tn = 1024 if N % 1024 == 0 else 512
    hd = ATTN_HEAD_DIM
    n_norm = ATTN_Q_W + ATTN_KV_W
    n_norm_tiles = -(-n_norm // tn)
    pad = n_norm_tiles * tn - n_norm
    col_w = jnp.concatenate([jnp.tile(q_norm_w.astype(F32) * (hd ** -0.5 * LOG2E), ATTN_HEADS),
                             jnp.tile(k_norm_w.astype(F32), ATTN_KV_HEADS), jnp.ones((pad,), F32)]).reshape(1, -1)
    col_flag = jnp.concatenate([jnp.ones((n_norm,), F32), jnp.zeros((pad,), F32)]).reshape(1, -1)
    cb = min(512, tn)
    lane_head = jnp.arange(cb, dtype=I32) // hd
    block_ones = (lane_head[:, None] == lane_head[None, :]).astype(BF16)
    last = n_norm_tiles - 1
    return pl.pallas_call(
        functools.partial(_in_proj_kernel, n_norm_tiles=n_norm_tiles),
        out_shape=jax.ShapeDtypeStruct((T, N), BF16),
        grid=(N // tn, T // tm),
        in_specs=[
            pl.BlockSpec((tm, K), lambda j, i: (i, 0)),
            pl.BlockSpec(memory_space=pl.ANY),
            pl.BlockSpec((1, tn), lambda j, i: (0, jnp.minimum(j, last))),
            pl.BlockSpec((1, tn), lambda j, i: (0, jnp.minimum(j, last))),
            pl.BlockSpec((cb, cb), lambda j, i: (0, 0)),
        ],
        out_specs=pl.BlockSpec((tm, tn), lambda j, i: (i, j)),
        scratch_shapes=[pltpu.VMEM((K, tn), F32), pltpu.VMEM((K, tn), BF16), pltpu.SemaphoreType.DMA],
        compiler_params=_cparams(("arbitrary", "arbitrary")),
        name="in_proj",
    )(h, w, col_w, col_flag, block_ones)


def _t5_bucket(rel):
    nb = N_BUCKETS // 2
    max_exact = nb // 2
    base = jnp.where(rel > 0, nb, 0)
    n = jnp.abs(rel)
    nf = jnp.maximum(n, 1).astype(F32)
    large = max_exact + (jnp.log(nf / max_exact) / math.log(MAX_DISTANCE / max_exact) * (nb - max_exact)).astype(I32)
    large = jnp.minimum(large, nb - 1)
    return base + jnp.where(n < max_exact, n, large)


def _attn_bias_table(rel_bias):
    blk = ATTN_BLOCK
    qi = jnp.arange(blk, dtype=I32)[:, None]
    kj = jnp.arange(3 * blk, dtype=I32)[None, :]
    rel = kj - blk - qi
    bucket = _t5_bucket(rel)
    table = rel_bias.astype(F32).T
    bias = jnp.zeros((table.shape[0],) + bucket.shape, F32)
    for b in range(N_BUCKETS):
        bias = jnp.where(bucket[None] == b, table[:, b][:, None, None], bias)
    bias = jnp.where((jnp.abs(rel) <= WINDOW)[None], bias * LOG2E, NEG_INF)
    G = ATTN_HEADS // ATTN_KV_HEADS
    bias = bias.reshape(ATTN_KV_HEADS, G, blk, 3 * blk)
    bias = jnp.transpose(bias, (0, 3, 1, 2)).reshape(ATTN_KV_HEADS, 3 * blk, G * blk)
    key = jnp.arange(3 * blk, dtype=I32)[None, :, None]
    no_prev, no_next = key < blk, key >= 2 * blk
    return jnp.stack([bias, jnp.where(no_prev, NEG_INF, bias), jnp.where(no_next, NEG_INF, bias),
                      jnp.where(jnp.logical_or(no_prev, no_next), NEG_INF, bias)])


def _attn_kernel(sink_ref, q_ref, kp_ref, kc_ref, kn_ref, vp_ref, vc_ref, vn_ref, bias_ref, o_ref):
    blk, hd = ATTN_BLOCK, ATTN_HEAD_DIM
    G = ATTN_HEADS // ATTN_KV_HEADS
    qb = q_ref.shape[0] // blk
    n = pl.program_id(1)
    last = pl.num_programs(1) - 1
    for s in range(qb):
        first = jnp.logical_and(n == 0, s == 0) if s == 0 else False
        final = jnp.logical_and(n == last, s == qb - 1) if s == qb - 1 else False
        var = jnp.where(first, 1, 0) + jnp.where(final, 2, 0)
        rows = slice(s * blk, (s + 1) * blk)
        for hk in range(ATTN_KV_HEADS):
            sl = slice(hk * hd, (hk + 1) * hd)
            kwin = [kp_ref[:, sl]] + [kc_ref[j * blk:(j + 1) * blk, sl] for j in range(qb)] + [kn_ref[:, sl]]
            vwin = [vp_ref[:, sl]] + [vc_ref[j * blk:(j + 1) * blk, sl] for j in range(qb)] + [vn_ref[:, sl]]
            k3 = jnp.concatenate(kwin[s:s + 3], axis=0)
            v3 = jnp.concatenate(vwin[s:s + 3], axis=0)
            q4 = jnp.concatenate([q_ref[rows, (hk * G + g) * hd:(hk * G + g + 1) * hd] for g in range(G)],
                                 axis=0)
            sink = jnp.concatenate([jnp.full((1, blk), sink_ref[hk * G + g], F32) for g in range(G)], axis=1)
            logits = _dot_nt(k3, q4) + bias_ref[var, hk]
            m = jnp.maximum(jnp.max(logits, axis=0, keepdims=True), sink)
            p = jnp.exp2(logits - m)
            denom = jnp.sum(p, axis=0, keepdims=True) + jnp.exp2(sink - m)
            out = _dot_tn((p * (1.0 / denom)).astype(BF16), v3)
            for g in range(G):
                h = hk * G + g
                o_ref[rows, h * hd:(h + 1) * hd] = out[g * blk:(g + 1) * blk].astype(o_ref.dtype)


def _window_attention(proj, bias_tab, sink, B, S):
    T = proj.shape[0]
    blk = ATTN_BLOCK
    nb = S // blk
    qb = 2 if nb % 2 == 0 else 1
    ns = nb // qb
    kcol = ATTN_Q_W // ATTN_KV_W
    vcol = kcol + 1

    def cur_spec(col):
        return pl.BlockSpec((qb * blk, ATTN_KV_W), lambda b, n, s: (b * ns + n, col))

    def side_spec(col, off):
        return pl.BlockSpec((blk, ATTN_KV_W), lambda b, n, s: (b * nb + jnp.clip(qb * n + off, 0, nb - 1), col))

    grid_spec = pltpu.PrefetchScalarGridSpec(
        num_scalar_prefetch=1,
        grid=(B, ns),
        in_specs=[
            pl.BlockSpec((qb * blk, ATTN_Q_W), lambda b, n, s: (b * ns + n, 0)),
            side_spec(kcol, -1), cur_spec(kcol), side_spec(kcol, qb),
            side_spec(vcol, -1), cur_spec(vcol), side_spec(vcol, qb),
            pl.BlockSpec(bias_tab.shape, lambda b, n, s: (0, 0, 0, 0)),
        ],
        out_specs=pl.BlockSpec((qb * blk, ATTN_Q_W), lambda b, n, s: (b * ns + n, 0)),
    )
    return pl.pallas_call(
        _attn_kernel,
        out_shape=jax.ShapeDtypeStruct((T, ATTN_Q_W), BF16),
        grid_spec=grid_spec,
        compiler_params=_cparams(("arbitrary", "arbitrary")),
        name="window_attn",
    )(sink.astype(F32) * LOG2E, proj, proj, proj, proj, proj, proj, proj, bias_tab)


def _rope_tables(S):
    d = RET_QK_DIM
    inv = ROPE_BASE ** (-jnp.arange(0, d, 2, dtype=F32) / d)
    ang = jnp.arange(S, dtype=F32)[:, None] * inv[None, :]
    cos, sin = jnp.cos(ang), jnp.sin(ang)
    return jnp.concatenate([cos, cos], axis=-1), jnp.concatenate([-sin, sin], axis=-1)


def _ret_kernel(df_ref, db_ref, q_ref, k_ref, v_ref, g_ref, cos_ref, sin_ref, gw_ref, gb_ref, o_ref,
                qb, kb, q2, kv, sprev, *, S, C):
    h = pl.program_id(1)
    nc = S // C
    dk = RET_QK_DIM
    half = dk // 2

    rowf = lax.broadcasted_iota(I32, (C, 1), 0).astype(F32)
    lg_f = -jnp.exp(jnp.full((1, 1), df_ref[h], F32))
    lg_b = -jnp.exp(jnp.full((1, 1), db_ref[h], F32))
    qdec_f = jnp.exp((rowf + 1.0) * lg_f)
    kdec_f = jnp.exp((C - 1.0 - rowf) * lg_f)
    qdec_b = jnp.exp((C - rowf) * lg_b)
    kdec_b = jnp.exp(rowf * lg_b)
    cd_f = jnp.exp(C * lg_f)
    cd_b = jnp.exp(C * lg_b)

    def a_body(n, carry):
        rows = pl.ds(pl.multiple_of(n * C, C), C)
        co = cos_ref[rows, :]
        si = sin_ref[rows, :]
        q = q_ref[rows, :].astype(F32)
        k = k_ref[rows, :].astype(F32)
        qr = q * co + pltpu.roll(q, half, 1) * si
        kr = (k * co + pltpu.roll(k, half, 1) * si) * (dk ** -0.5)
        qb[rows, :] = qr.astype(BF16)
        kb[rows, :] = kr.astype(BF16)
        q2[rows, :] = jnp.concatenate([qr * qdec_f, qr * qdec_b], axis=1).astype(BF16)
        k2 = jnp.concatenate([kr * kdec_f, kr * kdec_b], axis=1).astype(BF16)
        kv[n] = _dot_tn(k2, v_ref[rows, :])
        return carry

    lax.fori_loop(0, nc, a_body, 0, unroll=4)

    def scan_f(n, state):
        sprev[n, 0:dk, :] = state.astype(BF16)
        return state * cd_f + kv[n, 0:dk, :]

    lax.fori_loop(0, nc, scan_f, jnp.zeros((dk, RET_V_DIM), F32))

    def scan_b(t, state):
        n = nc - 1 - t
        sprev[n, dk:2 * dk, :] = state.astype(BF16)
        return state * cd_b + kv[n, dk:2 * dk, :]

    lax.fori_loop(0, nc, scan_b, jnp.zeros((dk, RET_V_DIM), F32))

    ri = lax.broadcasted_iota(I32, (C, C), 0)
    ci = lax.broadcasted_iota(I32, (C, C), 1)
    d = (ri - ci).astype(F32)
    dec = jnp.where(ri >= ci, jnp.exp(jnp.maximum(d, 0.0) * lg_f), jnp.exp(jnp.maximum(-d, 0.0) * lg_b))
    gw = gw_ref[...]
    gb = gb_ref[...]

    def c_body(n, carry):
        rows = pl.ds(pl.multiple_of(n * C, C), C)
        scores = _dot_nt(qb[rows, :], kb[rows, :]) * dec
        y = _dot(scores.astype(BF16), v_ref[rows, :]) + _dot(q2[rows, :], sprev[n])
        mu = jnp.mean(y, axis=-1, keepdims=True)
        yc = y - mu
        var = jnp.mean(yc * yc, axis=-1, keepdims=True)
        yn = yc * lax.rsqrt(var + EPS) * gw + gb
        g = g_ref[rows, :].astype(F32)
        o_ref[rows, :] = (g * _sigmoid(g) * yn).astype(o_ref.dtype)
        return carry

    lax.fori_loop(0, nc, c_body, 0, unroll=4)


def _retention(proj, decay_fwd, decay_bwd, gn_w, gn_b, B, S):
    T = proj.shape[0]
    dk, dv = RET_QK_DIM, RET_V_DIM
    q_off = (ATTN_Q_W + 2 * ATTN_KV_W) // dk
    k_off = q_off + RET_QK_W // dk
    v_off = (ATTN_Q_W + 2 * ATTN_KV_W + 2 * RET_QK_W) // dv
    g_off = v_off + RET_V_W // dv
    cos, sin = _rope_tables(S)
    C = min(RET_TILE, S)
    grid_spec = pltpu.PrefetchScalarGridSpec(
        num_scalar_prefetch=2,
        grid=(B, RET_HEADS),
        in_specs=[
            pl.BlockSpec((S, dk), lambda b, h, *_: (b, q_off + h)),
            pl.BlockSpec((S, dk), lambda b, h, *_: (b, k_off + h)),
            pl.BlockSpec((S, dv), lambda b, h, *_: (b, v_off + h)),
            pl.BlockSpec((S, dv), lambda b, h, *_: (b, g_off + h)),
            pl.BlockSpec((S, dk), lambda b, h, *_: (0, 0)),
            pl.BlockSpec((S, dk), lambda b, h, *_: (0, 0)),
            pl.BlockSpec((1, dv), lambda b, h, *_: (0, h)),
            pl.BlockSpec((1, dv), lambda b, h, *_: (0, h)),
        ],
        out_specs=pl.BlockSpec((S, dv), lambda b, h, *_: (b, h)),
        scratch_shapes=[
            pltpu.VMEM((S, dk), BF16),
            pltpu.VMEM((S, dk), BF16),
            pltpu.VMEM((S, 2 * dk), BF16),
            pltpu.VMEM((S // C, 2 * dk, dv), F32),
            pltpu.VMEM((S // C, 2 * dk, dv), BF16),
        ],
    )
    return pl.pallas_call(
        functools.partial(_ret_kernel, S=S, C=C),
        out_shape=jax.ShapeDtypeStruct((T, RET_V_W), BF16),
        grid_spec=grid_spec,
        compiler_params=_cparams(("arbitrary", "arbitrary")),
        name="retention",
    )(decay_fwd.astype(F32), decay_bwd.astype(F32), proj, proj, proj, proj, cos, sin,
      gn_w.reshape(1, -1), gn_b.reshape(1, -1))


def _merge_kernel(a_ref, r_ref, wa_hbm, wr_hbm, ga_ref, gr_ref, o_ref, sta, stb, wab, wrb, sem):
    _stream_weight_tile(wa_hbm, sta, wab, sem.at[0])
    _stream_weight_tile(wr_hbm, stb, wrb, sem.at[1])
    ua = _dot(a_ref[...], wab[...])
    ur = _dot(r_ref[...], wrb[...])
    out = _sigmoid(ga_ref[...].astype(F32)) * ua + _sigmoid(gr_ref[...].astype(F32)) * ur
    o_ref[...] = out.astype(o_ref.dtype)


def _merge_up(attn, ret, wa, wr, proj, D):
    T = attn.shape[0]
    tm = min(512, T)
    tn = min(1024, D)
    ga_off = (ATTN_Q_W + 2 * ATTN_KV_W + 2 * RET_QK_W + 2 * RET_V_W) // tn
    gr_off = ga_off + D // tn
    return pl.pallas_call(
        _merge_kernel,
        out_shape=jax.ShapeDtypeStruct((T, D), BF16),
        grid=(D // tn, T // tm),
        in_specs=[
            pl.BlockSpec((tm, ATTN_Q_W), lambda j, i: (i, 0)),
            pl.BlockSpec((tm, RET_V_W), lambda j, i: (i, 0)),
            pl.BlockSpec(memory_space=pl.ANY),
            pl.BlockSpec(memory_space=pl.ANY),
            pl.BlockSpec((tm, tn), lambda j, i: (i, ga_off + j)),
            pl.BlockSpec((tm, tn), lambda j, i: (i, gr_off + j)),
        ],
        out_specs=pl.BlockSpec((tm, tn), lambda j, i: (i, j)),
        scratch_shapes=[pltpu.VMEM((ATTN_Q_W, tn), F32), pltpu.VMEM((RET_V_W, tn), F32),
                        pltpu.VMEM((ATTN_Q_W, tn), BF16), pltpu.VMEM((RET_V_W, tn), BF16),
                        pltpu.SemaphoreType.DMA((2,))],
        compiler_params=_cparams(("arbitrary", "arbitrary")),
        name="merge_up",
    )(attn, ret, wa, wr, proj, proj)


def _out_proj_kernel(m_ref, w_hbm, x_ref, mod_ref, o_ref, stage, wb, sem, *, gate_idx):
    _stream_weight_tile(w_hbm, stage, wb, sem)
    y = _dot(m_ref[...], wb[...])
    o_ref[...] = x_ref[...] + mod_ref[gate_idx:gate_idx + 1, :] * y


def _out_proj(merged, w, x2, mod, S, gate_idx):
    T, D = x2.shape
    tm = min(512, S)
    tn = min(1024, D)
    return pl.pallas_call(
        functools.partial(_out_proj_kernel, gate_idx=gate_idx),
        out_shape=jax.ShapeDtypeStruct((T, D), F32),
        grid=(D // tn, T // tm),
        in_specs=[
            pl.BlockSpec((tm, D), lambda j, i: (i, 0)),
            pl.BlockSpec(memory_space=pl.ANY),
            pl.BlockSpec((tm, tn), lambda j, i: (i, j)),
            pl.BlockSpec((None, N_ADA, tn), lambda j, i: ((i * tm) // S, 0, j)),
        ],
        out_specs=pl.BlockSpec((tm, tn), lambda j, i: (i, j)),
        scratch_shapes=[pltpu.VMEM((D, tn), F32), pltpu.VMEM((D, tn), BF16), pltpu.SemaphoreType.DMA],
        compiler_params=_cparams(("arbitrary", "arbitrary")),
        name="out_proj",
    )(merged, w, x2, mod)


def _pack_bf16_pairs(v):
    n = v.shape[1] // 2
    bits = lax.bitcast_convert_type(v.astype(BF16).astype(F32), U32)
    return jnp.bitwise_or(jnp.bitwise_and(bits[:, n:], jnp.uint32(0xFFFF0000)),
                          jnp.right_shift(bits[:, :n], jnp.uint32(16)))


def _unpack_bf16_pairs(w):
    lo = lax.bitcast_convert_type(jnp.left_shift(w, jnp.uint32(16)), F32)
    hi = lax.bitcast_convert_type(jnp.bitwise_and(w, jnp.uint32(0xFFFF0000)), F32)
    return jnp.concatenate([lo, hi], axis=1)


def _norm_router_kernel(x_ref, nw_ref, mod_ref, rwt_ref, rb_ref, hp_ref, e_ref, w_ref, r_ref, cnt_ref, carry,
                        *, shift_idx, scale_idx):
    i = pl.program_id(0)

    @pl.when(i == 0)
    def _():
        carry[...] = jnp.zeros_like(carry)

    h = _norm_mod_value(x_ref[...], nw_ref[...], mod_ref[shift_idx:shift_idx + 1, :],
                        mod_ref[scale_idx:scale_idx + 1, :])
    hp_ref[...] = _pack_bf16_pairs(h)
    hb = h.astype(BF16)
    tm = hb.shape[0]
    E = rwt_ref.shape[0]
    logits = _dot_nt(rwt_ref[...], hb) + rb_ref[...]
    iota_e = lax.broadcasted_iota(I32, (E, tm), 0)
    vals, idxs = [], []
    work = logits
    sel = jnp.zeros((E, tm), F32)
    for _k in range(TOP_K):
        m = jnp.max(work, axis=0, keepdims=True)
        idx = jnp.min(jnp.where(work == m, iota_e, E), axis=0, keepdims=True)
        hit = iota_e == idx
        vals.append(m)
        idxs.append(idx)
        work = jnp.where(hit, NEG_BIG, work)
        sel = sel + hit.astype(F32)
    ex = [jnp.exp(v - vals[0]) for v in vals]
    tot = ex[0]
    for v in ex[1:]:
        tot = tot + v
    ri = lax.broadcasted_iota(I32, (tm, tm), 0)
    ci = lax.broadcasted_iota(I32, (tm, tm), 1)
    upper = (ri < ci).astype(BF16)
    before = _dot(sel.astype(BF16), upper) + carry[:, 0:1]
    for k in range(TOP_K):
        e_ref[k:k + 1, :] = idxs[k]
        w_ref[k:k + 1, :] = ex[k] / tot
        r_ref[k:k + 1, :] = jnp.sum(jnp.where(iota_e == idxs[k], before, 0.0), axis=0, keepdims=True).astype(I32)
    carry[...] = carry[...] + jnp.sum(sel, axis=1, keepdims=True)
    cnt_ref[...] = carry[...]


def _norm_router(x1, nw, mod, router_w, router_b, S, shift_idx, scale_idx):
    T, D = x1.shape
    E = router_w.shape[1]
    tm = min(512, S)
    outs = pl.pallas_call(
        functools.partial(_norm_router_kernel, shift_idx=shift_idx, scale_idx=scale_idx),
        out_shape=(
            jax.ShapeDtypeStruct((T, D // 2), U32),
            jax.ShapeDtypeStruct((TOP_K, T), I32),
            jax.ShapeDtypeStruct((TOP_K, T), F32),
            jax.ShapeDtypeStruct((TOP_K, T), I32),
            jax.ShapeDtypeStruct((E, 128), F32),
        ),
        grid=(T // tm,),
        in_specs=[
            pl.BlockSpec((tm, D), lambda i: (i, 0)),
            pl.BlockSpec((1, D), lambda i: (0, 0)),
            pl.BlockSpec((None, N_ADA, D), lambda i: ((i * tm) // S, 0, 0)),
            pl.BlockSpec((E, D), lambda i: (0, 0)),
            pl.BlockSpec((E, 1), lambda i: (0, 0)),
        ],
        out_specs=(
            pl.BlockSpec((tm, D // 2), lambda i: (i, 0)),
            pl.BlockSpec((TOP_K, tm), lambda i: (0, i)),
            pl.BlockSpec((TOP_K, tm), lambda i: (0, i)),
            pl.BlockSpec((TOP_K, tm), lambda i: (0, i)),
            pl.BlockSpec((E, 128), lambda i: (0, 0)),
        ),
        scratch_shapes=[pltpu.VMEM((E, 128), F32)],
        compiler_params=_cparams(("arbitrary",)),
        name="norm_router",
    )(x1, nw.reshape(1, D), mod, router_w.T.astype(BF16), router_b.reshape(E, 1).astype(F32))
    return outs


def _dispatch_kernel(pos_ref, pad_ref, hp_ref, xs_ref, zeros, sem, zsem, *, tm, T, E, n_tiles):
    base = pl.program_id(0) * tm

    @pl.when(pl.program_id(0) == 0)
    def _():
        zeros[...] = jnp.zeros_like(zeros)
        def pad_copy(row):
            return pltpu.make_async_copy(zeros.at[pl.ds(0, 1)], xs_ref.at[pl.ds(row, 1)], zsem)

        def tail_copy(t):
            return pltpu.make_async_copy(
                zeros, xs_ref.at[pl.ds(pl.multiple_of(t * MOE_TILE, MOE_TILE), MOE_TILE)], zsem)

        def pad8_copy(row):
            return pltpu.make_async_copy(zeros.at[pl.ds(0, 8)], xs_ref.at[pl.ds(pl.multiple_of(row, 8), 8)], zsem)

        def pad_pieces(e):
            start, count = pad_ref[e], pad_ref[E + e]
            head = jnp.minimum(count, jnp.bitwise_and(-start, 7))
            return start, head, start + head, (count - head) // 8

        def pad_start(e, carry):
            start, head, body, n8 = pad_pieces(e)
            lax.fori_loop(0, head, lambda r, c: (pad_copy(start + r).start(), c)[1], 0)
            lax.fori_loop(0, n8, lambda r, c: (pad8_copy(body + 8 * r).start(), c)[1], 0)
            return carry

        def pad_wait(e, carry):
            start, head, body, n8 = pad_pieces(e)
            lax.fori_loop(0, head, lambda r, c: (pad_copy(start + r).wait(), c)[1], 0)
            lax.fori_loop(0, n8, lambda r, c: (pad8_copy(body + 8 * r).wait(), c)[1], 0)
            return carry

        def tail_start(t, carry):
            tail_copy(t).start()
            return carry

        def tail_wait(t, carry):
            tail_copy(t).wait()
            return carry

        lax.fori_loop(0, E, pad_start, 0)
        lax.fori_loop(pad_ref[2 * E], n_tiles, tail_start, 0)
        lax.fori_loop(0, E, pad_wait, 0)
        lax.fori_loop(pad_ref[2 * E], n_tiles, tail_wait, 0)

    def start_body(t, carry):
        for k in range(TOP_K):
            dst = pos_ref[k * T + base + t]
            pltpu.make_async_copy(hp_ref.at[pl.ds(t, 1)], xs_ref.at[pl.ds(dst, 1)], sem).start(priority=k % 2)
        return carry

    lax.fori_loop(0, tm, start_body, 0, unroll=8)
    for k in range(TOP_K):
        pltpu.make_async_copy(hp_ref, xs_ref.at[pl.ds(0, tm)], sem).wait()


def _dispatch(pos_flat, pad_info, hp, n_tiles, E):
    T, W = hp.shape
    tm = min(512, T)
    grid_spec = pltpu.PrefetchScalarGridSpec(
        num_scalar_prefetch=2,
        grid=(T // tm,),
        in_specs=[pl.BlockSpec((tm, W), lambda i, p, q: (i, 0))],
        out_specs=pl.BlockSpec(memory_space=pl.ANY),
        scratch_shapes=[pltpu.VMEM((MOE_TILE, W), U32), pltpu.SemaphoreType.DMA, pltpu.SemaphoreType.DMA],
    )
    return pl.pallas_call(
        functools.partial(_dispatch_kernel, tm=tm, T=T, E=E, n_tiles=n_tiles),
        out_shape=jax.ShapeDtypeStruct((n_tiles * MOE_TILE, W), U32),
        grid_spec=grid_spec,
        compiler_params=_cparams(("arbitrary",)),
        name="dispatch",
    )(pos_flat, pad_info, hp)


def _expert_changed(te_ref, i):
    return jnp.logical_or(i == 0, te_ref[i] != te_ref[jnp.maximum(i - 1, 0)])


def _cast_rows(src_ref, dst_ref, rows_per_pass=256):
    def body(r, carry):
        rows = pl.ds(pl.multiple_of(r * rows_per_pass, rows_per_pass), rows_per_pass)
        dst_ref[rows, :] = src_ref[rows, :].astype(dst_ref.dtype)
        return carry

    lax.fori_loop(0, src_ref.shape[0] // rows_per_pass, body, 0)


def _expert_weight_stage(te_ref, nx_ref, i, first_step, n, n_passes, copies, consume):
    @pl.when(_expert_changed(te_ref, i))
    def _():
        @pl.when(first_step)
        def _():
            for cp in copies(te_ref[0], 0):
                cp.start()

        for cp in copies(te_ref[i], n):
            cp.wait()
        consume()
        same_pass = nx_ref[i] >= 0
        nxt_e = jnp.where(same_pass, nx_ref[i], te_ref[0])
        nxt_n = jnp.where(same_pass, n, n + 1)

        @pl.when(jnp.logical_or(same_pass, n + 1 < n_passes))
        def _():
            for cp in copies(nxt_e, nxt_n):
                cp.start()


def _expert_up_kernel(te_ref, nu_ref, nx_ref, tr_ref, xs_ref, w1_hbm, bg_ref, bu_ref, o_ref, stage, wgb, wub, sem,
                      *, tn, F, nf):
    n = pl.program_id(0)
    i = pl.program_id(1)

    def copies(e, nn):
        c0 = pl.multiple_of(nn * tn, tn)
        return (pltpu.make_async_copy(w1_hbm.at[e, :, pl.ds(c0, tn)], stage.at[0], sem),
                pltpu.make_async_copy(w1_hbm.at[e, :, pl.ds(F + c0, tn)], stage.at[1], sem))

    def consume():
        _cast_rows(stage.at[0], wgb)
        _cast_rows(stage.at[1], wub)

    _expert_weight_stage(te_ref, nx_ref, i, jnp.logical_and(n == 0, i == 0), n, nf, copies, consume)

    def ffn_up(rows):
        x = _unpack_bf16_pairs(xs_ref[rows, :]).astype(BF16)
        gate = jnp.minimum(_dot(x, wgb[...]) + bg_ref[...], SWIGLU_LIMIT)
        up = jnp.clip(_dot(x, wub[...]) + bu_ref[...], -SWIGLU_LIMIT, SWIGLU_LIMIT)
        act = gate * _sigmoid(SWIGLU_ALPHA * gate) * (up + 1.0)
        o_ref[rows, :] = act.astype(o_ref.dtype)

    _per_tile_rows(tr_ref[i], o_ref, ffn_up)


def _per_tile_rows(valid_rows, o_ref, compute):
    tm = o_ref.shape[0]
    half = tm // 2

    @pl.when(valid_rows > half)
    def _():
        compute(slice(0, tm))

    @pl.when(jnp.logical_and(valid_rows > 0, valid_rows <= half))
    def _():
        compute(slice(0, half))
        o_ref[half:, :] = jnp.zeros((tm - half, o_ref.shape[1]), o_ref.dtype)

    @pl.when(valid_rows <= 0)
    def _():
        o_ref[...] = jnp.zeros_like(o_ref)


def _expert_down_kernel(te_ref, nu_ref, nx_ref, tr_ref, a_ref, w2_hbm, b2_ref, o_ref, stage, w2b, sem):
    i = pl.program_id(0)

    def copies(e, nn):
        del nn
        return (pltpu.make_async_copy(w2_hbm.at[e], stage, sem),)

    def consume():
        _cast_rows(stage, w2b)

    _expert_weight_stage(te_ref, nx_ref, i, i == 0, 0, 1, copies, consume)

    def ffn_down(rows):
        y = _dot(a_ref[rows, :], w2b[...]) + b2_ref[...]
        o_ref[rows, :] = _pack_bf16_pairs(y)

    _per_tile_rows(tr_ref[i], o_ref, ffn_down)


def _expert_ffn(xs, tile_e, n_used, next_e, tile_rows, w1, b1, w2, b2):
    R, W = xs.shape
    E, D, F2 = w1.shape
    F = F2 // 2
    tm = MOE_TILE
    nt = R // tm

    tn = min(512, F)
    nf = F // tn
    b1r = b1.reshape(E, 1, F2)

    act = pl.pallas_call(
        functools.partial(_expert_up_kernel, tn=tn, F=F, nf=nf),
        out_shape=jax.ShapeDtypeStruct((R, F), BF16),
        grid_spec=pltpu.PrefetchScalarGridSpec(
            num_scalar_prefetch=4,
            grid=(nf, nt),
            in_specs=[
                pl.BlockSpec((tm, W), lambda n, i, te, nu, nx, tr: (jnp.minimum(i, nu[0] - 1), 0)),
                pl.BlockSpec(memory_space=pl.ANY),
                pl.BlockSpec((None, 1, tn), lambda n, i, te, nu, nx, tr: (te[i], 0, n)),
                pl.BlockSpec((None, 1, tn), lambda n, i, te, nu, nx, tr: (te[i], 0, nf + n)),
            ],
            out_specs=pl.BlockSpec((tm, tn), lambda n, i, te, nu, nx, tr: (i, n)),
            scratch_shapes=[pltpu.VMEM((2, D, tn), F32), pltpu.VMEM((D, tn), BF16), pltpu.VMEM((D, tn), BF16),
                            pltpu.SemaphoreType.DMA],
        ),
        compiler_params=_cparams(("arbitrary", "arbitrary")),
        name="expert_up",
    )(tile_e, n_used, next_e, tile_rows, xs, w1, b1r, b1r)
    y = pl.pallas_call(
        _expert_down_kernel,
        out_shape=jax.ShapeDtypeStruct((R, D // 2), U32),
        grid_spec=pltpu.PrefetchScalarGridSpec(
            num_scalar_prefetch=4,
            grid=(nt,),
            in_specs=[
                pl.BlockSpec((tm, F), lambda i, te, nu, nx, tr: (jnp.minimum(i, nu[0] - 1), 0)),
                pl.BlockSpec(memory_space=pl.ANY),
                pl.BlockSpec((None, 1, D), lambda i, te, nu, nx, tr: (te[i], 0, 0)),
            ],
            out_specs=pl.BlockSpec((tm, D // 2), lambda i, te, nu, nx, tr: (i, 0)),
            scratch_shapes=[pltpu.VMEM((F, D), F32), pltpu.VMEM((F, D), BF16), pltpu.SemaphoreType.DMA],
        ),
        compiler_params=_cparams(("arbitrary",)),
        name="expert_down",
    )(tile_e, n_used, next_e, tile_rows, act, w2, b2.reshape(E, 1, D))
    return y


def _combine_kernel(pos_ref, x_ref, w_ref, mod_ref, y_ref, o_ref, buf, sem, *, tm, T, gate_idx):
    i = pl.program_id(0)
    n = pl.num_programs(0)
    W = buf.shape[-1]
    th = tm
    ring = buf.shape[0]

    def issue_row(first_token, slot, t):
        for k in range(TOP_K):
            src = pos_ref[k * T + first_token + t]
            pltpu.make_async_copy(y_ref.at[pl.ds(src, 1)], buf.at[slot, k, pl.ds(t, 1)],
                                  sem.at[slot]).start(priority=k % 2)

    def wait_half(slot):
        for k in range(TOP_K):
            pltpu.make_async_copy(y_ref.at[pl.ds(0, th)], buf.at[slot, k], sem.at[slot]).wait()

    rc = 8
    cw = min(512, W)

    def consume(slot, next_first_token, next_slot):
        def rows_body(r, carry):
            rows = pl.ds(pl.multiple_of(r * rc, rc), rc)
            brow = rows
            wv = w_ref[rows, :]
            wk = [jnp.broadcast_to(wv[:, k:k + 1], (rc, cw)) for k in range(TOP_K)]
            for c in range(W // cw):
                lo = hi = None
                for k in range(TOP_K):
                    u = buf[slot, k, brow, c * cw:(c + 1) * cw]
                    l = wk[k] * lax.bitcast_convert_type(jnp.left_shift(u, jnp.uint32(16)), F32)
                    h = wk[k] * lax.bitcast_convert_type(jnp.bitwise_and(u, jnp.uint32(0xFFFF0000)), F32)
                    lo = l if lo is None else lo + l
                    hi = h if hi is None else hi + h
                for half, acc in ((0, lo), (1, hi)):
                    cols = slice(half * W + c * cw, half * W + (c + 1) * cw)
                    o_ref[rows, cols] = x_ref[rows, cols] + mod_ref[gate_idx:gate_idx + 1, cols] * acc
            for t in range(rc):
                issue_row(next_first_token, next_slot, r * rc + t)
            return carry

        lax.fori_loop(0, th // rc, rows_body, 0)

    @pl.when(i == 0)
    def _():
        for b in range(2):
            lax.fori_loop(0, tm, lambda t, c: (issue_row(b * tm, b, t), c)[1], 0, unroll=8)

    wait_half(i % ring)
    consume(i % ring, jnp.minimum((i + 2) * tm, T - tm), (i + 2) % ring)

    @pl.when(i == n - 1)
    def _():
        wait_half((i + 1) % ring)
        wait_half((i + 2) % ring)


def _combine(pos_flat, x1, w_tk, mod, y, S, gate_idx):
    T, D = x1.shape
    tm = min(128, S)
    assert T // tm >= 2
    grid_spec = pltpu.PrefetchScalarGridSpec(
        num_scalar_prefetch=1,
        grid=(T // tm,),
        in_specs=[
            pl.BlockSpec((tm, D), lambda i, p: (i, 0)),
            pl.BlockSpec((tm, TOP_K), lambda i, p: (i, 0)),
            pl.BlockSpec((None, N_ADA, D), lambda i, p: ((i * tm) // S, 0, 0)),
            pl.BlockSpec(memory_space=pl.ANY),
        ],
        out_specs=pl.BlockSpec((tm, D), lambda i, p: (i, 0)),
        scratch_shapes=[pltpu.VMEM((3, TOP_K, tm, D // 2), U32), pltpu.SemaphoreType.DMA((3,))],
    )
    return pl.pallas_call(
        functools.partial(_combine_kernel, tm=tm, T=T, gate_idx=gate_idx),
        out_shape=jax.ShapeDtypeStruct((T, D), F32),
        grid_spec=grid_spec,
        compiler_params=_cparams(("arbitrary",)),
        name="combine",
    )(pos_flat, x1, w_tk, mod, y)


def _layer(x2, mod, B, S, rel_bias, norm_mix_w, w_in, q_norm_w, k_norm_w, attn_sink, ret_decay_fwd, ret_decay_bwd,
           ret_gn_w, ret_gn_b, w_up_attn, w_up_ret, w_out, norm_ffn_w, router_w, router_b,
           expert_w1, expert_b1, expert_w2, expert_b2):
    T, D = x2.shape
    E = router_w.shape[1]
    h = _norm_mod(x2, norm_mix_w, mod, S, 0, 1)
    proj = _in_proj(h, w_in, q_norm_w, k_norm_w)
    attn = _window_attention(proj, _attn_bias_table(rel_bias), attn_sink, B, S)
    ret = _retention(proj, ret_decay_fwd, ret_decay_bwd, ret_gn_w, ret_gn_b, B, S)
    merged = _merge_up(attn, ret, w_up_attn, w_up_ret, proj, D)
    x1 = _out_proj(merged, w_out, x2, mod, S, 2)

    hp, top_e, top_w, rank, cnt = _norm_router(x1, norm_ffn_w, mod, router_w, router_b, S, 3, 4)
    tm = MOE_TILE
    counts = cnt[:, 0].astype(I32)
    tiles_per_e = (counts + tm - 1) // tm
    tile_end = jnp.cumsum(tiles_per_e)
    group_start = (tile_end - tiles_per_e) * tm
    n_tiles = (T * TOP_K) // tm + E
    n_used = tile_end[-1]
    tile_ids = jnp.minimum(jnp.arange(n_tiles, dtype=I32), n_used - 1)
    tile_e = jnp.minimum(jnp.sum((tile_ids[:, None] >= tile_end[None, :]).astype(I32), axis=1), E - 1)
    e_ids = jnp.arange(E, dtype=I32)
    later = jnp.logical_and(e_ids[None, :] > e_ids[:, None], tiles_per_e[None, :] > 0)
    next_of_e = jnp.min(jnp.where(later, e_ids[None, :], E), axis=1)
    next_of_e = jnp.where(next_of_e == E, -1, next_of_e)
    next_e = jnp.sum(jnp.where(tile_e[:, None] == e_ids, next_of_e, 0), axis=1).astype(I32)
    all_tiles = jnp.arange(n_tiles, dtype=I32)
    in_e = tile_e[:, None] == e_ids
    rows_left = jnp.sum(jnp.where(in_e, counts + group_start, 0), axis=1) - all_tiles * tm
    tile_rows = jnp.where(all_tiles < n_used, jnp.clip(rows_left, 0, tm), 0).astype(I32)
    start_of = jnp.sum(jnp.where(top_e[:, :, None] == e_ids, group_start, 0), axis=-1)
    pos_flat = (start_of + rank).astype(I32).reshape(-1)
    pad_info = jnp.concatenate([group_start + counts, tiles_per_e * tm - counts, n_used.reshape(1)]).astype(I32)

    xs = _dispatch(pos_flat, pad_info, hp, n_tiles, E)
    y = _expert_ffn(xs, tile_e, n_used.reshape(1).astype(I32), next_e, tile_rows, expert_w1, expert_b1,
                    expert_w2, expert_b2)
    return _combine(pos_flat, x1, top_w.T, mod, y, S, 5)


def kernel(x, c, rel_bias, ada_w, ada_b, norm_mix_w, w_in, q_norm_w, k_norm_w, attn_sink, ret_decay_fwd,
           ret_decay_bwd, ret_gn_w, ret_gn_b, w_up_attn, w_up_ret, w_out, norm_ffn_w, router_w, router_b,
           expert_w1, expert_b1, expert_w2, expert_b2):
    B, S, D = x.shape
    x2 = x.reshape(B * S, D)
    for l in range(ada_w.shape[0]):
        mod = _ada_mod(c, ada_w[l], ada_b[l])
        x2 = _layer(x2, mod, B, S, rel_bias, norm_mix_w[l], w_in[l], q_norm_w[l], k_norm_w[l], attn_sink[l],
                    ret_decay_fwd[l], ret_decay_bwd[l], ret_gn_w[l], ret_gn_b[l], w_up_attn[l], w_up_ret[l],
                    w_out[l], norm_ffn_w[l], router_w[l], router_b[l], expert_w1[l], expert_b1[l],
                    expert_w2[l], expert_b2[l])
    return x2.reshape(B, S, D)
```

```python
import functools
import math

import jax
import jax.numpy as jnp
from jax import lax
from jax.experimental import pallas as pl
from jax.experimental.pallas import tpu as pltpu

F32 = jnp.float32
BF16 = jnp.bfloat16
U32 = jnp.uint32
I32 = jnp.int32

ATTN_HEADS = 16
ATTN_KV_HEADS = 4
ATTN_HEAD_DIM = 128
WINDOW = 128
ATTN_BLOCK = 128
N_BUCKETS = 32
MAX_DISTANCE = 128
RET_HEADS = 8
RET_QK_DIM = 128
RET_V_DIM = 256
RET_CHUNK = 128
ROPE_BASE = 10000.0
TOP_K = 4
SWIGLU_LIMIT = 7.0
SWIGLU_ALPHA = 1.702
N_ADA = 6
EPS = 1e-6
NEG_INF = -1e30
NEG_BIG = -3.0e38
LOG2E = 1.4426950408889634

ATTN_Q_W = ATTN_HEADS * ATTN_HEAD_DIM
ATTN_KV_W = ATTN_KV_HEADS * ATTN_HEAD_DIM
RET_QK_W = RET_HEADS * RET_QK_DIM
RET_V_W = RET_HEADS * RET_V_DIM

VMEM_LIMIT_BYTES = 56 * 1024 * 1024
MOE_TILE = 512
RET_TILE = 256


def _cparams(sem):
    return pltpu.CompilerParams(dimension_semantics=sem, vmem_limit_bytes=VMEM_LIMIT_BYTES)


def _dot(a, b):
    return jnp.dot(a, b, preferred_element_type=F32)


def _dot_nt(a, b):
    return lax.dot_general(a, b, (((1,), (1,)), ((), ())), preferred_element_type=F32)


def _dot_tn(a, b):
    return lax.dot_general(a, b, (((0,), (0,)), ((), ())), preferred_element_type=F32)


def _sigmoid(x):
    return 1.0 / (1.0 + jnp.exp(-x))


def _ada_kernel(c_ref, w_ref, b_ref, o_ref):
    c = c_ref[...]
    cs = (c * _sigmoid(c)).astype(BF16)
    o_ref[...] = _dot(cs, w_ref[...].astype(BF16)) + b_ref[...]


def _ada_mod(c, ada_w, ada_b):
    B, D = c.shape
    N = ada_w.shape[1]
    rows = 8
    cp = jnp.zeros((rows, D), F32).at[:B].set(c)
    tn = min(1024, N)
    out = pl.pallas_call(
        _ada_kernel,
        out_shape=jax.ShapeDtypeStruct((rows, N), F32),
        grid=(N // tn,),
        in_specs=[
            pl.BlockSpec((rows, D), lambda j: (0, 0)),
            pl.BlockSpec((D, tn), lambda j: (0, j)),
            pl.BlockSpec((1, tn), lambda j: (0, j)),
        ],
        out_specs=pl.BlockSpec((rows, tn), lambda j: (0, j)),
        compiler_params=_cparams(("arbitrary",)),
        name="ada_mod",
    )(cp, ada_w, ada_b.reshape(1, N))
    return out[:B].reshape(B, N_ADA, D)


def _norm_mod_value(x, nw, shift, scale):
    ms = jnp.mean(x * x, axis=-1, keepdims=True)
    h = x * lax.rsqrt(ms + EPS) * nw
    return h * (1.0 + scale) + shift


def _norm_mod_kernel(x_ref, nw_ref, mod_ref, o_ref, *, shift_idx, scale_idx):
    h = _norm_mod_value(x_ref[...], nw_ref[...], mod_ref[shift_idx:shift_idx + 1, :],
                        mod_ref[scale_idx:scale_idx + 1, :])
    o_ref[...] = h.astype(BF16)


def _norm_mod(x2, nw, mod, S, shift_idx, scale_idx):
    T, D = x2.shape
    tm = min(512, S)
    return pl.pallas_call(
        functools.partial(_norm_mod_kernel, shift_idx=shift_idx, scale_idx=scale_idx),
        out_shape=jax.ShapeDtypeStruct((T, D), BF16),
        grid=(T // tm,),
        in_specs=[
            pl.BlockSpec((tm, D), lambda i: (i, 0)),
            pl.BlockSpec((1, D), lambda i: (0, 0)),
            pl.BlockSpec((None, N_ADA, D), lambda i: ((i * tm) // S, 0, 0)),
        ],
        out_specs=pl.BlockSpec((tm, D), lambda i: (i, 0)),
        compiler_params=_cparams(("arbitrary",)),
        name="norm_mod",
    )(x2, nw.reshape(1, D), mod)


def _stream_weight_tile(w_hbm, stage, wb, sem):
    j = pl.program_id(0)
    tn = wb.shape[1]

    def w_copy(jj):
        return pltpu.make_async_copy(w_hbm.at[:, pl.ds(pl.multiple_of(jj * tn, tn), tn)], stage, sem)

    @pl.when(pl.program_id(1) == 0)
    def _():
        @pl.when(j == 0)
        def _():
            w_copy(0).start()

        w_copy(j).wait()
        _cast_rows(stage, wb)

        @pl.when(j + 1 < pl.num_programs(0))
        def _():
            w_copy(j + 1).start()


def _in_proj_kernel(a_ref, w_hbm, cw_ref, cf_ref, bd_ref, o_ref, stage, wb, sem, *, n_norm_tiles):
    j = pl.program_id(0)
    tn = wb.shape[1]
    _stream_weight_tile(w_hbm, stage, wb, sem)

    @pl.when(j < n_norm_tiles)
    def _():
        cb = bd_ref.shape[0]
        for c in range(tn // cb):
            cols = slice(c * cb, (c + 1) * cb)
            acc = _dot(a_ref[...], wb[:, cols])
            ms = _dot((acc * acc).astype(BF16), bd_ref[...]) * (1.0 / ATTN_HEAD_DIM)
            inv = jnp.where(cf_ref[:, cols] > 0.0, lax.rsqrt(ms + EPS), 1.0)
            o_ref[:, cols] = (acc * inv * cw_ref[:, cols]).astype(o_ref.dtype)

    @pl.when(j >= n_norm_tiles)
    def _():
        o_ref[...] = _dot(a_ref[...], wb[...]).astype(o_ref.dtype)


def _in_proj(h, w, q_norm_w, k_norm_w):
    T, K = h.shape
    N = w.shape[1]
    tm = min(1024, T)
    tn = 1024 if N % 1024 == 0 else 512
    hd = ATTN_HEAD_DIM
    n_norm = ATTN_Q_W + ATTN_KV_W
    n_norm_tiles = -(-n_norm // tn)
    pad = n_norm_tiles * tn - n_norm
    col_w = jnp.concatenate([jnp.tile(q_norm_w.astype(F32) * (hd ** -0.5 * LOG2E), ATTN_HEADS),
                             jnp.tile(k_norm_w.astype(F32), ATTN_KV_HEADS), jnp.ones((pad,), F32)]).reshape(1, -1)
    col_flag = jnp.concatenate([jnp.ones((n_norm,), F32), jnp.zeros((pad,), F32)]).reshape(1, -1)
    cb = min(512, tn)
    lane_head = jnp.arange(cb, dtype=I32) // hd
    block_ones = (lane_head[:, None] == lane_head[None, :]).astype(BF16)
    last = n_norm_tiles - 1
    return pl.pallas_call(
        functools.partial(_in_proj_kernel, n_norm_tiles=n_norm_tiles),
        out_shape=jax.ShapeDtypeStruct((T, N), BF16),
        grid=(N // tn, T // tm),
        in_specs=[
            pl.BlockSpec((tm, K), lambda j, i: (i, 0)),
            pl.BlockSpec(memory_space=pl.ANY),
            pl.BlockSpec((1, tn), lambda j, i: (0, jnp.minimum(j, last))),
            pl.BlockSpec((1, tn), lambda j, i: (0, jnp.minimum(j, last))),
            pl.BlockSpec((cb, cb), lambda j, i: (0, 0)),
        ],
        out_specs=pl.BlockSpec((tm, tn), lambda j, i: (i, j)),
        scratch_shapes=[pltpu.VMEM((K, tn), F32), pltpu.VMEM((K, tn), BF16), pltpu.SemaphoreType.DMA],
        compiler_params=_cparams(("arbitrary", "arbitrary")),
        name="in_proj",
    )(h, w, col_w, col_flag, block_ones)


def _t5_bucket(rel):
    nb = N_BUCKETS // 2
    max_exact = nb // 2
    base = jnp.where(rel > 0, nb, 0)
    n = jnp.abs(rel)
    nf = jnp.maximum(n, 1).astype(F32)
    large = max_exact + (jnp.log(nf / max_exact) / math.log(MAX_DISTANCE / max_exact) * (nb - max_exact)).astype(I32)
    large = jnp.minimum(large, nb - 1)
    return base + jnp.where(n < max_exact, n, large)


def _attn_bias_table(rel_bias):
    blk = ATTN_BLOCK
    qi = jnp.arange(blk, dtype=I32)[:, None]
    kj = jnp.arange(3 * blk, dtype=I32)[None, :]
    rel = kj - blk - qi
    bucket = _t5_bucket(rel)
    table = rel_bias.astype(F32).T
    bias = jnp.zeros((table.shape[0],) + bucket.shape, F32)
    for b in range(N_BUCKETS):
        bias = jnp.where(bucket[None] == b, table[:, b][:, None, None], bias)
    bias = jnp.where((jnp.abs(rel) <= WINDOW)[None], bias * LOG2E, NEG_INF)
    G = ATTN_HEADS // ATTN_KV_HEADS
    bias = bias.reshape(ATTN_KV_HEADS, G, blk, 3 * blk)
    bias = jnp.transpose(bias, (0, 3, 1, 2)).reshape(ATTN_KV_HEADS, 3 * blk, G * blk)
    key = jnp.arange(3 * blk, dtype=I32)[None, :, None]
    no_prev, no_next = key < blk, key >= 2 * blk
    return jnp.stack([bias, jnp.where(no_prev, NEG_INF, bias), jnp.where(no_next, NEG_INF, bias),
                      jnp.where(jnp.logical_or(no_prev, no_next), NEG_INF, bias)])


def _attn_kernel(sink_ref, q_ref, kp_ref, kc_ref, kn_ref, vp_ref, vc_ref, vn_ref, bias_ref, o_ref):
    blk, hd = ATTN_BLOCK, ATTN_HEAD_DIM
    G = ATTN_HEADS // ATTN_KV_HEADS
    qb = q_ref.shape[0] // blk
    n = pl.program_id(1)
    last = pl.num_programs(1) - 1
    for s in range(qb):
        first = jnp.logical_and(n == 0, s == 0) if s == 0 else False
        final = jnp.logical_and(n == last, s == qb - 1) if s == qb - 1 else False
        var = jnp.where(first, 1, 0) + jnp.where(final, 2, 0)
        rows = slice(s * blk, (s + 1) * blk)
        for hk in range(ATTN_KV_HEADS):
            sl = slice(hk * hd, (hk + 1) * hd)
            kwin = [kp_ref[:, sl]] + [kc_ref[j * blk:(j + 1) * blk, sl] for j in range(qb)] + [kn_ref[:, sl]]
            vwin = [vp_ref[:, sl]] + [vc_ref[j * blk:(j + 1) * blk, sl] for j in range(qb)] + [vn_ref[:, sl]]
            k3 = jnp.concatenate(kwin[s:s + 3], axis=0)
            v3 = jnp.concatenate(vwin[s:s + 3], axis=0)
            q4 = jnp.concatenate([q_ref[rows, (hk * G + g) * hd:(hk * G + g + 1) * hd] for g in range(G)],
                                 axis=0)
            sink = jnp.concatenate([jnp.full((1, blk), sink_ref[hk * G + g], F32) for g in range(G)], axis=1)
            logits = _dot_nt(k3, q4) + bias_ref[var, hk]
            m = jnp.maximum(jnp.max(logits, axis=0, keepdims=True), sink)
            p = jnp.exp2(logits - m)
            denom = jnp.sum(p, axis=0, keepdims=True) + jnp.exp2(sink - m)
            out = _dot_tn((p * (1.0 / denom)).astype(BF16), v3)
            for g in range(G):
                h = hk * G + g
                o_ref[rows, h * hd:(h + 1) * hd] = out[g * blk:(g + 1) * blk].astype(o_ref.dtype)


def _window_attention(proj, bias_tab, sink, B, S):
    T = proj.shape[0]
    blk = ATTN_BLOCK
    nb = S // blk
    qb = next(c for c in (4, 2, 1) if nb % c == 0)
    ns = nb // qb
    kcol = ATTN_Q_W // ATTN_KV_W
    vcol = kcol + 1

    def cur_spec(col):
        return pl.BlockSpec((qb * blk, ATTN_KV_W), lambda b, n, s: (b * ns + n, col))

    def side_spec(col, off):
        return pl.BlockSpec((blk, ATTN_KV_W), lambda b, n, s: (b * nb + jnp.clip(qb * n + off, 0, nb - 1), col))

    grid_spec = pltpu.PrefetchScalarGridSpec(
        num_scalar_prefetch=1,
        grid=(B, ns),
        in_specs=[
            pl.BlockSpec((qb * blk, ATTN_Q_W), lambda b, n, s: (b * ns + n, 0)),
            side_spec(kcol, -1), cur_spec(kcol), side_spec(kcol, qb),
            side_spec(vcol, -1), cur_spec(vcol), side_spec(vcol, qb),
            pl.BlockSpec(bias_tab.shape, lambda b, n, s: (0, 0, 0, 0)),
        ],
        out_specs=pl.BlockSpec((qb * blk, ATTN_Q_W), lambda b, n, s: (b * ns + n, 0)),
    )
    return pl.pallas_call(
        _attn_kernel,
        out_shape=jax.ShapeDtypeStruct((T, ATTN_Q_W), BF16),
        grid_spec=grid_spec,
        compiler_params=_cparams(("arbitrary", "arbitrary")),
        name="window_attn",
    )(sink.astype(F32) * LOG2E, proj, proj, proj, proj, proj, proj, proj, bias_tab)


def _rope_tables(S):
    d = RET_QK_DIM
    inv = ROPE_BASE ** (-jnp.arange(0, d, 2, dtype=F32) / d)
    ang = jnp.arange(S, dtype=F32)[:, None] * inv[None, :]
    cos, sin = jnp.cos(ang), jnp.sin(ang)
    return jnp.concatenate([cos, cos], axis=-1), jnp.concatenate([-sin, sin], axis=-1)


def _ret_kernel(df_ref, db_ref, q_ref, k_ref, v_ref, g_ref, cos_ref, sin_ref, gw_ref, gb_ref, o_ref,
                qb, kb, q2, kv, sprev, *, S, C):
    h = pl.program_id(1)
    nc = S // C
    dk = RET_QK_DIM
    half = dk // 2

    rowf = lax.broadcasted_iota(I32, (C, 1), 0).astype(F32)
    lg_f = -jnp.exp(jnp.full((1, 1), df_ref[h], F32))
    lg_b = -jnp.exp(jnp.full((1, 1), db_ref[h], F32))
    qdec_f = jnp.exp((rowf + 1.0) * lg_f)
    kdec_f = jnp.exp((C - 1.0 - rowf) * lg_f)
    qdec_b = jnp.exp((C - rowf) * lg_b)
    kdec_b = jnp.exp(rowf * lg_b)
    cd_f = jnp.exp(C * lg_f)
    cd_b = jnp.exp(C * lg_b)

    def a_body(n, carry):
        rows = pl.ds(pl.multiple_of(n * C, C), C)
        co = cos_ref[rows, :]
        si = sin_ref[rows, :]
        q = q_ref[rows, :].astype(F32)
        k = k_ref[rows, :].astype(F32)
        qr = q * co + pltpu.roll(q, half, 1) * si
        kr = (k * co + pltpu.roll(k, half, 1) * si) * (dk ** -0.5)
        qb[rows, :] = qr.astype(BF16)
        kb[rows, :] = kr.astype(BF16)
        q2[rows, :] = jnp.concatenate([qr * qdec_f, qr * qdec_b], axis=1).astype(BF16)
        k2 = jnp.concatenate([kr * kdec_f, kr * kdec_b], axis=1).astype(BF16)
        kv[n] = _dot_tn(k2, v_ref[rows, :])
        return carry

    lax.fori_loop(0, nc, a_body, 0, unroll=4)

    def scan_f(n, state):
        sprev[n, 0:dk, :] = state.astype(BF16)
        return state * cd_f + kv[n, 0:dk, :]

    lax.fori_loop(0, nc, scan_f, jnp.zeros((dk, RET_V_DIM), F32))

    def scan_b(t, state):
        n = nc - 1 - t
        sprev[n, dk:2 * dk, :] = state.astype(BF16)
        return state * cd_b + kv[n, dk:2 * dk, :]

    lax.fori_loop(0, nc, scan_b, jnp.zeros((dk, RET_V_DIM), F32))

    ri = lax.broadcasted_iota(I32, (C, C), 0)
    ci = lax.broadcasted_iota(I32, (C, C), 1)
    d = (ri - ci).astype(F32)
    dec = jnp.where(ri >= ci, jnp.exp(jnp.maximum(d, 0.0) * lg_f), jnp.exp(jnp.maximum(-d, 0.0) * lg_b))
    gw = gw_ref[...]
    gb = gb_ref[...]

    def c_body(n, carry):
        rows = pl.ds(pl.multiple_of(n * C, C), C)
        scores = _dot_nt(qb[rows, :], kb[rows, :]) * dec
        y = _dot(scores.astype(BF16), v_ref[rows, :]) + _dot(q2[rows, :], sprev[n])
        mu = jnp.mean(y, axis=-1, keepdims=True)
        yc = y - mu
        var = jnp.mean(yc * yc, axis=-1, keepdims=True)
        yn = yc * lax.rsqrt(var + EPS) * gw + gb
        g = g_ref[rows, :].astype(F32)
        o_ref[rows, :] = (g * _sigmoid(g) * yn).astype(o_ref.dtype)
        return carry

    lax.fori_loop(0, nc, c_body, 0, unroll=4)


def _retention(proj, decay_fwd, decay_bwd, gn_w, gn_b, B, S):
    T = proj.shape[0]
    dk, dv = RET_QK_DIM, RET_V_DIM
    q_off = (ATTN_Q_W + 2 * ATTN_KV_W) // dk
    k_off = q_off + RET_QK_W // dk
    v_off = (ATTN_Q_W + 2 * ATTN_KV_W + 2 * RET_QK_W) // dv
    g_off = v_off + RET_V_W // dv
    cos, sin = _rope_tables(S)
    C = min(RET_TILE, S)
    grid_spec = pltpu.PrefetchScalarGridSpec(
        num_scalar_prefetch=2,
        grid=(B, RET_HEADS),
        in_specs=[
            pl.BlockSpec((S, dk), lambda b, h, *_: (b, q_off + h)),
            pl.BlockSpec((S, dk), lambda b, h, *_: (b, k_off + h)),
            pl.BlockSpec((S, dv), lambda b, h, *_: (b, v_off + h)),
            pl.BlockSpec((S, dv), lambda b, h, *_: (b, g_off + h)),
            pl.BlockSpec((S, dk), lambda b, h, *_: (0, 0)),
            pl.BlockSpec((S, dk), lambda b, h, *_: (0, 0)),
            pl.BlockSpec((1, dv), lambda b, h, *_: (0, h)),
            pl.BlockSpec((1, dv), lambda b, h, *_: (0, h)),
        ],
        out_specs=pl.BlockSpec((S, dv), lambda b, h, *_: (b, h)),
        scratch_shapes=[
            pltpu.VMEM((S, dk), BF16),
            pltpu.VMEM((S, dk), BF16),
            pltpu.VMEM((S, 2 * dk), BF16),
            pltpu.VMEM((S // C, 2 * dk, dv), F32),
            pltpu.VMEM((S // C, 2 * dk, dv), BF16),
        ],
    )
    return pl.pallas_call(
        functools.partial(_ret_kernel, S=S, C=C),
        out_shape=jax.ShapeDtypeStruct((T, RET_V_W), BF16),
        grid_spec=grid_spec,
        compiler_params=_cparams(("arbitrary", "arbitrary")),
        name="retention",
    )(decay_fwd.astype(F32), decay_bwd.astype(F32), proj, proj, proj, proj, cos, sin,
      gn_w.reshape(1, -1), gn_b.reshape(1, -1))


def _merge_kernel(a_ref, r_ref, wa_hbm, wr_hbm, ga_ref, gr_ref, o_ref, sta, stb, wab, wrb, sem):
    _stream_weight_tile(wa_hbm, sta, wab, sem.at[0])
    _stream_weight_tile(wr_hbm, stb, wrb, sem.at[1])
    ua = _dot(a_ref[...], wab[...])
    ur = _dot(r_ref[...], wrb[...])
    out = _sigmoid(ga_ref[...].astype(F32)) * ua + _sigmoid(gr_ref[...].astype(F32)) * ur
    o_ref[...] = out.astype(o_ref.dtype)


def _merge_up(attn, ret, wa, wr, proj, D):
    T = attn.shape[0]
    tm = min(512, T)
    tn = min(1024, D)
    ga_off = (ATTN_Q_W + 2 * ATTN_KV_W + 2 * RET_QK_W + 2 * RET_V_W) // tn
    gr_off = ga_off + D // tn
    return pl.pallas_call(
        _merge_kernel,
        out_shape=jax.ShapeDtypeStruct((T, D), BF16),
        grid=(D // tn, T // tm),
        in_specs=[
            pl.BlockSpec((tm, ATTN_Q_W), lambda j, i: (i, 0)),
            pl.BlockSpec((tm, RET_V_W), lambda j, i: (i, 0)),
            pl.BlockSpec(memory_space=pl.ANY),
            pl.BlockSpec(memory_space=pl.ANY),
            pl.BlockSpec((tm, tn), lambda j, i: (i, ga_off + j)),
            pl.BlockSpec((tm, tn), lambda j, i: (i, gr_off + j)),
        ],
        out_specs=pl.BlockSpec((tm, tn), lambda j, i: (i, j)),
        scratch_shapes=[pltpu.VMEM((ATTN_Q_W, tn), F32), pltpu.VMEM((RET_V_W, tn), F32),
                        pltpu.VMEM((ATTN_Q_W, tn), BF16), pltpu.VMEM((RET_V_W, tn), BF16),
                        pltpu.SemaphoreType.DMA((2,))],
        compiler_params=_cparams(("arbitrary", "arbitrary")),
        name="merge_up",
    )(attn, ret, wa, wr, proj, proj)


def _out_proj_kernel(m_ref, w_hbm, x_ref, mod_ref, o_ref, stage, wb, sem, *, gate_idx):
    _stream_weight_tile(w_hbm, stage, wb, sem)
    y = _dot(m_ref[...], wb[...])
    o_ref[...] = x_ref[...] + mod_ref[gate_idx:gate_idx + 1, :] * y


def _out_proj(merged, w, x2, mod, S, gate_idx):
    T, D = x2.shape
    tm = min(512, S)
    tn = min(1024, D)
    return pl.pallas_call(
        functools.partial(_out_proj_kernel, gate_idx=gate_idx),
        out_shape=jax.ShapeDtypeStruct((T, D), F32),
        grid=(D // tn, T // tm),
        in_specs=[
            pl.BlockSpec((tm, D), lambda j, i: (i, 0)),
            pl.BlockSpec(memory_space=pl.ANY),
            pl.BlockSpec((tm, tn), lambda j, i: (i, j)),
            pl.BlockSpec((None, N_ADA, tn), lambda j, i: ((i * tm) // S, 0, j)),
        ],
        out_specs=pl.BlockSpec((tm, tn), lambda j, i: (i, j)),
        scratch_shapes=[pltpu.VMEM((D, tn), F32), pltpu.VMEM((D, tn), BF16), pltpu.SemaphoreType.DMA],
        compiler_params=_cparams(("arbitrary", "arbitrary")),
        name="out_proj",
    )(merged, w, x2, mod)


def _pack_bf16_pairs(v):
    n = v.shape[1] // 2
    bits = lax.bitcast_convert_type(v.astype(BF16).astype(F32), U32)
    return jnp.bitwise_or(jnp.bitwise_and(bits[:, n:], jnp.uint32(0xFFFF0000)),
                          jnp.right_shift(bits[:, :n], jnp.uint32(16)))


def _unpack_bf16_pairs(w):
    lo = lax.bitcast_convert_type(jnp.left_shift(w, jnp.uint32(16)), F32)
    hi = lax.bitcast_convert_type(jnp.bitwise_and(w, jnp.uint32(0xFFFF0000)), F32)
    return jnp.concatenate([lo, hi], axis=1)


def _norm_router_kernel(x_ref, nw_ref, mod_ref, rwt_ref, rb_ref, hp_ref, e_ref, w_ref, r_ref, cnt_ref, carry,
                        *, shift_idx, scale_idx):
    i = pl.program_id(0)

    @pl.when(i == 0)
    def _():
        carry[...] = jnp.zeros_like(carry)

    h = _norm_mod_value(x_ref[...], nw_ref[...], mod_ref[shift_idx:shift_idx + 1, :],
                        mod_ref[scale_idx:scale_idx + 1, :])
    hp_ref[...] = _pack_bf16_pairs(h)
    hb = h.astype(BF16)
    tm = hb.shape[0]
    E = rwt_ref.shape[0]
    logits = _dot_nt(rwt_ref[...], hb) + rb_ref[...]
    iota_e = lax.broadcasted_iota(I32, (E, tm), 0)
    vals, idxs = [], []
    work = logits
    sel = jnp.zeros((E, tm), F32)
    for _k in range(TOP_K):
        m = jnp.max(work, axis=0, keepdims=True)
        idx = jnp.min(jnp.where(work == m, iota_e, E), axis=0, keepdims=True)
        hit = iota_e == idx
        vals.append(m)
        idxs.append(idx)
        work = jnp.where(hit, NEG_BIG, work)
        sel = sel + hit.astype(F32)
    ex = [jnp.exp(v - vals[0]) for v in vals]
    tot = ex[0]
    for v in ex[1:]:
        tot = tot + v
    ri = lax.broadcasted_iota(I32, (tm, tm), 0)
    ci = lax.broadcasted_iota(I32, (tm, tm), 1)
    upper = (ri < ci).astype(BF16)
    before = _dot(sel.astype(BF16), upper) + carry[:, 0:1]
    for k in range(TOP_K):
        e_ref[k:k + 1, :] = idxs[k]
        w_ref[k:k + 1, :] = ex[k] / tot
        r_ref[k:k + 1, :] = jnp.sum(jnp.where(iota_e == idxs[k], before, 0.0), axis=0, keepdims=True).astype(I32)
    carry[...] = carry[...] + jnp.sum(sel, axis=1, keepdims=True)
    cnt_ref[...] = carry[...]


def _norm_router(x1, nw, mod, router_w, router_b, S, shift_idx, scale_idx):
    T, D = x1.shape
    E = router_w.shape[1]
    tm = min(512, S)
    outs = pl.pallas_call(
        functools.partial(_norm_router_kernel, shift_idx=shift_idx, scale_idx=scale_idx),
        out_shape=(
            jax.ShapeDtypeStruct((T, D // 2), U32),
            jax.ShapeDtypeStruct((TOP_K, T), I32),
            jax.ShapeDtypeStruct((TOP_K, T), F32),
            jax.ShapeDtypeStruct((TOP_K, T), I32),
            jax.ShapeDtypeStruct((E, 128), F32),
        ),
        grid=(T // tm,),
        in_specs=[
            pl.BlockSpec((tm, D), lambda i: (i, 0)),
            pl.BlockSpec((1, D), lambda i: (0, 0)),
            pl.BlockSpec((None, N_ADA, D), lambda i: ((i * tm) // S, 0, 0)),
            pl.BlockSpec((E, D), lambda i: (0, 0)),
            pl.BlockSpec((E, 1), lambda i: (0, 0)),
        ],
        out_specs=(
            pl.BlockSpec((tm, D // 2), lambda i: (i, 0)),
            pl.BlockSpec((TOP_K, tm), lambda i: (0, i)),
            pl.BlockSpec((TOP_K, tm), lambda i: (0, i)),
            pl.BlockSpec((TOP_K, tm), lambda i: (0, i)),
            pl.BlockSpec((E, 128), lambda i: (0, 0)),
        ),
        scratch_shapes=[pltpu.VMEM((E, 128), F32)],
        compiler_params=_cparams(("arbitrary",)),
        name="norm_router",
    )(x1, nw.reshape(1, D), mod, router_w.T.astype(BF16), router_b.reshape(E, 1).astype(F32))
    return outs


def _dispatch_kernel(pos_ref, pad_ref, hp_ref, xs_ref, zeros, sem, zsem, *, tm, T, E, n_tiles):
    base = pl.program_id(0) * tm

    @pl.when(pl.program_id(0) == 0)
    def _():
        zeros[...] = jnp.zeros_like(zeros)
        def pad_copy(row):
            return pltpu.make_async_copy(zeros.at[pl.ds(0, 1)], xs_ref.at[pl.ds(row, 1)], zsem)

        def tail_copy(t):
            return pltpu.make_async_copy(
                zeros, xs_ref.at[pl.ds(pl.multiple_of(t * MOE_TILE, MOE_TILE), MOE_TILE)], zsem)

        def pad8_copy(row):
            return pltpu.make_async_copy(zeros.at[pl.ds(0, 8)], xs_ref.at[pl.ds(pl.multiple_of(row, 8), 8)], zsem)

        def pad_pieces(e):
            start, count = pad_ref[e], pad_ref[E + e]
            head = jnp.minimum(count, jnp.bitwise_and(-start, 7))
            return start, head, start + head, (count - head) // 8

        def pad_start(e, carry):
            start, head, body, n8 = pad_pieces(e)
            lax.fori_loop(0, head, lambda r, c: (pad_copy(start + r).start(), c)[1], 0)
            lax.fori_loop(0, n8, lambda r, c: (pad8_copy(body + 8 * r).start(), c)[1], 0)
            return carry

        def pad_wait(e, carry):
            start, head, body, n8 = pad_pieces(e)
            lax.fori_loop(0, head, lambda r, c: (pad_copy(start + r).wait(), c)[1], 0)
            lax.fori_loop(0, n8, lambda r, c: (pad8_copy(body + 8 * r).wait(), c)[1], 0)
            return carry

        def tail_start(t, carry):
            tail_copy(t).start()
            return carry

        def tail_wait(t, carry):
            tail_copy(t).wait()
            return carry

        lax.fori_loop(0, E, pad_start, 0)
        lax.fori_loop(pad_ref[2 * E], n_tiles, tail_start, 0)
        lax.fori_loop(0, E, pad_wait, 0)
        lax.fori_loop(pad_ref[2 * E], n_tiles, tail_wait, 0)

    def start_body(t, carry):
        for k in range(TOP_K):
            dst = pos_ref[k * T + base + t]
            pltpu.make_async_copy(hp_ref.at[pl.ds(t, 1)], xs_ref.at[pl.ds(dst, 1)], sem).start(priority=k % 2)
        return carry

    lax.fori_loop(0, tm, start_body, 0, unroll=8)
    for k in range(TOP_K):
        pltpu.make_async_copy(hp_ref, xs_ref.at[pl.ds(0, tm)], sem).wait()


def _dispatch(pos_flat, pad_info, hp, n_tiles, E):
    T, W = hp.shape
    tm = min(512, T)
    grid_spec = pltpu.PrefetchScalarGridSpec(
        num_scalar_prefetch=2,
        grid=(T // tm,),
        in_specs=[pl.BlockSpec((tm, W), lambda i, p, q: (i, 0))],
        out_specs=pl.BlockSpec(memory_space=pl.ANY),
        scratch_shapes=[pltpu.VMEM((MOE_TILE, W), U32), pltpu.SemaphoreType.DMA, pltpu.SemaphoreType.DMA],
    )
    return pl.pallas_call(
        functools.partial(_dispatch_kernel, tm=tm, T=T, E=E, n_tiles=n_tiles),
        out_shape=jax.ShapeDtypeStruct((n_tiles * MOE_TILE, W), U32),
        grid_spec=grid_spec,
        compiler_params=_cparams(("arbitrary",)),
        name="dispatch",
    )(pos_flat, pad_info, hp)


def _expert_changed(te_ref, i):
    return jnp.logical_or(i == 0, te_ref[i] != te_ref[jnp.maximum(i - 1, 0)])


def _cast_rows(src_ref, dst_ref, rows_per_pass=256):
    def body(r, carry):
        rows = pl.ds(pl.multiple_of(r * rows_per_pass, rows_per_pass), rows_per_pass)
        dst_ref[rows, :] = src_ref[rows, :].astype(dst_ref.dtype)
        return carry

    lax.fori_loop(0, src_ref.shape[0] // rows_per_pass, body, 0)


def _expert_weight_stage(te_ref, nx_ref, i, first_step, n, n_passes, copies, consume):
    @pl.when(_expert_changed(te_ref, i))
    def _():
        @pl.when(first_step)
        def _():
            for cp in copies(te_ref[0], 0):
                cp.start()

        for cp in copies(te_ref[i], n):
            cp.wait()
        consume()
        same_pass = nx_ref[i] >= 0
        nxt_e = jnp.where(same_pass, nx_ref[i], te_ref[0])
        nxt_n = jnp.where(same_pass, n, n + 1)

        @pl.when(jnp.logical_or(same_pass, n + 1 < n_passes))
        def _():
            for cp in copies(nxt_e, nxt_n):
                cp.start()


def _expert_up_kernel(te_ref, nu_ref, nx_ref, tr_ref, xs_ref, w1_hbm, bg_ref, bu_ref, o_ref, stage, wgb, wub, sem,
                      *, tn, F, nf):
    n = pl.program_id(0)
    i = pl.program_id(1)

    def copies(e, nn):
        c0 = pl.multiple_of(nn * tn, tn)
        return (pltpu.make_async_copy(w1_hbm.at[e, :, pl.ds(c0, tn)], stage.at[0], sem),
                pltpu.make_async_copy(w1_hbm.at[e, :, pl.ds(F + c0, tn)], stage.at[1], sem))

    def consume():
        _cast_rows(stage.at[0], wgb)
        _cast_rows(stage.at[1], wub)

    _expert_weight_stage(te_ref, nx_ref, i, jnp.logical_and(n == 0, i == 0), n, nf, copies, consume)

    def ffn_up(rows):
        x = _unpack_bf16_pairs(xs_ref[rows, :]).astype(BF16)
        gate = jnp.minimum(_dot(x, wgb[...]) + bg_ref[...], SWIGLU_LIMIT)
        up = jnp.clip(_dot(x, wub[...]) + bu_ref[...], -SWIGLU_LIMIT, SWIGLU_LIMIT)
        act = gate * _sigmoid(SWIGLU_ALPHA * gate) * (up + 1.0)
        o_ref[rows, :] = act.astype(o_ref.dtype)

    _per_tile_rows(tr_ref[i], o_ref, ffn_up)


def _per_tile_rows(valid_rows, o_ref, compute):
    tm = o_ref.shape[0]
    half = tm // 2

    @pl.when(valid_rows > half)
    def _():
        compute(slice(0, tm))

    @pl.when(jnp.logical_and(valid_rows > 0, valid_rows <= half))
    def _():
        compute(slice(0, half))
        o_ref[half:, :] = jnp.zeros((tm - half, o_ref.shape[1]), o_ref.dtype)

    @pl.when(valid_rows <= 0)
    def _():
        o_ref[...] = jnp.zeros_like(o_ref)


def _expert_down_kernel(te_ref, nu_ref, nx_ref, tr_ref, a_ref, w2_hbm, b2_ref, o_ref, stage, w2b, sem):
    i = pl.program_id(0)

    def copies(e, nn):
        del nn
        return (pltpu.make_async_copy(w2_hbm.at[e], stage, sem),)

    def consume():
        _cast_rows(stage, w2b)

    _expert_weight_stage(te_ref, nx_ref, i, i == 0, 0, 1, copies, consume)

    def ffn_down(rows):
        y = _dot(a_ref[rows, :], w2b[...]) + b2_ref[...]
        o_ref[rows, :] = _pack_bf16_pairs(y)

    _per_tile_rows(tr_ref[i], o_ref, ffn_down)


def _expert_ffn(xs, tile_e, n_used, next_e, tile_rows, w1, b1, w2, b2):
    R, W = xs.shape
    E, D, F2 = w1.shape
    F = F2 // 2
    tm = MOE_TILE
    nt = R // tm

    tn = min(512, F)
    nf = F // tn
    b1r = b1.reshape(E, 1, F2)

    act = pl.pallas_call(
        functools.partial(_expert_up_kernel, tn=tn, F=F, nf=nf),
        out_shape=jax.ShapeDtypeStruct((R, F), BF16),
        grid_spec=pltpu.PrefetchScalarGridSpec(
            num_scalar_prefetch=4,
            grid=(nf, nt),
            in_specs=[
                pl.BlockSpec((tm, W), lambda n, i, te, nu, nx, tr: (jnp.minimum(i, nu[0] - 1), 0)),
                pl.BlockSpec(memory_space=pl.ANY),
                pl.BlockSpec((None, 1, tn), lambda n, i, te, nu, nx, tr: (te[i], 0, n)),
                pl.BlockSpec((None, 1, tn), lambda n, i, te, nu, nx, tr: (te[i], 0, nf + n)),
            ],
            out_specs=pl.BlockSpec((tm, tn), lambda n, i, te, nu, nx, tr: (i, n)),
            scratch_shapes=[pltpu.VMEM((2, D, tn), F32), pltpu.VMEM((D, tn), BF16), pltpu.VMEM((D, tn), BF16),
                            pltpu.SemaphoreType.DMA],
        ),
        compiler_params=_cparams(("arbitrary", "arbitrary")),
        name="expert_up",
    )(tile_e, n_used, next_e, tile_rows, xs, w1, b1r, b1r)
    y = pl.pallas_call(
        _expert_down_kernel,
        out_shape=jax.ShapeDtypeStruct((R, D // 2), U32),
        grid_spec=pltpu.PrefetchScalarGridSpec(
            num_scalar_prefetch=4,
            grid=(nt,),
            in_specs=[
                pl.BlockSpec((tm, F), lambda i, te, nu, nx, tr: (jnp.minimum(i, nu[0] - 1), 0)),
                pl.BlockSpec(memory_space=pl.ANY),
                pl.BlockSpec((None, 1, D), lambda i, te, nu, nx, tr: (te[i], 0, 0)),
            ],
            out_specs=pl.BlockSpec((tm, D // 2), lambda i, te, nu, nx, tr: (i, 0)),
            scratch_shapes=[pltpu.VMEM((F, D), F32), pltpu.VMEM((F, D), BF16), pltpu.SemaphoreType.DMA],
        ),
        compiler_params=_cparams(("arbitrary",)),
        name="expert_down",
    )(tile_e, n_used, next_e, tile_rows, act, w2, b2.reshape(E, 1, D))
    return y


def _combine_kernel(pos_ref, x_ref, w_ref, mod_ref, y_ref, o_ref, buf, sem, *, tm, T, gate_idx):
    i = pl.program_id(0)
    n = pl.num_programs(0)
    W = buf.shape[-1]
    th = tm
    ring = buf.shape[0]

    def issue_row(first_token, slot, t):
        for k in range(TOP_K):
            src = pos_ref[k * T + first_token + t]
            pltpu.make_async_copy(y_ref.at[pl.ds(src, 1)], buf.at[slot, k, pl.ds(t, 1)],
                                  sem.at[slot]).start(priority=k % 2)

    def wait_half(slot):
        for k in range(TOP_K):
            pltpu.make_async_copy(y_ref.at[pl.ds(0, th)], buf.at[slot, k], sem.at[slot]).wait()

    rc = 8
    cw = min(512, W)

    def consume(slot, next_first_token, next_slot):
        def rows_body(r, carry):
            rows = pl.ds(pl.multiple_of(r * rc, rc), rc)
            brow = rows
            wv = w_ref[rows, :]
            wk = [jnp.broadcast_to(wv[:, k:k + 1], (rc, cw)) for k in range(TOP_K)]
            for c in range(W // cw):
                lo = hi = None
                for k in range(TOP_K):
                    u = buf[slot, k, brow, c * cw:(c + 1) * cw]
                    l = wk[k] * lax.bitcast_convert_type(jnp.left_shift(u, jnp.uint32(16)), F32)
                    h = wk[k] * lax.bitcast_convert_type(jnp.bitwise_and(u, jnp.uint32(0xFFFF0000)), F32)
                    lo = l if lo is None else lo + l
                    hi = h if hi is None else hi + h
                for half, acc in ((0, lo), (1, hi)):
                    cols = slice(half * W + c * cw, half * W + (c + 1) * cw)
                    o_ref[rows, cols] = x_ref[rows, cols] + mod_ref[gate_idx:gate_idx + 1, cols] * acc
            for t in range(rc):
                issue_row(next_first_token, next_slot, r * rc + t)
            return carry

        lax.fori_loop(0, th // rc, rows_body, 0)

    @pl.when(i == 0)
    def _():
        for b in range(2):
            lax.fori_loop(0, tm, lambda t, c: (issue_row(b * tm, b, t), c)[1], 0, unroll=8)

    wait_half(i % ring)
    consume(i % ring, jnp.minimum((i + 2) * tm, T - tm), (i + 2) % ring)

    @pl.when(i == n - 1)
    def _():
        wait_half((i + 1) % ring)
        wait_half((i + 2) % ring)


def _combine(pos_flat, x1, w_tk, mod, y, S, gate_idx):
    T, D = x1.shape
    tm = min(128, S)
    assert T // tm >= 2
    grid_spec = pltpu.PrefetchScalarGridSpec(
        num_scalar_prefetch=1,
        grid=(T // tm,),
        in_specs=[
            pl.BlockSpec((tm, D), lambda i, p: (i, 0)),
            pl.BlockSpec((tm, TOP_K), lambda i, p: (i, 0)),
            pl.BlockSpec((None, N_ADA, D), lambda i, p: ((i * tm) // S, 0, 0)),
            pl.BlockSpec(memory_space=pl.ANY),
        ],
        out_specs=pl.BlockSpec((tm, D), lambda i, p: (i, 0)),
        scratch_shapes=[pltpu.VMEM((3, TOP_K, tm, D // 2), U32), pltpu.SemaphoreType.DMA((3,))],
    )
    return pl.pallas_call(
        functools.partial(_combine_kernel, tm=tm, T=T, gate_idx=gate_idx),
        out_shape=jax.ShapeDtypeStruct((T, D), F32),
        grid_spec=grid_spec,
        compiler_params=_cparams(("arbitrary",)),
        name="combine",
    )(pos_flat, x1, w_tk, mod, y)


def _layer(x2, mod, B, S, rel_bias, norm_mix_w, w_in, q_norm_w, k_norm_w, attn_sink, ret_decay_fwd, ret_decay_bwd,
           ret_gn_w, ret_gn_b, w_up_attn, w_up_ret, w_out, norm_ffn_w, router_w, router_b,
           expert_w1, expert_b1, expert_w2, expert_b2):
    T, D = x2.shape
    E = router_w.shape[1]
    h = _norm_mod(x2, norm_mix_w, mod, S, 0, 1)
    proj = _in_proj(h, w_in, q_norm_w, k_norm_w)
    attn = _window_attention(proj, _attn_bias_table(rel_bias), attn_sink, B, S)
    ret = _retention(proj, ret_decay_fwd, ret_decay_bwd, ret_gn_w, ret_gn_b, B, S)
    merged = _merge_up(attn, ret, w_up_attn, w_up_ret, proj, D)
    x1 = _out_proj(merged, w_out, x2, mod, S, 2)

    hp, top_e, top_w, rank, cnt = _norm_router(x1, norm_ffn_w, mod, router_w, router_b, S, 3, 4)
    tm = MOE_TILE
    counts = cnt[:, 0].astype(I32)
    tiles_per_e = (counts + tm - 1) // tm
    tile_end = jnp.cumsum(tiles_per_e)
    group_start = (tile_end - tiles_per_e) * tm
    n_tiles = (T * TOP_K) // tm + E
    n_used = tile_end[-1]
    tile_ids = jnp.minimum(jnp.arange(n_tiles, dtype=I32), n_used - 1)
    tile_e = jnp.minimum(jnp.sum((tile_ids[:, None] >= tile_end[None, :]).astype(I32), axis=1), E - 1)
    e_ids = jnp.arange(E, dtype=I32)
    later = jnp.logical_and(e_ids[None, :] > e_ids[:, None], tiles_per_e[None, :] > 0)
    next_of_e = jnp.min(jnp.where(later, e_ids[None, :], E), axis=1)
    next_of_e = jnp.where(next_of_e == E, -1, next_of_e)
    next_e = jnp.sum(jnp.where(tile_e[:, None] == e_ids, next_of_e, 0), axis=1).astype(I32)
    all_tiles = jnp.arange(n_tiles, dtype=I32)
    in_e = tile_e[:, None] == e_ids
    rows_left = jnp.sum(jnp.where(in_e, counts + group_start, 0), axis=1) - all_tiles * tm
    tile_rows = jnp.where(all_tiles < n_used, jnp.clip(rows_left, 0, tm), 0).astype(I32)
    start_of = jnp.sum(jnp.where(top_e[:, :, None] == e_ids, group_start, 0), axis=-1)
    pos_flat = (start_of + rank).astype(I32).reshape(-1)
    pad_info = jnp.concatenate([group_start + counts, tiles_per_e * tm - counts, n_used.reshape(1)]).astype(I32)

    xs = _dispatch(pos_flat, pad_info, hp, n_tiles, E)
    y = _expert_ffn(xs, tile_e, n_used.reshape(1).astype(I32), next_e, tile_rows, expert_w1, expert_b1,
                    expert_w2, expert_b2)
    return _combine(pos_flat, x1, top_w.T, mod, y, S, 5)


def kernel(x, c, rel_bias, ada_w, ada_b, norm_mix_w, w_in, q_norm_w, k_norm_w, attn_sink, ret_decay_fwd,
           ret_decay_bwd, ret_gn_w, ret_gn_b, w_up_attn, w_up_ret, w_out, norm_ffn_w, router_w, router_b,
           expert_w1, expert_b1, expert_w2, expert_b2):
    B, S, D = x.shape
    x2 = x.reshape(B * S, D)
    for l in range(ada_w.shape[0]):
        mod = _ada_mod(c, ada_w[l], ada_b[l])
        x2 = _layer(x2, mod, B, S, rel_bias, norm_mix_w[l], w_in[l], q_norm_w[l], k_norm_w[l], attn_sink[l],
                    ret_decay_fwd[l], ret_decay_bwd[l], ret_gn_w[l], ret_gn_b[l], w_up_attn[l], w_up_ret[l],
                    w_out[l], norm_ffn_w[l], router_w[l], router_b[l], expert_w1[l], expert_b1[l],
                    expert_w2[l], expert_b2[l])
    return x2.reshape(B, S, D)
```
